```python
import jax, jax.numpy as jnp
from jax import lax
import numpy as np

D_MODEL = 2048
BATCH = 4
SEQ = 4096
DEPTH = 2

N_EVEN = (DEPTH + 1) // 2
N_ODD = DEPTH // 2
N_MEM = 256
ROPE_THETA = 500000.0
ROT_FRAC = 4
EPS = 1e-6
Q_BLOCK = 128

MLA_HEADS = 8
MLA_Q_RANK = 512
MLA_KV_RANK = 256
MLA_NOPE = 128
MLA_ROPE = 64
MLA_V = 128
MOBA_HEADS = 8
MOBA_DH = 128
MOBA_BLOCK = 256
MOBA_TOPK = 3
MOBA_QCHUNK = 32
SWA_HEADS = 16
SWA_KV_HEADS = 2
SWA_DH = 64
SWA_WINDOW = 128
DIL_HEADS = 6
DIL_DH = 128
DIL_PATTERNS = ((128, 1), (512, 4), (2048, 16))
MEM_HEADS = 4
MEM_DH = 128

MEMQ_W = MEM_HEADS * MEM_DH
EVEN_VW = MLA_HEADS * MLA_V + MOBA_HEADS * MOBA_DH + MEMQ_W
EVEN_SPLITS = (MLA_Q_RANK, MLA_KV_RANK, MLA_ROPE, 3 * MOBA_HEADS * MOBA_DH, MEMQ_W, EVEN_VW)
EVEN_IN = sum(EVEN_SPLITS)
ODD_VW = SWA_HEADS * SWA_DH + DIL_HEADS * DIL_DH + MEMQ_W
ODD_SPLITS = ((SWA_HEADS + 2 * SWA_KV_HEADS) * SWA_DH, 3 * DIL_HEADS * DIL_DH, MEMQ_W, ODD_VW)
ODD_IN = sum(ODD_SPLITS)

kernel_name = "hybrid_mla_moba_swa_dilated_memory"


def split_cols(h, sizes):
    return jnp.split(h, [int(i) for i in np.cumsum(sizes)[:-1]], axis=-1)


def rms_norm(x, g):
    xf = x.astype(jnp.float32)
    y = xf * lax.rsqrt(jnp.mean(xf * xf, axis=-1, keepdims=True) + EPS)
    return (y * g.astype(jnp.float32)).astype(x.dtype)


def apply_rope(x, rot_dim):
    S = x.shape[1]
    inv = 1.0 / (ROPE_THETA ** (jnp.arange(0, rot_dim, 2, dtype=jnp.float32) / rot_dim))
    ang = jnp.arange(S, dtype=jnp.float32)[:, None] * inv[None, :]
    c, s = jnp.cos(ang)[None, :, None, :], jnp.sin(ang)[None, :, None, :]
    xr = x[..., :rot_dim].astype(jnp.float32)
    x1, x2 = xr[..., : rot_dim // 2], xr[..., rot_dim // 2:]
    rot = jnp.concatenate([x1 * c - x2 * s, x1 * s + x2 * c], axis=-1).astype(x.dtype)
    return jnp.concatenate([rot, x[..., rot_dim:]], axis=-1)


def causal_attention_qblocks(q, k, v, scale):
    B, H, S, dq = q.shape
    nb = S // Q_BLOCK
    qb = q.reshape(B, H, nb, Q_BLOCK, dq).transpose(2, 0, 1, 3, 4)
    kpos = jnp.arange(S)

    def one(args):
        qi, n = args
        s = jnp.einsum('bhqd,bhkd->bhqk', qi, k).astype(jnp.float32) * scale
        qpos = n * Q_BLOCK + jnp.arange(Q_BLOCK)
        s = jnp.where(kpos[None, :] <= qpos[:, None], s, -jnp.inf)
        p = jax.nn.softmax(s, axis=-1).astype(v.dtype)
        return jnp.einsum('bhqk,bhkd->bhqd', p, v)

    o = lax.map(one, (qb, jnp.arange(nb)))
    return o.transpose(1, 2, 0, 3, 4).reshape(B, H, S, v.shape[-1])


def moba_attention(q, k, v):
    B, H, S, dh = q.shape
    L = MOBA_BLOCK
    nblk = -(-S // L)
    Sp = nblk * L
    pad = ((0, 0), (0, 0), (0, Sp - S), (0, 0))
    kb = jnp.pad(k, pad).reshape(B, H, nblk, L, dh)
    vb = jnp.pad(v, pad).reshape(B, H, nblk, L, dh)
    kmean = jnp.mean(kb.astype(jnp.float32), axis=3)
    topk = min(MOBA_TOPK, nblk - 1)
    scale = dh ** -0.5
    QC = MOBA_QCHUNK
    nc = S // QC
    qc = q.reshape(B, H, nc, QC, dh).transpose(2, 0, 1, 3, 4)
    offs = jnp.arange(L)
    gather = jax.vmap(jax.vmap(lambda t, i: t[i]))

    def one(args):
        qi, c = args
        blk = (c * QC) // L
        qpos = c * QC + jnp.arange(QC)
        k_own = lax.dynamic_index_in_dim(kb, blk, axis=2, keepdims=False)
        v_own = lax.dynamic_index_in_dim(vb, blk, axis=2, keepdims=False)
        s_own = jnp.einsum('bhqd,bhld->bhql', qi, k_own).astype(jnp.float32) * scale
        s_own = jnp.where((blk * L + offs)[None, :] <= qpos[:, None], s_own, -jnp.inf)
        if topk == 0:
            p = jax.nn.softmax(s_own, axis=-1).astype(v.dtype)
            return jnp.einsum('bhql,bhld->bhqd', p, v_own)
        gate = jnp.einsum('bhqd,bhnd->bhqn', qi.astype(jnp.float32), kmean)
        gate = jnp.where(jnp.arange(nblk) < blk, gate, -jnp.inf)
        _, idx = lax.top_k(gate, topk)
        valid = jnp.arange(topk) < blk
        kg = gather(kb, idx)
        vg = gather(vb, idx)
        s_sel = jnp.einsum('bhqd,bhqkld->bhqkl', qi, kg).astype(jnp.float32) * scale
        s_sel = jnp.where(valid[:, None], s_sel, -jnp.inf)
        s_all = jnp.concatenate([s_sel.reshape(B, H, QC, topk * L), s_own], axis=-1)
        p = jax.nn.softmax(s_all, axis=-1).astype(v.dtype)
        p_sel = p[..., : topk * L].reshape(B, H, QC, topk, L)
        return (jnp.einsum('bhqkl,bhqkld->bhqd', p_sel, vg)
                + jnp.einsum('bhql,bhld->bhqd', p[..., topk * L:], v_own))

    o = lax.map(one, (qc, jnp.arange(nc)))
    return o.transpose(1, 2, 0, 3, 4).reshape(B, H, S, dh)


def banded_attention(q, k, v, max_dist, scale):
    B, Hk, G, L, dh = q.shape
    QB = Q_BLOCK
    nb = -(-L // QB)
    Lp = nb * QB
    qp = jnp.pad(q, ((0, 0), (0, 0), (0, 0), (0, Lp - L), (0, 0))).reshape(B, Hk, G, nb, QB, dh)
    kp = jnp.pad(k, ((0, 0), (0, 0), (QB, Lp - L), (0, 0)))
    vp = jnp.pad(v, ((0, 0), (0, 0), (QB, Lp - L), (0, 0)))
    kband = jnp.concatenate([kp[:, :, :Lp].reshape(B, Hk, nb, QB, dh),
                             kp[:, :, QB:].reshape(B, Hk, nb, QB, dh)], axis=3)
    vband = jnp.concatenate([vp[:, :, :Lp].reshape(B, Hk, nb, QB, dh),
                             vp[:, :, QB:].reshape(B, Hk, nb, QB, dh)], axis=3)
    s = jnp.einsum('bkgnqd,bknsd->bkgnqs', qp, kband).astype(jnp.float32) * scale
    dist = QB + jnp.arange(QB)[:, None] - jnp.arange(2 * QB)[None, :]
    kpos = jnp.arange(nb)[:, None] * QB - QB + jnp.arange(2 * QB)[None, :]
    mask = ((dist >= 0) & (dist <= max_dist))[None] & (kpos >= 0)[:, None, :]
    s = jnp.where(mask, s, -jnp.inf)
    m = jnp.max(s, axis=-1, keepdims=True)
    p = jnp.exp(s - m)
    l = jnp.sum(p, axis=-1, keepdims=True)
    o = jnp.einsum('bkgnqs,bknsd->bkgnqd', (p / l).astype(v.dtype), vband)
    lse = (m + jnp.log(l))[..., 0]
    return (o.reshape(B, Hk, G, Lp, dh)[:, :, :, :L], lse.reshape(B, Hk, G, Lp)[:, :, :, :L])


def dilated_attention(q, k, v):
    B, S, H, dh = q.shape
    outs, lses = [], []
    for window, dil in DIL_PATTERNS:
        L = S // dil

        def to_sub(t):
            return t.reshape(B, L, dil, H, dh).transpose(0, 2, 3, 1, 4).reshape(B * dil, H, L, dh)

        o, lse = banded_attention(to_sub(q)[:, :, None], to_sub(k), to_sub(v), window // dil, dh ** -0.5)
        outs.append(o[:, :, 0].reshape(B, dil, H, L, dh).transpose(0, 3, 1, 2, 4).reshape(B, S, H, dh))
        lses.append(lse[:, :, 0].reshape(B, dil, H, L).transpose(0, 3, 1, 2).reshape(B, S, H))
    w = jax.nn.softmax(jnp.stack(lses, axis=0), axis=0)
    return jnp.sum(w[..., None] * jnp.stack(outs, axis=0).astype(jnp.float32), axis=0).astype(q.dtype)


def memory_attention(q_mem, mem, mem_norm_g, w_mem_kv):
    B, S, _ = q_mem.shape
    kv = (rms_norm(mem, mem_norm_g) @ w_mem_kv).reshape(B, -1, 2, MEM_HEADS, MEM_DH)
    q = q_mem.reshape(B, S, MEM_HEADS, MEM_DH)
    s = jnp.einsum('bshd,bmhd->bhsm', q, kv[:, :, 0]).astype(jnp.float32) * MEM_DH ** -0.5
    p = jax.nn.softmax(s, axis=-1).astype(q.dtype)
    return jnp.einsum('bhsm,bmhd->bshd', p, kv[:, :, 1]).reshape(B, S, MEMQ_W)


def even_layer(x, mem, norm_g, w_in, q_norm_g, w_uq, kv_norm_g, w_ukv, mem_norm_g, w_mem_kv, w_out):
    B, S, _ = x.shape
    h = rms_norm(x, norm_g) @ w_in
    c_q, c_kv, k_pe, qkv_b, q_mem, gate = split_cols(h, EVEN_SPLITS)
    q_a = (rms_norm(c_q, q_norm_g) @ w_uq).reshape(B, S, MLA_HEADS, MLA_NOPE + MLA_ROPE)
    q_a = jnp.concatenate([q_a[..., :MLA_NOPE], apply_rope(q_a[..., MLA_NOPE:], MLA_ROPE)], axis=-1)
    kv_a = (rms_norm(c_kv, kv_norm_g) @ w_ukv).reshape(B, S, MLA_HEADS, MLA_NOPE + MLA_V)
    k_pe = apply_rope(k_pe[:, :, None, :], MLA_ROPE)
    k_a = jnp.concatenate([kv_a[..., :MLA_NOPE],
                           jnp.broadcast_to(k_pe, (B, S, MLA_HEADS, MLA_ROPE))], axis=-1)
    v_a = kv_a[..., MLA_NOPE:]
    o_a = causal_attention_qblocks(q_a.transpose(0, 2, 1, 3), k_a.transpose(0, 2, 1, 3),
                                   v_a.transpose(0, 2, 1, 3), (MLA_NOPE + MLA_ROPE) ** -0.5)
    o_a = o_a.transpose(0, 2, 1, 3).reshape(B, S, MLA_HEADS * MLA_V)
    qkv_b = qkv_b.reshape(B, S, 3, MOBA_HEADS, MOBA_DH)
    rot = MOBA_DH // ROT_FRAC
    q_b = apply_rope(qkv_b[:, :, 0], rot).transpose(0, 2, 1, 3)
    k_b = apply_rope(qkv_b[:, :, 1], rot).transpose(0, 2, 1, 3)
    v_b = qkv_b[:, :, 2].transpose(0, 2, 1, 3)
    o_b = moba_attention(q_b, k_b, v_b).transpose(0, 2, 1, 3).reshape(B, S, MOBA_HEADS * MOBA_DH)
    o_m = memory_attention(q_mem, mem, mem_norm_g, w_mem_kv)
    y = jnp.concatenate([o_a, o_b.astype(o_a.dtype), o_m.astype(o_a.dtype)], axis=-1) * jax.nn.silu(gate)
    return x + (y @ w_out).astype(x.dtype)


def odd_layer(x, mem, norm_g, w_in, sinks, mem_norm_g, w_mem_kv, w_out):
    B, S, _ = x.shape
    h = rms_norm(x, norm_g) @ w_in
    qkv_c, qkv_d, q_mem, gate = split_cols(h, ODD_SPLITS)
    q_c, k_c, v_c = split_cols(qkv_c, (SWA_HEADS * SWA_DH, SWA_KV_HEADS * SWA_DH, SWA_KV_HEADS * SWA_DH))
    rot_c = SWA_DH // ROT_FRAC
    G = SWA_HEADS // SWA_KV_HEADS
    q_c = apply_rope(q_c.reshape(B, S, SWA_HEADS, SWA_DH), rot_c)
    k_c = apply_rope(k_c.reshape(B, S, SWA_KV_HEADS, SWA_DH), rot_c)
    v_c = v_c.reshape(B, S, SWA_KV_HEADS, SWA_DH)
    qg = q_c.reshape(B, S, SWA_KV_HEADS, G, SWA_DH).transpose(0, 2, 3, 1, 4)
    o_c, lse_c = banded_attention(qg, k_c.transpose(0, 2, 1, 3), v_c.transpose(0, 2, 1, 3),
                                  SWA_WINDOW - 1, SWA_DH ** -0.5)
    o_c = o_c.transpose(0, 3, 1, 2, 4).reshape(B, S, SWA_HEADS, SWA_DH)
    lse_c = lse_c.transpose(0, 3, 1, 2).reshape(B, S, SWA_HEADS)
    sink_w = jax.nn.sigmoid(lse_c - sinks.astype(jnp.float32))
    o_c = (o_c.astype(jnp.float32) * sink_w[..., None]).astype(x.dtype).reshape(B, S, SWA_HEADS * SWA_DH)
    qkv_d = qkv_d.reshape(B, S, 3, DIL_HEADS, DIL_DH)
    rot_d = DIL_DH // ROT_FRAC
    o_d = dilated_attention(apply_rope(qkv_d[:, :, 0], rot_d), apply_rope(qkv_d[:, :, 1], rot_d),
                            qkv_d[:, :, 2]).reshape(B, S, DIL_HEADS * DIL_DH)
    o_m = memory_attention(q_mem, mem, mem_norm_g, w_mem_kv)
    y = jnp.concatenate([o_c, o_d.astype(o_c.dtype), o_m.astype(o_c.dtype)], axis=-1) * jax.nn.silu(gate).astype(o_c.dtype)
    return x + (y @ w_out).astype(x.dtype)


def setup_inputs(seed: int = 0) -> dict:
    key = jax.random.key(seed)
    ks = jax.random.split(key, 20)

    def dense(k, shape):
        return jax.random.normal(k, shape, jnp.float32) * shape[-2] ** -0.5

    def gain(k, shape):
        return 1.0 + 0.02 * jax.random.normal(k, shape, jnp.float32)

    return {
        "x": jax.random.normal(ks[0], (BATCH, SEQ, D_MODEL), jnp.float32),
        "mem": jax.random.normal(ks[1], (BATCH, N_MEM, D_MODEL), jnp.float32),
        "ev_norm_g": gain(ks[2], (N_EVEN, D_MODEL)),
        "ev_w_in": dense(ks[3], (N_EVEN, D_MODEL, EVEN_IN)),
        "ev_q_norm_g": gain(ks[4], (N_EVEN, MLA_Q_RANK)),
        "ev_w_uq": dense(ks[5], (N_EVEN, MLA_Q_RANK, MLA_HEADS * (MLA_NOPE + MLA_ROPE))),
        "ev_kv_norm_g": gain(ks[6], (N_EVEN, MLA_KV_RANK)),
        "ev_w_ukv": dense(ks[7], (N_EVEN, MLA_KV_RANK, MLA_HEADS * (MLA_NOPE + MLA_V))),
        "ev_mem_norm_g": gain(ks[8], (N_EVEN, D_MODEL)),
        "ev_w_mem_kv": dense(ks[9], (N_EVEN, D_MODEL, 2 * MEMQ_W)),
        "ev_w_out": dense(ks[10], (N_EVEN, EVEN_VW, D_MODEL)),
        "od_norm_g": gain(ks[11], (N_ODD, D_MODEL)),
        "od_w_in": dense(ks[12], (N_ODD, D_MODEL, ODD_IN)),
        "od_sinks": 0.5 * jax.random.normal(ks[13], (N_ODD, SWA_HEADS), jnp.float32),
        "od_mem_norm_g": gain(ks[14], (N_ODD, D_MODEL)),
        "od_w_mem_kv": dense(ks[15], (N_ODD, D_MODEL, 2 * MEMQ_W)),
        "od_w_out": dense(ks[16], (N_ODD, ODD_VW, D_MODEL)),
        "final_norm_g": gain(ks[17], (D_MODEL,)),
    }


def reference(x, mem, ev_norm_g, ev_w_in, ev_q_norm_g, ev_w_uq, ev_kv_norm_g, ev_w_ukv,
              ev_mem_norm_g, ev_w_mem_kv, ev_w_out, od_norm_g, od_w_in, od_sinks,
              od_mem_norm_g, od_w_mem_kv, od_w_out, final_norm_g):
    for layer in range(DEPTH):
        i = layer // 2
        if layer % 2 == 0:
            x = even_layer(x, mem, ev_norm_g[i], ev_w_in[i], ev_q_norm_g[i], ev_w_uq[i],
                           ev_kv_norm_g[i], ev_w_ukv[i], ev_mem_norm_g[i], ev_w_mem_kv[i], ev_w_out[i])
        else:
            x = odd_layer(x, mem, od_norm_g[i], od_w_in[i], od_sinks[i], od_mem_norm_g[i],
                          od_w_mem_kv[i], od_w_out[i])
    return rms_norm(x, final_norm_g)
```

```python
import functools

import numpy as np
import jax
import jax.numpy as jnp
from jax import lax
from jax.experimental import pallas as pl
from jax.experimental.pallas import tpu as pltpu

D_MODEL = 2048
BATCH = 4
SEQ = 4096
N_TOK = BATCH * SEQ
N_MEM = 256
ROPE_THETA = 500000.0
ROT_FRAC = 4
EPS = 1e-6

MLA_HEADS = 8
MLA_Q_RANK = 512
MLA_KV_RANK = 256
MLA_NOPE = 128
MLA_ROPE = 64
MLA_V = 128
MOBA_HEADS = 8
MOBA_DH = 128
MOBA_BLOCK = 256
MOBA_TOPK = 3
SWA_HEADS = 16
SWA_KV_HEADS = 2
SWA_DH = 64
SWA_WINDOW = 128
DIL_HEADS = 6
DIL_DH = 128
DIL_PATTERNS = ((128, 1), (512, 4), (2048, 16))
MEM_HEADS = 4
MEM_DH = 128
MEMQ_W = MEM_HEADS * MEM_DH
Q_BLOCK = 128

LANES = 128
VMEM_LIMIT = 52 * 1024 * 1024

NEG = -1e30
BF16 = jnp.bfloat16
F32 = jnp.float32


def _cparams(sem):
    return pltpu.CompilerParams(dimension_semantics=sem, vmem_limit_bytes=VMEM_LIMIT)


def _rms(x, g):
    ms = jnp.mean(x * x, axis=-1, keepdims=True)
    return (x * lax.rsqrt(ms + EPS)) * g


def _silu(g):
    return g / (1.0 + jnp.exp(-g))


def _sigmoid(z):
    return 1.0 / (1.0 + jnp.exp(-z))


def _rope_slab(x, cos_t, sin_t, half, period):
    lane = lax.broadcasted_iota(jnp.int32, x.shape, 1)
    up = pltpu.roll(x, LANES - half, 1)
    dn = pltpu.roll(x, half, 1)
    sw = jnp.where((lane & (period - 1)) < half, up, dn)
    return x * cos_t + sw * sin_t


def _qk(q, k):
    return lax.dot_general(q, k, (((1,), (1,)), ((), ())), preferred_element_type=F32)


def _rope_tables(rot_dim, period):
    half = rot_dim // 2
    inv = 1.0 / (ROPE_THETA ** (jnp.arange(0, rot_dim, 2, dtype=F32) / rot_dim))
    ang = jnp.arange(SEQ, dtype=F32)[:, None] * inv[None, :]
    c, s = jnp.cos(ang), jnp.sin(ang)
    d = np.arange(LANES) % period
    idx = d % half
    cos_t = jnp.where(d < rot_dim, c[:, idx], 1.0)
    sin_t = jnp.where(d < half, -s[:, idx], jnp.where(d < rot_dim, s[:, idx], 0.0))
    return cos_t.astype(F32), sin_t.astype(F32)


def _inproj_kernel(groups, ntab, tn, x_ref, g_ref, *rest):
    tabs = rest[:2 * ntab]
    w_ref = rest[2 * ntab]
    o_refs = rest[2 * ntab + 1:2 * ntab + 1 + len(groups)]
    xn_ref = rest[-1]
    j = pl.program_id(1)

    @pl.when(j == 0)
    def _():
        xn_ref[...] = _rms(x_ref[...], g_ref[...]).astype(BF16)

    acc = jnp.dot(xn_ref[...], w_ref[...], preferred_element_type=F32)
    j0 = 0
    for (ncols, kinds), o_ref in zip(groups, o_refs):
        nt = ncols // tn

        @pl.when((j >= j0) & (j < j0 + nt))
        def _(kinds=kinds, o_ref=o_ref):
            for c, kind in enumerate(kinds):
                piece = acc[:, c * LANES:(c + 1) * LANES]
                if kind[0] == "rope":
                    _, ti, half, period, sc = kind
                    piece = _rope_slab(piece, tabs[2 * ti][...], tabs[2 * ti + 1][...], half, period)
                    if sc != 1.0:
                        piece = piece * sc
                elif kind[0] == "scale":
                    piece = piece * kind[1]
                o_ref[:, c * LANES:(c + 1) * LANES] = piece.astype(o_ref.dtype)

        j0 += nt


def _inproj(x, g, w, groups, tables, tm, tn):
    n_tok, d = x.shape
    starts = np.cumsum([0] + [nc // tn for nc, _ in groups])
    ntiles = int(starts[-1])
    assert ntiles * tn == w.shape[1]
    seq_tiles = SEQ // tm

    def out_map(j0, nt):
        return lambda i, j: (i, jnp.clip(j - j0, 0, nt - 1))

    in_specs = [pl.BlockSpec((tm, d), lambda i, j: (i, 0)),
                pl.BlockSpec((1, d), lambda i, j: (0, 0))]
    args = [x, g.reshape(1, d)]
    for t in tables:
        in_specs.append(pl.BlockSpec((tm, LANES), lambda i, j: (i % seq_tiles, 0)))
        args.append(t)
    in_specs.append(pl.BlockSpec((d, tn), lambda i, j: (0, j)))
    args.append(w)
    out_specs, out_shapes = [], []
    for gi, (nc, _) in enumerate(groups):
        out_specs.append(pl.BlockSpec((tm, tn), out_map(int(starts[gi]), nc // tn)))
        out_shapes.append(jax.ShapeDtypeStruct((n_tok, nc), BF16))
    return pl.pallas_call(
        functools.partial(_inproj_kernel, groups, len(tables) // 2, tn),
        out_shape=out_shapes,
        grid=(n_tok // tm, ntiles),
        in_specs=in_specs,
        out_specs=out_specs,
        scratch_shapes=[pltpu.VMEM((tm, d), BF16)],
        compiler_params=_cparams(("parallel", "arbitrary")),
        name="inproj",
    )(*args)


MLA_HW = 256


def _mla_prep_kernel(scale, x_ref, g_ref, wlat_ref, qg_ref, wuq_ref, kvg_ref, wukv_ref,
                     cos_ref, sin_ref, qa_ref, ka_ref, va_ref):
    xn = _rms(x_ref[...], g_ref[...]).astype(BF16)
    lat = jnp.dot(xn, wlat_ref[...], preferred_element_type=F32)
    cq = _rms(lat[:, :MLA_Q_RANK], qg_ref[...]).astype(BF16)
    ckv = _rms(lat[:, MLA_Q_RANK:MLA_Q_RANK + MLA_KV_RANK], kvg_ref[...]).astype(BF16)
    kpe = lat[:, MLA_Q_RANK + MLA_KV_RANK:]
    cos_t, sin_t = cos_ref[...], sin_ref[...]
    kpe = _rope_slab(kpe, cos_t, sin_t, MLA_ROPE // 2, LANES).astype(BF16)
    q = jnp.dot(cq, wuq_ref[...], preferred_element_type=F32)
    kv = jnp.dot(ckv, wukv_ref[...], preferred_element_type=F32)
    for h in range(MLA_HEADS):
        a = h * MLA_HW
        qa_ref[:, a:a + LANES] = (q[:, a:a + LANES] * scale).astype(BF16)
        qr = _rope_slab(q[:, a + LANES:a + 2 * LANES], cos_t, sin_t, MLA_ROPE // 2, LANES)
        qa_ref[:, a + LANES:a + 2 * LANES] = (qr * scale).astype(BF16)
        ka_ref[:, a:a + LANES] = kv[:, h * LANES:(h + 1) * LANES].astype(BF16)
        ka_ref[:, a + LANES:a + 2 * LANES] = kpe
    va_ref[...] = kv[:, MLA_HEADS * MLA_NOPE:].astype(BF16)


def _mla_prep(x, g, wlat, qg, wuq, kvg, wukv, cos_t, sin_t, tm):
    n_tok, d = x.shape
    seq_tiles = SEQ // tm
    full = lambda a: pl.BlockSpec(a.shape, lambda i: (0, 0))
    qg2, kvg2, g2 = qg.reshape(1, -1), kvg.reshape(1, -1), g.reshape(1, d)
    scale = (MLA_NOPE + MLA_ROPE) ** -0.5
    return pl.pallas_call(
        functools.partial(_mla_prep_kernel, scale),
        out_shape=[jax.ShapeDtypeStruct((n_tok, MLA_HEADS * MLA_HW), BF16),
                   jax.ShapeDtypeStruct((n_tok, MLA_HEADS * MLA_HW), BF16),
                   jax.ShapeDtypeStruct((n_tok, MLA_HEADS * MLA_V), BF16)],
        grid=(n_tok // tm,),
        in_specs=[pl.BlockSpec((tm, d), lambda i: (i, 0)), full(g2), full(wlat), full(qg2), full(wuq),
                  full(kvg2), full(wukv),
                  pl.BlockSpec((tm, LANES), lambda i: (i % seq_tiles, 0)),
                  pl.BlockSpec((tm, LANES), lambda i: (i % seq_tiles, 0))],
        out_specs=[pl.BlockSpec((tm, MLA_HEADS * MLA_HW), lambda i: (i, 0)),
                   pl.BlockSpec((tm, MLA_HEADS * MLA_HW), lambda i: (i, 0)),
                   pl.BlockSpec((tm, MLA_HEADS * MLA_V), lambda i: (i, 0))],
        compiler_params=_cparams(("parallel",)),
        name="mla_prep",
    )(x, g2, wlat, qg2, wuq, kvg2, wukv, cos_t, sin_t)


def _online_step(s, v, m, l, acc_ref):
    m_new = jnp.maximum(m, jnp.max(s, axis=1, keepdims=True))
    alpha = jnp.exp(m - m_new)
    p = jnp.exp(s - m_new)
    l_new = alpha * l + jnp.sum(p, axis=1, keepdims=True)
    acc_ref[...] = alpha * acc_ref[...] + jnp.dot(p.astype(BF16), v, preferred_element_type=F32)
    return m_new, l_new


def _mla_kernel(tq, q_ref, k_ref, v_ref, gate_ref, y_ref, acc_ref):
    qi = pl.program_id(2)
    q = q_ref[...]
    acc_ref[...] = jnp.zeros_like(acc_ref)
    m0 = jnp.full((tq, 1), NEG, F32)
    l0 = jnp.zeros((tq, 1), F32)

    off = pl.multiple_of(qi * tq, tq)
    row = lax.broadcasted_iota(jnp.int32, (tq, tq), 0)
    col = lax.broadcasted_iota(jnp.int32, (tq, tq), 1)
    s = jnp.where(col <= row, _qk(q, k_ref[pl.ds(off, tq), :]), NEG)
    m, l = _online_step(s, v_ref[pl.ds(off, tq), :], m0, l0, acc_ref)

    def body(ki, carry):
        o = pl.multiple_of(ki * tq, tq)
        return _online_step(_qk(q, k_ref[pl.ds(o, tq), :]), v_ref[pl.ds(o, tq), :], *carry, acc_ref)

    m, l = lax.fori_loop(0, qi, body, (m, l))
    o = acc_ref[...] / l
    y_ref[...] = (o * _silu(gate_ref[...].astype(F32))).astype(y_ref.dtype)


def _mla_attn(qa, ka, va, gate, tq):
    nq = SEQ // tq
    return pl.pallas_call(
        functools.partial(_mla_kernel, tq),
        out_shape=jax.ShapeDtypeStruct((N_TOK, MLA_HEADS * MLA_V), BF16),
        grid=(BATCH, MLA_HEADS, nq),
        in_specs=[pl.BlockSpec((tq, MLA_HW), lambda b, h, i: (b * nq + i, h)),
                  pl.BlockSpec((SEQ, MLA_HW), lambda b, h, i: (b, h)),
                  pl.BlockSpec((SEQ, MLA_V), lambda b, h, i: (b, h)),
                  pl.BlockSpec((tq, MLA_V), lambda b, h, i: (b * nq + i, h))],
        out_specs=pl.BlockSpec((tq, MLA_V), lambda b, h, i: (b * nq + i, h)),
        scratch_shapes=[pltpu.VMEM((tq, MLA_V), F32)],
        compiler_params=_cparams(("parallel", "parallel", "arbitrary")),
        name="mla_attn",
    )(qa, ka, va, gate)


def _moba_kernel(q_ref, k_ref, v_ref, gate_ref, y_ref, km_ref, acc_ref):
    L = MOBA_BLOCK
    nblk = SEQ // L
    qi = pl.program_id(2)

    @pl.when(qi == 0)
    def _():
        km_ref[...] = jnp.zeros_like(km_ref)
        for j in range(nblk):
            blk = k_ref[j * L:(j + 1) * L, :].astype(F32)
            km_ref[j:j + 1, :] = jnp.sum(blk, axis=0, keepdims=True) * (1.0 / L)

    q = q_ref[...]
    lane_i = lax.broadcasted_iota(jnp.int32, (L, LANES), 1)
    lane_f = lane_i.astype(F32)
    past = lane_i < qi
    g = jnp.where(past, _qk(q, km_ref[...].astype(BF16)), -jnp.inf)
    sel = jnp.zeros((L, LANES), jnp.bool_)
    for _ in range(MOBA_TOPK):
        mx = jnp.max(g, axis=1, keepdims=True)
        first = jnp.min(jnp.where(g == mx, lane_f, float(LANES)), axis=1, keepdims=True)
        pick = lane_f == first
        sel = jnp.logical_or(sel, pick)
        g = jnp.where(pick, -jnp.inf, g)
    sel = jnp.logical_and(sel, past)
    q_aug = jnp.concatenate([q, jnp.where(sel, 0.0, NEG).astype(BF16)], axis=1)

    acc_ref[...] = jnp.zeros_like(acc_ref)
    m0 = jnp.full((L, 1), NEG, F32)
    l0 = jnp.zeros((L, 1), F32)
    off = pl.multiple_of(qi * L, L)
    row = lax.broadcasted_iota(jnp.int32, (L, L), 0)
    col = lax.broadcasted_iota(jnp.int32, (L, L), 1)
    s = jnp.where(col <= row, _qk(q, k_ref[pl.ds(off, L), :]), NEG)
    m, l = _online_step(s, v_ref[pl.ds(off, L), :], m0, l0, acc_ref)

    def body(j, carry):
        o = pl.multiple_of(j * L, L)
        k_aug = jnp.concatenate([k_ref[pl.ds(o, L), :], jnp.where(lane_i == j, 1.0, 0.0).astype(BF16)], axis=1)
        return _online_step(_qk(q_aug, k_aug), v_ref[pl.ds(o, L), :], *carry, acc_ref)

    m, l = lax.fori_loop(0, qi, body, (m, l))
    o = acc_ref[...] / l
    y_ref[...] = (o * _silu(gate_ref[...].astype(F32))).astype(y_ref.dtype)


def _moba_attn(qb, kb, vb, gate):
    L = MOBA_BLOCK
    nq = SEQ // L
    return pl.pallas_call(
        _moba_kernel,
        out_shape=jax.ShapeDtypeStruct((N_TOK, MOBA_HEADS * MOBA_DH), BF16),
        grid=(BATCH, MOBA_HEADS, nq),
        in_specs=[pl.BlockSpec((L, MOBA_DH), lambda b, h, i: (b * nq + i, h)),
                  pl.BlockSpec((SEQ, MOBA_DH), lambda b, h, i: (b, h)),
                  pl.BlockSpec((SEQ, MOBA_DH), lambda b, h, i: (b, h)),
                  pl.BlockSpec((L, MOBA_DH), lambda b, h, i: (b * nq + i, h))],
        out_specs=pl.BlockSpec((L, MOBA_DH), lambda b, h, i: (b * nq + i, h)),
        scratch_shapes=[pltpu.VMEM((LANES, MOBA_DH), F32), pltpu.VMEM((L, MOBA_DH), F32)],
        compiler_params=_cparams(("parallel", "parallel", "arbitrary")),
        name="moba_attn",
    )(qb, kb, vb, gate)


def _mem_prep_kernel(mem_ref, g_ref, w_ref, kv_ref):
    mn = _rms(mem_ref[...], g_ref[...]).astype(BF16)
    kv_ref[...] = jnp.dot(mn, w_ref[...], preferred_element_type=F32).astype(BF16)


def _mem_prep(mem2d, g, w):
    g2 = g.reshape(1, -1)
    return pl.pallas_call(
        _mem_prep_kernel,
        out_shape=jax.ShapeDtypeStruct((BATCH * N_MEM, 2 * MEMQ_W), BF16),
        grid=(BATCH,),
        in_specs=[pl.BlockSpec((N_MEM, D_MODEL), lambda b: (b, 0)),
                  pl.BlockSpec(g2.shape, lambda b: (0, 0)),
                  pl.BlockSpec(w.shape, lambda b: (0, 0))],
        out_specs=pl.BlockSpec((N_MEM, 2 * MEMQ_W), lambda b: (b, 0)),
        compiler_params=_cparams(("parallel",)),
        name="mem_prep",
    )(mem2d, g2, w)


def _mem_attn_kernel(q_ref, kv_ref, gate_ref, y_ref):
    for h in range(MEM_HEADS):
        a = h * MEM_DH
        s = _qk(q_ref[:, a:a + MEM_DH], kv_ref[:, a:a + MEM_DH])
        p = jnp.exp(s - jnp.max(s, axis=1, keepdims=True))
        l = jnp.sum(p, axis=1, keepdims=True)
        o = jnp.dot(p.astype(BF16), kv_ref[:, MEMQ_W + a:MEMQ_W + a + MEM_DH],
                    preferred_element_type=F32) / l
        y_ref[:, a:a + MEM_DH] = (o * _silu(gate_ref[:, a:a + MEM_DH].astype(F32))).astype(y_ref.dtype)


def _mem_attn(qm, kvm, gate, tq):
    nq = SEQ // tq
    return pl.pallas_call(
        _mem_attn_kernel,
        out_shape=jax.ShapeDtypeStruct((N_TOK, MEMQ_W), BF16),
        grid=(BATCH, nq),
        in_specs=[pl.BlockSpec((tq, MEMQ_W), lambda b, i: (b * nq + i, 0)),
                  pl.BlockSpec((N_MEM, 2 * MEMQ_W), lambda b, i: (b, 0)),
                  pl.BlockSpec((tq, MEMQ_W), lambda b, i: (b * nq + i, 0))],
        out_specs=pl.BlockSpec((tq, MEMQ_W), lambda b, i: (b * nq + i, 0)),
        compiler_params=_cparams(("parallel", "parallel")),
        name="mem_attn",
    )(qm, kvm, gate)


def _band_bias(n, lo_off, hi_off):
    i = lax.broadcasted_iota(jnp.int32, (Q_BLOCK, 2 * Q_BLOCK), 0)
    c = lax.broadcasted_iota(jnp.int32, (Q_BLOCK, 2 * Q_BLOCK), 1)
    vis = (c - i >= lo_off) & (c - i <= hi_off) & ((c >= Q_BLOCK) | (n > 0))
    return jnp.where(vis, 0.0, NEG).astype(F32)


def _swa_kernel(q_ref, kvp_ref, kvo_ref, gate_ref, sink_ref, y_ref):
    QB = Q_BLOCK
    G = SWA_HEADS // SWA_KV_HEADS
    n = pl.program_id(1)
    bias = _band_bias(n, QB - (SWA_WINDOW - 1), QB)
    bias4 = jnp.concatenate([bias] * (G // 2), axis=0)
    k2 = jnp.concatenate([kvp_ref[:, :LANES], kvo_ref[:, :LANES]], axis=0).astype(F32)
    v2 = jnp.concatenate([kvp_ref[:, LANES:], kvo_ref[:, LANES:]], axis=0).astype(F32)
    k2r = pltpu.roll(k2, SWA_DH, 1)
    v2r = pltpu.roll(v2, SWA_DH, 1)
    lane = lax.broadcasted_iota(jnp.int32, (2 * QB, LANES), 1)
    lo = lane < SWA_DH
    lane_q = lax.broadcasted_iota(jnp.int32, (QB, LANES), 1)
    lo_q = lane_q < SWA_DH
    for kv in range(SWA_KV_HEADS):
        ka, kb_ = (k2, k2r) if kv == 0 else (k2r, k2)
        va, vb_ = (v2, v2r) if kv == 0 else (v2r, v2)
        k_lo = jnp.where(lo, ka, 0.0).astype(BF16)
        k_hi = jnp.where(lo, 0.0, kb_).astype(BF16)
        v_lo = jnp.where(lo, va, 0.0).astype(BF16)
        v_hi = jnp.where(lo, 0.0, vb_).astype(BF16)
        base = kv * (G // 2)
        q4 = jnp.concatenate([q_ref[:, (base + p) * LANES:(base + p + 1) * LANES] for p in range(G // 2)],
                             axis=0)
        s_lo = _qk(q4, k_lo) + bias4
        s_hi = _qk(q4, k_hi) + bias4
        m_lo = jnp.max(s_lo, axis=1, keepdims=True)
        m_hi = jnp.max(s_hi, axis=1, keepdims=True)
        p_lo = jnp.exp(s_lo - m_lo)
        p_hi = jnp.exp(s_hi - m_hi)
        l_lo = jnp.sum(p_lo, axis=1, keepdims=True)
        l_hi = jnp.sum(p_hi, axis=1, keepdims=True)
        o = (jnp.dot(p_lo.astype(BF16), v_lo, preferred_element_type=F32)
             + jnp.dot(p_hi.astype(BF16), v_hi, preferred_element_type=F32))
        lse_lo = m_lo + jnp.log(l_lo)
        lse_hi = m_hi + jnp.log(l_hi)
        for p in range(G // 2):
            r = slice(p * QB, (p + 1) * QB)
            c = slice((base + p) * LANES, (base + p + 1) * LANES)
            lse = jnp.where(lo_q, lse_lo[r], lse_hi[r])
            l = jnp.where(lo_q, l_lo[r], l_hi[r])
            w = _sigmoid(lse - sink_ref[:, c])
            y = (o[r] / l) * w * _silu(gate_ref[:, c].astype(F32))
            y_ref[:, c] = y.astype(y_ref.dtype)


def _swa_attn(qc, kvc, gate, sink_row):
    nb = SEQ // Q_BLOCK
    w = SWA_HEADS * SWA_DH
    return pl.pallas_call(
        _swa_kernel,
        out_shape=jax.ShapeDtypeStruct((N_TOK, w), BF16),
        grid=(BATCH, nb),
        in_specs=[pl.BlockSpec((Q_BLOCK, w), lambda b, n: (b * nb + n, 0)),
                  pl.BlockSpec((Q_BLOCK, 2 * LANES), lambda b, n: (jnp.maximum(b * nb + n - 1, 0), 0)),
                  pl.BlockSpec((Q_BLOCK, 2 * LANES), lambda b, n: (b * nb + n, 0)),
                  pl.BlockSpec((Q_BLOCK, w), lambda b, n: (b * nb + n, 0)),
                  pl.BlockSpec((1, w), lambda b, n: (0, 0))],
        out_specs=pl.BlockSpec((Q_BLOCK, w), lambda b, n: (b * nb + n, 0)),
        compiler_params=_cparams(("parallel", "parallel")),
        name="swa_attn",
    )(qc, kvc, kvc, gate, sink_row)


def _dil_kernel(q_ref, kp_ref, ko_ref, vp_ref, vo_ref, o_ref, lse_ref):
    QB = Q_BLOCK
    n = pl.program_id(2)
    bias = _band_bias(n, 0, QB)
    lane = lax.broadcasted_iota(jnp.int32, (QB, LANES), 1)
    lse_all = jnp.zeros((QB, LANES), F32)
    for h in range(DIL_HEADS):
        c = slice(h * DIL_DH, (h + 1) * DIL_DH)
        k = jnp.concatenate([kp_ref[:, c], ko_ref[:, c]], axis=0)
        v = jnp.concatenate([vp_ref[:, c], vo_ref[:, c]], axis=0)
        s = _qk(q_ref[:, c], k) + bias
        m = jnp.max(s, axis=1, keepdims=True)
        p = jnp.exp(s - m)
        l = jnp.sum(p, axis=1, keepdims=True)
        o_ref[:, c] = jnp.dot(p.astype(BF16), v, preferred_element_type=F32) / l
        lse_all = jnp.where(lane == h, m + jnp.log(l), lse_all)
    lse_ref[...] = lse_all


def _dil_attn(qd, kd, vd, dil):
    w = DIL_HEADS * DIL_DH
    L = SEQ // dil
    nb = L // Q_BLOCK
    view = lambda a: a.reshape(BATCH * L, dil * w)
    cur = lambda b, r, n: (b * nb + n, r)
    prev = lambda b, r, n: (jnp.maximum(b * nb + n - 1, 0), r)
    blk = (Q_BLOCK, w)
    o, lse = pl.pallas_call(
        _dil_kernel,
        out_shape=[jax.ShapeDtypeStruct((BATCH * L, dil * w), F32),
                   jax.ShapeDtypeStruct((BATCH * L, dil * LANES), F32)],
        grid=(BATCH, dil, nb),
        in_specs=[pl.BlockSpec(blk, cur), pl.BlockSpec(blk, prev), pl.BlockSpec(blk, cur),
                  pl.BlockSpec(blk, prev), pl.BlockSpec(blk, cur)],
        out_specs=[pl.BlockSpec(blk, cur), pl.BlockSpec((Q_BLOCK, LANES), cur)],
        compiler_params=_cparams(("parallel", "parallel", "parallel")),
        name=f"dil_attn_d{dil}",
    )(view(qd), view(kd), view(kd), view(vd), view(vd))
    return o.reshape(N_TOK, w), lse.reshape(N_TOK, LANES)


def _dil_combine_kernel(o1, o2, o3, l1, l2, l3, gate_ref, y_ref):
    ls = [l1[...], l2[...], l3[...]]
    mx = jnp.maximum(jnp.maximum(ls[0], ls[1]), ls[2])
    es = [jnp.exp(x - mx) for x in ls]
    den = es[0] + es[1] + es[2]
    ws = [e / den for e in es]
    lane = lax.broadcasted_iota(jnp.int32, ws[0].shape, 1)
    for h in range(DIL_HEADS):
        c = slice(h * DIL_DH, (h + 1) * DIL_DH)
        wh = [jnp.sum(jnp.where(lane == h, w, 0.0), axis=1, keepdims=True) for w in ws]
        o = wh[0] * o1[:, c] + wh[1] * o2[:, c] + wh[2] * o3[:, c]
        y_ref[:, c] = (o * _silu(gate_ref[:, c].astype(F32))).astype(y_ref.dtype)


def _dil_combine(outs, lses, gate, tm):
    w = DIL_HEADS * DIL_DH
    ob = pl.BlockSpec((tm, w), lambda i: (i, 0))
    lb = pl.BlockSpec((tm, LANES), lambda i: (i, 0))
    return pl.pallas_call(
        _dil_combine_kernel,
        out_shape=jax.ShapeDtypeStruct((N_TOK, w), BF16),
        grid=(N_TOK // tm,),
        in_specs=[ob, ob, ob, lb, lb, lb, ob],
        out_specs=ob,
        compiler_params=_cparams(("parallel",)),
        name="dil_combine",
    )(*outs, *lses, gate)


def _outproj_kernel(nparts, final, *refs):
    ys, ws = refs[:nparts], refs[nparts:2 * nparts]
    x_ref = refs[2 * nparts]
    o_ref = refs[-1]
    acc = x_ref[...]
    for y, w in zip(ys, ws):
        acc = acc + jnp.dot(y[...], w[...], preferred_element_type=F32)
    if final:
        acc = _rms(acc, refs[2 * nparts + 1][...])
    o_ref[...] = acc


def _outproj(ys, ws, x, final_g, tm):
    n_tok, d = x.shape
    in_specs = [pl.BlockSpec((tm, y.shape[1]), lambda i: (i, 0)) for y in ys]
    in_specs += [pl.BlockSpec(w.shape, lambda i: (0, 0)) for w in ws]
    in_specs.append(pl.BlockSpec((tm, d), lambda i: (i, 0)))
    args = [*ys, *ws, x]
    if final_g is not None:
        in_specs.append(pl.BlockSpec((1, d), lambda i: (0, 0)))
        args.append(final_g.reshape(1, d))
    return pl.pallas_call(
        functools.partial(_outproj_kernel, len(ys), final_g is not None),
        out_shape=jax.ShapeDtypeStruct((n_tok, d), F32),
        grid=(n_tok // tm,),
        in_specs=in_specs,
        out_specs=pl.BlockSpec((tm, d), lambda i: (i, 0)),
        compiler_params=_cparams(("parallel",)),
        name="outproj",
    )(*args)


PROJ_TM = 512
ATTN_TQ = 512


def _even_layer(x, mem2d, norm_g, w_in, q_norm_g, w_uq, kv_norm_g, w_ukv, mem_norm_g, w_mem_kv, w_out):
    n_lat = MLA_Q_RANK + MLA_KV_RANK + MLA_ROPE
    wlat = jnp.pad(w_in[:, :n_lat], ((0, 0), (0, LANES - MLA_ROPE))).astype(BF16)
    wmain = w_in[:, n_lat:].astype(BF16)
    wuq = w_uq.reshape(MLA_Q_RANK, MLA_HEADS, MLA_NOPE + MLA_ROPE)
    wuq = jnp.pad(wuq, ((0, 0), (0, 0), (0, MLA_HW - MLA_NOPE - MLA_ROPE)))
    wuq = wuq.reshape(MLA_Q_RANK, MLA_HEADS * MLA_HW).astype(BF16)
    wukv = w_ukv.reshape(MLA_KV_RANK, MLA_HEADS, MLA_NOPE + MLA_V)
    wukv = jnp.concatenate([wukv[:, :, :MLA_NOPE].reshape(MLA_KV_RANK, -1),
                            wukv[:, :, MLA_NOPE:].reshape(MLA_KV_RANK, -1)], axis=1).astype(BF16)

    cos_a, sin_a = _rope_tables(MLA_ROPE, LANES)
    qa, ka, va = _mla_prep(x, norm_g, wlat, q_norm_g, wuq, kv_norm_g, wukv, cos_a, sin_a, PROJ_TM)

    rot = MOBA_DH // ROT_FRAC
    cos_b, sin_b = _rope_tables(rot, MOBA_DH)
    tn = 512
    nsl = tn // LANES
    sc = MOBA_DH ** -0.5
    groups = [
        (MOBA_HEADS * MOBA_DH, [("rope", 0, rot // 2, MOBA_DH, sc)] * nsl),
        (MOBA_HEADS * MOBA_DH, [("rope", 0, rot // 2, MOBA_DH, 1.0)] * nsl),
        (MOBA_HEADS * MOBA_DH, [("plain",)] * nsl),
        (MEMQ_W, [("scale", MEM_DH ** -0.5)] * nsl),
        (MLA_HEADS * MLA_V, [("plain",)] * nsl),
        (MOBA_HEADS * MOBA_DH, [("plain",)] * nsl),
        (MEMQ_W, [("plain",)] * nsl),
    ]
    qb, kb, vb, qm, gate_a, gate_b, gate_m = _inproj(x, norm_g, wmain, groups, [cos_b, sin_b], PROJ_TM, tn)

    y_a = _mla_attn(qa, ka, va, gate_a, ATTN_TQ)
    y_b = _moba_attn(qb, kb, vb, gate_b)
    kvm = _mem_prep(mem2d, mem_norm_g, w_mem_kv.astype(BF16))
    y_m = _mem_attn(qm, kvm, gate_m, ATTN_TQ)

    wo = w_out.astype(BF16)
    a, b = MLA_HEADS * MLA_V, MLA_HEADS * MLA_V + MOBA_HEADS * MOBA_DH
    return [y_a, y_b, y_m], [wo[:a], wo[a:b], wo[b:]]


def _odd_layer(x, mem2d, norm_g, w_in, sinks, mem_norm_g, w_mem_kv, w_out):
    rot_c = SWA_DH // ROT_FRAC
    rot_d = DIL_DH // ROT_FRAC
    cos_c, sin_c = _rope_tables(rot_c, SWA_DH)
    cos_d, sin_d = _rope_tables(rot_d, DIL_DH)
    tn = 256
    nsl = tn // LANES
    wq = SWA_HEADS * SWA_DH
    wd = DIL_HEADS * DIL_DH
    rc = lambda s: ("rope", 0, rot_c // 2, SWA_DH, s)
    rd = lambda s: ("rope", 1, rot_d // 2, DIL_DH, s)
    groups = [
        (wq, [rc(SWA_DH ** -0.5)] * nsl),
        (2 * SWA_KV_HEADS * SWA_DH, [rc(1.0), ("plain",)]),
        (wd, [rd(DIL_DH ** -0.5)] * nsl),
        (wd, [rd(1.0)] * nsl),
        (wd, [("plain",)] * nsl),
        (MEMQ_W, [("scale", MEM_DH ** -0.5)] * nsl),
        (wq, [("plain",)] * nsl),
        (wd, [("plain",)] * nsl),
        (MEMQ_W, [("plain",)] * nsl),
    ]
    qc, kvc, qd, kd, vd, qm, gate_c, gate_d, gate_m = _inproj(
        x, norm_g, w_in.astype(BF16), groups, [cos_c, sin_c, cos_d, sin_d], PROJ_TM, tn)

    sink_row = jnp.repeat(sinks.astype(F32), SWA_DH).reshape(1, wq)
    y_c = _swa_attn(qc, kvc, gate_c, sink_row)
    outs, lses = zip(*[_dil_attn(qd, kd, vd, dil) for _, dil in DIL_PATTERNS])
    y_d = _dil_combine(outs, lses, gate_d, PROJ_TM)
    kvm = _mem_prep(mem2d, mem_norm_g, w_mem_kv.astype(BF16))
    y_m = _mem_attn(qm, kvm, gate_m, ATTN_TQ)

    wo = w_out.astype(BF16)
    return [y_c, y_d, y_m], [wo[:wq], wo[wq:wq + wd], wo[wq + wd:]]


def kernel(x, mem, ev_norm_g, ev_w_in, ev_q_norm_g, ev_w_uq, ev_kv_norm_g, ev_w_ukv, ev_mem_norm_g,
           ev_w_mem_kv, ev_w_out, od_norm_g, od_w_in, od_sinks, od_mem_norm_g, od_w_mem_kv, od_w_out,
           final_norm_g):
    x2 = x.reshape(N_TOK, D_MODEL)
    mem2d = mem.reshape(BATCH * N_MEM, D_MODEL)
    ys, ws = _even_layer(x2, mem2d, ev_norm_g[0], ev_w_in[0], ev_q_norm_g[0], ev_w_uq[0], ev_kv_norm_g[0],
                         ev_w_ukv[0], ev_mem_norm_g[0], ev_w_mem_kv[0], ev_w_out[0])
    x2 = _outproj(ys, ws, x2, None, PROJ_TM)
    ys, ws = _odd_layer(x2, mem2d, od_norm_g[0], od_w_in[0], od_sinks[0], od_mem_norm_g[0],
                        od_w_mem_kv[0], od_w_out[0])
    x2 = _outproj(ys, ws, x2, final_norm_g, PROJ_TM)
    return x2.reshape(BATCH, SEQ, D_MODEL)
```

```python
import functools

import numpy as np
import jax
import jax.numpy as jnp
from jax import lax
from jax.experimental import pallas as pl
from jax.experimental.pallas import tpu as pltpu

D_MODEL = 2048
BATCH = 4
SEQ = 4096
N_TOK = BATCH * SEQ
N_MEM = 256
ROPE_THETA = 500000.0
ROT_FRAC = 4
EPS = 1e-6

MLA_HEADS = 8
MLA_Q_RANK = 512
MLA_KV_RANK = 256
MLA_NOPE = 128
MLA_ROPE = 64
MLA_V = 128
MOBA_HEADS = 8
MOBA_DH = 128
MOBA_BLOCK = 256
MOBA_TOPK = 3
SWA_HEADS = 16
SWA_KV_HEADS = 2
SWA_DH = 64
SWA_WINDOW = 128
DIL_HEADS = 6
DIL_DH = 128
DIL_PATTERNS = ((128, 1), (512, 4), (2048, 16))
MEM_HEADS = 4
MEM_DH = 128
MEMQ_W = MEM_HEADS * MEM_DH
Q_BLOCK = 128

LANES = 128
VMEM_LIMIT = 52 * 1024 * 1024

NEG = -1e30
BF16 = jnp.bfloat16
F32 = jnp.float32


def _cparams(sem):
    return pltpu.CompilerParams(dimension_semantics=sem, vmem_limit_bytes=VMEM_LIMIT)


def _rms(x, g):
    ms = jnp.mean(x * x, axis=-1, keepdims=True)
    return (x * lax.rsqrt(ms + EPS)) * g


def _silu(g):
    return g / (1.0 + jnp.exp(-g))


def _sigmoid(z):
    return 1.0 / (1.0 + jnp.exp(-z))


def _rope_slab(x, cos_t, sin_t, half, period):
    lane = lax.broadcasted_iota(jnp.int32, x.shape, 1)
    up = pltpu.roll(x, LANES - half, 1)
    dn = pltpu.roll(x, half, 1)
    sw = jnp.where((lane & (period - 1)) < half, up, dn)
    return x * cos_t + sw * sin_t


def _qk(q, k):
    return lax.dot_general(q, k, (((1,), (1,)), ((), ())), preferred_element_type=F32)


def _rope_tables(rot_dim, period):
    half = rot_dim // 2
    inv = 1.0 / (ROPE_THETA ** (jnp.arange(0, rot_dim, 2, dtype=F32) / rot_dim))
    ang = jnp.arange(SEQ, dtype=F32)[:, None] * inv[None, :]
    c, s = jnp.cos(ang), jnp.sin(ang)
    d = np.arange(LANES) % period
    idx = d % half
    cos_t = jnp.where(d < rot_dim, c[:, idx], 1.0)
    sin_t = jnp.where(d < half, -s[:, idx], jnp.where(d < rot_dim, s[:, idx], 0.0))
    return cos_t.astype(F32), sin_t.astype(F32)


def _inproj_kernel(groups, ntab, tn, x_ref, g_ref, *rest):
    tabs = rest[:2 * ntab]
    w_ref = rest[2 * ntab]
    o_refs = rest[2 * ntab + 1:2 * ntab + 1 + len(groups)]
    xn_ref = rest[-1]
    j = pl.program_id(1)

    @pl.when(j == 0)
    def _():
        xn_ref[...] = _rms(x_ref[...], g_ref[...]).astype(BF16)

    acc = jnp.dot(xn_ref[...], w_ref[...], preferred_element_type=F32)
    j0 = 0
    for (ncols, kinds), o_ref in zip(groups, o_refs):
        nt = ncols // tn

        @pl.when((j >= j0) & (j < j0 + nt))
        def _(kinds=kinds, o_ref=o_ref):
            for c, kind in enumerate(kinds):
                piece = acc[:, c * LANES:(c + 1) * LANES]
                if kind[0] == "vT":
                    tb = kind[1]
                    for u in range(piece.shape[0] // tb):
                        o_ref[c, u] = piece[u * tb:(u + 1) * tb, :].T.astype(o_ref.dtype)
                    continue
                if kind[0] == "rope":
                    _, ti, half, period, sc = kind
                    piece = _rope_slab(piece, tabs[2 * ti][...], tabs[2 * ti + 1][...], half, period)
                    if sc != 1.0:
                        piece = piece * sc
                elif kind[0] == "scale":
                    piece = piece * kind[1]
                o_ref[:, c * LANES:(c + 1) * LANES] = piece.astype(o_ref.dtype)

        j0 += nt


def _inproj(x, g, w, groups, tables, tm, tn):
    n_tok, d = x.shape
    starts = np.cumsum([0] + [nc // tn for nc, _ in groups])
    ntiles = int(starts[-1])
    assert ntiles * tn == w.shape[1]
    seq_tiles = SEQ // tm

    def out_map(j0, nt):
        return lambda i, j: (i, jnp.clip(j - j0, 0, nt - 1))

    in_specs = [pl.BlockSpec((tm, d), lambda i, j: (i, 0)),
                pl.BlockSpec((1, d), lambda i, j: (0, 0))]
    args = [x, g.reshape(1, d)]
    for t in tables:
        in_specs.append(pl.BlockSpec((tm, LANES), lambda i, j: (i % seq_tiles, 0)))
        args.append(t)
    in_specs.append(pl.BlockSpec((d, tn), lambda i, j: (0, j)))
    args.append(w)
    out_specs, out_shapes = [], []
    for gi, (nc, kinds) in enumerate(groups):
        omap = out_map(int(starts[gi]), nc // tn)
        if kinds[0][0] == "vT":
            tb = kinds[0][1]
            out_specs.append(pl.BlockSpec((tn // LANES, tm // tb, LANES, tb),
                                          lambda i, j, omap=omap: (omap(i, j)[1], i, 0, 0)))
            out_shapes.append(jax.ShapeDtypeStruct((nc // LANES, n_tok // tb, LANES, tb), BF16))
        else:
            out_specs.append(pl.BlockSpec((tm, tn), omap))
            out_shapes.append(jax.ShapeDtypeStruct((n_tok, nc), BF16))
    return pl.pallas_call(
        functools.partial(_inproj_kernel, groups, len(tables) // 2, tn),
        out_shape=out_shapes,
        grid=(n_tok // tm, ntiles),
        in_specs=in_specs,
        out_specs=out_specs,
        scratch_shapes=[pltpu.VMEM((tm, d), BF16)],
        compiler_params=_cparams(("parallel", "arbitrary")),
        name="inproj",
    )(*args)


MLA_HW = 256


def _mla_prep_kernel(scale, x_ref, g_ref, wlat_ref, qg_ref, wuq_ref, kvg_ref, wukv_ref,
                     cos_ref, sin_ref, qa_ref, ka_ref, va_ref):
    xn = _rms(x_ref[...], g_ref[...]).astype(BF16)
    lat = jnp.dot(xn, wlat_ref[...], preferred_element_type=F32)
    cq = _rms(lat[:, :MLA_Q_RANK], qg_ref[...]).astype(BF16)
    ckv = _rms(lat[:, MLA_Q_RANK:MLA_Q_RANK + MLA_KV_RANK], kvg_ref[...]).astype(BF16)
    kpe = lat[:, MLA_Q_RANK + MLA_KV_RANK:]
    cos_t, sin_t = cos_ref[...], sin_ref[...]
    kpe = _rope_slab(kpe, cos_t, sin_t, MLA_ROPE // 2, LANES).astype(BF16)
    q = jnp.dot(cq, wuq_ref[...], preferred_element_type=F32)
    kv = jnp.dot(ckv, wukv_ref[...], preferred_element_type=F32)
    for h in range(MLA_HEADS):
        a = h * MLA_HW
        qa_ref[:, a:a + LANES] = (q[:, a:a + LANES] * scale).astype(BF16)
        qr = _rope_slab(q[:, a + LANES:a + 2 * LANES], cos_t, sin_t, MLA_ROPE // 2, LANES)
        qa_ref[:, a + LANES:a + 2 * LANES] = (qr * scale).astype(BF16)
        ka_ref[:, a:a + LANES] = kv[:, h * LANES:(h + 1) * LANES].astype(BF16)
        ka_ref[:, a + LANES:a + 2 * LANES] = kpe
        v0 = MLA_HEADS * MLA_NOPE + h * MLA_V
        va_ref[h, 0] = kv[:, v0:v0 + MLA_V].T.astype(BF16)


def _mla_prep(x, g, wlat, qg, wuq, kvg, wukv, cos_t, sin_t, tm):
    n_tok, d = x.shape
    seq_tiles = SEQ // tm
    full = lambda a: pl.BlockSpec(a.shape, lambda i: (0, 0))
    qg2, kvg2, g2 = qg.reshape(1, -1), kvg.reshape(1, -1), g.reshape(1, d)
    scale = (MLA_NOPE + MLA_ROPE) ** -0.5
    return pl.pallas_call(
        functools.partial(_mla_prep_kernel, scale),
        out_shape=[jax.ShapeDtypeStruct((n_tok, MLA_HEADS * MLA_HW), BF16),
                   jax.ShapeDtypeStruct((n_tok, MLA_HEADS * MLA_HW), BF16),
                   jax.ShapeDtypeStruct((MLA_HEADS, n_tok // tm, MLA_V, tm), BF16)],
        grid=(n_tok // tm,),
        in_specs=[pl.BlockSpec((tm, d), lambda i: (i, 0)), full(g2), full(wlat), full(qg2), full(wuq),
                  full(kvg2), full(wukv),
                  pl.BlockSpec((tm, LANES), lambda i: (i % seq_tiles, 0)),
                  pl.BlockSpec((tm, LANES), lambda i: (i % seq_tiles, 0))],
        out_specs=[pl.BlockSpec((tm, MLA_HEADS * MLA_HW), lambda i: (i, 0)),
                   pl.BlockSpec((tm, MLA_HEADS * MLA_HW), lambda i: (i, 0)),
                   pl.BlockSpec((MLA_HEADS, 1, MLA_V, tm), lambda i: (0, i, 0, 0))],
        compiler_params=_cparams(("parallel",)),
        name="mla_prep",
    )(x, g2, wlat, qg2, wuq, kvg2, wukv, cos_t, sin_t)


def _flash_step_t(s, vt, m, l, acc_ref):
    m_new = jnp.maximum(m, jnp.max(s, axis=0, keepdims=True))
    alpha = jnp.exp(m - m_new)
    p = jnp.exp(s - m_new)
    l_new = alpha * l + jnp.sum(p, axis=0, keepdims=True)
    acc_ref[...] = alpha * acc_ref[...] + jnp.dot(vt, p.astype(BF16), preferred_element_type=F32)
    return m_new, l_new


def _flash_finish(acc_ref, l, gate, y_dtype):
    o = (acc_ref[...] / l).T
    return (o * _silu(gate.astype(F32))).astype(y_dtype)


def _flash_causal_t(nh, qi, tq, diag_scores, past_scores, vt_fn, s_ref, acc_ref):
    acc_ref[...] = jnp.zeros_like(acc_ref)
    m0 = jnp.full((1, tq), NEG, F32)
    l0 = jnp.zeros((1, tq), F32)
    for g in range(nh):
        s_ref[g] = diag_scores(g)

    def consume(j, carry):
        out = []
        for g in range(nh):
            out.extend(_flash_step_t(s_ref[g], vt_fn(g, j), carry[2 * g], carry[2 * g + 1], acc_ref.at[g]))
        return tuple(out)

    def body(j, carry):
        nxt = [past_scores(g, j) for g in range(nh)]
        carry = consume(jnp.where(j == 0, qi, j - 1), carry)
        for g in range(nh):
            s_ref[g] = nxt[g]
        return carry

    carry = lax.fori_loop(0, qi, body, (m0, l0) * nh)
    carry = consume(jnp.maximum(qi - 1, 0), carry)
    return [carry[2 * g + 1] for g in range(nh)]


def _mla_kernel(tq, nh, q_ref, k_ref, vt_ref, gate_ref, y_ref, s_ref, acc_ref):
    qi = pl.program_id(2)
    qs = [q_ref[:, g * MLA_HW:(g + 1) * MLA_HW] for g in range(nh)]
    key = lax.broadcasted_iota(jnp.int32, (tq, tq), 0)
    qry = lax.broadcasted_iota(jnp.int32, (tq, tq), 1)

    def scores(g, j):
        off = pl.multiple_of(j * tq, tq)
        return _qk(k_ref[pl.ds(off, tq), g * MLA_HW:(g + 1) * MLA_HW], qs[g])

    ls = _flash_causal_t(nh, qi, tq,
                         lambda g: jnp.where(key <= qry, scores(g, qi), NEG),
                         scores, lambda g, j: vt_ref[g, j], s_ref, acc_ref)
    for g in range(nh):
        c = slice(g * MLA_V, (g + 1) * MLA_V)
        y_ref[:, c] = _flash_finish(acc_ref.at[g], ls[g], gate_ref[:, c], y_ref.dtype)


def _mla_attn(qa, ka, vat, gate, tq, nh):
    nq = SEQ // tq
    return pl.pallas_call(
        functools.partial(_mla_kernel, tq, nh),
        out_shape=jax.ShapeDtypeStruct((N_TOK, MLA_HEADS * MLA_V), BF16),
        grid=(BATCH, MLA_HEADS // nh, nq),
        in_specs=[pl.BlockSpec((tq, nh * MLA_HW), lambda b, h, i: (b * nq + i, h)),
                  pl.BlockSpec((SEQ, nh * MLA_HW), lambda b, h, i: (b, h)),
                  pl.BlockSpec((nh, nq, MLA_V, tq), lambda b, h, i: (h, b, 0, 0)),
                  pl.BlockSpec((tq, nh * MLA_V), lambda b, h, i: (b * nq + i, h))],
        out_specs=pl.BlockSpec((tq, nh * MLA_V), lambda b, h, i: (b * nq + i, h)),
        scratch_shapes=[pltpu.VMEM((nh, tq, tq), F32), pltpu.VMEM((nh, MLA_V, tq), F32)],
        compiler_params=_cparams(("parallel", "parallel", "arbitrary")),
        name="mla_attn",
    )(qa, ka, vat, gate)


def _moba_kernel(nh, q_ref, k_ref, vt_ref, gate_ref, y_ref, km_ref, nb_ref, s_ref, acc_ref):
    L = MOBA_BLOCK
    D = MOBA_DH
    nblk = SEQ // L
    qi = pl.program_id(2)

    @pl.when(qi == 0)
    def _():
        for g in range(nh):
            for j in range(nblk):
                blk = k_ref[j * L:(j + 1) * L, g * D:(g + 1) * D].astype(F32)
                km_ref[g, j:j + 1, :] = jnp.sum(blk, axis=0, keepdims=True) * (1.0 / L)

    qs = [q_ref[:, g * D:(g + 1) * D] for g in range(nh)]
    blk_i = lax.broadcasted_iota(jnp.int32, (nblk, L), 0)
    blk_f = blk_i.astype(F32)
    past = blk_i < qi
    for g in range(nh):
        gt = jnp.where(past, _qk(km_ref[g].astype(BF16), qs[g]), -jnp.inf)
        sel = jnp.zeros((nblk, L), jnp.bool_)
        for _ in range(MOBA_TOPK):
            mx = jnp.max(gt, axis=0, keepdims=True)
            first = jnp.min(jnp.where(gt == mx, blk_f, float(nblk)), axis=0, keepdims=True)
            pick = blk_f == first
            sel = jnp.logical_or(sel, pick)
            gt = jnp.where(pick, -jnp.inf, gt)
        nb_ref[g] = jnp.where(jnp.logical_and(sel, past), 0.0, NEG)

    key = lax.broadcasted_iota(jnp.int32, (L, L), 0)
    qry = lax.broadcasted_iota(jnp.int32, (L, L), 1)

    def scores(g, j):
        off = pl.multiple_of(j * L, L)
        return _qk(k_ref[pl.ds(off, L), g * D:(g + 1) * D], qs[g])

    ls = _flash_causal_t(nh, qi, L,
                         lambda g: jnp.where(key <= qry, scores(g, qi), NEG),
                         lambda g, j: scores(g, j) + nb_ref[g, pl.ds(j, 1), :],
                         lambda g, j: vt_ref[g, j], s_ref, acc_ref)
    for g in range(nh):
        c = slice(g * D, (g + 1) * D)
        y_ref[:, c] = _flash_finish(acc_ref.at[g], ls[g], gate_ref[:, c], y_ref.dtype)


def _moba_attn(qb, kb, vbt, gate, nh):
    L = MOBA_BLOCK
    D = MOBA_DH
    nq = SEQ // L
    return pl.pallas_call(
        functools.partial(_moba_kernel, nh),
        out_shape=jax.ShapeDtypeStruct((N_TOK, MOBA_HEADS * D), BF16),
        grid=(BATCH, MOBA_HEADS // nh, nq),
        in_specs=[pl.BlockSpec((L, nh * D), lambda b, h, i: (b * nq + i, h)),
                  pl.BlockSpec((SEQ, nh * D), lambda b, h, i: (b, h)),
                  pl.BlockSpec((nh, nq, D, L), lambda b, h, i: (h, b, 0, 0)),
                  pl.BlockSpec((L, nh * D), lambda b, h, i: (b * nq + i, h))],
        out_specs=pl.BlockSpec((L, nh * D), lambda b, h, i: (b * nq + i, h)),
        scratch_shapes=[pltpu.VMEM((nh, nq, D), F32), pltpu.VMEM((nh, nq, L), F32),
                        pltpu.VMEM((nh, L, L), F32), pltpu.VMEM((nh, D, L), F32)],
        compiler_params=_cparams(("parallel", "parallel", "arbitrary")),
        name="moba_attn",
    )(qb, kb, vbt, gate)


def _mem_prep_kernel(mem_ref, g_ref, w_ref, kv_ref):
    mn = _rms(mem_ref[...], g_ref[...]).astype(BF16)
    kv_ref[...] = jnp.dot(mn, w_ref[...], preferred_element_type=F32).astype(BF16)


def _mem_prep(mem2d, g, w):
    g2 = g.reshape(1, -1)
    return pl.pallas_call(
        _mem_prep_kernel,
        out_shape=jax.ShapeDtypeStruct((BATCH * N_MEM, 2 * MEMQ_W), BF16),
        grid=(BATCH,),
        in_specs=[pl.BlockSpec((N_MEM, D_MODEL), lambda b: (b, 0)),
                  pl.BlockSpec(g2.shape, lambda b: (0, 0)),
                  pl.BlockSpec(w.shape, lambda b: (0, 0))],
        out_specs=pl.BlockSpec((N_MEM, 2 * MEMQ_W), lambda b: (b, 0)),
        compiler_params=_cparams(("parallel",)),
        name="mem_prep",
    )(mem2d, g2, w)


def _mem_attn_kernel(q_ref, kv_ref, gate_ref, y_ref):
    for h in range(MEM_HEADS):
        a = h * MEM_DH
        s = _qk(q_ref[:, a:a + MEM_DH], kv_ref[:, a:a + MEM_DH])
        p = jnp.exp(s - jnp.max(s, axis=1, keepdims=True))
        l = jnp.sum(p, axis=1, keepdims=True)
        o = jnp.dot(p.astype(BF16), kv_ref[:, MEMQ_W + a:MEMQ_W + a + MEM_DH],
                    preferred_element_type=F32) / l
        y_ref[:, a:a + MEM_DH] = (o * _silu(gate_ref[:, a:a + MEM_DH].astype(F32))).astype(y_ref.dtype)


def _mem_attn(qm, kvm, gate, tq):
    nq = SEQ // tq
    return pl.pallas_call(
        _mem_attn_kernel,
        out_shape=jax.ShapeDtypeStruct((N_TOK, MEMQ_W), BF16),
        grid=(BATCH, nq),
        in_specs=[pl.BlockSpec((tq, MEMQ_W), lambda b, i: (b * nq + i, 0)),
                  pl.BlockSpec((N_MEM, 2 * MEMQ_W), lambda b, i: (b, 0)),
                  pl.BlockSpec((tq, MEMQ_W), lambda b, i: (b * nq + i, 0))],
        out_specs=pl.BlockSpec((tq, MEMQ_W), lambda b, i: (b * nq + i, 0)),
        compiler_params=_cparams(("parallel", "parallel")),
        name="mem_attn",
    )(qm, kvm, gate)


def _band_bias(n, lo_off, hi_off):
    i = lax.broadcasted_iota(jnp.int32, (Q_BLOCK, 2 * Q_BLOCK), 0)
    c = lax.broadcasted_iota(jnp.int32, (Q_BLOCK, 2 * Q_BLOCK), 1)
    vis = (c - i >= lo_off) & (c - i <= hi_off) & ((c >= Q_BLOCK) | (n > 0))
    return jnp.where(vis, 0.0, NEG).astype(F32)


def _swa_kernel(q_ref, kvp_ref, kvo_ref, gate_ref, sink_ref, y_ref):
    QB = Q_BLOCK
    G = SWA_HEADS // SWA_KV_HEADS
    n = pl.program_id(1)
    bias = _band_bias(n, QB - (SWA_WINDOW - 1), QB)
    bias4 = jnp.concatenate([bias] * (G // 2), axis=0)
    k2 = jnp.concatenate([kvp_ref[:, :LANES], kvo_ref[:, :LANES]], axis=0).astype(F32)
    v2 = jnp.concatenate([kvp_ref[:, LANES:], kvo_ref[:, LANES:]], axis=0).astype(F32)
    k2r = pltpu.roll(k2, SWA_DH, 1)
    v2r = pltpu.roll(v2, SWA_DH, 1)
    lane = lax.broadcasted_iota(jnp.int32, (2 * QB, LANES), 1)
    lo = lane < SWA_DH
    lane_q = lax.broadcasted_iota(jnp.int32, (QB, LANES), 1)
    lo_q = lane_q < SWA_DH
    for kv in range(SWA_KV_HEADS):
        ka, kb_ = (k2, k2r) if kv == 0 else (k2r, k2)
        va, vb_ = (v2, v2r) if kv == 0 else (v2r, v2)
        k_lo = jnp.where(lo, ka, 0.0).astype(BF16)
        k_hi = jnp.where(lo, 0.0, kb_).astype(BF16)
        v_lo = jnp.where(lo, va, 0.0).astype(BF16)
        v_hi = jnp.where(lo, 0.0, vb_).astype(BF16)
        base = kv * (G // 2)
        q4 = jnp.concatenate([q_ref[:, (base + p) * LANES:(base + p + 1) * LANES] for p in range(G // 2)],
                             axis=0)
        s_lo = _qk(q4, k_lo) + bias4
        s_hi = _qk(q4, k_hi) + bias4
        m_lo = jnp.max(s_lo, axis=1, keepdims=True)
        m_hi = jnp.max(s_hi, axis=1, keepdims=True)
        p_lo = jnp.exp(s_lo - m_lo)
        p_hi = jnp.exp(s_hi - m_hi)
        l_lo = jnp.sum(p_lo, axis=1, keepdims=True)
        l_hi = jnp.sum(p_hi, axis=1, keepdims=True)
        o = (jnp.dot(p_lo.astype(BF16), v_lo, preferred_element_type=F32)
             + jnp.dot(p_hi.astype(BF16), v_hi, preferred_element_type=F32))
        lse_lo = m_lo + jnp.log(l_lo)
        lse_hi = m_hi + jnp.log(l_hi)
        for p in range(G // 2):
            r = slice(p * QB, (p + 1) * QB)
            c = slice((base + p) * LANES, (base + p + 1) * LANES)
            lse = jnp.where(lo_q, lse_lo[r], lse_hi[r])
            l = jnp.where(lo_q, l_lo[r], l_hi[r])
            w = _sigmoid(lse - sink_ref[:, c])
            y = (o[r] / l) * w * _silu(gate_ref[:, c].astype(F32))
            y_ref[:, c] = y.astype(y_ref.dtype)


def _swa_attn(qc, kvc, gate, sink_row):
    nb = SEQ // Q_BLOCK
    w = SWA_HEADS * SWA_DH
    return pl.pallas_call(
        _swa_kernel,
        out_shape=jax.ShapeDtypeStruct((N_TOK, w), BF16),
        grid=(BATCH, nb),
        in_specs=[pl.BlockSpec((Q_BLOCK, w), lambda b, n: (b * nb + n, 0)),
                  pl.BlockSpec((Q_BLOCK, 2 * LANES), lambda b, n: (jnp.maximum(b * nb + n - 1, 0), 0)),
                  pl.BlockSpec((Q_BLOCK, 2 * LANES), lambda b, n: (b * nb + n, 0)),
                  pl.BlockSpec((Q_BLOCK, w), lambda b, n: (b * nb + n, 0)),
                  pl.BlockSpec((1, w), lambda b, n: (0, 0))],
        out_specs=pl.BlockSpec((Q_BLOCK, w), lambda b, n: (b * nb + n, 0)),
        compiler_params=_cparams(("parallel", "parallel")),
        name="swa_attn",
    )(qc, kvc, kvc, gate, sink_row)


def _dil_kernel(q_ref, kp_ref, ko_ref, vp_ref, vo_ref, o_ref, lse_ref):
    QB = Q_BLOCK
    n = pl.program_id(2)
    bias = _band_bias(n, 0, QB)
    lane = lax.broadcasted_iota(jnp.int32, (QB, LANES), 1)
    lse_all = jnp.zeros((QB, LANES), F32)
    for h in range(DIL_HEADS):
        c = slice(h * DIL_DH, (h + 1) * DIL_DH)
        k = jnp.concatenate([kp_ref[:, c], ko_ref[:, c]], axis=0)
        v = jnp.concatenate([vp_ref[:, c], vo_ref[:, c]], axis=0)
        s = _qk(q_ref[:, c], k) + bias
        m = jnp.max(s, axis=1, keepdims=True)
        p = jnp.exp(s - m)
        l = jnp.sum(p, axis=1, keepdims=True)
        o_ref[:, c] = jnp.dot(p.astype(BF16), v, preferred_element_type=F32) / l
        lse_all = jnp.where(lane == h, m + jnp.log(l), lse_all)
    lse_ref[...] = lse_all


def _dil_attn(qd, kd, vd, dil):
    w = DIL_HEADS * DIL_DH
    L = SEQ // dil
    nb = L // Q_BLOCK
    view = lambda a: a.reshape(BATCH * L, dil * w)
    cur = lambda b, r, n: (b * nb + n, r)
    prev = lambda b, r, n: (jnp.maximum(b * nb + n - 1, 0), r)
    blk = (Q_BLOCK, w)
    o, lse = pl.pallas_call(
        _dil_kernel,
        out_shape=[jax.ShapeDtypeStruct((BATCH * L, dil * w), F32),
                   jax.ShapeDtypeStruct((BATCH * L, dil * LANES), F32)],
        grid=(BATCH, dil, nb),
        in_specs=[pl.BlockSpec(blk, cur), pl.BlockSpec(blk, prev), pl.BlockSpec(blk, cur),
                  pl.BlockSpec(blk, prev), pl.BlockSpec(blk, cur)],
        out_specs=[pl.BlockSpec(blk, cur), pl.BlockSpec((Q_BLOCK, LANES), cur)],
        compiler_params=_cparams(("parallel", "parallel", "parallel")),
        name=f"dil_attn_d{dil}",
    )(view(qd), view(kd), view(kd), view(vd), view(vd))
    return o.reshape(N_TOK, w), lse.reshape(N_TOK, LANES)


def _dil_combine_kernel(o1, o2, o3, l1, l2, l3, gate_ref, y_ref):
    ls = [l1[...], l2[...], l3[...]]
    mx = jnp.maximum(jnp.maximum(ls[0], ls[1]), ls[2])
    es = [jnp.exp(x - mx) for x in ls]
    den = es[0] + es[1] + es[2]
    ws = [e / den for e in es]
    lane = lax.broadcasted_iota(jnp.int32, ws[0].shape, 1)
    for h in range(DIL_HEADS):
        c = slice(h * DIL_DH, (h + 1) * DIL_DH)
        wh = [jnp.sum(jnp.where(lane == h, w, 0.0), axis=1, keepdims=True) for w in ws]
        o = wh[0] * o1[:, c] + wh[1] * o2[:, c] + wh[2] * o3[:, c]
        y_ref[:, c] = (o * _silu(gate_ref[:, c].astype(F32))).astype(y_ref.dtype)


def _dil_combine(outs, lses, gate, tm):
    w = DIL_HEADS * DIL_DH
    ob = pl.BlockSpec((tm, w), lambda i: (i, 0))
    lb = pl.BlockSpec((tm, LANES), lambda i: (i, 0))
    return pl.pallas_call(
        _dil_combine_kernel,
        out_shape=jax.ShapeDtypeStruct((N_TOK, w), BF16),
        grid=(N_TOK // tm,),
        in_specs=[ob, ob, ob, lb, lb, lb, ob],
        out_specs=ob,
        compiler_params=_cparams(("parallel",)),
        name="dil_combine",
    )(*outs, *lses, gate)


def _outproj_kernel(nparts, final, *refs):
    ys, ws = refs[:nparts], refs[nparts:2 * nparts]
    x_ref = refs[2 * nparts]
    o_ref = refs[-1]
    acc = x_ref[...]
    for y, w in zip(ys, ws):
        acc = acc + jnp.dot(y[...], w[...], preferred_element_type=F32)
    if final:
        acc = _rms(acc, refs[2 * nparts + 1][...])
    o_ref[...] = acc


def _outproj(ys, ws, x, final_g, tm):
    n_tok, d = x.shape
    in_specs = [pl.BlockSpec((tm, y.shape[1]), lambda i: (i, 0)) for y in ys]
    in_specs += [pl.BlockSpec(w.shape, lambda i: (0, 0)) for w in ws]
    in_specs.append(pl.BlockSpec((tm, d), lambda i: (i, 0)))
    args = [*ys, *ws, x]
    if final_g is not None:
        in_specs.append(pl.BlockSpec((1, d), lambda i: (0, 0)))
        args.append(final_g.reshape(1, d))
    return pl.pallas_call(
        functools.partial(_outproj_kernel, len(ys), final_g is not None),
        out_shape=jax.ShapeDtypeStruct((n_tok, d), F32),
        grid=(n_tok // tm,),
        in_specs=in_specs,
        out_specs=pl.BlockSpec((tm, d), lambda i: (i, 0)),
        compiler_params=_cparams(("parallel",)),
        name="outproj",
    )(*args)


PROJ_TM = 512
ATTN_TQ = 512
MLA_HEADS_PER_STEP = 2
MOBA_HEADS_PER_STEP = 4


def _even_layer(x, mem2d, norm_g, w_in, q_norm_g, w_uq, kv_norm_g, w_ukv, mem_norm_g, w_mem_kv, w_out):
    n_lat = MLA_Q_RANK + MLA_KV_RANK + MLA_ROPE
    wlat = jnp.pad(w_in[:, :n_lat], ((0, 0), (0, LANES - MLA_ROPE))).astype(BF16)
    wmain = w_in[:, n_lat:].astype(BF16)
    wuq = w_uq.reshape(MLA_Q_RANK, MLA_HEADS, MLA_NOPE + MLA_ROPE)
    wuq = jnp.pad(wuq, ((0, 0), (0, 0), (0, MLA_HW - MLA_NOPE - MLA_ROPE)))
    wuq = wuq.reshape(MLA_Q_RANK, MLA_HEADS * MLA_HW).astype(BF16)
    wukv = w_ukv.reshape(MLA_KV_RANK, MLA_HEADS, MLA_NOPE + MLA_V)
    wukv = jnp.concatenate([wukv[:, :, :MLA_NOPE].reshape(MLA_KV_RANK, -1),
                            wukv[:, :, MLA_NOPE:].reshape(MLA_KV_RANK, -1)], axis=1).astype(BF16)

    cos_a, sin_a = _rope_tables(MLA_ROPE, LANES)
    qa, ka, va = _mla_prep(x, norm_g, wlat, q_norm_g, wuq, kv_norm_g, wukv, cos_a, sin_a, ATTN_TQ)

    rot = MOBA_DH // ROT_FRAC
    cos_b, sin_b = _rope_tables(rot, MOBA_DH)
    tn = 512
    nsl = tn // LANES
    sc = MOBA_DH ** -0.5
    groups = [
        (MOBA_HEADS * MOBA_DH, [("rope", 0, rot // 2, MOBA_DH, sc)] * nsl),
        (MOBA_HEADS * MOBA_DH, [("rope", 0, rot // 2, MOBA_DH, 1.0)] * nsl),
        (MOBA_HEADS * MOBA_DH, [("vT", MOBA_BLOCK)] * nsl),
        (MEMQ_W, [("scale", MEM_DH ** -0.5)] * nsl),
        (MLA_HEADS * MLA_V, [("plain",)] * nsl),
        (MOBA_HEADS * MOBA_DH, [("plain",)] * nsl),
        (MEMQ_W, [("plain",)] * nsl),
    ]
    qb, kb, vb, qm, gate_a, gate_b, gate_m = _inproj(x, norm_g, wmain, groups, [cos_b, sin_b], PROJ_TM, tn)

    y_a = _mla_attn(qa, ka, va, gate_a, ATTN_TQ, MLA_HEADS_PER_STEP)
    y_b = _moba_attn(qb, kb, vb, gate_b, MOBA_HEADS_PER_STEP)
    kvm = _mem_prep(mem2d, mem_norm_g, w_mem_kv.astype(BF16))
    y_m = _mem_attn(qm, kvm, gate_m, ATTN_TQ)

    wo = w_out.astype(BF16)
    a, b = MLA_HEADS * MLA_V, MLA_HEADS * MLA_V + MOBA_HEADS * MOBA_DH
    return [y_a, y_b, y_m], [wo[:a], wo[a:b], wo[b:]]


def _odd_layer(x, mem2d, norm_g, w_in, sinks, mem_norm_g, w_mem_kv, w_out):
    rot_c = SWA_DH // ROT_FRAC
    rot_d = DIL_DH // ROT_FRAC
    cos_c, sin_c = _rope_tables(rot_c, SWA_DH)
    cos_d, sin_d = _rope_tables(rot_d, DIL_DH)
    tn = 256
    nsl = tn // LANES
    wq = SWA_HEADS * SWA_DH
    wd = DIL_HEADS * DIL_DH
    rc = lambda s: ("rope", 0, rot_c // 2, SWA_DH, s)
    rd = lambda s: ("rope", 1, rot_d // 2, DIL_DH, s)
    groups = [
        (wq, [rc(SWA_DH ** -0.5)] * nsl),
        (2 * SWA_KV_HEADS * SWA_DH, [rc(1.0), ("plain",)]),
        (wd, [rd(DIL_DH ** -0.5)] * nsl),
        (wd, [rd(1.0)] * nsl),
        (wd, [("plain",)] * nsl),
        (MEMQ_W, [("scale", MEM_DH ** -0.5)] * nsl),
        (wq, [("plain",)] * nsl),
        (wd, [("plain",)] * nsl),
        (MEMQ_W, [("plain",)] * nsl),
    ]
    qc, kvc, qd, kd, vd, qm, gate_c, gate_d, gate_m = _inproj(
        x, norm_g, w_in.astype(BF16), groups, [cos_c, sin_c, cos_d, sin_d], PROJ_TM, tn)

    sink_row = jnp.repeat(sinks.astype(F32), SWA_DH).reshape(1, wq)
    y_c = _swa_attn(qc, kvc, gate_c, sink_row)
    outs, lses = zip(*[_dil_attn(qd, kd, vd, dil) for _, dil in DIL_PATTERNS])
    y_d = _dil_combine(outs, lses, gate_d, PROJ_TM)
    kvm = _mem_prep(mem2d, mem_norm_g, w_mem_kv.astype(BF16))
    y_m = _mem_attn(qm, kvm, gate_m, ATTN_TQ)

    wo = w_out.astype(BF16)
    return [y_c, y_d, y_m], [wo[:wq], wo[wq:wq + wd], wo[wq + wd:]]


def kernel(x, mem, ev_norm_g, ev_w_in, ev_q_norm_g, ev_w_uq, ev_kv_norm_g, ev_w_ukv, ev_mem_norm_g,
           ev_w_mem_kv, ev_w_out, od_norm_g, od_w_in, od_sinks, od_mem_norm_g, od_w_mem_kv, od_w_out,
           final_norm_g):
    x2 = x.reshape(N_TOK, D_MODEL)
    mem2d = mem.reshape(BATCH * N_MEM, D_MODEL)
    ys, ws = _even_layer(x2, mem2d, ev_norm_g[0], ev_w_in[0], ev_q_norm_g[0], ev_w_uq[0], ev_kv_norm_g[0],
                         ev_w_ukv[0], ev_mem_norm_g[0], ev_w_mem_kv[0], ev_w_out[0])
    x2 = _outproj(ys, ws, x2, None, PROJ_TM)
    ys, ws = _odd_layer(x2, mem2d, od_norm_g[0], od_w_in[0], od_sinks[0], od_mem_norm_g[0],
                        od_w_mem_kv[0], od_w_out[0])
    x2 = _outproj(ys, ws, x2, final_norm_g, PROJ_TM)
    return x2.reshape(BATCH, SEQ, D_MODEL)
```

```python
import functools

import numpy as np
import jax
import jax.numpy as jnp
from jax import lax
from jax.experimental import pallas as pl
from jax.experimental.pallas import tpu as pltpu

D_MODEL = 2048
BATCH = 4
SEQ = 4096
N_TOK = BATCH * SEQ
N_MEM = 256
ROPE_THETA = 500000.0
ROT_FRAC = 4
EPS = 1e-6

MLA_HEADS = 8
MLA_Q_RANK = 512
MLA_KV_RANK = 256
MLA_NOPE = 128
MLA_ROPE = 64
MLA_V = 128
MOBA_HEADS = 8
MOBA_DH = 128
MOBA_BLOCK = 256
MOBA_TOPK = 3
SWA_HEADS = 16
SWA_KV_HEADS = 2
SWA_DH = 64
SWA_WINDOW = 128
DIL_HEADS = 6
DIL_DH = 128
DIL_PATTERNS = ((128, 1), (512, 4), (2048, 16))
MEM_HEADS = 4
MEM_DH = 128
MEMQ_W = MEM_HEADS * MEM_DH
Q_BLOCK = 128

LANES = 128
VMEM_LIMIT = 52 * 1024 * 1024

NEG = -1e30
BF16 = jnp.bfloat16
F32 = jnp.float32


def _cparams(sem):
    return pltpu.CompilerParams(dimension_semantics=sem, vmem_limit_bytes=VMEM_LIMIT)


def _rms(x, g):
    ms = jnp.mean(x * x, axis=-1, keepdims=True)
    return (x * lax.rsqrt(ms + EPS)) * g


def _silu(g):
    return g / (1.0 + jnp.exp(-g))


def _sigmoid(z):
    return 1.0 / (1.0 + jnp.exp(-z))


def _rope_slab(x, cos_t, sin_t, half, period):
    lane = lax.broadcasted_iota(jnp.int32, x.shape, 1)
    up = pltpu.roll(x, LANES - half, 1)
    dn = pltpu.roll(x, half, 1)
    sw = jnp.where((lane & (period - 1)) < half, up, dn)
    return x * cos_t + sw * sin_t


def _qk(q, k):
    return lax.dot_general(q, k, (((1,), (1,)), ((), ())), preferred_element_type=F32)


def _rope_tables(rot_dim, period):
    half = rot_dim // 2
    inv = 1.0 / (ROPE_THETA ** (jnp.arange(0, rot_dim, 2, dtype=F32) / rot_dim))
    ang = jnp.arange(SEQ, dtype=F32)[:, None] * inv[None, :]
    c, s = jnp.cos(ang), jnp.sin(ang)
    d = np.arange(LANES) % period
    idx = d % half
    cos_t = jnp.where(d < rot_dim, c[:, idx], 1.0)
    sin_t = jnp.where(d < half, -s[:, idx], jnp.where(d < rot_dim, s[:, idx], 0.0))
    return cos_t.astype(F32), sin_t.astype(F32)


PROJ_CHUNK = 512
DIL_VIEWS = tuple(d for _, d in DIL_PATTERNS if d > 1)


def _inproj_kernel(groups, ntab, tm, x_ref, g_ref, *rest):
    tabs = rest[:2 * ntab]
    w_ref = rest[2 * ntab]
    ndil = sum(layout == "dilated" for _, _, layout in groups)
    refs = list(rest[2 * ntab + 1:len(rest) - ndil])
    stage = list(rest[len(rest) - ndil:])
    xn = _rms(x_ref[...], g_ref[...]).astype(BF16)
    c0 = 0
    for ncols, kinds, layout in groups:
        o_ref = refs.pop(0)
        views = [refs.pop(0) for _ in DIL_VIEWS] if layout == "dilated" else []
        st_ref = stage.pop(0) if layout == "dilated" else None
        for s0 in range(0, ncols, PROJ_CHUNK):
            n = min(PROJ_CHUNK, ncols - s0)
            acc = jnp.dot(xn, w_ref[:, c0 + s0:c0 + s0 + n], preferred_element_type=F32)
            for c in range(n // LANES):
                col = s0 + c * LANES
                kind = kinds[col // LANES]
                piece = acc[:, c * LANES:(c + 1) * LANES]
                if kind[0] == "rope":
                    _, ti, half, period, sc = kind
                    piece = _rope_slab(piece, tabs[2 * ti][...], tabs[2 * ti + 1][...], half, period)
                    if sc != 1.0:
                        piece = piece * sc
                elif kind[0] == "scale":
                    piece = piece * kind[1]
                if layout == "rows" or layout == "dilated":
                    o_ref[:, col:col + LANES] = piece.astype(o_ref.dtype)
                    if layout == "dilated":
                        st_ref[col // LANES] = piece
                else:
                    tb = layout[1]
                    for u in range(tm // tb):
                        o_ref[col // LANES, u] = piece[u * tb:(u + 1) * tb, :].T.astype(o_ref.dtype)
        for d, v_ref in zip(DIL_VIEWS, views):
            for r in range(d):
                for c in range(ncols // LANES):
                    a = r * ncols + c * LANES
                    v_ref[:, a:a + LANES] = st_ref[c, pl.ds(r, tm // d, stride=d), :].astype(v_ref.dtype)
        c0 += ncols


def _inproj(x, g, w, groups, tables, tm):
    n_tok, d = x.shape
    assert sum(nc for nc, _, _ in groups) == w.shape[1]
    seq_tiles = SEQ // tm
    in_specs = [pl.BlockSpec((tm, d), lambda i: (i, 0)),
                pl.BlockSpec((1, d), lambda i: (0, 0))]
    args = [x, g.reshape(1, d)]
    for t in tables:
        in_specs.append(pl.BlockSpec((tm, LANES), lambda i: (i % seq_tiles, 0)))
        args.append(t)
    in_specs.append(pl.BlockSpec(w.shape, lambda i: (0, 0), pipeline_mode=pl.Buffered(1)))
    args.append(w)
    out_specs, out_shapes = [], []
    for nc, _, layout in groups:
        if layout == "rows" or layout == "dilated":
            out_specs.append(pl.BlockSpec((tm, nc), lambda i: (i, 0)))
            out_shapes.append(jax.ShapeDtypeStruct((n_tok, nc), BF16))
            if layout == "dilated":
                for dil in DIL_VIEWS:
                    out_specs.append(pl.BlockSpec((tm // dil, dil * nc), lambda i: (i, 0)))
                    out_shapes.append(jax.ShapeDtypeStruct((n_tok // dil, dil * nc), BF16))
        else:
            tb = layout[1]
            out_specs.append(pl.BlockSpec((nc // LANES, tm // tb, LANES, tb), lambda i: (0, i, 0, 0)))
            out_shapes.append(jax.ShapeDtypeStruct((nc // LANES, n_tok // tb, LANES, tb), BF16))
    return pl.pallas_call(
        functools.partial(_inproj_kernel, groups, len(tables) // 2, tm),
        out_shape=out_shapes,
        grid=(n_tok // tm,),
        in_specs=in_specs,
        out_specs=out_specs,
        scratch_shapes=[pltpu.VMEM((nc // LANES, tm, LANES), F32) for nc, _, layout in groups
                        if layout == "dilated"],
        compiler_params=_cparams(("parallel",)),
        name="inproj",
    )(*args)


MLA_HW = 256


def _mla_prep_kernel(scale, x_ref, g_ref, wlat_ref, qg_ref, wuq_ref, kvg_ref, wukv_ref,
                     cos_ref, sin_ref, qa_ref, ka_ref, va_ref):
    xn = _rms(x_ref[...], g_ref[...]).astype(BF16)
    lat = jnp.dot(xn, wlat_ref[...], preferred_element_type=F32)
    cq = _rms(lat[:, :MLA_Q_RANK], qg_ref[...]).astype(BF16)
    ckv = _rms(lat[:, MLA_Q_RANK:MLA_Q_RANK + MLA_KV_RANK], kvg_ref[...]).astype(BF16)
    kpe = lat[:, MLA_Q_RANK + MLA_KV_RANK:]
    cos_t, sin_t = cos_ref[...], sin_ref[...]
    kpe = _rope_slab(kpe, cos_t, sin_t, MLA_ROPE // 2, LANES).astype(BF16)
    q = jnp.dot(cq, wuq_ref[...], preferred_element_type=F32)
    kv = jnp.dot(ckv, wukv_ref[...], preferred_element_type=F32)
    for h in range(MLA_HEADS):
        a = h * MLA_HW
        qa_ref[:, a:a + LANES] = (q[:, a:a + LANES] * scale).astype(BF16)
        qr = _rope_slab(q[:, a + LANES:a + 2 * LANES], cos_t, sin_t, MLA_ROPE // 2, LANES)
        qa_ref[:, a + LANES:a + 2 * LANES] = (qr * scale).astype(BF16)
        ka_ref[:, a:a + LANES] = kv[:, h * LANES:(h + 1) * LANES].astype(BF16)
        ka_ref[:, a + LANES:a + 2 * LANES] = kpe
        v0 = MLA_HEADS * MLA_NOPE + h * MLA_V
        va_ref[h, 0] = kv[:, v0:v0 + MLA_V].T.astype(BF16)


def _mla_prep(x, g, wlat, qg, wuq, kvg, wukv, cos_t, sin_t, tm):
    n_tok, d = x.shape
    seq_tiles = SEQ // tm
    full = lambda a: pl.BlockSpec(a.shape, lambda i: (0, 0))
    qg2, kvg2, g2 = qg.reshape(1, -1), kvg.reshape(1, -1), g.reshape(1, d)
    scale = (MLA_NOPE + MLA_ROPE) ** -0.5
    return pl.pallas_call(
        functools.partial(_mla_prep_kernel, scale),
        out_shape=[jax.ShapeDtypeStruct((n_tok, MLA_HEADS * MLA_HW), BF16),
                   jax.ShapeDtypeStruct((n_tok, MLA_HEADS * MLA_HW), BF16),
                   jax.ShapeDtypeStruct((MLA_HEADS, n_tok // tm, MLA_V, tm), BF16)],
        grid=(n_tok // tm,),
        in_specs=[pl.BlockSpec((tm, d), lambda i: (i, 0)), full(g2), full(wlat), full(qg2), full(wuq),
                  full(kvg2), full(wukv),
                  pl.BlockSpec((tm, LANES), lambda i: (i % seq_tiles, 0)),
                  pl.BlockSpec((tm, LANES), lambda i: (i % seq_tiles, 0))],
        out_specs=[pl.BlockSpec((tm, MLA_HEADS * MLA_HW), lambda i: (i, 0)),
                   pl.BlockSpec((tm, MLA_HEADS * MLA_HW), lambda i: (i, 0)),
                   pl.BlockSpec((MLA_HEADS, 1, MLA_V, tm), lambda i: (0, i, 0, 0))],
        compiler_params=_cparams(("parallel",)),
        name="mla_prep",
    )(x, g2, wlat, qg2, wuq, kvg2, wukv, cos_t, sin_t)


def _flash_step_t(s, vt, m, l, acc_ref):
    m_new = jnp.maximum(m, jnp.max(s, axis=0, keepdims=True))
    alpha = jnp.exp(m - m_new)
    p = jnp.exp(s - m_new)
    l_new = alpha * l + jnp.sum(p, axis=0, keepdims=True)
    acc_ref[...] = alpha * acc_ref[...] + jnp.dot(vt, p.astype(BF16), preferred_element_type=F32)
    return m_new, l_new


def _flash_finish(acc_ref, l, gate, y_dtype):
    o = (acc_ref[...] / l).T
    return (o * _silu(gate.astype(F32))).astype(y_dtype)


def _flash_causal_t(nh, qi, tq, diag_scores, past_scores, vt_fn, s_ref, acc_ref):
    acc_ref[...] = jnp.zeros_like(acc_ref)
    m0 = jnp.full((1, tq), NEG, F32)
    l0 = jnp.zeros((1, tq), F32)
    for g in range(nh):
        s_ref[g] = diag_scores(g)

    def consume(j, carry):
        out = []
        for g in range(nh):
            out.extend(_flash_step_t(s_ref[g], vt_fn(g, j), carry[2 * g], carry[2 * g + 1], acc_ref.at[g]))
        return tuple(out)

    def body(j, carry):
        nxt = [past_scores(g, j) for g in range(nh)]
        carry = consume(jnp.where(j == 0, qi, j - 1), carry)
        for g in range(nh):
            s_ref[g] = nxt[g]
        return carry

    carry = lax.fori_loop(0, qi, body, (m0, l0) * nh)
    carry = consume(jnp.maximum(qi - 1, 0), carry)
    return [carry[2 * g + 1] for g in range(nh)]


def _mla_kernel(tq, nh, q_ref, k_ref, vt_ref, gate_ref, y_ref, s_ref, acc_ref):
    qi = pl.program_id(2)
    qs = [q_ref[:, g * MLA_HW:(g + 1) * MLA_HW] for g in range(nh)]
    key = lax.broadcasted_iota(jnp.int32, (tq, tq), 0)
    qry = lax.broadcasted_iota(jnp.int32, (tq, tq), 1)

    def scores(g, j):
        off = pl.multiple_of(j * tq, tq)
        return _qk(k_ref[pl.ds(off, tq), g * MLA_HW:(g + 1) * MLA_HW], qs[g])

    ls = _flash_causal_t(nh, qi, tq,
                         lambda g: jnp.where(key <= qry, scores(g, qi), NEG),
                         scores, lambda g, j: vt_ref[g, j], s_ref, acc_ref)
    for g in range(nh):
        c = slice(g * MLA_V, (g + 1) * MLA_V)
        y_ref[:, c] = _flash_finish(acc_ref.at[g], ls[g], gate_ref[:, c], y_ref.dtype)


def _mla_attn(qa, ka, vat, gate, tq, nh):
    nq = SEQ // tq
    return pl.pallas_call(
        functools.partial(_mla_kernel, tq, nh),
        out_shape=jax.ShapeDtypeStruct((N_TOK, MLA_HEADS * MLA_V), BF16),
        grid=(BATCH, MLA_HEADS // nh, nq),
        in_specs=[pl.BlockSpec((tq, nh * MLA_HW), lambda b, h, i: (b * nq + i, h)),
                  pl.BlockSpec((SEQ, nh * MLA_HW), lambda b, h, i: (b, h)),
                  pl.BlockSpec((nh, nq, MLA_V, tq), lambda b, h, i: (h, b, 0, 0)),
                  pl.BlockSpec((tq, nh * MLA_V), lambda b, h, i: (b * nq + i, h))],
        out_specs=pl.BlockSpec((tq, nh * MLA_V), lambda b, h, i: (b * nq + i, h)),
        scratch_shapes=[pltpu.VMEM((nh, tq, tq), F32), pltpu.VMEM((nh, MLA_V, tq), F32)],
        compiler_params=_cparams(("parallel", "parallel", "arbitrary")),
        name="mla_attn",
    )(qa, ka, vat, gate)


def _moba_kernel(nh, q_ref, k_ref, vt_ref, gate_ref, y_ref, km_ref, nb_ref, s_ref, acc_ref):
    L = MOBA_BLOCK
    D = MOBA_DH
    nblk = SEQ // L
    qi = pl.program_id(2)

    @pl.when(qi == 0)
    def _():
        for g in range(nh):
            for j in range(nblk):
                blk = k_ref[j * L:(j + 1) * L, g * D:(g + 1) * D].astype(F32)
                km_ref[g, j:j + 1, :] = jnp.sum(blk, axis=0, keepdims=True) * (1.0 / L)

    qs = [q_ref[:, g * D:(g + 1) * D] for g in range(nh)]
    blk_i = lax.broadcasted_iota(jnp.int32, (nblk, L), 0)
    blk_f = blk_i.astype(F32)
    past = blk_i < qi
    for g in range(nh):
        gt = jnp.where(past, _qk(km_ref[g].astype(BF16), qs[g]), -jnp.inf)
        sel = jnp.zeros((nblk, L), jnp.bool_)
        for _ in range(MOBA_TOPK):
            mx = jnp.max(gt, axis=0, keepdims=True)
            first = jnp.min(jnp.where(gt == mx, blk_f, float(nblk)), axis=0, keepdims=True)
            pick = blk_f == first
            sel = jnp.logical_or(sel, pick)
            gt = jnp.where(pick, -jnp.inf, gt)
        nb_ref[g] = jnp.where(jnp.logical_and(sel, past), 0.0, NEG)

    key = lax.broadcasted_iota(jnp.int32, (L, L), 0)
    qry = lax.broadcasted_iota(jnp.int32, (L, L), 1)

    def scores(g, j):
        off = pl.multiple_of(j * L, L)
        return _qk(k_ref[pl.ds(off, L), g * D:(g + 1) * D], qs[g])

    ls = _flash_causal_t(nh, qi, L,
                         lambda g: jnp.where(key <= qry, scores(g, qi), NEG),
                         lambda g, j: scores(g, j) + nb_ref[g, pl.ds(j, 1), :],
                         lambda g, j: vt_ref[g, j], s_ref, acc_ref)
    for g in range(nh):
        c = slice(g * D, (g + 1) * D)
        y_ref[:, c] = _flash_finish(acc_ref.at[g], ls[g], gate_ref[:, c], y_ref.dtype)


def _moba_attn(qb, kb, vbt, gate, nh):
    L = MOBA_BLOCK
    D = MOBA_DH
    nq = SEQ // L
    return pl.pallas_call(
        functools.partial(_moba_kernel, nh),
        out_shape=jax.ShapeDtypeStruct((N_TOK, MOBA_HEADS * D), BF16),
        grid=(BATCH, MOBA_HEADS // nh, nq),
        in_specs=[pl.BlockSpec((L, nh * D), lambda b, h, i: (b * nq + i, h)),
                  pl.BlockSpec((SEQ, nh * D), lambda b, h, i: (b, h)),
                  pl.BlockSpec((nh, nq, D, L), lambda b, h, i: (h, b, 0, 0)),
                  pl.BlockSpec((L, nh * D), lambda b, h, i: (b * nq + i, h))],
        out_specs=pl.BlockSpec((L, nh * D), lambda b, h, i: (b * nq + i, h)),
        scratch_shapes=[pltpu.VMEM((nh, nq, D), F32), pltpu.VMEM((nh, nq, L), F32),
                        pltpu.VMEM((nh, L, L), F32), pltpu.VMEM((nh, D, L), F32)],
        compiler_params=_cparams(("parallel", "parallel", "arbitrary")),
        name="moba_attn",
    )(qb, kb, vbt, gate)


def _mem_prep_kernel(mem_ref, g_ref, w_ref, kv_ref):
    mn = _rms(mem_ref[...], g_ref[...]).astype(BF16)
    kv_ref[...] = jnp.dot(mn, w_ref[...], preferred_element_type=F32).astype(BF16)


def _mem_prep(mem2d, g, w):
    g2 = g.reshape(1, -1)
    return pl.pallas_call(
        _mem_prep_kernel,
        out_shape=jax.ShapeDtypeStruct((BATCH * N_MEM, 2 * MEMQ_W), BF16),
        grid=(BATCH,),
        in_specs=[pl.BlockSpec((N_MEM, D_MODEL), lambda b: (b, 0)),
                  pl.BlockSpec(g2.shape, lambda b: (0, 0)),
                  pl.BlockSpec(w.shape, lambda b: (0, 0))],
        out_specs=pl.BlockSpec((N_MEM, 2 * MEMQ_W), lambda b: (b, 0)),
        compiler_params=_cparams(("parallel",)),
        name="mem_prep",
    )(mem2d, g2, w)


def _mem_attn_kernel(q_ref, kv_ref, gate_ref, y_ref):
    for h in range(MEM_HEADS):
        a = h * MEM_DH
        s = _qk(q_ref[:, a:a + MEM_DH], kv_ref[:, a:a + MEM_DH])
        p = jnp.exp(s - jnp.max(s, axis=1, keepdims=True))
        l = jnp.sum(p, axis=1, keepdims=True)
        o = jnp.dot(p.astype(BF16), kv_ref[:, MEMQ_W + a:MEMQ_W + a + MEM_DH],
                    preferred_element_type=F32) / l
        y_ref[:, a:a + MEM_DH] = (o * _silu(gate_ref[:, a:a + MEM_DH].astype(F32))).astype(y_ref.dtype)


def _mem_attn(qm, kvm, gate, tq):
    nq = SEQ // tq
    return pl.pallas_call(
        _mem_attn_kernel,
        out_shape=jax.ShapeDtypeStruct((N_TOK, MEMQ_W), BF16),
        grid=(BATCH, nq),
        in_specs=[pl.BlockSpec((tq, MEMQ_W), lambda b, i: (b * nq + i, 0)),
                  pl.BlockSpec((N_MEM, 2 * MEMQ_W), lambda b, i: (b, 0)),
                  pl.BlockSpec((tq, MEMQ_W), lambda b, i: (b * nq + i, 0))],
        out_specs=pl.BlockSpec((tq, MEMQ_W), lambda b, i: (b * nq + i, 0)),
        compiler_params=_cparams(("parallel", "parallel")),
        name="mem_attn",
    )(qm, kvm, gate)


def _band_bias(n, lo_off, hi_off):
    i = lax.broadcasted_iota(jnp.int32, (Q_BLOCK, 2 * Q_BLOCK), 0)
    c = lax.broadcasted_iota(jnp.int32, (Q_BLOCK, 2 * Q_BLOCK), 1)
    vis = (c - i >= lo_off) & (c - i <= hi_off) & ((c >= Q_BLOCK) | (n > 0))
    return jnp.where(vis, 0.0, NEG).astype(F32)


def _swa_kernel(q_ref, kvp_ref, kvo_ref, gate_ref, sink_ref, y_ref):
    QB = Q_BLOCK
    G = SWA_HEADS // SWA_KV_HEADS
    n = pl.program_id(1)
    bias = _band_bias(n, QB - (SWA_WINDOW - 1), QB)
    bias4 = jnp.concatenate([bias] * (G // 2), axis=0)
    k2 = jnp.concatenate([kvp_ref[:, :LANES], kvo_ref[:, :LANES]], axis=0).astype(F32)
    v2 = jnp.concatenate([kvp_ref[:, LANES:], kvo_ref[:, LANES:]], axis=0).astype(F32)
    k2r = pltpu.roll(k2, SWA_DH, 1)
    v2r = pltpu.roll(v2, SWA_DH, 1)
    lane = lax.broadcasted_iota(jnp.int32, (2 * QB, LANES), 1)
    lo = lane < SWA_DH
    lane_q = lax.broadcasted_iota(jnp.int32, (QB, LANES), 1)
    lo_q = lane_q < SWA_DH
    for kv in range(SWA_KV_HEADS):
        ka, kb_ = (k2, k2r) if kv == 0 else (k2r, k2)
        va, vb_ = (v2, v2r) if kv == 0 else (v2r, v2)
        k_lo = jnp.where(lo, ka, 0.0).astype(BF16)
        k_hi = jnp.where(lo, 0.0, kb_).astype(BF16)
        v_lo = jnp.where(lo, va, 0.0).astype(BF16)
        v_hi = jnp.where(lo, 0.0, vb_).astype(BF16)
        base = kv * (G // 2)
        q4 = jnp.concatenate([q_ref[:, (base + p) * LANES:(base + p + 1) * LANES] for p in range(G // 2)],
                             axis=0)
        s_lo = _qk(q4, k_lo) + bias4
        s_hi = _qk(q4, k_hi) + bias4
        m_lo = jnp.max(s_lo, axis=1, keepdims=True)
        m_hi = jnp.max(s_hi, axis=1, keepdims=True)
        p_lo = jnp.exp(s_lo - m_lo)
        p_hi = jnp.exp(s_hi - m_hi)
        l_lo = jnp.sum(p_lo, axis=1, keepdims=True)
        l_hi = jnp.sum(p_hi, axis=1, keepdims=True)
        o = (jnp.dot(p_lo.astype(BF16), v_lo, preferred_element_type=F32)
             + jnp.dot(p_hi.astype(BF16), v_hi, preferred_element_type=F32))
        lse_lo = m_lo + jnp.log(l_lo)
        lse_hi = m_hi + jnp.log(l_hi)
        for p in range(G // 2):
            r = slice(p * QB, (p + 1) * QB)
            c = slice((base + p) * LANES, (base + p + 1) * LANES)
            lse = jnp.where(lo_q, lse_lo[r], lse_hi[r])
            l = jnp.where(lo_q, l_lo[r], l_hi[r])
            w = _sigmoid(lse - sink_ref[:, c])
            y = (o[r] / l) * w * _silu(gate_ref[:, c].astype(F32))
            y_ref[:, c] = y.astype(y_ref.dtype)


def _swa_attn(qc, kvc, gate, sink_row):
    nb = SEQ // Q_BLOCK
    w = SWA_HEADS * SWA_DH
    return pl.pallas_call(
        _swa_kernel,
        out_shape=jax.ShapeDtypeStruct((N_TOK, w), BF16),
        grid=(BATCH, nb),
        in_specs=[pl.BlockSpec((Q_BLOCK, w), lambda b, n: (b * nb + n, 0)),
                  pl.BlockSpec((Q_BLOCK, 2 * LANES), lambda b, n: (jnp.maximum(b * nb + n - 1, 0), 0)),
                  pl.BlockSpec((Q_BLOCK, 2 * LANES), lambda b, n: (b * nb + n, 0)),
                  pl.BlockSpec((Q_BLOCK, w), lambda b, n: (b * nb + n, 0)),
                  pl.BlockSpec((1, w), lambda b, n: (0, 0))],
        out_specs=pl.BlockSpec((Q_BLOCK, w), lambda b, n: (b * nb + n, 0)),
        compiler_params=_cparams(("parallel", "parallel")),
        name="swa_attn",
    )(qc, kvc, kvc, gate, sink_row)


def _dil_kernel(q_ref, kp_ref, ko_ref, vp_ref, vo_ref, o_ref, lse_ref):
    QB = Q_BLOCK
    n = pl.program_id(2)
    bias = _band_bias(n, 0, QB)
    lane = lax.broadcasted_iota(jnp.int32, (QB, LANES), 1)
    lse_all = jnp.zeros((QB, LANES), F32)
    for h in range(DIL_HEADS):
        c = slice(h * DIL_DH, (h + 1) * DIL_DH)
        k = jnp.concatenate([kp_ref[:, c], ko_ref[:, c]], axis=0)
        v = jnp.concatenate([vp_ref[:, c], vo_ref[:, c]], axis=0)
        s = _qk(q_ref[:, c], k) + bias
        m = jnp.max(s, axis=1, keepdims=True)
        p = jnp.exp(s - m)
        l = jnp.sum(p, axis=1, keepdims=True)
        o_ref[:, c] = jnp.dot(p.astype(BF16), v, preferred_element_type=F32) / l
        lse_all = jnp.where(lane == h, m + jnp.log(l), lse_all)
    lse_ref[...] = lse_all


def _dil_attn(qv, kv, vv, dil):
    w = DIL_HEADS * DIL_DH
    L = SEQ // dil
    nb = L // Q_BLOCK
    cur = lambda b, r, n: (b * nb + n, r)
    prev = lambda b, r, n: (jnp.maximum(b * nb + n - 1, 0), r)
    blk = (Q_BLOCK, w)
    return pl.pallas_call(
        _dil_kernel,
        out_shape=[jax.ShapeDtypeStruct((BATCH * L, dil * w), F32),
                   jax.ShapeDtypeStruct((BATCH * L, dil * LANES), F32)],
        grid=(BATCH, dil, nb),
        in_specs=[pl.BlockSpec(blk, cur), pl.BlockSpec(blk, prev), pl.BlockSpec(blk, cur),
                  pl.BlockSpec(blk, prev), pl.BlockSpec(blk, cur)],
        out_specs=[pl.BlockSpec(blk, cur), pl.BlockSpec((Q_BLOCK, LANES), cur)],
        compiler_params=_cparams(("parallel", "parallel", "parallel")),
        name=f"dil_attn_d{dil}",
    )(qv, kv, kv, vv, vv)


def _dil_combine_kernel(tm, *refs):
    np_ = len(DIL_PATTERNS)
    o_refs, l_refs = refs[:np_], refs[np_:2 * np_]
    gate_ref, y_ref = refs[2 * np_], refs[2 * np_ + 1]
    stage = refs[2 * np_ + 2:]
    w = DIL_HEADS * DIL_DH
    os_, ls = [], []
    k = 0
    for (_, d), o_ref, l_ref in zip(DIL_PATTERNS, o_refs, l_refs):
        if d == 1:
            os_.append(lambda h, o_ref=o_ref: o_ref[:, h * DIL_DH:(h + 1) * DIL_DH])
            ls.append(l_ref[...])
            continue
        so, sl = stage[2 * k], stage[2 * k + 1]
        k += 1
        for r in range(d):
            for h in range(DIL_HEADS):
                a = r * w + h * DIL_DH
                so[h, pl.ds(r, tm // d, stride=d), :] = o_ref[:, a:a + DIL_DH]
            sl[pl.ds(r, tm // d, stride=d), :] = l_ref[:, r * LANES:(r + 1) * LANES]
        os_.append(lambda h, so=so: so[h])
        ls.append(sl[...])
    mx = functools.reduce(jnp.maximum, ls)
    es = [jnp.exp(x - mx) for x in ls]
    den = functools.reduce(lambda a, b: a + b, es)
    ws = [e / den for e in es]
    lane = lax.broadcasted_iota(jnp.int32, ws[0].shape, 1)
    for h in range(DIL_HEADS):
        c = slice(h * DIL_DH, (h + 1) * DIL_DH)
        wh = [jnp.sum(jnp.where(lane == h, wgt, 0.0), axis=1, keepdims=True) for wgt in ws]
        o = functools.reduce(lambda a, b: a + b, [wh[p] * os_[p](h) for p in range(np_)])
        y_ref[:, c] = (o * _silu(gate_ref[:, c].astype(F32))).astype(y_ref.dtype)


def _dil_combine(outs, lses, gate, tm):
    w = DIL_HEADS * DIL_DH
    in_specs = [pl.BlockSpec((tm // d, d * w), lambda i: (i, 0)) for _, d in DIL_PATTERNS]
    in_specs += [pl.BlockSpec((tm // d, d * LANES), lambda i: (i, 0)) for _, d in DIL_PATTERNS]
    in_specs.append(pl.BlockSpec((tm, w), lambda i: (i, 0)))
    scratch = []
    for _, d in DIL_PATTERNS:
        if d > 1:
            scratch += [pltpu.VMEM((DIL_HEADS, tm, DIL_DH), F32), pltpu.VMEM((tm, LANES), F32)]
    return pl.pallas_call(
        functools.partial(_dil_combine_kernel, tm),
        out_shape=jax.ShapeDtypeStruct((N_TOK, w), BF16),
        grid=(N_TOK // tm,),
        in_specs=in_specs,
        out_specs=pl.BlockSpec((tm, w), lambda i: (i, 0)),
        scratch_shapes=scratch,
        compiler_params=_cparams(("parallel",)),
        name="dil_combine",
    )(*outs, *lses, gate)


def _outproj_kernel(nparts, final, *refs):
    ys, ws = refs[:nparts], refs[nparts:2 * nparts]
    x_ref = refs[2 * nparts]
    o_ref = refs[-1]
    acc = x_ref[...]
    for y, w in zip(ys, ws):
        acc = acc + jnp.dot(y[...], w[...], preferred_element_type=F32)
    if final:
        acc = _rms(acc, refs[2 * nparts + 1][...])
    o_ref[...] = acc


def _outproj(ys, ws, x, final_g, tm):
    n_tok, d = x.shape
    in_specs = [pl.BlockSpec((tm, y.shape[1]), lambda i: (i, 0)) for y in ys]
    in_specs += [pl.BlockSpec(w.shape, lambda i: (0, 0)) for w in ws]
    in_specs.append(pl.BlockSpec((tm, d), lambda i: (i, 0)))
    args = [*ys, *ws, x]
    if final_g is not None:
        in_specs.append(pl.BlockSpec((1, d), lambda i: (0, 0)))
        args.append(final_g.reshape(1, d))
    return pl.pallas_call(
        functools.partial(_outproj_kernel, len(ys), final_g is not None),
        out_shape=jax.ShapeDtypeStruct((n_tok, d), F32),
        grid=(n_tok // tm,),
        in_specs=in_specs,
        out_specs=pl.BlockSpec((tm, d), lambda i: (i, 0)),
        compiler_params=_cparams(("parallel",)),
        name="outproj",
    )(*args)


PROJ_TM = 512
ATTN_TQ = 512
MLA_HEADS_PER_STEP = 2
MOBA_HEADS_PER_STEP = 4


def _even_layer(x, mem2d, norm_g, w_in, q_norm_g, w_uq, kv_norm_g, w_ukv, mem_norm_g, w_mem_kv, w_out):
    n_lat = MLA_Q_RANK + MLA_KV_RANK + MLA_ROPE
    wlat = jnp.pad(w_in[:, :n_lat], ((0, 0), (0, LANES - MLA_ROPE))).astype(BF16)
    wmain = w_in[:, n_lat:].astype(BF16)
    wuq = w_uq.reshape(MLA_Q_RANK, MLA_HEADS, MLA_NOPE + MLA_ROPE)
    wuq = jnp.pad(wuq, ((0, 0), (0, 0), (0, MLA_HW - MLA_NOPE - MLA_ROPE)))
    wuq = wuq.reshape(MLA_Q_RANK, MLA_HEADS * MLA_HW).astype(BF16)
    wukv = w_ukv.reshape(MLA_KV_RANK, MLA_HEADS, MLA_NOPE + MLA_V)
    wukv = jnp.concatenate([wukv[:, :, :MLA_NOPE].reshape(MLA_KV_RANK, -1),
                            wukv[:, :, MLA_NOPE:].reshape(MLA_KV_RANK, -1)], axis=1).astype(BF16)

    cos_a, sin_a = _rope_tables(MLA_ROPE, LANES)
    qa, ka, va = _mla_prep(x, norm_g, wlat, q_norm_g, wuq, kv_norm_g, wukv, cos_a, sin_a, ATTN_TQ)

    rot = MOBA_DH // ROT_FRAC
    cos_b, sin_b = _rope_tables(rot, MOBA_DH)
    sc = MOBA_DH ** -0.5
    wb = MOBA_HEADS * MOBA_DH
    plain = [("plain",)]
    groups_qkv = [
        (wb, [("rope", 0, rot // 2, MOBA_DH, sc)] * MOBA_HEADS, "rows"),
        (wb, [("rope", 0, rot // 2, MOBA_DH, 1.0)] * MOBA_HEADS, "rows"),
        (wb, plain * MOBA_HEADS, ("vT", MOBA_BLOCK)),
    ]
    groups_rest = [
        (MEMQ_W, [("scale", MEM_DH ** -0.5)] * MEM_HEADS, "rows"),
        (MLA_HEADS * MLA_V, plain * MLA_HEADS, "rows"),
        (wb, plain * MOBA_HEADS, "rows"),
        (MEMQ_W, plain * MEM_HEADS, "rows"),
    ]
    qb, kb, vb = _inproj(x, norm_g, wmain[:, :3 * wb], groups_qkv, [cos_b, sin_b], PROJ_TM)
    qm, gate_a, gate_b, gate_m = _inproj(x, norm_g, wmain[:, 3 * wb:], groups_rest, [], PROJ_TM)

    y_a = _mla_attn(qa, ka, va, gate_a, ATTN_TQ, MLA_HEADS_PER_STEP)
    y_b = _moba_attn(qb, kb, vb, gate_b, MOBA_HEADS_PER_STEP)
    kvm = _mem_prep(mem2d, mem_norm_g, w_mem_kv.astype(BF16))
    y_m = _mem_attn(qm, kvm, gate_m, ATTN_TQ)

    wo = w_out.astype(BF16)
    a, b = MLA_HEADS * MLA_V, MLA_HEADS * MLA_V + MOBA_HEADS * MOBA_DH
    return [y_a, y_b, y_m], [wo[:a], wo[a:b], wo[b:]]


def _odd_layer(x, mem2d, norm_g, w_in, sinks, mem_norm_g, w_mem_kv, w_out):
    rot_c = SWA_DH // ROT_FRAC
    rot_d = DIL_DH // ROT_FRAC
    cos_c, sin_c = _rope_tables(rot_c, SWA_DH)
    cos_d, sin_d = _rope_tables(rot_d, DIL_DH)
    wq = SWA_HEADS * SWA_DH
    wd = DIL_HEADS * DIL_DH
    wkv = 2 * SWA_KV_HEADS * SWA_DH
    rc = lambda s: ("rope", 0, rot_c // 2, SWA_DH, s)
    rd = lambda s: ("rope", 1, rot_d // 2, DIL_DH, s)
    plain = [("plain",)]
    groups_qkv = [
        (wq, [rc(SWA_DH ** -0.5)] * (wq // LANES), "rows"),
        (wkv, [rc(1.0), ("plain",)], "rows"),
        (wd, [rd(DIL_DH ** -0.5)] * DIL_HEADS, "dilated"),
        (wd, [rd(1.0)] * DIL_HEADS, "dilated"),
        (wd, plain * DIL_HEADS, "dilated"),
    ]
    groups_rest = [
        (MEMQ_W, [("scale", MEM_DH ** -0.5)] * MEM_HEADS, "rows"),
        (wq, plain * (wq // LANES), "rows"),
        (wd, plain * DIL_HEADS, "rows"),
        (MEMQ_W, plain * MEM_HEADS, "rows"),
    ]
    w_bf = w_in.astype(BF16)
    n_qkv = wq + wkv + 3 * wd
    res = _inproj(x, norm_g, w_bf[:, :n_qkv], groups_qkv, [cos_c, sin_c, cos_d, sin_d], PROJ_TM)
    qc, kvc = res[0], res[1]
    nv = 1 + len(DIL_VIEWS)
    qd, kd, vd = (res[2 + i * nv:2 + (i + 1) * nv] for i in range(3))
    qm, gate_c, gate_d, gate_m = _inproj(x, norm_g, w_bf[:, n_qkv:], groups_rest, [], PROJ_TM)

    sink_row = jnp.repeat(sinks.astype(F32), SWA_DH).reshape(1, wq)
    y_c = _swa_attn(qc, kvc, gate_c, sink_row)
    view_of = {1: 0, **{d: 1 + i for i, d in enumerate(DIL_VIEWS)}}
    outs, lses = zip(*[_dil_attn(qd[view_of[dil]], kd[view_of[dil]], vd[view_of[dil]], dil)
                       for _, dil in DIL_PATTERNS])
    y_d = _dil_combine(outs, lses, gate_d, PROJ_TM)
    kvm = _mem_prep(mem2d, mem_norm_g, w_mem_kv.astype(BF16))
    y_m = _mem_attn(qm, kvm, gate_m, ATTN_TQ)

    wo = w_out.astype(BF16)
    return [y_c, y_d, y_m], [wo[:wq], wo[wq:wq + wd], wo[wq + wd:]]


def kernel(x, mem, ev_norm_g, ev_w_in, ev_q_norm_g, ev_w_uq, ev_kv_norm_g, ev_w_ukv, ev_mem_norm_g,
           ev_w_mem_kv, ev_w_out, od_norm_g, od_w_in, od_sinks, od_mem_norm_g, od_w_mem_kv, od_w_out,
           final_norm_g):
    x2 = x.reshape(N_TOK, D_MODEL)
    mem2d = mem.reshape(BATCH * N_MEM, D_MODEL)
    ys, ws = _even_layer(x2, mem2d, ev_norm_g[0], ev_w_in[0], ev_q_norm_g[0], ev_w_uq[0], ev_kv_norm_g[0],
                         ev_w_ukv[0], ev_mem_norm_g[0], ev_w_mem_kv[0], ev_w_out[0])
    x2 = _outproj(ys, ws, x2, None, PROJ_TM)
    ys, ws = _odd_layer(x2, mem2d, od_norm_g[0], od_w_in[0], od_sinks[0], od_mem_norm_g[0],
                        od_w_mem_kv[0], od_w_out[0])
    x2 = _outproj(ys, ws, x2, final_norm_g, PROJ_TM)
    return x2.reshape(BATCH, SEQ, D_MODEL)
```

```python
import functools

import numpy as np
import jax
import jax.numpy as jnp
from jax import lax
from jax.experimental import pallas as pl
from jax.experimental.pallas import tpu as pltpu

D_MODEL = 2048
BATCH = 4
SEQ = 4096
N_TOK = BATCH * SEQ
N_MEM = 256
ROPE_THETA = 500000.0
ROT_FRAC = 4
EPS = 1e-6

MLA_HEADS = 8
MLA_Q_RANK = 512
MLA_KV_RANK = 256
MLA_NOPE = 128
MLA_ROPE = 64
MLA_V = 128
MOBA_HEADS = 8
MOBA_DH = 128
MOBA_BLOCK = 256
MOBA_TOPK = 3
SWA_HEADS = 16
SWA_KV_HEADS = 2
SWA_DH = 64
SWA_WINDOW = 128
DIL_HEADS = 6
DIL_DH = 128
DIL_PATTERNS = ((128, 1), (512, 4), (2048, 16))
MEM_HEADS = 4
MEM_DH = 128
MEMQ_W = MEM_HEADS * MEM_DH
Q_BLOCK = 128

LANES = 128
VMEM_LIMIT = 52 * 1024 * 1024

NEG = -1e30
BF16 = jnp.bfloat16
F32 = jnp.float32


def _cparams(sem):
    return pltpu.CompilerParams(dimension_semantics=sem, vmem_limit_bytes=VMEM_LIMIT)


def _rms(x, g):
    ms = jnp.mean(x * x, axis=-1, keepdims=True)
    return (x * lax.rsqrt(ms + EPS)) * g


def _silu(g):
    return g / (1.0 + jnp.exp(-g))


def _sigmoid(z):
    return 1.0 / (1.0 + jnp.exp(-z))


def _rope_slab(x, cos_t, sin_t, half, period):
    lane = lax.broadcasted_iota(jnp.int32, x.shape, 1)
    up = pltpu.roll(x, LANES - half, 1)
    dn = pltpu.roll(x, half, 1)
    sw = jnp.where((lane & (period - 1)) < half, up, dn)
    return x * cos_t + sw * sin_t


def _qk(q, k):
    return lax.dot_general(q, k, (((1,), (1,)), ((), ())), preferred_element_type=F32)


def _rope_tables(rot_dim, period):
    half = rot_dim // 2
    inv = 1.0 / (ROPE_THETA ** (jnp.arange(0, rot_dim, 2, dtype=F32) / rot_dim))
    ang = jnp.arange(SEQ, dtype=F32)[:, None] * inv[None, :]
    c, s = jnp.cos(ang), jnp.sin(ang)
    d = np.arange(LANES) % period
    idx = d % half
    cos_t = jnp.where(d < rot_dim, c[:, idx], 1.0)
    sin_t = jnp.where(d < half, -s[:, idx], jnp.where(d < rot_dim, s[:, idx], 0.0))
    return cos_t.astype(F32), sin_t.astype(F32)


PROJ_CHUNK = 512
DIL_VIEWS = tuple(d for _, d in DIL_PATTERNS if d > 1)


def _inproj_kernel(groups, ntab, tm, x_ref, g_ref, *rest):
    tabs = rest[:2 * ntab]
    w_ref = rest[2 * ntab]
    ndil = sum(layout == "dilated" for _, _, layout in groups)
    refs = list(rest[2 * ntab + 1:len(rest) - ndil])
    stage = list(rest[len(rest) - ndil:])
    xn = _rms(x_ref[...], g_ref[...]).astype(BF16)
    c0 = 0
    for ncols, kinds, layout in groups:
        o_ref = refs.pop(0)
        views = [refs.pop(0) for _ in DIL_VIEWS] if layout == "dilated" else []
        st_ref = stage.pop(0) if layout == "dilated" else None
        for s0 in range(0, ncols, PROJ_CHUNK):
            n = min(PROJ_CHUNK, ncols - s0)
            acc = jnp.dot(xn, w_ref[:, c0 + s0:c0 + s0 + n], preferred_element_type=F32)
            for c in range(n // LANES):
                col = s0 + c * LANES
                kind = kinds[col // LANES]
                piece = acc[:, c * LANES:(c + 1) * LANES]
                if kind[0] == "rope":
                    _, ti, half, period, sc = kind
                    piece = _rope_slab(piece, tabs[2 * ti][...], tabs[2 * ti + 1][...], half, period)
                    if sc != 1.0:
                        piece = piece * sc
                elif kind[0] == "scale":
                    piece = piece * kind[1]
                if layout == "rows" or layout == "dilated":
                    o_ref[:, col:col + LANES] = piece.astype(o_ref.dtype)
                    if layout == "dilated":
                        st_ref[col // LANES] = piece
                else:
                    tb = layout[1]
                    for u in range(tm // tb):
                        o_ref[col // LANES, u] = piece[u * tb:(u + 1) * tb, :].T.astype(o_ref.dtype)
        for d, v_ref in zip(DIL_VIEWS, views):
            for r in range(d):
                for c in range(ncols // LANES):
                    a = r * ncols + c * LANES
                    v_ref[:, a:a + LANES] = st_ref[c, pl.ds(r, tm // d, stride=d), :].astype(v_ref.dtype)
        c0 += ncols


def _inproj(x, g, w, groups, tables, tm):
    n_tok, d = x.shape
    assert sum(nc for nc, _, _ in groups) == w.shape[1]
    seq_tiles = SEQ // tm
    in_specs = [pl.BlockSpec((tm, d), lambda i: (i, 0)),
                pl.BlockSpec((1, d), lambda i: (0, 0))]
    args = [x, g.reshape(1, d)]
    for t in tables:
        in_specs.append(pl.BlockSpec((tm, LANES), lambda i: (i % seq_tiles, 0)))
        args.append(t)
    in_specs.append(pl.BlockSpec(w.shape, lambda i: (0, 0), pipeline_mode=pl.Buffered(1)))
    args.append(w)
    out_specs, out_shapes = [], []
    for nc, _, layout in groups:
        if layout == "rows" or layout == "dilated":
            out_specs.append(pl.BlockSpec((tm, nc), lambda i: (i, 0)))
            out_shapes.append(jax.ShapeDtypeStruct((n_tok, nc), BF16))
            if layout == "dilated":
                for dil in DIL_VIEWS:
                    out_specs.append(pl.BlockSpec((tm // dil, dil * nc), lambda i: (i, 0)))
                    out_shapes.append(jax.ShapeDtypeStruct((n_tok // dil, dil * nc), BF16))
        else:
            tb = layout[1]
            out_specs.append(pl.BlockSpec((nc // LANES, tm // tb, LANES, tb), lambda i: (0, i, 0, 0)))
            out_shapes.append(jax.ShapeDtypeStruct((nc // LANES, n_tok // tb, LANES, tb), BF16))
    return pl.pallas_call(
        functools.partial(_inproj_kernel, groups, len(tables) // 2, tm),
        out_shape=out_shapes,
        grid=(n_tok // tm,),
        in_specs=in_specs,
        out_specs=out_specs,
        scratch_shapes=[pltpu.VMEM((nc // LANES, tm, LANES), F32) for nc, _, layout in groups
                        if layout == "dilated"],
        compiler_params=_cparams(("parallel",)),
        name="inproj",
    )(*args)


MLA_HW = 256


def _mla_prep_kernel(scale, x_ref, g_ref, wlat_ref, qg_ref, wuq_ref, kvg_ref, wukv_ref,
                     cos_ref, sin_ref, qa_ref, ka_ref, va_ref):
    xn = _rms(x_ref[...], g_ref[...]).astype(BF16)
    lat = jnp.dot(xn, wlat_ref[...], preferred_element_type=F32)
    cq = _rms(lat[:, :MLA_Q_RANK], qg_ref[...]).astype(BF16)
    ckv = _rms(lat[:, MLA_Q_RANK:MLA_Q_RANK + MLA_KV_RANK], kvg_ref[...]).astype(BF16)
    kpe = lat[:, MLA_Q_RANK + MLA_KV_RANK:]
    cos_t, sin_t = cos_ref[...], sin_ref[...]
    kpe = _rope_slab(kpe, cos_t, sin_t, MLA_ROPE // 2, LANES).astype(BF16)
    q = jnp.dot(cq, wuq_ref[...], preferred_element_type=F32)
    kv = jnp.dot(ckv, wukv_ref[...], preferred_element_type=F32)
    for h in range(MLA_HEADS):
        a = h * MLA_HW
        qa_ref[:, a:a + LANES] = (q[:, a:a + LANES] * scale).astype(BF16)
        qr = _rope_slab(q[:, a + LANES:a + 2 * LANES], cos_t, sin_t, MLA_ROPE // 2, LANES)
        qa_ref[:, a + LANES:a + 2 * LANES] = (qr * scale).astype(BF16)
        ka_ref[:, a:a + LANES] = kv[:, h * LANES:(h + 1) * LANES].astype(BF16)
        ka_ref[:, a + LANES:a + 2 * LANES] = kpe
        v0 = MLA_HEADS * MLA_NOPE + h * MLA_V
        va_ref[h, 0] = kv[:, v0:v0 + MLA_V].T.astype(BF16)


def _mla_prep(x, g, wlat, qg, wuq, kvg, wukv, cos_t, sin_t, tm):
    n_tok, d = x.shape
    seq_tiles = SEQ // tm
    full = lambda a: pl.BlockSpec(a.shape, lambda i: (0, 0))
    qg2, kvg2, g2 = qg.reshape(1, -1), kvg.reshape(1, -1), g.reshape(1, d)
    scale = (MLA_NOPE + MLA_ROPE) ** -0.5
    return pl.pallas_call(
        functools.partial(_mla_prep_kernel, scale),
        out_shape=[jax.ShapeDtypeStruct((n_tok, MLA_HEADS * MLA_HW), BF16),
                   jax.ShapeDtypeStruct((n_tok, MLA_HEADS * MLA_HW), BF16),
                   jax.ShapeDtypeStruct((MLA_HEADS, n_tok // tm, MLA_V, tm), BF16)],
        grid=(n_tok // tm,),
        in_specs=[pl.BlockSpec((tm, d), lambda i: (i, 0)), full(g2), full(wlat), full(qg2), full(wuq),
                  full(kvg2), full(wukv),
                  pl.BlockSpec((tm, LANES), lambda i: (i % seq_tiles, 0)),
                  pl.BlockSpec((tm, LANES), lambda i: (i % seq_tiles, 0))],
        out_specs=[pl.BlockSpec((tm, MLA_HEADS * MLA_HW), lambda i: (i, 0)),
                   pl.BlockSpec((tm, MLA_HEADS * MLA_HW), lambda i: (i, 0)),
                   pl.BlockSpec((MLA_HEADS, 1, MLA_V, tm), lambda i: (0, i, 0, 0))],
        compiler_params=_cparams(("parallel",)),
        name="mla_prep",
    )(x, g2, wlat, qg2, wuq, kvg2, wukv, cos_t, sin_t)


def _flash_finish(acc_ref, l, gate, y_dtype):
    o = (acc_ref[...] / l).T
    return (o * _silu(gate.astype(F32))).astype(y_dtype)


def _flash_causal_t(nh, qi, tq, diag_scores, past_scores, vt_fn, s_ref, p_ref, acc_ref):
    acc_ref[...] = jnp.zeros_like(acc_ref)
    p_ref[...] = jnp.zeros_like(p_ref)
    m0 = jnp.full((1, tq), NEG, F32)
    l0 = jnp.zeros((1, tq), F32)
    a0 = jnp.ones((1, tq), F32)
    for g in range(nh):
        s_ref[0, g] = diag_scores(g)

    def softmax(s, m, l):
        m_new = jnp.maximum(m, jnp.max(s, axis=0, keepdims=True))
        alpha = jnp.exp(m - m_new)
        p = jnp.exp(s - m_new)
        return m_new, alpha * l + jnp.sum(p, axis=0, keepdims=True), alpha, p.astype(BF16)

    def pv(g, j):
        return jnp.dot(vt_fn(g, j), p_ref[g], preferred_element_type=F32)

    def block_of(t):
        return jnp.where(t == 0, qi, t - 1)

    def step(t, rd, wr, carry):
        for g in range(nh):
            s_ref[wr, g] = past_scores(g, t)
        pend = [pv(g, block_of(jnp.maximum(t - 1, 0))) for g in range(nh)]
        out = []
        for g in range(nh):
            m, l, alpha, p = softmax(s_ref[rd, g], carry[3 * g], carry[3 * g + 1])
            acc_ref[g] = carry[3 * g + 2] * acc_ref[g] + pend[g]
            p_ref[g] = p
            out.extend((m, l, alpha))
        return tuple(out)

    def pair(u, carry):
        return step(2 * u + 1, 1, 0, step(2 * u, 0, 1, carry))

    carry = lax.fori_loop(0, qi // 2, pair, (m0, l0, a0) * nh)
    carry = lax.cond(qi % 2 == 1, lambda c: step(qi - 1, 0, 1, c), lambda c: c, carry)
    pend = [pv(g, block_of(jnp.maximum(qi - 1, 0))) for g in range(nh)]
    last = [softmax(s_ref[qi % 2, g], carry[3 * g], carry[3 * g + 1]) for g in range(nh)]
    for g in range(nh):
        _, _, alpha, p = last[g]
        fin = jnp.dot(vt_fn(g, block_of(qi)), p, preferred_element_type=F32)
        acc_ref[g] = alpha * (carry[3 * g + 2] * acc_ref[g] + pend[g]) + fin
    return [last[g][1] for g in range(nh)]


def _mla_kernel(tq, nh, q_ref, k_ref, vt_ref, gate_ref, y_ref, s_ref, p_ref, acc_ref):
    qi = pl.program_id(2)
    qs = [q_ref[:, g * MLA_HW:(g + 1) * MLA_HW] for g in range(nh)]
    key = lax.broadcasted_iota(jnp.int32, (tq, tq), 0)
    qry = lax.broadcasted_iota(jnp.int32, (tq, tq), 1)

    def scores(g, j):
        off = pl.multiple_of(j * tq, tq)
        return _qk(k_ref[pl.ds(off, tq), g * MLA_HW:(g + 1) * MLA_HW], qs[g])

    ls = _flash_causal_t(nh, qi, tq,
                         lambda g: jnp.where(key <= qry, scores(g, qi), NEG),
                         scores, lambda g, j: vt_ref[g, j], s_ref, p_ref, acc_ref)
    for g in range(nh):
        c = slice(g * MLA_V, (g + 1) * MLA_V)
        y_ref[:, c] = _flash_finish(acc_ref.at[g], ls[g], gate_ref[:, c], y_ref.dtype)


def _mla_attn(qa, ka, vat, gate, tq, nh):
    nq = SEQ // tq
    return pl.pallas_call(
        functools.partial(_mla_kernel, tq, nh),
        out_shape=jax.ShapeDtypeStruct((N_TOK, MLA_HEADS * MLA_V), BF16),
        grid=(BATCH, MLA_HEADS // nh, nq),
        in_specs=[pl.BlockSpec((tq, nh * MLA_HW), lambda b, h, i: (b * nq + i, h)),
                  pl.BlockSpec((SEQ, nh * MLA_HW), lambda b, h, i: (b, h)),
                  pl.BlockSpec((nh, nq, MLA_V, tq), lambda b, h, i: (h, b, 0, 0)),
                  pl.BlockSpec((tq, nh * MLA_V), lambda b, h, i: (b * nq + i, h))],
        out_specs=pl.BlockSpec((tq, nh * MLA_V), lambda b, h, i: (b * nq + i, h)),
        scratch_shapes=[pltpu.VMEM((2, nh, tq, tq), F32), pltpu.VMEM((nh, tq, tq), BF16),
                        pltpu.VMEM((nh, MLA_V, tq), F32)],
        compiler_params=_cparams(("parallel", "parallel", "arbitrary")),
        name="mla_attn",
    )(qa, ka, vat, gate)


def _moba_kernel(nh, q_ref, k_ref, vt_ref, gate_ref, y_ref, km_ref, nb_ref, s_ref, p_ref, acc_ref):
    L = MOBA_BLOCK
    D = MOBA_DH
    nblk = SEQ // L
    qi = pl.program_id(2)

    @pl.when(qi == 0)
    def _():
        for g in range(nh):
            for j in range(nblk):
                blk = k_ref[j * L:(j + 1) * L, g * D:(g + 1) * D].astype(F32)
                km_ref[g, j:j + 1, :] = jnp.sum(blk, axis=0, keepdims=True) * (1.0 / L)

    qs = [q_ref[:, g * D:(g + 1) * D] for g in range(nh)]
    blk_i = lax.broadcasted_iota(jnp.int32, (nblk, L), 0)
    blk_f = blk_i.astype(F32)
    past = blk_i < qi
    for g in range(nh):
        gt = jnp.where(past, _qk(km_ref[g].astype(BF16), qs[g]), -jnp.inf)
        sel = jnp.zeros((nblk, L), jnp.bool_)
        for _ in range(MOBA_TOPK):
            mx = jnp.max(gt, axis=0, keepdims=True)
            first = jnp.min(jnp.where(gt == mx, blk_f, float(nblk)), axis=0, keepdims=True)
            pick = blk_f == first
            sel = jnp.logical_or(sel, pick)
            gt = jnp.where(pick, -jnp.inf, gt)
        nb_ref[g] = jnp.where(jnp.logical_and(sel, past), 0.0, NEG)

    key = lax.broadcasted_iota(jnp.int32, (L, L), 0)
    qry = lax.broadcasted_iota(jnp.int32, (L, L), 1)

    def scores(g, j):
        off = pl.multiple_of(j * L, L)
        return _qk(k_ref[pl.ds(off, L), g * D:(g + 1) * D], qs[g])

    ls = _flash_causal_t(nh, qi, L,
                         lambda g: jnp.where(key <= qry, scores(g, qi), NEG),
                         lambda g, j: scores(g, j) + nb_ref[g, pl.ds(j, 1), :],
                         lambda g, j: vt_ref[g, j], s_ref, p_ref, acc_ref)
    for g in range(nh):
        c = slice(g * D, (g + 1) * D)
        y_ref[:, c] = _flash_finish(acc_ref.at[g], ls[g], gate_ref[:, c], y_ref.dtype)


def _moba_attn(qb, kb, vbt, gate, nh):
    L = MOBA_BLOCK
    D = MOBA_DH
    nq = SEQ // L
    return pl.pallas_call(
        functools.partial(_moba_kernel, nh),
        out_shape=jax.ShapeDtypeStruct((N_TOK, MOBA_HEADS * D), BF16),
        grid=(BATCH, MOBA_HEADS // nh, nq),
        in_specs=[pl.BlockSpec((L, nh * D), lambda b, h, i: (b * nq + i, h)),
                  pl.BlockSpec((SEQ, nh * D), lambda b, h, i: (b, h)),
                  pl.BlockSpec((nh, nq, D, L), lambda b, h, i: (h, b, 0, 0)),
                  pl.BlockSpec((L, nh * D), lambda b, h, i: (b * nq + i, h))],
        out_specs=pl.BlockSpec((L, nh * D), lambda b, h, i: (b * nq + i, h)),
        scratch_shapes=[pltpu.VMEM((nh, nq, D), F32), pltpu.VMEM((nh, nq, L), F32),
                        pltpu.VMEM((2, nh, L, L), F32), pltpu.VMEM((nh, L, L), BF16),
                        pltpu.VMEM((nh, D, L), F32)],
        compiler_params=_cparams(("parallel", "parallel", "arbitrary")),
        name="moba_attn",
    )(qb, kb, vbt, gate)


def _mem_prep_kernel(mem_ref, g_ref, w_ref, kv_ref):
    mn = _rms(mem_ref[...], g_ref[...]).astype(BF16)
    kv_ref[...] = jnp.dot(mn, w_ref[...], preferred_element_type=F32).astype(BF16)


def _mem_prep(mem2d, g, w):
    g2 = g.reshape(1, -1)
    return pl.pallas_call(
        _mem_prep_kernel,
        out_shape=jax.ShapeDtypeStruct((BATCH * N_MEM, 2 * MEMQ_W), BF16),
        grid=(BATCH,),
        in_specs=[pl.BlockSpec((N_MEM, D_MODEL), lambda b: (b, 0)),
                  pl.BlockSpec(g2.shape, lambda b: (0, 0)),
                  pl.BlockSpec(w.shape, lambda b: (0, 0))],
        out_specs=pl.BlockSpec((N_MEM, 2 * MEMQ_W), lambda b: (b, 0)),
        compiler_params=_cparams(("parallel",)),
        name="mem_prep",
    )(mem2d, g2, w)


def _mem_attn_kernel(q_ref, kv_ref, gate_ref, y_ref):
    for h in range(MEM_HEADS):
        a = h * MEM_DH
        s = _qk(q_ref[:, a:a + MEM_DH], kv_ref[:, a:a + MEM_DH])
        p = jnp.exp(s - jnp.max(s, axis=1, keepdims=True))
        l = jnp.sum(p, axis=1, keepdims=True)
        o = jnp.dot(p.astype(BF16), kv_ref[:, MEMQ_W + a:MEMQ_W + a + MEM_DH],
                    preferred_element_type=F32) / l
        y_ref[:, a:a + MEM_DH] = (o * _silu(gate_ref[:, a:a + MEM_DH].astype(F32))).astype(y_ref.dtype)


def _mem_attn(qm, kvm, gate, tq):
    nq = SEQ // tq
    return pl.pallas_call(
        _mem_attn_kernel,
        out_shape=jax.ShapeDtypeStruct((N_TOK, MEMQ_W), BF16),
        grid=(BATCH, nq),
        in_specs=[pl.BlockSpec((tq, MEMQ_W), lambda b, i: (b * nq + i, 0)),
                  pl.BlockSpec((N_MEM, 2 * MEMQ_W), lambda b, i: (b, 0)),
                  pl.BlockSpec((tq, MEMQ_W), lambda b, i: (b * nq + i, 0))],
        out_specs=pl.BlockSpec((tq, MEMQ_W), lambda b, i: (b * nq + i, 0)),
        compiler_params=_cparams(("parallel", "parallel")),
        name="mem_attn",
    )(qm, kvm, gate)


def _band_bias(n, lo_off, hi_off):
    i = lax.broadcasted_iota(jnp.int32, (Q_BLOCK, 2 * Q_BLOCK), 0)
    c = lax.broadcasted_iota(jnp.int32, (Q_BLOCK, 2 * Q_BLOCK), 1)
    vis = (c - i >= lo_off) & (c - i <= hi_off) & ((c >= Q_BLOCK) | (n > 0))
    return jnp.where(vis, 0.0, NEG).astype(F32)


def _swa_kernel(q_ref, kvp_ref, kvo_ref, gate_ref, sink_ref, y_ref):
    QB = Q_BLOCK
    G = SWA_HEADS // SWA_KV_HEADS
    n = pl.program_id(1)
    bias = _band_bias(n, QB - (SWA_WINDOW - 1), QB)
    bias4 = jnp.concatenate([bias] * (G // 2), axis=0)
    k2 = jnp.concatenate([kvp_ref[:, :LANES], kvo_ref[:, :LANES]], axis=0).astype(F32)
    v2 = jnp.concatenate([kvp_ref[:, LANES:], kvo_ref[:, LANES:]], axis=0).astype(F32)
    k2r = pltpu.roll(k2, SWA_DH, 1)
    v2r = pltpu.roll(v2, SWA_DH, 1)
    lane = lax.broadcasted_iota(jnp.int32, (2 * QB, LANES), 1)
    lo = lane < SWA_DH
    lane_q = lax.broadcasted_iota(jnp.int32, (QB, LANES), 1)
    lo_q = lane_q < SWA_DH
    vs, ss = [], []
    for kv in range(SWA_KV_HEADS):
        ka, kb_ = (k2, k2r) if kv == 0 else (k2r, k2)
        va, vb_ = (v2, v2r) if kv == 0 else (v2r, v2)
        k_lo = jnp.where(lo, ka, 0.0).astype(BF16)
        k_hi = jnp.where(lo, 0.0, kb_).astype(BF16)
        vs.append((jnp.where(lo, va, 0.0).astype(BF16), jnp.where(lo, 0.0, vb_).astype(BF16)))
        base = kv * (G // 2)
        q4 = jnp.concatenate([q_ref[:, (base + p) * LANES:(base + p + 1) * LANES] for p in range(G // 2)],
                             axis=0)
        ss.append((_qk(q4, k_lo) + bias4, _qk(q4, k_hi) + bias4))
    stats = []
    for s_lo, s_hi in ss:
        m_lo = jnp.max(s_lo, axis=1, keepdims=True)
        m_hi = jnp.max(s_hi, axis=1, keepdims=True)
        p_lo = jnp.exp(s_lo - m_lo)
        p_hi = jnp.exp(s_hi - m_hi)
        stats.append((m_lo, m_hi, p_lo, p_hi, jnp.sum(p_lo, axis=1, keepdims=True),
                      jnp.sum(p_hi, axis=1, keepdims=True)))
    for kv in range(SWA_KV_HEADS):
        m_lo, m_hi, p_lo, p_hi, l_lo, l_hi = stats[kv]
        v_lo, v_hi = vs[kv]
        base = kv * (G // 2)
        o = (jnp.dot(p_lo.astype(BF16), v_lo, preferred_element_type=F32)
             + jnp.dot(p_hi.astype(BF16), v_hi, preferred_element_type=F32))
        lse_lo = m_lo + jnp.log(l_lo)
        lse_hi = m_hi + jnp.log(l_hi)
        for p in range(G // 2):
            r = slice(p * QB, (p + 1) * QB)
            c = slice((base + p) * LANES, (base + p + 1) * LANES)
            lse = jnp.where(lo_q, lse_lo[r], lse_hi[r])
            l = jnp.where(lo_q, l_lo[r], l_hi[r])
            w = _sigmoid(lse - sink_ref[:, c])
            y = (o[r] / l) * w * _silu(gate_ref[:, c].astype(F32))
            y_ref[:, c] = y.astype(y_ref.dtype)


def _swa_attn(qc, kvc, gate, sink_row):
    nb = SEQ // Q_BLOCK
    w = SWA_HEADS * SWA_DH
    return pl.pallas_call(
        _swa_kernel,
        out_shape=jax.ShapeDtypeStruct((N_TOK, w), BF16),
        grid=(BATCH, nb),
        in_specs=[pl.BlockSpec((Q_BLOCK, w), lambda b, n: (b * nb + n, 0)),
                  pl.BlockSpec((Q_BLOCK, 2 * LANES), lambda b, n: (jnp.maximum(b * nb + n - 1, 0), 0)),
                  pl.BlockSpec((Q_BLOCK, 2 * LANES), lambda b, n: (b * nb + n, 0)),
                  pl.BlockSpec((Q_BLOCK, w), lambda b, n: (b * nb + n, 0)),
                  pl.BlockSpec((1, w), lambda b, n: (0, 0))],
        out_specs=pl.BlockSpec((Q_BLOCK, w), lambda b, n: (b * nb + n, 0)),
        compiler_params=_cparams(("parallel", "parallel")),
        name="swa_attn",
    )(qc, kvc, kvc, gate, sink_row)


DIL_BLOCKS_PER_STEP = 2


def _dil_kernel(q_ref, kp_ref, ko_ref, vp_ref, vo_ref, o_ref, lse_ref):
    QB = Q_BLOCK
    R = DIL_BLOCKS_PER_STEP
    n = pl.program_id(2)
    lane = lax.broadcasted_iota(jnp.int32, (QB, LANES), 1)
    tasks = []
    for u in range(R):
        rows = slice(u * QB, (u + 1) * QB)
        bias = _band_bias(n if u == 0 else 1, 0, QB)
        for h in range(DIL_HEADS):
            c = slice(h * DIL_DH, (h + 1) * DIL_DH)
            k_prev = kp_ref[:, c] if u == 0 else ko_ref[(u - 1) * QB:u * QB, c]
            v_prev = vp_ref[:, c] if u == 0 else vo_ref[(u - 1) * QB:u * QB, c]
            k = jnp.concatenate([k_prev, ko_ref[rows, c]], axis=0)
            v = jnp.concatenate([v_prev, vo_ref[rows, c]], axis=0)
            tasks.append((u, h, _qk(q_ref[rows, c], k) + bias, v))
    soft = []
    for u, h, s, v in tasks:
        m = jnp.max(s, axis=1, keepdims=True)
        p = jnp.exp(s - m)
        soft.append((m, p, jnp.sum(p, axis=1, keepdims=True)))
    lse_all = [jnp.zeros((QB, LANES), F32) for _ in range(R)]
    for (u, h, _, v), (m, p, l) in zip(tasks, soft):
        o_ref[u * QB:(u + 1) * QB, h * DIL_DH:(h + 1) * DIL_DH] = (
            jnp.dot(p.astype(BF16), v, preferred_element_type=F32) / l)
        lse_all[u] = jnp.where(lane == h, m + jnp.log(l), lse_all[u])
    for u in range(R):
        lse_ref[u * QB:(u + 1) * QB, :] = lse_all[u]


def _dil_attn(qv, kv, vv, dil):
    w = DIL_HEADS * DIL_DH
    R = DIL_BLOCKS_PER_STEP
    L = SEQ // dil
    ns = L // (R * Q_BLOCK)
    cur = lambda b, r, n: (b * ns + n, r)
    prev = lambda b, r, n: (jnp.maximum((b * ns + n) * R - 1, 0), r)
    blk = (R * Q_BLOCK, w)
    pblk = (Q_BLOCK, w)
    return pl.pallas_call(
        _dil_kernel,
        out_shape=[jax.ShapeDtypeStruct((BATCH * L, dil * w), F32),
                   jax.ShapeDtypeStruct((BATCH * L, dil * LANES), F32)],
        grid=(BATCH, dil, ns),
        in_specs=[pl.BlockSpec(blk, cur), pl.BlockSpec(pblk, prev), pl.BlockSpec(blk, cur),
                  pl.BlockSpec(pblk, prev), pl.BlockSpec(blk, cur)],
        out_specs=[pl.BlockSpec(blk, cur), pl.BlockSpec((R * Q_BLOCK, LANES), cur)],
        compiler_params=_cparams(("parallel", "parallel", "parallel")),
        name=f"dil_attn_d{dil}",
    )(qv, kv, kv, vv, vv)


def _dil_combine_kernel(tm, *refs):
    np_ = len(DIL_PATTERNS)
    o_refs, l_refs = refs[:np_], refs[np_:2 * np_]
    gate_ref, y_ref = refs[2 * np_], refs[2 * np_ + 1]
    stage = refs[2 * np_ + 2:]
    w = DIL_HEADS * DIL_DH
    os_, ls = [], []
    k = 0
    for (_, d), o_ref, l_ref in zip(DIL_PATTERNS, o_refs, l_refs):
        if d == 1:
            os_.append(lambda h, o_ref=o_ref: o_ref[:, h * DIL_DH:(h + 1) * DIL_DH])
            ls.append(l_ref[...])
            continue
        so, sl = stage[2 * k], stage[2 * k + 1]
        k += 1
        for r in range(d):
            for h in range(DIL_HEADS):
                a = r * w + h * DIL_DH
                so[h, pl.ds(r, tm // d, stride=d), :] = o_ref[:, a:a + DIL_DH]
            sl[pl.ds(r, tm // d, stride=d), :] = l_ref[:, r * LANES:(r + 1) * LANES]
        os_.append(lambda h, so=so: so[h])
        ls.append(sl[...])
    mx = functools.reduce(jnp.maximum, ls)
    es = [jnp.exp(x - mx) for x in ls]
    den = functools.reduce(lambda a, b: a + b, es)
    ws = [e / den for e in es]
    lane = lax.broadcasted_iota(jnp.int32, ws[0].shape, 1)
    for h in range(DIL_HEADS):
        c = slice(h * DIL_DH, (h + 1) * DIL_DH)
        wh = [jnp.sum(jnp.where(lane == h, wgt, 0.0), axis=1, keepdims=True) for wgt in ws]
        o = functools.reduce(lambda a, b: a + b, [wh[p] * os_[p](h) for p in range(np_)])
        y_ref[:, c] = (o * _silu(gate_ref[:, c].astype(F32))).astype(y_ref.dtype)


def _dil_combine(outs, lses, gate, tm):
    w = DIL_HEADS * DIL_DH
    in_specs = [pl.BlockSpec((tm // d, d * w), lambda i: (i, 0)) for _, d in DIL_PATTERNS]
    in_specs += [pl.BlockSpec((tm // d, d * LANES), lambda i: (i, 0)) for _, d in DIL_PATTERNS]
    in_specs.append(pl.BlockSpec((tm, w), lambda i: (i, 0)))
    scratch = []
    for _, d in DIL_PATTERNS:
        if d > 1:
            scratch += [pltpu.VMEM((DIL_HEADS, tm, DIL_DH), F32), pltpu.VMEM((tm, LANES), F32)]
    return pl.pallas_call(
        functools.partial(_dil_combine_kernel, tm),
        out_shape=jax.ShapeDtypeStruct((N_TOK, w), BF16),
        grid=(N_TOK // tm,),
        in_specs=in_specs,
        out_specs=pl.BlockSpec((tm, w), lambda i: (i, 0)),
        scratch_shapes=scratch,
        compiler_params=_cparams(("parallel",)),
        name="dil_combine",
    )(*outs, *lses, gate)


def _outproj_kernel(nparts, final, *refs):
    ys, ws = refs[:nparts], refs[nparts:2 * nparts]
    x_ref = refs[2 * nparts]
    o_ref = refs[-1]
    acc = x_ref[...]
    for y, w in zip(ys, ws):
        acc = acc + jnp.dot(y[...], w[...], preferred_element_type=F32)
    if final:
        acc = _rms(acc, refs[2 * nparts + 1][...])
    o_ref[...] = acc


def _outproj(ys, ws, x, final_g, tm):
    n_tok, d = x.shape
    in_specs = [pl.BlockSpec((tm, y.shape[1]), lambda i: (i, 0)) for y in ys]
    in_specs += [pl.BlockSpec(w.shape, lambda i: (0, 0)) for w in ws]
    in_specs.append(pl.BlockSpec((tm, d), lambda i: (i, 0)))
    args = [*ys, *ws, x]
    if final_g is not None:
        in_specs.append(pl.BlockSpec((1, d), lambda i: (0, 0)))
        args.append(final_g.reshape(1, d))
    return pl.pallas_call(
        functools.partial(_outproj_kernel, len(ys), final_g is not None),
        out_shape=jax.ShapeDtypeStruct((n_tok, d), F32),
        grid=(n_tok // tm,),
        in_specs=in_specs,
        out_specs=pl.BlockSpec((tm, d), lambda i: (i, 0)),
        compiler_params=_cparams(("parallel",)),
        name="outproj",
    )(*args)


PROJ_TM = 512
ATTN_TQ = 512
MLA_HEADS_PER_STEP = 2
MOBA_HEADS_PER_STEP = 4


def _even_layer(x, mem2d, norm_g, w_in, q_norm_g, w_uq, kv_norm_g, w_ukv, mem_norm_g, w_mem_kv, w_out):
    n_lat = MLA_Q_RANK + MLA_KV_RANK + MLA_ROPE
    wlat = jnp.pad(w_in[:, :n_lat], ((0, 0), (0, LANES - MLA_ROPE))).astype(BF16)
    wmain = w_in[:, n_lat:].astype(BF16)
    wuq = w_uq.reshape(MLA_Q_RANK, MLA_HEADS, MLA_NOPE + MLA_ROPE)
    wuq = jnp.pad(wuq, ((0, 0), (0, 0), (0, MLA_HW - MLA_NOPE - MLA_ROPE)))
    wuq = wuq.reshape(MLA_Q_RANK, MLA_HEADS * MLA_HW).astype(BF16)
    wukv = w_ukv.reshape(MLA_KV_RANK, MLA_HEADS, MLA_NOPE + MLA_V)
    wukv = jnp.concatenate([wukv[:, :, :MLA_NOPE].reshape(MLA_KV_RANK, -1),
                            wukv[:, :, MLA_NOPE:].reshape(MLA_KV_RANK, -1)], axis=1).astype(BF16)

    cos_a, sin_a = _rope_tables(MLA_ROPE, LANES)
    qa, ka, va = _mla_prep(x, norm_g, wlat, q_norm_g, wuq, kv_norm_g, wukv, cos_a, sin_a, ATTN_TQ)

    rot = MOBA_DH // ROT_FRAC
    cos_b, sin_b = _rope_tables(rot, MOBA_DH)
    sc = MOBA_DH ** -0.5
    wb = MOBA_HEADS * MOBA_DH
    plain = [("plain",)]
    groups_qkv = [
        (wb, [("rope", 0, rot // 2, MOBA_DH, sc)] * MOBA_HEADS, "rows"),
        (wb, [("rope", 0, rot // 2, MOBA_DH, 1.0)] * MOBA_HEADS, "rows"),
        (wb, plain * MOBA_HEADS, ("vT", MOBA_BLOCK)),
    ]
    groups_rest = [
        (MEMQ_W, [("scale", MEM_DH ** -0.5)] * MEM_HEADS, "rows"),
        (MLA_HEADS * MLA_V, plain * MLA_HEADS, "rows"),
        (wb, plain * MOBA_HEADS, "rows"),
        (MEMQ_W, plain * MEM_HEADS, "rows"),
    ]
    qb, kb, vb = _inproj(x, norm_g, wmain[:, :3 * wb], groups_qkv, [cos_b, sin_b], PROJ_TM)
    qm, gate_a, gate_b, gate_m = _inproj(x, norm_g, wmain[:, 3 * wb:], groups_rest, [], PROJ_TM)

    y_a = _mla_attn(qa, ka, va, gate_a, ATTN_TQ, MLA_HEADS_PER_STEP)
    y_b = _moba_attn(qb, kb, vb, gate_b, MOBA_HEADS_PER_STEP)
    kvm = _mem_prep(mem2d, mem_norm_g, w_mem_kv.astype(BF16))
    y_m = _mem_attn(qm, kvm, gate_m, ATTN_TQ)

    wo = w_out.astype(BF16)
    a, b = MLA_HEADS * MLA_V, MLA_HEADS * MLA_V + MOBA_HEADS * MOBA_DH
    return [y_a, y_b, y_m], [wo[:a], wo[a:b], wo[b:]]


def _odd_layer(x, mem2d, norm_g, w_in, sinks, mem_norm_g, w_mem_kv, w_out):
    rot_c = SWA_DH // ROT_FRAC
    rot_d = DIL_DH // ROT_FRAC
    cos_c, sin_c = _rope_tables(rot_c, SWA_DH)
    cos_d, sin_d = _rope_tables(rot_d, DIL_DH)
    wq = SWA_HEADS * SWA_DH
    wd = DIL_HEADS * DIL_DH
    wkv = 2 * SWA_KV_HEADS * SWA_DH
    rc = lambda s: ("rope", 0, rot_c // 2, SWA_DH, s)
    rd = lambda s: ("rope", 1, rot_d // 2, DIL_DH, s)
    plain = [("plain",)]
    groups_qkv = [
        (wq, [rc(SWA_DH ** -0.5)] * (wq // LANES), "rows"),
        (wkv, [rc(1.0), ("plain",)], "rows"),
        (wd, [rd(DIL_DH ** -0.5)] * DIL_HEADS, "dilated"),
        (wd, [rd(1.0)] * DIL_HEADS, "dilated"),
        (wd, plain * DIL_HEADS, "dilated"),
    ]
    groups_rest = [
        (MEMQ_W, [("scale", MEM_DH ** -0.5)] * MEM_HEADS, "rows"),
        (wq, plain * (wq // LANES), "rows"),
        (wd, plain * DIL_HEADS, "rows"),
        (MEMQ_W, plain * MEM_HEADS, "rows"),
    ]
    w_bf = w_in.astype(BF16)
    n_qkv = wq + wkv + 3 * wd
    res = _inproj(x, norm_g, w_bf[:, :n_qkv], groups_qkv, [cos_c, sin_c, cos_d, sin_d], PROJ_TM)
    qc, kvc = res[0], res[1]
    nv = 1 + len(DIL_VIEWS)
    qd, kd, vd = (res[2 + i * nv:2 + (i + 1) * nv] for i in range(3))
    qm, gate_c, gate_d, gate_m = _inproj(x, norm_g, w_bf[:, n_qkv:], groups_rest, [], PROJ_TM)

    sink_row = jnp.repeat(sinks.astype(F32), SWA_DH).reshape(1, wq)
    y_c = _swa_attn(qc, kvc, gate_c, sink_row)
    view_of = {1: 0, **{d: 1 + i for i, d in enumerate(DIL_VIEWS)}}
    outs, lses = zip(*[_dil_attn(qd[view_of[dil]], kd[view_of[dil]], vd[view_of[dil]], dil)
                       for _, dil in DIL_PATTERNS])
    y_d = _dil_combine(outs, lses, gate_d, PROJ_TM)
    kvm = _mem_prep(mem2d, mem_norm_g, w_mem_kv.astype(BF16))
    y_m = _mem_attn(qm, kvm, gate_m, ATTN_TQ)

    wo = w_out.astype(BF16)
    return [y_c, y_d, y_m], [wo[:wq], wo[wq:wq + wd], wo[wq + wd:]]


def kernel(x, mem, ev_norm_g, ev_w_in, ev_q_norm_g, ev_w_uq, ev_kv_norm_g, ev_w_ukv, ev_mem_norm_g,
           ev_w_mem_kv, ev_w_out, od_norm_g, od_w_in, od_sinks, od_mem_norm_g, od_w_mem_kv, od_w_out,
           final_norm_g):
    x2 = x.reshape(N_TOK, D_MODEL)
    mem2d = mem.reshape(BATCH * N_MEM, D_MODEL)
    ys, ws = _even_layer(x2, mem2d, ev_norm_g[0], ev_w_in[0], ev_q_norm_g[0], ev_w_uq[0], ev_kv_norm_g[0],
                         ev_w_ukv[0], ev_mem_norm_g[0], ev_w_mem_kv[0], ev_w_out[0])
    x2 = _outproj(ys, ws, x2, None, PROJ_TM)
    ys, ws = _odd_layer(x2, mem2d, od_norm_g[0], od_w_in[0], od_sinks[0], od_mem_norm_g[0],
                        od_w_mem_kv[0], od_w_out[0])
    x2 = _outproj(ys, ws, x2, final_norm_g, PROJ_TM)
    return x2.reshape(BATCH, SEQ, D_MODEL)
```

```python
import functools

import numpy as np
import jax
import jax.numpy as jnp
from jax import lax
from jax.experimental import pallas as pl
from jax.experimental.pallas import tpu as pltpu

D_MODEL = 2048
BATCH = 4
SEQ = 4096
N_TOK = BATCH * SEQ
N_MEM = 256
ROPE_THETA = 500000.0
ROT_FRAC = 4
EPS = 1e-6

MLA_HEADS = 8
MLA_Q_RANK = 512
MLA_KV_RANK = 256
MLA_NOPE = 128
MLA_ROPE = 64
MLA_V = 128
MOBA_HEADS = 8
MOBA_DH = 128
MOBA_BLOCK = 256
MOBA_TOPK = 3
SWA_HEADS = 16
SWA_KV_HEADS = 2
SWA_DH = 64
SWA_WINDOW = 128
DIL_HEADS = 6
DIL_DH = 128
DIL_PATTERNS = ((128, 1), (512, 4), (2048, 16))
MEM_HEADS = 4
MEM_DH = 128
MEMQ_W = MEM_HEADS * MEM_DH
Q_BLOCK = 128

LANES = 128
VMEM_LIMIT = 52 * 1024 * 1024

NEG = -1e30
BF16 = jnp.bfloat16
F32 = jnp.float32


def _cparams(sem):
    return pltpu.CompilerParams(dimension_semantics=sem, vmem_limit_bytes=VMEM_LIMIT)


def _rms(x, g):
    ms = jnp.mean(x * x, axis=-1, keepdims=True)
    return (x * lax.rsqrt(ms + EPS)) * g


def _silu(g):
    return g / (1.0 + jnp.exp(-g))


def _sigmoid(z):
    return 1.0 / (1.0 + jnp.exp(-z))


def _rope_slab(x, cos_t, sin_t, half, period):
    lane = lax.broadcasted_iota(jnp.int32, x.shape, 1)
    up = pltpu.roll(x, LANES - half, 1)
    dn = pltpu.roll(x, half, 1)
    sw = jnp.where((lane & (period - 1)) < half, up, dn)
    return x * cos_t + sw * sin_t


def _qk(q, k):
    return lax.dot_general(q, k, (((1,), (1,)), ((), ())), preferred_element_type=F32)


def _rope_tables(rot_dim, period):
    half = rot_dim // 2
    inv = 1.0 / (ROPE_THETA ** (jnp.arange(0, rot_dim, 2, dtype=F32) / rot_dim))
    ang = jnp.arange(SEQ, dtype=F32)[:, None] * inv[None, :]
    c, s = jnp.cos(ang), jnp.sin(ang)
    d = np.arange(LANES) % period
    idx = d % half
    cos_t = jnp.where(d < rot_dim, c[:, idx], 1.0)
    sin_t = jnp.where(d < half, -s[:, idx], jnp.where(d < rot_dim, s[:, idx], 0.0))
    return cos_t.astype(F32), sin_t.astype(F32)


PROJ_CHUNK = 512
DIL_VIEWS = tuple(d for _, d in DIL_PATTERNS if d > 1)


def _inproj_kernel(groups, ntab, tm, x_ref, g_ref, *rest):
    tabs = rest[:2 * ntab]
    w_ref = rest[2 * ntab]
    ndil = sum(layout == "dilated" for _, _, layout in groups)
    refs = list(rest[2 * ntab + 1:len(rest) - ndil])
    stage = list(rest[len(rest) - ndil:])
    xn = _rms(x_ref[...], g_ref[...]).astype(BF16)
    c0 = 0
    for ncols, kinds, layout in groups:
        o_ref = refs.pop(0)
        views = [refs.pop(0) for _ in DIL_VIEWS] if layout == "dilated" else []
        st_ref = stage.pop(0) if layout == "dilated" else None
        for s0 in range(0, ncols, PROJ_CHUNK):
            n = min(PROJ_CHUNK, ncols - s0)
            acc = jnp.dot(xn, w_ref[:, c0 + s0:c0 + s0 + n], preferred_element_type=F32)
            for c in range(n // LANES):
                col = s0 + c * LANES
                kind = kinds[col // LANES]
                piece = acc[:, c * LANES:(c + 1) * LANES]
                if kind[0] == "rope":
                    _, ti, half, period, sc = kind
                    piece = _rope_slab(piece, tabs[2 * ti][...], tabs[2 * ti + 1][...], half, period)
                    if sc != 1.0:
                        piece = piece * sc
                elif kind[0] == "scale":
                    piece = piece * kind[1]
                if layout == "rows" or layout == "dilated":
                    o_ref[:, col:col + LANES] = piece.astype(o_ref.dtype)
                    if layout == "dilated":
                        st_ref[col // LANES] = piece
                else:
                    tb = layout[1]
                    for u in range(tm // tb):
                        o_ref[col // LANES, u] = _vt_block(piece[u * tb:(u + 1) * tb, :])
        for d, v_ref in zip(DIL_VIEWS, views):
            for r in range(d):
                for c in range(ncols // LANES):
                    a = r * ncols + c * LANES
                    v_ref[:, a:a + LANES] = st_ref[c, pl.ds(r, tm // d, stride=d), :].astype(v_ref.dtype)
        c0 += ncols


def _inproj(x, g, w, groups, tables, tm):
    n_tok, d = x.shape
    assert sum(nc for nc, _, _ in groups) == w.shape[1]
    seq_tiles = SEQ // tm
    in_specs = [pl.BlockSpec((tm, d), lambda i: (i, 0)),
                pl.BlockSpec((1, d), lambda i: (0, 0))]
    args = [x, g.reshape(1, d)]
    for t in tables:
        in_specs.append(pl.BlockSpec((tm, LANES), lambda i: (i % seq_tiles, 0)))
        args.append(t)
    in_specs.append(pl.BlockSpec(w.shape, lambda i: (0, 0), pipeline_mode=pl.Buffered(1)))
    args.append(w)
    out_specs, out_shapes = [], []
    for nc, _, layout in groups:
        if layout == "rows" or layout == "dilated":
            out_specs.append(pl.BlockSpec((tm, nc), lambda i: (i, 0)))
            out_shapes.append(jax.ShapeDtypeStruct((n_tok, nc), BF16))
            if layout == "dilated":
                for dil in DIL_VIEWS:
                    out_specs.append(pl.BlockSpec((tm // dil, dil * nc), lambda i: (i, 0)))
                    out_shapes.append(jax.ShapeDtypeStruct((n_tok // dil, dil * nc), BF16))
        else:
            tb = layout[1]
            out_specs.append(pl.BlockSpec((nc // LANES, tm // tb, VT_ROWS, tb), lambda i: (0, i, 0, 0)))
            out_shapes.append(jax.ShapeDtypeStruct((nc // LANES, n_tok // tb, VT_ROWS, tb), BF16))
    return pl.pallas_call(
        functools.partial(_inproj_kernel, groups, len(tables) // 2, tm),
        out_shape=out_shapes,
        grid=(n_tok // tm,),
        in_specs=in_specs,
        out_specs=out_specs,
        scratch_shapes=[pltpu.VMEM((nc // LANES, tm, LANES), F32) for nc, _, layout in groups
                        if layout == "dilated"],
        compiler_params=_cparams(("parallel",)),
        name="inproj",
    )(*args)


MLA_HW = 256


def _mla_prep_kernel(scale, x_ref, g_ref, wlat_ref, qg_ref, wuq_ref, kvg_ref, wukv_ref,
                     cos_ref, sin_ref, qa_ref, ka_ref, va_ref):
    xn = _rms(x_ref[...], g_ref[...]).astype(BF16)
    lat = jnp.dot(xn, wlat_ref[...], preferred_element_type=F32)
    cq = _rms(lat[:, :MLA_Q_RANK], qg_ref[...]).astype(BF16)
    ckv = _rms(lat[:, MLA_Q_RANK:MLA_Q_RANK + MLA_KV_RANK], kvg_ref[...]).astype(BF16)
    kpe = lat[:, MLA_Q_RANK + MLA_KV_RANK:]
    cos_t, sin_t = cos_ref[...], sin_ref[...]
    kpe = _rope_slab(kpe, cos_t, sin_t, MLA_ROPE // 2, LANES).astype(BF16)
    q = jnp.dot(cq, wuq_ref[...], preferred_element_type=F32)
    kv = jnp.dot(ckv, wukv_ref[...], preferred_element_type=F32)
    for h in range(MLA_HEADS):
        a = h * MLA_HW
        qa_ref[:, a:a + LANES] = (q[:, a:a + LANES] * scale).astype(BF16)
        qr = _rope_slab(q[:, a + LANES:a + 2 * LANES], cos_t, sin_t, MLA_ROPE // 2, LANES)
        qa_ref[:, a + LANES:a + 2 * LANES] = (qr * scale).astype(BF16)
        ka_ref[:, a:a + LANES] = kv[:, h * LANES:(h + 1) * LANES].astype(BF16)
        ka_ref[:, a + LANES:a + 2 * LANES] = kpe
        v0 = MLA_HEADS * MLA_NOPE + h * MLA_V
        va_ref[h, 0] = _vt_block(kv[:, v0:v0 + MLA_V])


def _mla_prep(x, g, wlat, qg, wuq, kvg, wukv, cos_t, sin_t, tm):
    n_tok, d = x.shape
    seq_tiles = SEQ // tm
    full = lambda a: pl.BlockSpec(a.shape, lambda i: (0, 0))
    qg2, kvg2, g2 = qg.reshape(1, -1), kvg.reshape(1, -1), g.reshape(1, d)
    scale = (MLA_NOPE + MLA_ROPE) ** -0.5 * LOG2E
    return pl.pallas_call(
        functools.partial(_mla_prep_kernel, scale),
        out_shape=[jax.ShapeDtypeStruct((n_tok, MLA_HEADS * MLA_HW), BF16),
                   jax.ShapeDtypeStruct((n_tok, MLA_HEADS * MLA_HW), BF16),
                   jax.ShapeDtypeStruct((MLA_HEADS, n_tok // tm, VT_ROWS, tm), BF16)],
        grid=(n_tok // tm,),
        in_specs=[pl.BlockSpec((tm, d), lambda i: (i, 0)), full(g2), full(wlat), full(qg2), full(wuq),
                  full(kvg2), full(wukv),
                  pl.BlockSpec((tm, LANES), lambda i: (i % seq_tiles, 0)),
                  pl.BlockSpec((tm, LANES), lambda i: (i % seq_tiles, 0))],
        out_specs=[pl.BlockSpec((tm, MLA_HEADS * MLA_HW), lambda i: (i, 0)),
                   pl.BlockSpec((tm, MLA_HEADS * MLA_HW), lambda i: (i, 0)),
                   pl.BlockSpec((MLA_HEADS, 1, VT_ROWS, tm), lambda i: (0, i, 0, 0))],
        compiler_params=_cparams(("parallel",)),
        name="mla_prep",
    )(x, g2, wlat, qg2, wuq, kvg2, wukv, cos_t, sin_t)


LOG2E = 1.4426950408889634
VT_TAIL = 16
VT_ROWS = LANES + VT_TAIL


def _vt_block(v):
    tb = v.shape[0]
    row = lax.broadcasted_iota(jnp.int32, (VT_TAIL, tb), 0)
    tail = jnp.where(row == 0, 1.0, 0.0).astype(F32)
    return jnp.concatenate([v.T, tail], axis=0).astype(BF16)


def _flash_finish(acc_ref, gate, y_dtype):
    l = acc_ref[LANES:LANES + 1, :]
    o = (acc_ref[:LANES, :] / l).T
    return (o * _silu(gate.astype(F32))).astype(y_dtype)


def _flash_causal_t(nh, qi, tq, diag_scores, past_scores, row_bias, vt_fn, s_ref, p_ref, acc_ref):
    acc_ref[...] = jnp.zeros_like(acc_ref)
    p_ref[...] = jnp.zeros_like(p_ref)
    m0 = jnp.full((1, tq), NEG, F32)
    a0 = jnp.ones((1, tq), F32)
    for g in range(nh):
        s_ref[0, g] = diag_scores(g)

    def block_of(t):
        return jnp.where(t == 0, qi, t - 1)

    def softmax(slot, g, t, m):
        s = s_ref[slot, g]
        mb = jnp.max(s, axis=0, keepdims=True)
        if row_bias is None:
            m_new = jnp.maximum(m, mb)
            shift = m_new
        else:
            nb = jnp.where(t == 0, 0.0, row_bias(g, block_of(t)))
            m_new = jnp.maximum(m, mb + nb)
            shift = m_new - nb
        alpha = jnp.exp2(m - m_new)
        p = jnp.exp2((s - shift).astype(BF16))
        return m_new, alpha, p

    def pv(slot, g, j):
        return jnp.dot(vt_fn(g, j), p_ref[slot, g], preferred_element_type=F32)

    def step(t, rd, wr, carry):
        for g in range(nh):
            s_ref[wr, g] = past_scores(g, t)
        pend = [pv(wr, g, block_of(jnp.maximum(t - 1, 0))) for g in range(nh)]
        out = []
        for g in range(nh):
            m, alpha, p = softmax(rd, g, t, carry[2 * g])
            acc_ref[g] = carry[2 * g + 1] * acc_ref[g] + pend[g]
            p_ref[rd, g] = p
            out.extend((m, alpha))
        return tuple(out)

    def pair(u, carry):
        return step(2 * u + 1, 1, 0, step(2 * u, 0, 1, carry))

    carry = lax.fori_loop(0, qi // 2, pair, (m0, a0) * nh)
    carry = lax.cond(qi % 2 == 1, lambda c: step(qi - 1, 0, 1, c), lambda c: c, carry)
    pend = [pv(1 - qi % 2, g, block_of(jnp.maximum(qi - 1, 0))) for g in range(nh)]
    last = [softmax(qi % 2, g, qi, carry[2 * g]) for g in range(nh)]
    for g in range(nh):
        _, alpha, p = last[g]
        fin = jnp.dot(vt_fn(g, block_of(qi)), p, preferred_element_type=F32)
        acc_ref[g] = alpha * (carry[2 * g + 1] * acc_ref[g] + pend[g]) + fin


def _mla_kernel(tq, nh, q_ref, k_ref, vt_ref, gate_ref, y_ref, s_ref, p_ref, acc_ref):
    qi = pl.program_id(2)
    qs = [q_ref[:, g * MLA_HW:(g + 1) * MLA_HW] for g in range(nh)]
    key = lax.broadcasted_iota(jnp.int32, (tq, tq), 0)
    qry = lax.broadcasted_iota(jnp.int32, (tq, tq), 1)

    def scores(g, j):
        off = pl.multiple_of(j * tq, tq)
        return _qk(k_ref[pl.ds(off, tq), g * MLA_HW:(g + 1) * MLA_HW], qs[g])

    _flash_causal_t(nh, qi, tq, lambda g: jnp.where(key <= qry, scores(g, qi), NEG),
                    scores, None, lambda g, j: vt_ref[g, j], s_ref, p_ref, acc_ref)
    for g in range(nh):
        c = slice(g * MLA_V, (g + 1) * MLA_V)
        y_ref[:, c] = _flash_finish(acc_ref.at[g], gate_ref[:, c], y_ref.dtype)


def _mla_attn(qa, ka, vat, gate, tq, nh):
    nq = SEQ // tq
    return pl.pallas_call(
        functools.partial(_mla_kernel, tq, nh),
        out_shape=jax.ShapeDtypeStruct((N_TOK, MLA_HEADS * MLA_V), BF16),
        grid=(BATCH, MLA_HEADS // nh, nq),
        in_specs=[pl.BlockSpec((tq, nh * MLA_HW), lambda b, h, i: (b * nq + i, h)),
                  pl.BlockSpec((SEQ, nh * MLA_HW), lambda b, h, i: (b, h)),
                  pl.BlockSpec((nh, nq, VT_ROWS, tq), lambda b, h, i: (h, b, 0, 0)),
                  pl.BlockSpec((tq, nh * MLA_V), lambda b, h, i: (b * nq + i, h))],
        out_specs=pl.BlockSpec((tq, nh * MLA_V), lambda b, h, i: (b * nq + i, h)),
        scratch_shapes=[pltpu.VMEM((2, nh, tq, tq), F32), pltpu.VMEM((2, nh, tq, tq), BF16),
                        pltpu.VMEM((nh, VT_ROWS, tq), F32)],
        compiler_params=_cparams(("parallel", "parallel", "arbitrary")),
        name="mla_attn",
    )(qa, ka, vat, gate)


def _moba_kernel(nh, q_ref, k_ref, vt_ref, gate_ref, y_ref, km_ref, nb_ref, s_ref, p_ref, acc_ref):
    L = MOBA_BLOCK
    D = MOBA_DH
    nblk = SEQ // L
    qi = pl.program_id(2)

    @pl.when(qi == 0)
    def _():
        for g in range(nh):
            for j in range(nblk):
                blk = k_ref[j * L:(j + 1) * L, g * D:(g + 1) * D].astype(F32)
                km_ref[g, j:j + 1, :] = jnp.sum(blk, axis=0, keepdims=True) * (1.0 / L)

    qs = [q_ref[:, g * D:(g + 1) * D] for g in range(nh)]
    blk_i = lax.broadcasted_iota(jnp.int32, (nblk, L), 0)
    blk_f = blk_i.astype(F32)
    past = blk_i < qi
    for g in range(nh):
        gt = jnp.where(past, _qk(km_ref[g].astype(BF16), qs[g]), -jnp.inf)
        sel = jnp.zeros((nblk, L), jnp.bool_)
        for _ in range(MOBA_TOPK):
            mx = jnp.max(gt, axis=0, keepdims=True)
            first = jnp.min(jnp.where(gt == mx, blk_f, float(nblk)), axis=0, keepdims=True)
            pick = blk_f == first
            sel = jnp.logical_or(sel, pick)
            gt = jnp.where(pick, -jnp.inf, gt)
        nb_ref[g] = jnp.where(jnp.logical_and(sel, past), 0.0, NEG)

    key = lax.broadcasted_iota(jnp.int32, (L, L), 0)
    qry = lax.broadcasted_iota(jnp.int32, (L, L), 1)

    def scores(g, j):
        off = pl.multiple_of(j * L, L)
        return _qk(k_ref[pl.ds(off, L), g * D:(g + 1) * D], qs[g])

    _flash_causal_t(nh, qi, L, lambda g: jnp.where(key <= qry, scores(g, qi), NEG),
                    scores, lambda g, j: nb_ref[g, pl.ds(j, 1), :],
                    lambda g, j: vt_ref[g, j], s_ref, p_ref, acc_ref)
    for g in range(nh):
        c = slice(g * D, (g + 1) * D)
        y_ref[:, c] = _flash_finish(acc_ref.at[g], gate_ref[:, c], y_ref.dtype)


def _moba_attn(qb, kb, vbt, gate, nh):
    L = MOBA_BLOCK
    D = MOBA_DH
    nq = SEQ // L
    return pl.pallas_call(
        functools.partial(_moba_kernel, nh),
        out_shape=jax.ShapeDtypeStruct((N_TOK, MOBA_HEADS * D), BF16),
        grid=(BATCH, MOBA_HEADS // nh, nq),
        in_specs=[pl.BlockSpec((L, nh * D), lambda b, h, i: (b * nq + i, h)),
                  pl.BlockSpec((SEQ, nh * D), lambda b, h, i: (b, h)),
                  pl.BlockSpec((nh, nq, VT_ROWS, L), lambda b, h, i: (h, b, 0, 0)),
                  pl.BlockSpec((L, nh * D), lambda b, h, i: (b * nq + i, h))],
        out_specs=pl.BlockSpec((L, nh * D), lambda b, h, i: (b * nq + i, h)),
        scratch_shapes=[pltpu.VMEM((nh, nq, D), F32), pltpu.VMEM((nh, nq, L), F32),
                        pltpu.VMEM((2, nh, L, L), F32), pltpu.VMEM((2, nh, L, L), BF16),
                        pltpu.VMEM((nh, VT_ROWS, L), F32)],
        compiler_params=_cparams(("parallel", "parallel", "arbitrary")),
        name="moba_attn",
    )(qb, kb, vbt, gate)


def _mem_prep_kernel(mem_ref, g_ref, w_ref, kv_ref):
    mn = _rms(mem_ref[...], g_ref[...]).astype(BF16)
    kv_ref[...] = jnp.dot(mn, w_ref[...], preferred_element_type=F32).astype(BF16)


def _mem_prep(mem2d, g, w):
    g2 = g.reshape(1, -1)
    return pl.pallas_call(
        _mem_prep_kernel,
        out_shape=jax.ShapeDtypeStruct((BATCH * N_MEM, 2 * MEMQ_W), BF16),
        grid=(BATCH,),
        in_specs=[pl.BlockSpec((N_MEM, D_MODEL), lambda b: (b, 0)),
                  pl.BlockSpec(g2.shape, lambda b: (0, 0)),
                  pl.BlockSpec(w.shape, lambda b: (0, 0))],
        out_specs=pl.BlockSpec((N_MEM, 2 * MEMQ_W), lambda b: (b, 0)),
        compiler_params=_cparams(("parallel",)),
        name="mem_prep",
    )(mem2d, g2, w)


def _mem_attn_kernel(q_ref, kv_ref, gate_ref, y_ref):
    for h in range(MEM_HEADS):
        a = h * MEM_DH
        s = _qk(q_ref[:, a:a + MEM_DH], kv_ref[:, a:a + MEM_DH])
        p = jnp.exp(s - jnp.max(s, axis=1, keepdims=True))
        l = jnp.sum(p, axis=1, keepdims=True)
        o = jnp.dot(p.astype(BF16), kv_ref[:, MEMQ_W + a:MEMQ_W + a + MEM_DH],
                    preferred_element_type=F32) / l
        y_ref[:, a:a + MEM_DH] = (o * _silu(gate_ref[:, a:a + MEM_DH].astype(F32))).astype(y_ref.dtype)


def _mem_attn(qm, kvm, gate, tq):
    nq = SEQ // tq
    return pl.pallas_call(
        _mem_attn_kernel,
        out_shape=jax.ShapeDtypeStruct((N_TOK, MEMQ_W), BF16),
        grid=(BATCH, nq),
        in_specs=[pl.BlockSpec((tq, MEMQ_W), lambda b, i: (b * nq + i, 0)),
                  pl.BlockSpec((N_MEM, 2 * MEMQ_W), lambda b, i: (b, 0)),
                  pl.BlockSpec((tq, MEMQ_W), lambda b, i: (b * nq + i, 0))],
        out_specs=pl.BlockSpec((tq, MEMQ_W), lambda b, i: (b * nq + i, 0)),
        compiler_params=_cparams(("parallel", "parallel")),
        name="mem_attn",
    )(qm, kvm, gate)


def _band_bias(n, lo_off, hi_off):
    i = lax.broadcasted_iota(jnp.int32, (Q_BLOCK, 2 * Q_BLOCK), 0)
    c = lax.broadcasted_iota(jnp.int32, (Q_BLOCK, 2 * Q_BLOCK), 1)
    vis = (c - i >= lo_off) & (c - i <= hi_off) & ((c >= Q_BLOCK) | (n > 0))
    return jnp.where(vis, 0.0, NEG).astype(F32)


def _swa_kernel(q_ref, kvp_ref, kvo_ref, gate_ref, sink_ref, y_ref):
    QB = Q_BLOCK
    G = SWA_HEADS // SWA_KV_HEADS
    n = pl.program_id(1)
    bias = _band_bias(n, QB - (SWA_WINDOW - 1), QB)
    bias4 = jnp.concatenate([bias] * (G // 2), axis=0)
    k2 = jnp.concatenate([kvp_ref[:, :LANES], kvo_ref[:, :LANES]], axis=0).astype(F32)
    v2 = jnp.concatenate([kvp_ref[:, LANES:], kvo_ref[:, LANES:]], axis=0).astype(F32)
    k2r = pltpu.roll(k2, SWA_DH, 1)
    v2r = pltpu.roll(v2, SWA_DH, 1)
    lane = lax.broadcasted_iota(jnp.int32, (2 * QB, LANES), 1)
    lo = lane < SWA_DH
    lane_q = lax.broadcasted_iota(jnp.int32, (QB, LANES), 1)
    lo_q = lane_q < SWA_DH
    vs, ss = [], []
    for kv in range(SWA_KV_HEADS):
        ka, kb_ = (k2, k2r) if kv == 0 else (k2r, k2)
        va, vb_ = (v2, v2r) if kv == 0 else (v2r, v2)
        k_lo = jnp.where(lo, ka, 0.0).astype(BF16)
        k_hi = jnp.where(lo, 0.0, kb_).astype(BF16)
        vs.append((jnp.where(lo, va, 0.0).astype(BF16), jnp.where(lo, 0.0, vb_).astype(BF16)))
        base = kv * (G // 2)
        q4 = jnp.concatenate([q_ref[:, (base + p) * LANES:(base + p + 1) * LANES] for p in range(G // 2)],
                             axis=0)
        ss.append((_qk(q4, k_lo) + bias4, _qk(q4, k_hi) + bias4))
    stats = []
    for s_lo, s_hi in ss:
        m_lo = jnp.max(s_lo, axis=1, keepdims=True)
        m_hi = jnp.max(s_hi, axis=1, keepdims=True)
        p_lo = jnp.exp(s_lo - m_lo)
        p_hi = jnp.exp(s_hi - m_hi)
        stats.append((m_lo, m_hi, p_lo, p_hi, jnp.sum(p_lo, axis=1, keepdims=True),
                      jnp.sum(p_hi, axis=1, keepdims=True)))
    for kv in range(SWA_KV_HEADS):
        m_lo, m_hi, p_lo, p_hi, l_lo, l_hi = stats[kv]
        v_lo, v_hi = vs[kv]
        base = kv * (G // 2)
        o = (jnp.dot(p_lo.astype(BF16), v_lo, preferred_element_type=F32)
             + jnp.dot(p_hi.astype(BF16), v_hi, preferred_element_type=F32))
        lse_lo = m_lo + jnp.log(l_lo)
        lse_hi = m_hi + jnp.log(l_hi)
        for p in range(G // 2):
            r = slice(p * QB, (p + 1) * QB)
            c = slice((base + p) * LANES, (base + p + 1) * LANES)
            lse = jnp.where(lo_q, lse_lo[r], lse_hi[r])
            l = jnp.where(lo_q, l_lo[r], l_hi[r])
            w = _sigmoid(lse - sink_ref[:, c])
            y = (o[r] / l) * w * _silu(gate_ref[:, c].astype(F32))
            y_ref[:, c] = y.astype(y_ref.dtype)


def _swa_attn(qc, kvc, gate, sink_row):
    nb = SEQ // Q_BLOCK
    w = SWA_HEADS * SWA_DH
    return pl.pallas_call(
        _swa_kernel,
        out_shape=jax.ShapeDtypeStruct((N_TOK, w), BF16),
        grid=(BATCH, nb),
        in_specs=[pl.BlockSpec((Q_BLOCK, w), lambda b, n: (b * nb + n, 0)),
                  pl.BlockSpec((Q_BLOCK, 2 * LANES), lambda b, n: (jnp.maximum(b * nb + n - 1, 0), 0)),
                  pl.BlockSpec((Q_BLOCK, 2 * LANES), lambda b, n: (b * nb + n, 0)),
                  pl.BlockSpec((Q_BLOCK, w), lambda b, n: (b * nb + n, 0)),
                  pl.BlockSpec((1, w), lambda b, n: (0, 0))],
        out_specs=pl.BlockSpec((Q_BLOCK, w), lambda b, n: (b * nb + n, 0)),
        compiler_params=_cparams(("parallel", "parallel")),
        name="swa_attn",
    )(qc, kvc, kvc, gate, sink_row)


DIL_BLOCKS_PER_STEP = 2


def _dil_kernel(q_ref, kp_ref, ko_ref, vp_ref, vo_ref, o_ref, lse_ref):
    QB = Q_BLOCK
    R = DIL_BLOCKS_PER_STEP
    n = pl.program_id(2)
    lane = lax.broadcasted_iota(jnp.int32, (QB, LANES), 1)
    tasks = []
    for u in range(R):
        rows = slice(u * QB, (u + 1) * QB)
        bias = _band_bias(n if u == 0 else 1, 0, QB)
        for h in range(DIL_HEADS):
            c = slice(h * DIL_DH, (h + 1) * DIL_DH)
            k_prev = kp_ref[:, c] if u == 0 else ko_ref[(u - 1) * QB:u * QB, c]
            v_prev = vp_ref[:, c] if u == 0 else vo_ref[(u - 1) * QB:u * QB, c]
            k = jnp.concatenate([k_prev, ko_ref[rows, c]], axis=0)
            v = jnp.concatenate([v_prev, vo_ref[rows, c]], axis=0)
            tasks.append((u, h, _qk(q_ref[rows, c], k) + bias, v))
    soft = []
    for u, h, s, v in tasks:
        m = jnp.max(s, axis=1, keepdims=True)
        p = jnp.exp(s - m)
        soft.append((m, p, jnp.sum(p, axis=1, keepdims=True)))
    lse_all = [jnp.zeros((QB, LANES), F32) for _ in range(R)]
    for (u, h, _, v), (m, p, l) in zip(tasks, soft):
        o_ref[u * QB:(u + 1) * QB, h * DIL_DH:(h + 1) * DIL_DH] = (
            jnp.dot(p.astype(BF16), v, preferred_element_type=F32) / l)
        lse_all[u] = jnp.where(lane == h, m + jnp.log(l), lse_all[u])
    for u in range(R):
        lse_ref[u * QB:(u + 1) * QB, :] = lse_all[u]


def _dil_attn(qv, kv, vv, dil):
    w = DIL_HEADS * DIL_DH
    R = DIL_BLOCKS_PER_STEP
    L = SEQ // dil
    ns = L // (R * Q_BLOCK)
    cur = lambda b, r, n: (b * ns + n, r)
    prev = lambda b, r, n: (jnp.maximum((b * ns + n) * R - 1, 0), r)
    blk = (R * Q_BLOCK, w)
    pblk = (Q_BLOCK, w)
    return pl.pallas_call(
        _dil_kernel,
        out_shape=[jax.ShapeDtypeStruct((BATCH * L, dil * w), F32),
                   jax.ShapeDtypeStruct((BATCH * L, dil * LANES), F32)],
        grid=(BATCH, dil, ns),
        in_specs=[pl.BlockSpec(blk, cur), pl.BlockSpec(pblk, prev), pl.BlockSpec(blk, cur),
                  pl.BlockSpec(pblk, prev), pl.BlockSpec(blk, cur)],
        out_specs=[pl.BlockSpec(blk, cur), pl.BlockSpec((R * Q_BLOCK, LANES), cur)],
        compiler_params=_cparams(("parallel", "parallel", "parallel")),
        name=f"dil_attn_d{dil}",
    )(qv, kv, kv, vv, vv)


def _dil_combine_kernel(tm, *refs):
    np_ = len(DIL_PATTERNS)
    o_refs, l_refs = refs[:np_], refs[np_:2 * np_]
    gate_ref, y_ref = refs[2 * np_], refs[2 * np_ + 1]
    stage = refs[2 * np_ + 2:]
    w = DIL_HEADS * DIL_DH
    os_, ls = [], []
    k = 0
    for (_, d), o_ref, l_ref in zip(DIL_PATTERNS, o_refs, l_refs):
        if d == 1:
            os_.append(lambda h, o_ref=o_ref: o_ref[:, h * DIL_DH:(h + 1) * DIL_DH])
            ls.append(l_ref[...])
            continue
        so, sl = stage[2 * k], stage[2 * k + 1]
        k += 1
        for r in range(d):
            for h in range(DIL_HEADS):
                a = r * w + h * DIL_DH
                so[h, pl.ds(r, tm // d, stride=d), :] = o_ref[:, a:a + DIL_DH]
            sl[pl.ds(r, tm // d, stride=d), :] = l_ref[:, r * LANES:(r + 1) * LANES]
        os_.append(lambda h, so=so: so[h])
        ls.append(sl[...])
    mx = functools.reduce(jnp.maximum, ls)
    es = [jnp.exp(x - mx) for x in ls]
    den = functools.reduce(lambda a, b: a + b, es)
    ws = [e / den for e in es]
    lane = lax.broadcasted_iota(jnp.int32, ws[0].shape, 1)
    for h in range(DIL_HEADS):
        c = slice(h * DIL_DH, (h + 1) * DIL_DH)
        wh = [jnp.sum(jnp.where(lane == h, wgt, 0.0), axis=1, keepdims=True) for wgt in ws]
        o = functools.reduce(lambda a, b: a + b, [wh[p] * os_[p](h) for p in range(np_)])
        y_ref[:, c] = (o * _silu(gate_ref[:, c].astype(F32))).astype(y_ref.dtype)


def _dil_combine(outs, lses, gate, tm):
    w = DIL_HEADS * DIL_DH
    in_specs = [pl.BlockSpec((tm // d, d * w), lambda i: (i, 0)) for _, d in DIL_PATTERNS]
    in_specs += [pl.BlockSpec((tm // d, d * LANES), lambda i: (i, 0)) for _, d in DIL_PATTERNS]
    in_specs.append(pl.BlockSpec((tm, w), lambda i: (i, 0)))
    scratch = []
    for _, d in DIL_PATTERNS:
        if d > 1:
            scratch += [pltpu.VMEM((DIL_HEADS, tm, DIL_DH), F32), pltpu.VMEM((tm, LANES), F32)]
    return pl.pallas_call(
        functools.partial(_dil_combine_kernel, tm),
        out_shape=jax.ShapeDtypeStruct((N_TOK, w), BF16),
        grid=(N_TOK // tm,),
        in_specs=in_specs,
        out_specs=pl.BlockSpec((tm, w), lambda i: (i, 0)),
        scratch_shapes=scratch,
        compiler_params=_cparams(("parallel",)),
        name="dil_combine",
    )(*outs, *lses, gate)


def _outproj_kernel(nparts, final, *refs):
    ys, ws = refs[:nparts], refs[nparts:2 * nparts]
    x_ref = refs[2 * nparts]
    o_ref = refs[-1]
    acc = x_ref[...]
    for y, w in zip(ys, ws):
        acc = acc + jnp.dot(y[...], w[...], preferred_element_type=F32)
    if final:
        acc = _rms(acc, refs[2 * nparts + 1][...])
    o_ref[...] = acc


def _outproj(ys, ws, x, final_g, tm):
    n_tok, d = x.shape
    in_specs = [pl.BlockSpec((tm, y.shape[1]), lambda i: (i, 0)) for y in ys]
    in_specs += [pl.BlockSpec(w.shape, lambda i: (0, 0)) for w in ws]
    in_specs.append(pl.BlockSpec((tm, d), lambda i: (i, 0)))
    args = [*ys, *ws, x]
    if final_g is not None:
        in_specs.append(pl.BlockSpec((1, d), lambda i: (0, 0)))
        args.append(final_g.reshape(1, d))
    return pl.pallas_call(
        functools.partial(_outproj_kernel, len(ys), final_g is not None),
        out_shape=jax.ShapeDtypeStruct((n_tok, d), F32),
        grid=(n_tok // tm,),
        in_specs=in_specs,
        out_specs=pl.BlockSpec((tm, d), lambda i: (i, 0)),
        compiler_params=_cparams(("parallel",)),
        name="outproj",
    )(*args)


PROJ_TM = 512
ATTN_TQ = 512
MLA_HEADS_PER_STEP = 2
MOBA_HEADS_PER_STEP = 4


def _even_layer(x, mem2d, norm_g, w_in, q_norm_g, w_uq, kv_norm_g, w_ukv, mem_norm_g, w_mem_kv, w_out):
    n_lat = MLA_Q_RANK + MLA_KV_RANK + MLA_ROPE
    wlat = jnp.pad(w_in[:, :n_lat], ((0, 0), (0, LANES - MLA_ROPE))).astype(BF16)
    wmain = w_in[:, n_lat:].astype(BF16)
    wuq = w_uq.reshape(MLA_Q_RANK, MLA_HEADS, MLA_NOPE + MLA_ROPE)
    wuq = jnp.pad(wuq, ((0, 0), (0, 0), (0, MLA_HW - MLA_NOPE - MLA_ROPE)))
    wuq = wuq.reshape(MLA_Q_RANK, MLA_HEADS * MLA_HW).astype(BF16)
    wukv = w_ukv.reshape(MLA_KV_RANK, MLA_HEADS, MLA_NOPE + MLA_V)
    wukv = jnp.concatenate([wukv[:, :, :MLA_NOPE].reshape(MLA_KV_RANK, -1),
                            wukv[:, :, MLA_NOPE:].reshape(MLA_KV_RANK, -1)], axis=1).astype(BF16)

    cos_a, sin_a = _rope_tables(MLA_ROPE, LANES)
    qa, ka, va = _mla_prep(x, norm_g, wlat, q_norm_g, wuq, kv_norm_g, wukv, cos_a, sin_a, ATTN_TQ)

    rot = MOBA_DH // ROT_FRAC
    cos_b, sin_b = _rope_tables(rot, MOBA_DH)
    sc = MOBA_DH ** -0.5 * LOG2E
    wb = MOBA_HEADS * MOBA_DH
    plain = [("plain",)]
    groups_qkv = [
        (wb, [("rope", 0, rot // 2, MOBA_DH, sc)] * MOBA_HEADS, "rows"),
        (wb, [("rope", 0, rot // 2, MOBA_DH, 1.0)] * MOBA_HEADS, "rows"),
        (wb, plain * MOBA_HEADS, ("vT", MOBA_BLOCK)),
    ]
    groups_rest = [
        (MEMQ_W, [("scale", MEM_DH ** -0.5)] * MEM_HEADS, "rows"),
        (MLA_HEADS * MLA_V, plain * MLA_HEADS, "rows"),
        (wb, plain * MOBA_HEADS, "rows"),
        (MEMQ_W, plain * MEM_HEADS, "rows"),
    ]
    qb, kb, vb = _inproj(x, norm_g, wmain[:, :3 * wb], groups_qkv, [cos_b, sin_b], PROJ_TM)
    qm, gate_a, gate_b, gate_m = _inproj(x, norm_g, wmain[:, 3 * wb:], groups_rest, [], PROJ_TM)

    y_a = _mla_attn(qa, ka, va, gate_a, ATTN_TQ, MLA_HEADS_PER_STEP)
    y_b = _moba_attn(qb, kb, vb, gate_b, MOBA_HEADS_PER_STEP)
    kvm = _mem_prep(mem2d, mem_norm_g, w_mem_kv.astype(BF16))
    y_m = _mem_attn(qm, kvm, gate_m, ATTN_TQ)

    wo = w_out.astype(BF16)
    a, b = MLA_HEADS * MLA_V, MLA_HEADS * MLA_V + MOBA_HEADS * MOBA_DH
    return [y_a, y_b, y_m], [wo[:a], wo[a:b], wo[b:]]


def _odd_layer(x, mem2d, norm_g, w_in, sinks, mem_norm_g, w_mem_kv, w_out):
    rot_c = SWA_DH // ROT_FRAC
    rot_d = DIL_DH // ROT_FRAC
    cos_c, sin_c = _rope_tables(rot_c, SWA_DH)
    cos_d, sin_d = _rope_tables(rot_d, DIL_DH)
    wq = SWA_HEADS * SWA_DH
    wd = DIL_HEADS * DIL_DH
    wkv = 2 * SWA_KV_HEADS * SWA_DH
    rc = lambda s: ("rope", 0, rot_c // 2, SWA_DH, s)
    rd = lambda s: ("rope", 1, rot_d // 2, DIL_DH, s)
    plain = [("plain",)]
    groups_qkv = [
        (wq, [rc(SWA_DH ** -0.5)] * (wq // LANES), "rows"),
        (wkv, [rc(1.0), ("plain",)], "rows"),
        (wd, [rd(DIL_DH ** -0.5)] * DIL_HEADS, "dilated"),
        (wd, [rd(1.0)] * DIL_HEADS, "dilated"),
        (wd, plain * DIL_HEADS, "dilated"),
    ]
    groups_rest = [
        (MEMQ_W, [("scale", MEM_DH ** -0.5)] * MEM_HEADS, "rows"),
        (wq, plain * (wq // LANES), "rows"),
        (wd, plain * DIL_HEADS, "rows"),
        (MEMQ_W, plain * MEM_HEADS, "rows"),
    ]
    w_bf = w_in.astype(BF16)
    n_qkv = wq + wkv + 3 * wd
    res = _inproj(x, norm_g, w_bf[:, :n_qkv], groups_qkv, [cos_c, sin_c, cos_d, sin_d], PROJ_TM)
    qc, kvc = res[0], res[1]
    nv = 1 + len(DIL_VIEWS)
    qd, kd, vd = (res[2 + i * nv:2 + (i + 1) * nv] for i in range(3))
    qm, gate_c, gate_d, gate_m = _inproj(x, norm_g, w_bf[:, n_qkv:], groups_rest, [], PROJ_TM)

    sink_row = jnp.repeat(sinks.astype(F32), SWA_DH).reshape(1, wq)
    y_c = _swa_attn(qc, kvc, gate_c, sink_row)
    view_of = {1: 0, **{d: 1 + i for i, d in enumerate(DIL_VIEWS)}}
    outs, lses = zip(*[_dil_attn(qd[view_of[dil]], kd[view_of[dil]], vd[view_of[dil]], dil)
                       for _, dil in DIL_PATTERNS])
    y_d = _dil_combine(outs, lses, gate_d, PROJ_TM)
    kvm = _mem_prep(mem2d, mem_norm_g, w_mem_kv.astype(BF16))
    y_m = _mem_attn(qm, kvm, gate_m, ATTN_TQ)

    wo = w_out.astype(BF16)
    return [y_c, y_d, y_m], [wo[:wq], wo[wq:wq + wd], wo[wq + wd:]]


def kernel(x, mem, ev_norm_g, ev_w_in, ev_q_norm_g, ev_w_uq, ev_kv_norm_g, ev_w_ukv, ev_mem_norm_g,
           ev_w_mem_kv, ev_w_out, od_norm_g, od_w_in, od_sinks, od_mem_norm_g, od_w_mem_kv, od_w_out,
           final_norm_g):
    x2 = x.reshape(N_TOK, D_MODEL)
    mem2d = mem.reshape(BATCH * N_MEM, D_MODEL)
    ys, ws = _even_layer(x2, mem2d, ev_norm_g[0], ev_w_in[0], ev_q_norm_g[0], ev_w_uq[0], ev_kv_norm_g[0],
                         ev_w_ukv[0], ev_mem_norm_g[0], ev_w_mem_kv[0], ev_w_out[0])
    x2 = _outproj(ys, ws, x2, None, PROJ_TM)
    ys, ws = _odd_layer(x2, mem2d, od_norm_g[0], od_w_in[0], od_sinks[0], od_mem_norm_g[0],
                        od_w_mem_kv[0], od_w_out[0])
    x2 = _outproj(ys, ws, x2, final_norm_g, PROJ_TM)
    return x2.reshape(BATCH, SEQ, D_MODEL)
```

```python
import functools

import numpy as np
import jax
import jax.numpy as jnp
from jax import lax
from jax.experimental import pallas as pl
from jax.experimental.pallas import tpu as pltpu

D_MODEL = 2048
BATCH = 4
SEQ = 4096
N_TOK = BATCH * SEQ
N_MEM = 256
ROPE_THETA = 500000.0
ROT_FRAC = 4
EPS = 1e-6

MLA_HEADS = 8
MLA_Q_RANK = 512
MLA_KV_RANK = 256
MLA_NOPE = 128
MLA_ROPE = 64
MLA_V = 128
MOBA_HEADS = 8
MOBA_DH = 128
MOBA_BLOCK = 256
MOBA_TOPK = 3
SWA_HEADS = 16
SWA_KV_HEADS = 2
SWA_DH = 64
SWA_WINDOW = 128
DIL_HEADS = 6
DIL_DH = 128
DIL_PATTERNS = ((128, 1), (512, 4), (2048, 16))
MEM_HEADS = 4
MEM_DH = 128
MEMQ_W = MEM_HEADS * MEM_DH
Q_BLOCK = 128

LANES = 128
VMEM_LIMIT = 52 * 1024 * 1024

NEG = -1e30
BF16 = jnp.bfloat16
F32 = jnp.float32


def _cparams(sem):
    return pltpu.CompilerParams(dimension_semantics=sem, vmem_limit_bytes=VMEM_LIMIT)


def _rms(x, g):
    ms = jnp.mean(x * x, axis=-1, keepdims=True)
    return (x * lax.rsqrt(ms + EPS)) * g


def _silu(g):
    return g / (1.0 + jnp.exp(-g))


def _sigmoid(z):
    return 1.0 / (1.0 + jnp.exp(-z))


def _rope_slab(x, cos_t, sin_t, half, period):
    lane = lax.broadcasted_iota(jnp.int32, x.shape, 1)
    up = pltpu.roll(x, LANES - half, 1)
    dn = pltpu.roll(x, half, 1)
    sw = jnp.where((lane & (period - 1)) < half, up, dn)
    return x * cos_t + sw * sin_t


def _qk(q, k):
    return lax.dot_general(q, k, (((1,), (1,)), ((), ())), preferred_element_type=F32)


def _rope_tables(rot_dim, period):
    half = rot_dim // 2
    inv = 1.0 / (ROPE_THETA ** (jnp.arange(0, rot_dim, 2, dtype=F32) / rot_dim))
    ang = jnp.arange(SEQ, dtype=F32)[:, None] * inv[None, :]
    c, s = jnp.cos(ang), jnp.sin(ang)
    d = np.arange(LANES) % period
    idx = d % half
    cos_t = jnp.where(d < rot_dim, c[:, idx], 1.0)
    sin_t = jnp.where(d < half, -s[:, idx], jnp.where(d < rot_dim, s[:, idx], 0.0))
    return cos_t.astype(F32), sin_t.astype(F32)


PROJ_CHUNK = 512
DIL_VIEWS = tuple(d for _, d in DIL_PATTERNS if d > 1)


def _inproj_kernel(groups, ntab, tm, x_ref, g_ref, *rest):
    tabs = rest[:2 * ntab]
    w_ref = rest[2 * ntab]
    ndil = sum(layout == "dilated" for _, _, layout in groups)
    refs = list(rest[2 * ntab + 1:len(rest) - ndil])
    stage = list(rest[len(rest) - ndil:])
    xn = _rms(x_ref[...], g_ref[...]).astype(BF16)
    c0 = 0
    for ncols, kinds, layout in groups:
        o_ref = refs.pop(0)
        views = [refs.pop(0) for _ in DIL_VIEWS] if layout == "dilated" else []
        st_ref = stage.pop(0) if layout == "dilated" else None
        for s0 in range(0, ncols, PROJ_CHUNK):
            n = min(PROJ_CHUNK, ncols - s0)
            acc = jnp.dot(xn, w_ref[:, c0 + s0:c0 + s0 + n], preferred_element_type=F32)
            for c in range(n // LANES):
                col = s0 + c * LANES
                kind = kinds[col // LANES]
                piece = acc[:, c * LANES:(c + 1) * LANES]
                if kind[0] == "rope":
                    _, ti, half, period, sc = kind
                    piece = _rope_slab(piece, tabs[2 * ti][...], tabs[2 * ti + 1][...], half, period)
                    if sc != 1.0:
                        piece = piece * sc
                elif kind[0] == "scale":
                    piece = piece * kind[1]
                if layout == "rows" or layout == "dilated":
                    o_ref[:, col:col + LANES] = piece.astype(o_ref.dtype)
                    if layout == "dilated":
                        st_ref[col // LANES] = piece
                else:
                    tb = layout[1]
                    for u in range(tm // tb):
                        o_ref[col // LANES, u] = _vt_block(piece[u * tb:(u + 1) * tb, :])
        for d, v_ref in zip(DIL_VIEWS, views):
            for r in range(d):
                for c in range(ncols // LANES):
                    a = r * ncols + c * LANES
                    v_ref[:, a:a + LANES] = st_ref[c, pl.ds(r, tm // d, stride=d), :].astype(v_ref.dtype)
        c0 += ncols


def _inproj(x, g, w, groups, tables, tm):
    n_tok, d = x.shape
    assert sum(nc for nc, _, _ in groups) == w.shape[1]
    seq_tiles = SEQ // tm
    in_specs = [pl.BlockSpec((tm, d), lambda i: (i, 0)),
                pl.BlockSpec((1, d), lambda i: (0, 0))]
    args = [x, g.reshape(1, d)]
    for t in tables:
        in_specs.append(pl.BlockSpec((tm, LANES), lambda i: (i % seq_tiles, 0)))
        args.append(t)
    in_specs.append(pl.BlockSpec(w.shape, lambda i: (0, 0), pipeline_mode=pl.Buffered(1)))
    args.append(w)
    out_specs, out_shapes = [], []
    for nc, _, layout in groups:
        if layout == "rows" or layout == "dilated":
            out_specs.append(pl.BlockSpec((tm, nc), lambda i: (i, 0)))
            out_shapes.append(jax.ShapeDtypeStruct((n_tok, nc), BF16))
            if layout == "dilated":
                for dil in DIL_VIEWS:
                    out_specs.append(pl.BlockSpec((tm // dil, dil * nc), lambda i: (i, 0)))
                    out_shapes.append(jax.ShapeDtypeStruct((n_tok // dil, dil * nc), BF16))
        else:
            tb = layout[1]
            out_specs.append(pl.BlockSpec((nc // LANES, tm // tb, VT_ROWS, tb), lambda i: (0, i, 0, 0)))
            out_shapes.append(jax.ShapeDtypeStruct((nc // LANES, n_tok // tb, VT_ROWS, tb), BF16))
    return pl.pallas_call(
        functools.partial(_inproj_kernel, groups, len(tables) // 2, tm),
        out_shape=out_shapes,
        grid=(n_tok // tm,),
        in_specs=in_specs,
        out_specs=out_specs,
        scratch_shapes=[pltpu.VMEM((nc // LANES, tm, LANES), F32) for nc, _, layout in groups
                        if layout == "dilated"],
        compiler_params=_cparams(("parallel",)),
        name="inproj",
    )(*args)


MLA_HW = 256


def _mla_prep_kernel(scale, x_ref, g_ref, wlat_ref, qg_ref, wuq_ref, kvg_ref, wukv_ref,
                     cos_ref, sin_ref, qa_ref, ka_ref, va_ref):
    xn = _rms(x_ref[...], g_ref[...]).astype(BF16)
    lat = jnp.dot(xn, wlat_ref[...], preferred_element_type=F32)
    cq = _rms(lat[:, :MLA_Q_RANK], qg_ref[...]).astype(BF16)
    ckv = _rms(lat[:, MLA_Q_RANK:MLA_Q_RANK + MLA_KV_RANK], kvg_ref[...]).astype(BF16)
    kpe = lat[:, MLA_Q_RANK + MLA_KV_RANK:]
    cos_t, sin_t = cos_ref[...], sin_ref[...]
    kpe = _rope_slab(kpe, cos_t, sin_t, MLA_ROPE // 2, LANES).astype(BF16)
    q = jnp.dot(cq, wuq_ref[...], preferred_element_type=F32)
    kv = jnp.dot(ckv, wukv_ref[...], preferred_element_type=F32)
    for h in range(MLA_HEADS):
        a = h * MLA_HW
        qa_ref[:, a:a + LANES] = (q[:, a:a + LANES] * scale).astype(BF16)
        qr = _rope_slab(q[:, a + LANES:a + 2 * LANES], cos_t, sin_t, MLA_ROPE // 2, LANES)
        qa_ref[:, a + LANES:a + 2 * LANES] = (qr * scale).astype(BF16)
        ka_ref[:, a:a + LANES] = kv[:, h * LANES:(h + 1) * LANES].astype(BF16)
        ka_ref[:, a + LANES:a + 2 * LANES] = kpe
        v0 = MLA_HEADS * MLA_NOPE + h * MLA_V
        va_ref[h, 0] = _vt_block(kv[:, v0:v0 + MLA_V])


def _mla_prep(x, g, wlat, qg, wuq, kvg, wukv, cos_t, sin_t, tm):
    n_tok, d = x.shape
    seq_tiles = SEQ // tm
    full = lambda a: pl.BlockSpec(a.shape, lambda i: (0, 0))
    qg2, kvg2, g2 = qg.reshape(1, -1), kvg.reshape(1, -1), g.reshape(1, d)
    scale = (MLA_NOPE + MLA_ROPE) ** -0.5 * LOG2E
    return pl.pallas_call(
        functools.partial(_mla_prep_kernel, scale),
        out_shape=[jax.ShapeDtypeStruct((n_tok, MLA_HEADS * MLA_HW), BF16),
                   jax.ShapeDtypeStruct((n_tok, MLA_HEADS * MLA_HW), BF16),
                   jax.ShapeDtypeStruct((MLA_HEADS, n_tok // tm, VT_ROWS, tm), BF16)],
        grid=(n_tok // tm,),
        in_specs=[pl.BlockSpec((tm, d), lambda i: (i, 0)), full(g2), full(wlat), full(qg2), full(wuq),
                  full(kvg2), full(wukv),
                  pl.BlockSpec((tm, LANES), lambda i: (i % seq_tiles, 0)),
                  pl.BlockSpec((tm, LANES), lambda i: (i % seq_tiles, 0))],
        out_specs=[pl.BlockSpec((tm, MLA_HEADS * MLA_HW), lambda i: (i, 0)),
                   pl.BlockSpec((tm, MLA_HEADS * MLA_HW), lambda i: (i, 0)),
                   pl.BlockSpec((MLA_HEADS, 1, VT_ROWS, tm), lambda i: (0, i, 0, 0))],
        compiler_params=_cparams(("parallel",)),
        name="mla_prep",
    )(x, g2, wlat, qg2, wuq, kvg2, wukv, cos_t, sin_t)


LOG2E = 1.4426950408889634
VT_TAIL = 16
VT_ROWS = LANES + VT_TAIL


def _vt_block(v):
    tb = v.shape[0]
    row = lax.broadcasted_iota(jnp.int32, (VT_TAIL, tb), 0)
    tail = jnp.where(row == 0, 1.0, 0.0).astype(F32)
    return jnp.concatenate([v.T, tail], axis=0).astype(BF16)


def _flash_finish(acc_ref, gate, y_dtype):
    l = acc_ref[LANES:LANES + 1, :]
    o = (acc_ref[:LANES, :] / l).T
    return (o * _silu(gate.astype(F32))).astype(y_dtype)


def _flash_causal_t(nh, qi, tq, diag_scores, past_scores, row_bias, vt_fn, s_ref, p_ref, acc_ref):
    @pl.when(qi == 0)
    def _():
        acc_ref[...] = jnp.zeros_like(acc_ref)
        p_ref[...] = jnp.zeros_like(p_ref)

    m0 = jnp.full((1, tq), NEG, F32)
    a0 = jnp.ones((1, tq), F32)
    for g in range(nh):
        s_ref[0, g] = diag_scores(g)

    def block_of(t):
        return jnp.where(t == 0, qi, t - 1)

    def softmax(slot, g, t, m):
        s = s_ref[slot, g]
        mb = jnp.max(s, axis=0, keepdims=True)
        if row_bias is None:
            m_new = jnp.maximum(m, mb)
            shift = m_new
        else:
            nb = jnp.where(t == 0, 0.0, row_bias(g, block_of(t)))
            m_new = jnp.maximum(m, mb + nb)
            shift = m_new - nb
        alpha = jnp.exp2(m - m_new)
        p = jnp.exp2((s - shift).astype(BF16))
        return m_new, alpha, p

    def pv(slot, g, j):
        return jnp.dot(vt_fn(g, j), p_ref[slot, g], preferred_element_type=F32)

    def step(t, rd, wr, carry):
        for g in range(nh):
            s_ref[wr, g] = past_scores(g, t)
        pend = [pv(wr, g, block_of(jnp.maximum(t - 1, 0))) for g in range(nh)]
        out = []
        for g in range(nh):
            m, alpha, p = softmax(rd, g, t, carry[2 * g])
            acc_ref[g] = carry[2 * g + 1] * acc_ref[g] + pend[g]
            p_ref[rd, g] = p
            out.extend((m, alpha))
        return tuple(out)

    def pair(u, carry):
        return step(2 * u + 1, 1, 0, step(2 * u, 0, 1, carry))

    carry = lax.fori_loop(0, qi // 2, pair, (m0, a0) * nh)
    carry = lax.cond(qi % 2 == 1, lambda c: step(qi - 1, 0, 1, c), lambda c: c, carry)
    pend = [pv(1 - qi % 2, g, block_of(jnp.maximum(qi - 1, 0))) for g in range(nh)]
    last = [softmax(qi % 2, g, qi, carry[2 * g]) for g in range(nh)]
    for g in range(nh):
        _, alpha, p = last[g]
        fin = jnp.dot(vt_fn(g, block_of(qi)), p, preferred_element_type=F32)
        acc_ref[g] = alpha * (carry[2 * g + 1] * acc_ref[g] + pend[g]) + fin


def _mla_kernel(tq, nh, q_ref, k_ref, vt_ref, gate_ref, y_ref, s_ref, p_ref, acc_ref):
    qi = pl.program_id(2)
    qs = [q_ref[:, g * MLA_HW:(g + 1) * MLA_HW] for g in range(nh)]
    key = lax.broadcasted_iota(jnp.int32, (tq, tq), 0)
    qry = lax.broadcasted_iota(jnp.int32, (tq, tq), 1)

    def scores(g, j):
        off = pl.multiple_of(j * tq, tq)
        return _qk(k_ref[pl.ds(off, tq), g * MLA_HW:(g + 1) * MLA_HW], qs[g])

    _flash_causal_t(nh, qi, tq, lambda g: jnp.where(key <= qry, scores(g, qi), NEG),
                    scores, None, lambda g, j: vt_ref[g, j], s_ref, p_ref, acc_ref)
    for g in range(nh):
        c = slice(g * MLA_V, (g + 1) * MLA_V)
        y_ref[:, c] = _flash_finish(acc_ref.at[g], gate_ref[:, c], y_ref.dtype)


def _mla_attn(qa, ka, vat, gate, tq, nh):
    nq = SEQ // tq
    return pl.pallas_call(
        functools.partial(_mla_kernel, tq, nh),
        out_shape=jax.ShapeDtypeStruct((N_TOK, MLA_HEADS * MLA_V), BF16),
        grid=(BATCH, MLA_HEADS // nh, nq),
        in_specs=[pl.BlockSpec((tq, nh * MLA_HW), lambda b, h, i: (b * nq + i, h)),
                  pl.BlockSpec((SEQ, nh * MLA_HW), lambda b, h, i: (b, h)),
                  pl.BlockSpec((nh, nq, VT_ROWS, tq), lambda b, h, i: (h, b, 0, 0)),
                  pl.BlockSpec((tq, nh * MLA_V), lambda b, h, i: (b * nq + i, h))],
        out_specs=pl.BlockSpec((tq, nh * MLA_V), lambda b, h, i: (b * nq + i, h)),
        scratch_shapes=[pltpu.VMEM((2, nh, tq, tq), F32), pltpu.VMEM((2, nh, tq, tq), BF16),
                        pltpu.VMEM((nh, VT_ROWS, tq), F32)],
        compiler_params=_cparams(("parallel", "parallel", "arbitrary")),
        name="mla_attn",
    )(qa, ka, vat, gate)


def _moba_kernel(nh, q_ref, k_ref, vt_ref, gate_ref, y_ref, km_ref, nb_ref, s_ref, p_ref, acc_ref):
    L = MOBA_BLOCK
    D = MOBA_DH
    nblk = SEQ // L
    qi = pl.program_id(2)

    @pl.when(qi == 0)
    def _():
        for g in range(nh):
            for j in range(nblk):
                blk = k_ref[j * L:(j + 1) * L, g * D:(g + 1) * D].astype(F32)
                km_ref[g, j:j + 1, :] = jnp.sum(blk, axis=0, keepdims=True) * (1.0 / L)

    qs = [q_ref[:, g * D:(g + 1) * D] for g in range(nh)]
    blk_i = lax.broadcasted_iota(jnp.int32, (nblk, L), 0)
    blk_f = blk_i.astype(F32)
    past = blk_i < qi
    for g in range(nh):
        gt = jnp.where(past, _qk(km_ref[g].astype(BF16), qs[g]), -jnp.inf)
        sel = jnp.zeros((nblk, L), jnp.bool_)
        for _ in range(MOBA_TOPK):
            mx = jnp.max(gt, axis=0, keepdims=True)
            first = jnp.min(jnp.where(gt == mx, blk_f, float(nblk)), axis=0, keepdims=True)
            pick = blk_f == first
            sel = jnp.logical_or(sel, pick)
            gt = jnp.where(pick, -jnp.inf, gt)
        nb_ref[g] = jnp.where(jnp.logical_and(sel, past), 0.0, NEG)

    key = lax.broadcasted_iota(jnp.int32, (L, L), 0)
    qry = lax.broadcasted_iota(jnp.int32, (L, L), 1)

    def scores(g, j):
        off = pl.multiple_of(j * L, L)
        return _qk(k_ref[pl.ds(off, L), g * D:(g + 1) * D], qs[g])

    _flash_causal_t(nh, qi, L, lambda g: jnp.where(key <= qry, scores(g, qi), NEG),
                    scores, lambda g, j: nb_ref[g, pl.ds(j, 1), :],
                    lambda g, j: vt_ref[g, j], s_ref, p_ref, acc_ref)
    for g in range(nh):
        c = slice(g * D, (g + 1) * D)
        y_ref[:, c] = _flash_finish(acc_ref.at[g], gate_ref[:, c], y_ref.dtype)


def _moba_attn(qb, kb, vbt, gate, nh):
    L = MOBA_BLOCK
    D = MOBA_DH
    nq = SEQ // L
    return pl.pallas_call(
        functools.partial(_moba_kernel, nh),
        out_shape=jax.ShapeDtypeStruct((N_TOK, MOBA_HEADS * D), BF16),
        grid=(BATCH, MOBA_HEADS // nh, nq),
        in_specs=[pl.BlockSpec((L, nh * D), lambda b, h, i: (b * nq + i, h)),
                  pl.BlockSpec((SEQ, nh * D), lambda b, h, i: (b, h)),
                  pl.BlockSpec((nh, nq, VT_ROWS, L), lambda b, h, i: (h, b, 0, 0)),
                  pl.BlockSpec((L, nh * D), lambda b, h, i: (b * nq + i, h))],
        out_specs=pl.BlockSpec((L, nh * D), lambda b, h, i: (b * nq + i, h)),
        scratch_shapes=[pltpu.VMEM((nh, nq, D), F32), pltpu.VMEM((nh, nq, L), F32),
                        pltpu.VMEM((2, nh, L, L), F32), pltpu.VMEM((2, nh, L, L), BF16),
                        pltpu.VMEM((nh, VT_ROWS, L), F32)],
        compiler_params=_cparams(("parallel", "parallel", "arbitrary")),
        name="moba_attn",
    )(qb, kb, vbt, gate)


def _mem_prep_kernel(mem_ref, g_ref, w_ref, kv_ref):
    mn = _rms(mem_ref[...], g_ref[...]).astype(BF16)
    kv_ref[...] = jnp.dot(mn, w_ref[...], preferred_element_type=F32).astype(BF16)


def _mem_prep(mem2d, g, w):
    g2 = g.reshape(1, -1)
    return pl.pallas_call(
        _mem_prep_kernel,
        out_shape=jax.ShapeDtypeStruct((BATCH * N_MEM, 2 * MEMQ_W), BF16),
        grid=(BATCH,),
        in_specs=[pl.BlockSpec((N_MEM, D_MODEL), lambda b: (b, 0)),
                  pl.BlockSpec(g2.shape, lambda b: (0, 0)),
                  pl.BlockSpec(w.shape, lambda b: (0, 0))],
        out_specs=pl.BlockSpec((N_MEM, 2 * MEMQ_W), lambda b: (b, 0)),
        compiler_params=_cparams(("parallel",)),
        name="mem_prep",
    )(mem2d, g2, w)


def _mem_attn_kernel(q_ref, kv_ref, gate_ref, y_ref):
    cols = [slice(h * MEM_DH, (h + 1) * MEM_DH) for h in range(MEM_HEADS)]
    ss = [_qk(q_ref[:, c], kv_ref[:, c]) for c in cols]
    ps = [jnp.exp(s - jnp.max(s, axis=1, keepdims=True)) for s in ss]
    for h, (c, p) in enumerate(zip(cols, ps)):
        l = jnp.sum(p, axis=1, keepdims=True)
        a = MEMQ_W + h * MEM_DH
        o = jnp.dot(p.astype(BF16), kv_ref[:, a:a + MEM_DH], preferred_element_type=F32) / l
        y_ref[:, c] = (o * _silu(gate_ref[:, c].astype(F32))).astype(y_ref.dtype)


def _mem_attn(qm, kvm, gate, tq):
    nq = SEQ // tq
    return pl.pallas_call(
        _mem_attn_kernel,
        out_shape=jax.ShapeDtypeStruct((N_TOK, MEMQ_W), BF16),
        grid=(BATCH, nq),
        in_specs=[pl.BlockSpec((tq, MEMQ_W), lambda b, i: (b * nq + i, 0)),
                  pl.BlockSpec((N_MEM, 2 * MEMQ_W), lambda b, i: (b, 0)),
                  pl.BlockSpec((tq, MEMQ_W), lambda b, i: (b * nq + i, 0))],
        out_specs=pl.BlockSpec((tq, MEMQ_W), lambda b, i: (b * nq + i, 0)),
        compiler_params=_cparams(("parallel", "parallel")),
        name="mem_attn",
    )(qm, kvm, gate)


def _band_bias(n, lo_off, hi_off):
    i = lax.broadcasted_iota(jnp.int32, (Q_BLOCK, 2 * Q_BLOCK), 0)
    c = lax.broadcasted_iota(jnp.int32, (Q_BLOCK, 2 * Q_BLOCK), 1)
    vis = (c - i >= lo_off) & (c - i <= hi_off) & ((c >= Q_BLOCK) | (n > 0))
    return jnp.where(vis, 0.0, NEG).astype(F32)


def _swa_kernel(q_ref, kvp_ref, kvo_ref, gate_ref, sink_ref, y_ref):
    QB = Q_BLOCK
    G = SWA_HEADS // SWA_KV_HEADS
    n = pl.program_id(1)
    bias = _band_bias(n, QB - (SWA_WINDOW - 1), QB)
    bias4 = jnp.concatenate([bias] * (G // 2), axis=0)
    k2 = jnp.concatenate([kvp_ref[:, :LANES], kvo_ref[:, :LANES]], axis=0).astype(F32)
    v2 = jnp.concatenate([kvp_ref[:, LANES:], kvo_ref[:, LANES:]], axis=0).astype(F32)
    k2r = pltpu.roll(k2, SWA_DH, 1)
    v2r = pltpu.roll(v2, SWA_DH, 1)
    lane = lax.broadcasted_iota(jnp.int32, (2 * QB, LANES), 1)
    lo = lane < SWA_DH
    lane_q = lax.broadcasted_iota(jnp.int32, (QB, LANES), 1)
    lo_q = lane_q < SWA_DH
    vs, ss = [], []
    for kv in range(SWA_KV_HEADS):
        ka, kb_ = (k2, k2r) if kv == 0 else (k2r, k2)
        va, vb_ = (v2, v2r) if kv == 0 else (v2r, v2)
        k_lo = jnp.where(lo, ka, 0.0).astype(BF16)
        k_hi = jnp.where(lo, 0.0, kb_).astype(BF16)
        vs.append((jnp.where(lo, va, 0.0).astype(BF16), jnp.where(lo, 0.0, vb_).astype(BF16)))
        base = kv * (G // 2)
        q4 = jnp.concatenate([q_ref[:, (base + p) * LANES:(base + p + 1) * LANES] for p in range(G // 2)],
                             axis=0)
        ss.append((_qk(q4, k_lo) + bias4, _qk(q4, k_hi) + bias4))
    stats = []
    for s_lo, s_hi in ss:
        m_lo = jnp.max(s_lo, axis=1, keepdims=True)
        m_hi = jnp.max(s_hi, axis=1, keepdims=True)
        p_lo = jnp.exp(s_lo - m_lo)
        p_hi = jnp.exp(s_hi - m_hi)
        stats.append((m_lo, m_hi, p_lo, p_hi, jnp.sum(p_lo, axis=1, keepdims=True),
                      jnp.sum(p_hi, axis=1, keepdims=True)))
    for kv in range(SWA_KV_HEADS):
        m_lo, m_hi, p_lo, p_hi, l_lo, l_hi = stats[kv]
        v_lo, v_hi = vs[kv]
        base = kv * (G // 2)
        o = (jnp.dot(p_lo.astype(BF16), v_lo, preferred_element_type=F32)
             + jnp.dot(p_hi.astype(BF16), v_hi, preferred_element_type=F32))
        lse_lo = m_lo + jnp.log(l_lo)
        lse_hi = m_hi + jnp.log(l_hi)
        for p in range(G // 2):
            r = slice(p * QB, (p + 1) * QB)
            c = slice((base + p) * LANES, (base + p + 1) * LANES)
            lse = jnp.where(lo_q, lse_lo[r], lse_hi[r])
            l = jnp.where(lo_q, l_lo[r], l_hi[r])
            w = _sigmoid(lse - sink_ref[:, c])
            y = (o[r] / l) * w * _silu(gate_ref[:, c].astype(F32))
            y_ref[:, c] = y.astype(y_ref.dtype)


def _swa_attn(qc, kvc, gate, sink_row):
    nb = SEQ // Q_BLOCK
    w = SWA_HEADS * SWA_DH
    return pl.pallas_call(
        _swa_kernel,
        out_shape=jax.ShapeDtypeStruct((N_TOK, w), BF16),
        grid=(BATCH, nb),
        in_specs=[pl.BlockSpec((Q_BLOCK, w), lambda b, n: (b * nb + n, 0)),
                  pl.BlockSpec((Q_BLOCK, 2 * LANES), lambda b, n: (jnp.maximum(b * nb + n - 1, 0), 0)),
                  pl.BlockSpec((Q_BLOCK, 2 * LANES), lambda b, n: (b * nb + n, 0)),
                  pl.BlockSpec((Q_BLOCK, w), lambda b, n: (b * nb + n, 0)),
                  pl.BlockSpec((1, w), lambda b, n: (0, 0))],
        out_specs=pl.BlockSpec((Q_BLOCK, w), lambda b, n: (b * nb + n, 0)),
        compiler_params=_cparams(("parallel", "parallel")),
        name="swa_attn",
    )(qc, kvc, kvc, gate, sink_row)


DIL_BLOCKS_PER_STEP = 2


def _dil_kernel(q_ref, kp_ref, ko_ref, vp_ref, vo_ref, o_ref, lse_ref):
    QB = Q_BLOCK
    R = DIL_BLOCKS_PER_STEP
    n = pl.program_id(2)
    lane = lax.broadcasted_iota(jnp.int32, (QB, LANES), 1)
    tasks = []
    for u in range(R):
        rows = slice(u * QB, (u + 1) * QB)
        bias = _band_bias(n if u == 0 else 1, 0, QB)
        for h in range(DIL_HEADS):
            c = slice(h * DIL_DH, (h + 1) * DIL_DH)
            k_prev = kp_ref[:, c] if u == 0 else ko_ref[(u - 1) * QB:u * QB, c]
            v_prev = vp_ref[:, c] if u == 0 else vo_ref[(u - 1) * QB:u * QB, c]
            k = jnp.concatenate([k_prev, ko_ref[rows, c]], axis=0)
            v = jnp.concatenate([v_prev, vo_ref[rows, c]], axis=0)
            tasks.append((u, h, _qk(q_ref[rows, c], k) + bias, v))
    soft = []
    for u, h, s, v in tasks:
        m = jnp.max(s, axis=1, keepdims=True)
        p = jnp.exp(s - m)
        soft.append((m, p, jnp.sum(p, axis=1, keepdims=True)))
    lse_all = [jnp.zeros((QB, LANES), F32) for _ in range(R)]
    for (u, h, _, v), (m, p, l) in zip(tasks, soft):
        o_ref[u * QB:(u + 1) * QB, h * DIL_DH:(h + 1) * DIL_DH] = (
            jnp.dot(p.astype(BF16), v, preferred_element_type=F32) / l).astype(o_ref.dtype)
        lse_all[u] = jnp.where(lane == h, m + jnp.log(l), lse_all[u])
    for u in range(R):
        lse_ref[u * QB:(u + 1) * QB, :] = lse_all[u]


def _dil_attn(qv, kv, vv, dil):
    w = DIL_HEADS * DIL_DH
    R = DIL_BLOCKS_PER_STEP
    L = SEQ // dil
    ns = L // (R * Q_BLOCK)
    cur = lambda b, r, n: (b * ns + n, r)
    prev = lambda b, r, n: (jnp.maximum((b * ns + n) * R - 1, 0), r)
    blk = (R * Q_BLOCK, w)
    pblk = (Q_BLOCK, w)
    return pl.pallas_call(
        _dil_kernel,
        out_shape=[jax.ShapeDtypeStruct((BATCH * L, dil * w), BF16),
                   jax.ShapeDtypeStruct((BATCH * L, dil * LANES), F32)],
        grid=(BATCH, dil, ns),
        in_specs=[pl.BlockSpec(blk, cur), pl.BlockSpec(pblk, prev), pl.BlockSpec(blk, cur),
                  pl.BlockSpec(pblk, prev), pl.BlockSpec(blk, cur)],
        out_specs=[pl.BlockSpec(blk, cur), pl.BlockSpec((R * Q_BLOCK, LANES), cur)],
        compiler_params=_cparams(("parallel", "parallel", "parallel")),
        name=f"dil_attn_d{dil}",
    )(qv, kv, kv, vv, vv)


def _dil_combine_kernel(tm, *refs):
    np_ = len(DIL_PATTERNS)
    o_refs, l_refs = refs[:np_], refs[np_:2 * np_]
    gate_ref, y_ref = refs[2 * np_], refs[2 * np_ + 1]
    stage = refs[2 * np_ + 2:]
    w = DIL_HEADS * DIL_DH
    os_, ls = [], []
    k = 0
    for (_, d), o_ref, l_ref in zip(DIL_PATTERNS, o_refs, l_refs):
        if d == 1:
            os_.append(lambda h, o_ref=o_ref: o_ref[:, h * DIL_DH:(h + 1) * DIL_DH].astype(F32))
            ls.append(l_ref[...])
            continue
        so, sl = stage[2 * k], stage[2 * k + 1]
        k += 1
        for r in range(d):
            for h in range(DIL_HEADS):
                a = r * w + h * DIL_DH
                so[h, pl.ds(r, tm // d, stride=d), :] = o_ref[:, a:a + DIL_DH].astype(F32)
            sl[pl.ds(r, tm // d, stride=d), :] = l_ref[:, r * LANES:(r + 1) * LANES]
        os_.append(lambda h, so=so: so[h])
        ls.append(sl[...])
    mx = functools.reduce(jnp.maximum, ls)
    es = [jnp.exp(x - mx) for x in ls]
    den = functools.reduce(lambda a, b: a + b, es)
    ws = [e / den for e in es]
    lane = lax.broadcasted_iota(jnp.int32, ws[0].shape, 1)
    for h in range(DIL_HEADS):
        c = slice(h * DIL_DH, (h + 1) * DIL_DH)
        wh = [jnp.sum(jnp.where(lane == h, wgt, 0.0), axis=1, keepdims=True) for wgt in ws]
        o = functools.reduce(lambda a, b: a + b, [wh[p] * os_[p](h) for p in range(np_)])
        y_ref[:, c] = (o * _silu(gate_ref[:, c].astype(F32))).astype(y_ref.dtype)


def _dil_combine(outs, lses, gate, tm):
    w = DIL_HEADS * DIL_DH
    in_specs = [pl.BlockSpec((tm // d, d * w), lambda i: (i, 0)) for _, d in DIL_PATTERNS]
    in_specs += [pl.BlockSpec((tm // d, d * LANES), lambda i: (i, 0)) for _, d in DIL_PATTERNS]
    in_specs.append(pl.BlockSpec((tm, w), lambda i: (i, 0)))
    scratch = []
    for _, d in DIL_PATTERNS:
        if d > 1:
            scratch += [pltpu.VMEM((DIL_HEADS, tm, DIL_DH), F32), pltpu.VMEM((tm, LANES), F32)]
    return pl.pallas_call(
        functools.partial(_dil_combine_kernel, tm),
        out_shape=jax.ShapeDtypeStruct((N_TOK, w), BF16),
        grid=(N_TOK // tm,),
        in_specs=in_specs,
        out_specs=pl.BlockSpec((tm, w), lambda i: (i, 0)),
        scratch_shapes=scratch,
        compiler_params=_cparams(("parallel",)),
        name="dil_combine",
    )(*outs, *lses, gate)


def _outproj_kernel(nparts, final, *refs):
    ys, ws = refs[:nparts], refs[nparts:2 * nparts]
    x_ref = refs[2 * nparts]
    o_ref = refs[-1]
    acc = x_ref[...]
    for y, w in zip(ys, ws):
        acc = acc + jnp.dot(y[...], w[...], preferred_element_type=F32)
    if final:
        acc = _rms(acc, refs[2 * nparts + 1][...])
    o_ref[...] = acc


def _outproj(ys, ws, x, final_g, tm):
    n_tok, d = x.shape
    in_specs = [pl.BlockSpec((tm, y.shape[1]), lambda i: (i, 0)) for y in ys]
    in_specs += [pl.BlockSpec(w.shape, lambda i: (0, 0)) for w in ws]
    in_specs.append(pl.BlockSpec((tm, d), lambda i: (i, 0)))
    args = [*ys, *ws, x]
    if final_g is not None:
        in_specs.append(pl.BlockSpec((1, d), lambda i: (0, 0)))
        args.append(final_g.reshape(1, d))
    return pl.pallas_call(
        functools.partial(_outproj_kernel, len(ys), final_g is not None),
        out_shape=jax.ShapeDtypeStruct((n_tok, d), F32),
        grid=(n_tok // tm,),
        in_specs=in_specs,
        out_specs=pl.BlockSpec((tm, d), lambda i: (i, 0)),
        compiler_params=_cparams(("parallel",)),
        name="outproj",
    )(*args)


PROJ_TM = 512
ATTN_TQ = 512
MLA_HEADS_PER_STEP = 2
MOBA_HEADS_PER_STEP = 4


def _even_layer(x, mem2d, norm_g, w_in, q_norm_g, w_uq, kv_norm_g, w_ukv, mem_norm_g, w_mem_kv, w_out):
    n_lat = MLA_Q_RANK + MLA_KV_RANK + MLA_ROPE
    wlat = jnp.pad(w_in[:, :n_lat], ((0, 0), (0, LANES - MLA_ROPE))).astype(BF16)
    n_bqkv = n_lat + 3 * MOBA_HEADS * MOBA_DH
    w_bqkv = w_in[:, n_lat:n_bqkv].astype(BF16)
    w_rest = w_in[:, n_bqkv:].astype(BF16)
    wuq = w_uq.reshape(MLA_Q_RANK, MLA_HEADS, MLA_NOPE + MLA_ROPE)
    wuq = jnp.pad(wuq, ((0, 0), (0, 0), (0, MLA_HW - MLA_NOPE - MLA_ROPE)))
    wuq = wuq.reshape(MLA_Q_RANK, MLA_HEADS * MLA_HW).astype(BF16)
    wukv = w_ukv.reshape(MLA_KV_RANK, MLA_HEADS, MLA_NOPE + MLA_V)
    wukv = jnp.concatenate([wukv[:, :, :MLA_NOPE].reshape(MLA_KV_RANK, -1),
                            wukv[:, :, MLA_NOPE:].reshape(MLA_KV_RANK, -1)], axis=1).astype(BF16)

    cos_a, sin_a = _rope_tables(MLA_ROPE, LANES)
    qa, ka, va = _mla_prep(x, norm_g, wlat, q_norm_g, wuq, kv_norm_g, wukv, cos_a, sin_a, ATTN_TQ)

    rot = MOBA_DH // ROT_FRAC
    cos_b, sin_b = _rope_tables(rot, MOBA_DH)
    sc = MOBA_DH ** -0.5 * LOG2E
    wb = MOBA_HEADS * MOBA_DH
    plain = [("plain",)]
    groups_qkv = [
        (wb, [("rope", 0, rot // 2, MOBA_DH, sc)] * MOBA_HEADS, "rows"),
        (wb, [("rope", 0, rot // 2, MOBA_DH, 1.0)] * MOBA_HEADS, "rows"),
        (wb, plain * MOBA_HEADS, ("vT", MOBA_BLOCK)),
    ]
    groups_rest = [
        (MEMQ_W, [("scale", MEM_DH ** -0.5)] * MEM_HEADS, "rows"),
        (MLA_HEADS * MLA_V, plain * MLA_HEADS, "rows"),
        (wb, plain * MOBA_HEADS, "rows"),
        (MEMQ_W, plain * MEM_HEADS, "rows"),
    ]
    qb, kb, vb = _inproj(x, norm_g, w_bqkv, groups_qkv, [cos_b, sin_b], PROJ_TM)
    qm, gate_a, gate_b, gate_m = _inproj(x, norm_g, w_rest, groups_rest, [], PROJ_TM)

    y_a = _mla_attn(qa, ka, va, gate_a, ATTN_TQ, MLA_HEADS_PER_STEP)
    y_b = _moba_attn(qb, kb, vb, gate_b, MOBA_HEADS_PER_STEP)
    kvm = _mem_prep(mem2d, mem_norm_g, w_mem_kv.astype(BF16))
    y_m = _mem_attn(qm, kvm, gate_m, ATTN_TQ)

    a, b = MLA_HEADS * MLA_V, MLA_HEADS * MLA_V + MOBA_HEADS * MOBA_DH
    return [y_a, y_b, y_m], [w_out[:a].astype(BF16), w_out[a:b].astype(BF16), w_out[b:].astype(BF16)]


def _odd_layer(x, mem2d, norm_g, w_in, sinks, mem_norm_g, w_mem_kv, w_out):
    rot_c = SWA_DH // ROT_FRAC
    rot_d = DIL_DH // ROT_FRAC
    cos_c, sin_c = _rope_tables(rot_c, SWA_DH)
    cos_d, sin_d = _rope_tables(rot_d, DIL_DH)
    wq = SWA_HEADS * SWA_DH
    wd = DIL_HEADS * DIL_DH
    wkv = 2 * SWA_KV_HEADS * SWA_DH
    rc = lambda s: ("rope", 0, rot_c // 2, SWA_DH, s)
    rd = lambda s: ("rope", 1, rot_d // 2, DIL_DH, s)
    plain = [("plain",)]
    groups_qkv = [
        (wq, [rc(SWA_DH ** -0.5)] * (wq // LANES), "rows"),
        (wkv, [rc(1.0), ("plain",)], "rows"),
        (wd, [rd(DIL_DH ** -0.5)] * DIL_HEADS, "dilated"),
        (wd, [rd(1.0)] * DIL_HEADS, "dilated"),
        (wd, plain * DIL_HEADS, "dilated"),
    ]
    groups_rest = [
        (MEMQ_W, [("scale", MEM_DH ** -0.5)] * MEM_HEADS, "rows"),
        (wq, plain * (wq // LANES), "rows"),
        (wd, plain * DIL_HEADS, "rows"),
        (MEMQ_W, plain * MEM_HEADS, "rows"),
    ]
    n_qkv = wq + wkv + 3 * wd
    res = _inproj(x, norm_g, w_in[:, :n_qkv].astype(BF16), groups_qkv, [cos_c, sin_c, cos_d, sin_d], PROJ_TM)
    qc, kvc = res[0], res[1]
    nv = 1 + len(DIL_VIEWS)
    qd, kd, vd = (res[2 + i * nv:2 + (i + 1) * nv] for i in range(3))
    qm, gate_c, gate_d, gate_m = _inproj(x, norm_g, w_in[:, n_qkv:].astype(BF16), groups_rest, [], PROJ_TM)

    sink_row = jnp.repeat(sinks.astype(F32), SWA_DH).reshape(1, wq)
    y_c = _swa_attn(qc, kvc, gate_c, sink_row)
    view_of = {1: 0, **{d: 1 + i for i, d in enumerate(DIL_VIEWS)}}
    outs, lses = zip(*[_dil_attn(qd[view_of[dil]], kd[view_of[dil]], vd[view_of[dil]], dil)
                       for _, dil in DIL_PATTERNS])
    y_d = _dil_combine(outs, lses, gate_d, PROJ_TM)
    kvm = _mem_prep(mem2d, mem_norm_g, w_mem_kv.astype(BF16))
    y_m = _mem_attn(qm, kvm, gate_m, ATTN_TQ)

    return [y_c, y_d, y_m], [w_out[:wq].astype(BF16), w_out[wq:wq + wd].astype(BF16),
                             w_out[wq + wd:].astype(BF16)]


def kernel(x, mem, ev_norm_g, ev_w_in, ev_q_norm_g, ev_w_uq, ev_kv_norm_g, ev_w_ukv, ev_mem_norm_g,
           ev_w_mem_kv, ev_w_out, od_norm_g, od_w_in, od_sinks, od_mem_norm_g, od_w_mem_kv, od_w_out,
           final_norm_g):
    x2 = x.reshape(N_TOK, D_MODEL)
    mem2d = mem.reshape(BATCH * N_MEM, D_MODEL)
    ys, ws = _even_layer(x2, mem2d, ev_norm_g[0], ev_w_in[0], ev_q_norm_g[0], ev_w_uq[0], ev_kv_norm_g[0],
                         ev_w_ukv[0], ev_mem_norm_g[0], ev_w_mem_kv[0], ev_w_out[0])
    x2 = _outproj(ys, ws, x2, None, PROJ_TM)
    ys, ws = _odd_layer(x2, mem2d, od_norm_g[0], od_w_in[0], od_sinks[0], od_mem_norm_g[0],
                        od_w_mem_kv[0], od_w_out[0])
    x2 = _outproj(ys, ws, x2, final_norm_g, PROJ_TM)
    return x2.reshape(BATCH, SEQ, D_MODEL)
```

```python
import functools

import numpy as np
import jax
import jax.numpy as jnp
from jax import lax
from jax.experimental import pallas as pl
from jax.experimental.pallas import tpu as pltpu

D_MODEL = 2048
BATCH = 4
SEQ = 4096
N_TOK = BATCH * SEQ
N_MEM = 256
ROPE_THETA = 500000.0
ROT_FRAC = 4
EPS = 1e-6

MLA_HEADS = 8
MLA_Q_RANK = 512
MLA_KV_RANK = 256
MLA_NOPE = 128
MLA_ROPE = 64
MLA_V = 128
MOBA_HEADS = 8
MOBA_DH = 128
MOBA_BLOCK = 256
MOBA_TOPK = 3
SWA_HEADS = 16
SWA_KV_HEADS = 2
SWA_DH = 64
SWA_WINDOW = 128
DIL_HEADS = 6
DIL_DH = 128
DIL_PATTERNS = ((128, 1), (512, 4), (2048, 16))
MEM_HEADS = 4
MEM_DH = 128
MEMQ_W = MEM_HEADS * MEM_DH
Q_BLOCK = 128

LANES = 128
VMEM_LIMIT = 52 * 1024 * 1024

NEG = -1e30
BF16 = jnp.bfloat16
F32 = jnp.float32


def _cparams(sem):
    return pltpu.CompilerParams(dimension_semantics=sem, vmem_limit_bytes=VMEM_LIMIT)


def _rms(x, g):
    ms = jnp.mean(x * x, axis=-1, keepdims=True)
    return (x * lax.rsqrt(ms + EPS)) * g


def _silu(g):
    return g / (1.0 + jnp.exp(-g))


def _sigmoid(z):
    return 1.0 / (1.0 + jnp.exp(-z))


def _rope_slab(x, cos_t, sin_t, half, period):
    lane = lax.broadcasted_iota(jnp.int32, x.shape, 1)
    up = pltpu.roll(x, LANES - half, 1)
    dn = pltpu.roll(x, half, 1)
    sw = jnp.where((lane & (period - 1)) < half, up, dn)
    return x * cos_t + sw * sin_t


def _qk(q, k):
    return lax.dot_general(q, k, (((1,), (1,)), ((), ())), preferred_element_type=F32)


def _rope_tables(rot_dim, period):
    half = rot_dim // 2
    inv = 1.0 / (ROPE_THETA ** (jnp.arange(0, rot_dim, 2, dtype=F32) / rot_dim))
    ang = jnp.arange(SEQ, dtype=F32)[:, None] * inv[None, :]
    c, s = jnp.cos(ang), jnp.sin(ang)
    d = np.arange(LANES) % period
    idx = d % half
    cos_t = jnp.where(d < rot_dim, c[:, idx], 1.0)
    sin_t = jnp.where(d < half, -s[:, idx], jnp.where(d < rot_dim, s[:, idx], 0.0))
    return cos_t.astype(F32), sin_t.astype(F32)


PROJ_CHUNK = 512
DIL_VIEWS = tuple(d for _, d in DIL_PATTERNS if d > 1)


def _inproj_kernel(groups, ntab, tm, x_ref, g_ref, *rest):
    tabs = rest[:2 * ntab]
    w_ref = rest[2 * ntab]
    ndil = sum(layout == "dilated" for _, _, layout in groups)
    refs = list(rest[2 * ntab + 1:len(rest) - ndil])
    stage = list(rest[len(rest) - ndil:])
    xn = _rms(x_ref[...], g_ref[...]).astype(BF16)
    c0 = 0
    for ncols, kinds, layout in groups:
        o_ref = refs.pop(0)
        views = [refs.pop(0) for _ in DIL_VIEWS] if layout == "dilated" else []
        st_ref = stage.pop(0) if layout == "dilated" else None
        for s0 in range(0, ncols, PROJ_CHUNK):
            n = min(PROJ_CHUNK, ncols - s0)
            acc = jnp.dot(xn, w_ref[:, c0 + s0:c0 + s0 + n], preferred_element_type=F32)
            for c in range(n // LANES):
                col = s0 + c * LANES
                kind = kinds[col // LANES]
                piece = acc[:, c * LANES:(c + 1) * LANES]
                if kind[0] == "rope":
                    _, ti, half, period, sc = kind
                    piece = _rope_slab(piece, tabs[2 * ti][...], tabs[2 * ti + 1][...], half, period)
                    if sc != 1.0:
                        piece = piece * sc
                elif kind[0] == "scale":
                    piece = piece * kind[1]
                if layout == "rows" or layout == "dilated":
                    o_ref[:, col:col + LANES] = piece.astype(o_ref.dtype)
                    if layout == "dilated":
                        st_ref[col // LANES] = piece
                else:
                    tb = layout[1]
                    for u in range(tm // tb):
                        o_ref[col // LANES, u] = _vt_block(piece[u * tb:(u + 1) * tb, :])
        for d, v_ref in zip(DIL_VIEWS, views):
            for r in range(d):
                for c in range(ncols // LANES):
                    a = r * ncols + c * LANES
                    v_ref[:, a:a + LANES] = st_ref[c, pl.ds(r, tm // d, stride=d), :].astype(v_ref.dtype)
        c0 += ncols


def _inproj(x, g, w, groups, tables, tm):
    n_tok, d = x.shape
    assert sum(nc for nc, _, _ in groups) == w.shape[1]
    seq_tiles = SEQ // tm
    in_specs = [pl.BlockSpec((tm, d), lambda i: (i, 0)),
                pl.BlockSpec((1, d), lambda i: (0, 0))]
    args = [x, g.reshape(1, d)]
    for t in tables:
        in_specs.append(pl.BlockSpec((tm, LANES), lambda i: (i % seq_tiles, 0)))
        args.append(t)
    in_specs.append(pl.BlockSpec(w.shape, lambda i: (0, 0), pipeline_mode=pl.Buffered(1)))
    args.append(w)
    out_specs, out_shapes = [], []
    for nc, _, layout in groups:
        if layout == "rows" or layout == "dilated":
            out_specs.append(pl.BlockSpec((tm, nc), lambda i: (i, 0)))
            out_shapes.append(jax.ShapeDtypeStruct((n_tok, nc), BF16))
            if layout == "dilated":
                for dil in DIL_VIEWS:
                    out_specs.append(pl.BlockSpec((tm // dil, dil * nc), lambda i: (i, 0)))
                    out_shapes.append(jax.ShapeDtypeStruct((n_tok // dil, dil * nc), BF16))
        else:
            tb = layout[1]
            out_specs.append(pl.BlockSpec((nc // LANES, tm // tb, VT_ROWS, tb), lambda i: (0, i, 0, 0)))
            out_shapes.append(jax.ShapeDtypeStruct((nc // LANES, n_tok // tb, VT_ROWS, tb), BF16))
    return pl.pallas_call(
        functools.partial(_inproj_kernel, groups, len(tables) // 2, tm),
        out_shape=out_shapes,
        grid=(n_tok // tm,),
        in_specs=in_specs,
        out_specs=out_specs,
        scratch_shapes=[pltpu.VMEM((nc // LANES, tm, LANES), F32) for nc, _, layout in groups
                        if layout == "dilated"],
        compiler_params=_cparams(("parallel",)),
        name="inproj",
    )(*args)


MLA_HW = 256


def _mla_prep_kernel(scale, x_ref, g_ref, wlat_ref, qg_ref, wuq_ref, kvg_ref, wukv_ref,
                     cos_ref, sin_ref, qa_ref, ka_ref, va_ref):
    xn = _rms(x_ref[...], g_ref[...]).astype(BF16)
    lat = jnp.dot(xn, wlat_ref[...], preferred_element_type=F32)
    cq = _rms(lat[:, :MLA_Q_RANK], qg_ref[...]).astype(BF16)
    ckv = _rms(lat[:, MLA_Q_RANK:MLA_Q_RANK + MLA_KV_RANK], kvg_ref[...]).astype(BF16)
    kpe = lat[:, MLA_Q_RANK + MLA_KV_RANK:]
    cos_t, sin_t = cos_ref[...], sin_ref[...]
    kpe = _rope_slab(kpe, cos_t, sin_t, MLA_ROPE // 2, LANES).astype(BF16)
    q = jnp.dot(cq, wuq_ref[...], preferred_element_type=F32)
    kv = jnp.dot(ckv, wukv_ref[...], preferred_element_type=F32)
    for h in range(MLA_HEADS):
        a = h * MLA_HW
        qa_ref[:, a:a + LANES] = (q[:, a:a + LANES] * scale).astype(BF16)
        qr = _rope_slab(q[:, a + LANES:a + 2 * LANES], cos_t, sin_t, MLA_ROPE // 2, LANES)
        qa_ref[:, a + LANES:a + 2 * LANES] = (qr * scale).astype(BF16)
        ka_ref[:, a:a + LANES] = kv[:, h * LANES:(h + 1) * LANES].astype(BF16)
        ka_ref[:, a + LANES:a + 2 * LANES] = kpe
        v0 = MLA_HEADS * MLA_NOPE + h * MLA_V
        va_ref[h, 0] = _vt_block(kv[:, v0:v0 + MLA_V])


def _mla_prep(x, g, wlat, qg, wuq, kvg, wukv, cos_t, sin_t, tm):
    n_tok, d = x.shape
    seq_tiles = SEQ // tm
    full = lambda a: pl.BlockSpec(a.shape, lambda i: (0, 0))
    qg2, kvg2, g2 = qg.reshape(1, -1), kvg.reshape(1, -1), g.reshape(1, d)
    scale = (MLA_NOPE + MLA_ROPE) ** -0.5 * LOG2E
    return pl.pallas_call(
        functools.partial(_mla_prep_kernel, scale),
        out_shape=[jax.ShapeDtypeStruct((n_tok, MLA_HEADS * MLA_HW), BF16),
                   jax.ShapeDtypeStruct((n_tok, MLA_HEADS * MLA_HW), BF16),
                   jax.ShapeDtypeStruct((MLA_HEADS, n_tok // tm, VT_ROWS, tm), BF16)],
        grid=(n_tok // tm,),
        in_specs=[pl.BlockSpec((tm, d), lambda i: (i, 0)), full(g2), full(wlat), full(qg2), full(wuq),
                  full(kvg2), full(wukv),
                  pl.BlockSpec((tm, LANES), lambda i: (i % seq_tiles, 0)),
                  pl.BlockSpec((tm, LANES), lambda i: (i % seq_tiles, 0))],
        out_specs=[pl.BlockSpec((tm, MLA_HEADS * MLA_HW), lambda i: (i, 0)),
                   pl.BlockSpec((tm, MLA_HEADS * MLA_HW), lambda i: (i, 0)),
                   pl.BlockSpec((MLA_HEADS, 1, VT_ROWS, tm), lambda i: (0, i, 0, 0))],
        compiler_params=_cparams(("parallel",)),
        name="mla_prep",
    )(x, g2, wlat, qg2, wuq, kvg2, wukv, cos_t, sin_t)


LOG2E = 1.4426950408889634
VT_TAIL = 16
VT_ROWS = LANES + VT_TAIL


def _vt_block(v):
    tb = v.shape[0]
    row = lax.broadcasted_iota(jnp.int32, (VT_TAIL, tb), 0)
    tail = jnp.where(row == 0, 1.0, 0.0).astype(F32)
    return jnp.concatenate([v.T, tail], axis=0).astype(BF16)


def _flash_finish_all(nh, n, tq, dv, acc_ref, gate_ref, y_ref):
    def body(c, _):
        rows = pl.ds(pl.multiple_of(c * tq, tq), tq)
        for g in range(nh):
            cols = slice(g * dv, (g + 1) * dv)
            a = acc_ref.at[g, c]
            o = (a[:LANES, :] / a[LANES:LANES + 1, :]).T
            y_ref[rows, cols] = (o * _silu(gate_ref[rows, cols].astype(F32))).astype(y_ref.dtype)
        return 0

    lax.fori_loop(0, n, body, 0)


def _flash_flat_t(nh, n, tq, score_fn, row_bias, vt_fn, s_ref, p_ref, acc_ref, m_ref):
    n_off = n * (n - 1) // 2
    assert n % 2 == 0 and n_off % 2 == 0
    acc_ref[...] = jnp.zeros_like(acc_ref)
    p_ref[...] = jnp.zeros_like(p_ref)
    m_ref[...] = jnp.full(m_ref.shape, NEG, F32)
    key = lax.broadcasted_iota(jnp.int32, (tq, tq), 0)
    qry = lax.broadcasted_iota(jnp.int32, (tq, tq), 1)

    def diag_scores(c):
        return lambda g: jnp.where(key <= qry, score_fn(g, c, c), NEG)

    def off_scores(j, c):
        return lambda g: score_fn(g, j, jnp.minimum(c, n - 1))

    def step(rd, wr, nxt, cur, prev, alphas):
        (jc, cc), (jp, cp) = cur, prev
        for g in range(nh):
            s_ref[wr, g] = nxt(g)
        pend = [jnp.dot(vt_fn(g, jp), p_ref[wr, g], preferred_element_type=F32) for g in range(nh)]
        out = []
        for g in range(nh):
            s = s_ref[rd, g]
            m_prev = m_ref[g, cc]
            mb = jnp.max(s, axis=0, keepdims=True)
            if row_bias is None:
                m_new = jnp.maximum(m_prev, mb)
                shift = m_new
            else:
                nb = jnp.where(jc == cc, 0.0, row_bias(g, jc, cc))
                m_new = jnp.maximum(m_prev, mb + nb)
                shift = m_new - nb
            m_ref[g, cc] = m_new
            acc_ref[g, cp] = alphas[g] * acc_ref[g, cp] + pend[g]
            p_ref[rd, g] = jnp.exp2((s - shift).astype(BF16))
            out.append(jnp.exp2(m_prev - m_new))
        return tuple(out)

    def advance(j, c):
        wrap = j + 1 >= c
        return jnp.where(wrap, 0, j + 1), jnp.where(wrap, c + 1, c)

    for g in range(nh):
        s_ref[0, g] = diag_scores(0)(g)
    alphas = (jnp.ones((1, tq), F32),) * nh

    def diag_pair(u, alphas):
        i = 2 * u
        ip = jnp.maximum(i - 1, 0)
        alphas = step(0, 1, diag_scores(i + 1), (i, i), (ip, ip), alphas)
        return step(1, 0, diag_scores(i + 2), (i + 1, i + 1), (i, i), alphas)

    alphas = lax.fori_loop(0, (n - 2) // 2, diag_pair, alphas)
    alphas = step(0, 1, diag_scores(n - 1), (n - 2, n - 2), (n - 3, n - 3), alphas)
    alphas = step(1, 0, off_scores(0, 1), (n - 1, n - 1), (n - 2, n - 2), alphas)

    def off_pair(u, carry):
        jc, cc, jp, cp = carry[:4]
        jn, cn = advance(jc, cc)
        alphas = step(0, 1, off_scores(jn, cn), (jc, cc), (jp, cp), carry[4:])
        jn2, cn2 = advance(jn, cn)
        alphas = step(1, 0, off_scores(jn2, cn2), (jn, cn), (jc, cc), alphas)
        return (jn2, cn2, jn, cn) + tuple(alphas)

    i32 = jnp.int32
    carry = lax.fori_loop(0, n_off // 2, off_pair, (i32(0), i32(1), i32(n - 1), i32(n - 1)) + tuple(alphas))
    jl, cl = carry[2], carry[3]
    for g in range(nh):
        pend = jnp.dot(vt_fn(g, jl), p_ref[1, g], preferred_element_type=F32)
        acc_ref[g, cl] = carry[4 + g] * acc_ref[g, cl] + pend


def _rows(i, t):
    return pl.ds(i * t, t) if isinstance(i, int) else pl.ds(pl.multiple_of(i * t, t), t)


def _flash_scratch(nh, n, tq):
    return [pltpu.VMEM((2, nh, tq, tq), F32), pltpu.VMEM((2, nh, tq, tq), BF16),
            pltpu.VMEM((nh, n, VT_ROWS, tq), F32), pltpu.VMEM((nh, n, 1, tq), F32)]


def _mla_kernel(tq, nh, q_ref, k_ref, vt_ref, gate_ref, y_ref, s_ref, p_ref, acc_ref, m_ref):
    n = SEQ // tq

    def scores(g, j, c):
        cols = slice(g * MLA_HW, (g + 1) * MLA_HW)
        return _qk(k_ref[_rows(j, tq), cols], q_ref[_rows(c, tq), cols])

    _flash_flat_t(nh, n, tq, scores, None, lambda g, j: vt_ref[g, j], s_ref, p_ref, acc_ref, m_ref)
    _flash_finish_all(nh, n, tq, MLA_V, acc_ref, gate_ref, y_ref)


def _mla_attn(qa, ka, vat, gate, tq, nh):
    nq = SEQ // tq
    return pl.pallas_call(
        functools.partial(_mla_kernel, tq, nh),
        out_shape=jax.ShapeDtypeStruct((N_TOK, MLA_HEADS * MLA_V), BF16),
        grid=(BATCH, MLA_HEADS // nh),
        in_specs=[pl.BlockSpec((SEQ, nh * MLA_HW), lambda b, h: (b, h)),
                  pl.BlockSpec((SEQ, nh * MLA_HW), lambda b, h: (b, h)),
                  pl.BlockSpec((nh, nq, VT_ROWS, tq), lambda b, h: (h, b, 0, 0)),
                  pl.BlockSpec((SEQ, nh * MLA_V), lambda b, h: (b, h))],
        out_specs=pl.BlockSpec((SEQ, nh * MLA_V), lambda b, h: (b, h)),
        scratch_shapes=_flash_scratch(nh, nq, tq),
        compiler_params=_cparams(("parallel", "parallel")),
        name="mla_attn",
    )(qa, ka, vat, gate)


def _flash_tile_t(nh, qi, tq, diag_scores, past_scores, row_bias, vt_fn, s_ref, p_ref, acc_ref):
    acc_ref[...] = jnp.zeros_like(acc_ref)
    p_ref[...] = jnp.zeros_like(p_ref)
    m0 = jnp.full((1, tq), NEG, F32)
    a0 = jnp.ones((1, tq), F32)
    for g in range(nh):
        s_ref[0, g] = diag_scores(g)

    def block_of(t):
        return jnp.where(t == 0, qi, t - 1)

    def softmax(slot, g, t, m):
        s = s_ref[slot, g]
        mb = jnp.max(s, axis=0, keepdims=True)
        nb = jnp.where(t == 0, 0.0, row_bias(g, block_of(t)))
        m_new = jnp.maximum(m, mb + nb)
        alpha = jnp.exp2(m - m_new)
        p = jnp.exp2((s - (m_new - nb)).astype(BF16))
        return m_new, alpha, p

    def pv(slot, g, j):
        return jnp.dot(vt_fn(g, j), p_ref[slot, g], preferred_element_type=F32)

    def step(t, rd, wr, carry):
        for g in range(nh):
            s_ref[wr, g] = past_scores(g, t)
        pend = [pv(wr, g, block_of(jnp.maximum(t - 1, 0))) for g in range(nh)]
        out = []
        for g in range(nh):
            m, alpha, p = softmax(rd, g, t, carry[2 * g])
            acc_ref[g] = carry[2 * g + 1] * acc_ref[g] + pend[g]
            p_ref[rd, g] = p
            out.extend((m, alpha))
        return tuple(out)

    def pair(u, carry):
        return step(2 * u + 1, 1, 0, step(2 * u, 0, 1, carry))

    carry = lax.fori_loop(0, qi // 2, pair, (m0, a0) * nh)
    carry = lax.cond(qi % 2 == 1, lambda c: step(qi - 1, 0, 1, c), lambda c: c, carry)
    pend = [pv(1 - qi % 2, g, block_of(jnp.maximum(qi - 1, 0))) for g in range(nh)]
    last = [softmax(qi % 2, g, qi, carry[2 * g]) for g in range(nh)]
    for g in range(nh):
        _, alpha, p = last[g]
        fin = jnp.dot(vt_fn(g, block_of(qi)), p, preferred_element_type=F32)
        acc_ref[g] = alpha * (carry[2 * g + 1] * acc_ref[g] + pend[g]) + fin


def _moba_kernel(nh, q_ref, k_ref, vt_ref, gate_ref, y_ref, km_ref, nb_ref, s_ref, p_ref, acc_ref):
    L = MOBA_BLOCK
    D = MOBA_DH
    nblk = SEQ // L
    qi = pl.program_id(2)

    @pl.when(qi == 0)
    def _():
        for g in range(nh):
            for j in range(nblk):
                blk = k_ref[j * L:(j + 1) * L, g * D:(g + 1) * D].astype(F32)
                km_ref[g, j:j + 1, :] = jnp.sum(blk, axis=0, keepdims=True) * (1.0 / L)

    qs = [q_ref[:, g * D:(g + 1) * D] for g in range(nh)]
    blk_i = lax.broadcasted_iota(jnp.int32, (nblk, L), 0)
    blk_f = blk_i.astype(F32)
    past = blk_i < qi
    for g in range(nh):
        gt = jnp.where(past, _qk(km_ref[g].astype(BF16), qs[g]), -jnp.inf)
        sel = jnp.zeros((nblk, L), jnp.bool_)
        for _ in range(MOBA_TOPK):
            mx = jnp.max(gt, axis=0, keepdims=True)
            first = jnp.min(jnp.where(gt == mx, blk_f, float(nblk)), axis=0, keepdims=True)
            pick = blk_f == first
            sel = jnp.logical_or(sel, pick)
            gt = jnp.where(pick, -jnp.inf, gt)
        nb_ref[g] = jnp.where(jnp.logical_and(sel, past), 0.0, NEG)

    key = lax.broadcasted_iota(jnp.int32, (L, L), 0)
    qry = lax.broadcasted_iota(jnp.int32, (L, L), 1)

    def scores(g, j):
        return _qk(k_ref[_rows(j, L), g * D:(g + 1) * D], qs[g])

    _flash_tile_t(nh, qi, L, lambda g: jnp.where(key <= qry, scores(g, qi), NEG),
                  scores, lambda g, j: nb_ref[g, pl.ds(j, 1), :],
                  lambda g, j: vt_ref[g, j], s_ref, p_ref, acc_ref)
    for g in range(nh):
        c = slice(g * D, (g + 1) * D)
        a = acc_ref.at[g]
        o = (a[:LANES, :] / a[LANES:LANES + 1, :]).T
        y_ref[:, c] = (o * _silu(gate_ref[:, c].astype(F32))).astype(y_ref.dtype)


def _moba_attn(qb, kb, vbt, gate, nh):
    L = MOBA_BLOCK
    D = MOBA_DH
    nq = SEQ // L
    return pl.pallas_call(
        functools.partial(_moba_kernel, nh),
        out_shape=jax.ShapeDtypeStruct((N_TOK, MOBA_HEADS * D), BF16),
        grid=(BATCH, MOBA_HEADS // nh, nq),
        in_specs=[pl.BlockSpec((L, nh * D), lambda b, h, i: (b * nq + i, h)),
                  pl.BlockSpec((SEQ, nh * D), lambda b, h, i: (b, h)),
                  pl.BlockSpec((nh, nq, VT_ROWS, L), lambda b, h, i: (h, b, 0, 0)),
                  pl.BlockSpec((L, nh * D), lambda b, h, i: (b * nq + i, h))],
        out_specs=pl.BlockSpec((L, nh * D), lambda b, h, i: (b * nq + i, h)),
        scratch_shapes=[pltpu.VMEM((nh, nq, D), F32), pltpu.VMEM((nh, nq, L), F32),
                        pltpu.VMEM((2, nh, L, L), F32), pltpu.VMEM((2, nh, L, L), BF16),
                        pltpu.VMEM((nh, VT_ROWS, L), F32)],
        compiler_params=_cparams(("parallel", "parallel", "arbitrary")),
        name="moba_attn",
    )(qb, kb, vbt, gate)


def _mem_prep_kernel(mem_ref, g_ref, w_ref, kv_ref):
    mn = _rms(mem_ref[...], g_ref[...]).astype(BF16)
    kv_ref[...] = jnp.dot(mn, w_ref[...], preferred_element_type=F32).astype(BF16)


def _mem_prep(mem2d, g, w):
    g2 = g.reshape(1, -1)
    return pl.pallas_call(
        _mem_prep_kernel,
        out_shape=jax.ShapeDtypeStruct((BATCH * N_MEM, 2 * MEMQ_W), BF16),
        grid=(BATCH,),
        in_specs=[pl.BlockSpec((N_MEM, D_MODEL), lambda b: (b, 0)),
                  pl.BlockSpec(g2.shape, lambda b: (0, 0)),
                  pl.BlockSpec(w.shape, lambda b: (0, 0))],
        out_specs=pl.BlockSpec((N_MEM, 2 * MEMQ_W), lambda b: (b, 0)),
        compiler_params=_cparams(("parallel",)),
        name="mem_prep",
    )(mem2d, g2, w)


def _mem_attn_kernel(q_ref, kv_ref, gate_ref, y_ref):
    cols = [slice(h * MEM_DH, (h + 1) * MEM_DH) for h in range(MEM_HEADS)]
    ss = [_qk(q_ref[:, c], kv_ref[:, c]) for c in cols]
    ps = [jnp.exp(s - jnp.max(s, axis=1, keepdims=True)) for s in ss]
    for h, (c, p) in enumerate(zip(cols, ps)):
        l = jnp.sum(p, axis=1, keepdims=True)
        a = MEMQ_W + h * MEM_DH
        o = jnp.dot(p.astype(BF16), kv_ref[:, a:a + MEM_DH], preferred_element_type=F32) / l
        y_ref[:, c] = (o * _silu(gate_ref[:, c].astype(F32))).astype(y_ref.dtype)


def _mem_attn(qm, kvm, gate, tq):
    nq = SEQ // tq
    return pl.pallas_call(
        _mem_attn_kernel,
        out_shape=jax.ShapeDtypeStruct((N_TOK, MEMQ_W), BF16),
        grid=(BATCH, nq),
        in_specs=[pl.BlockSpec((tq, MEMQ_W), lambda b, i: (b * nq + i, 0)),
                  pl.BlockSpec((N_MEM, 2 * MEMQ_W), lambda b, i: (b, 0)),
                  pl.BlockSpec((tq, MEMQ_W), lambda b, i: (b * nq + i, 0))],
        out_specs=pl.BlockSpec((tq, MEMQ_W), lambda b, i: (b * nq + i, 0)),
        compiler_params=_cparams(("parallel", "parallel")),
        name="mem_attn",
    )(qm, kvm, gate)


def _band_bias(n, lo_off, hi_off):
    i = lax.broadcasted_iota(jnp.int32, (Q_BLOCK, 2 * Q_BLOCK), 0)
    c = lax.broadcasted_iota(jnp.int32, (Q_BLOCK, 2 * Q_BLOCK), 1)
    vis = (c - i >= lo_off) & (c - i <= hi_off) & ((c >= Q_BLOCK) | (n > 0))
    return jnp.where(vis, 0.0, NEG).astype(F32)


def _swa_kernel(q_ref, kvp_ref, kvo_ref, gate_ref, sink_ref, y_ref):
    QB = Q_BLOCK
    G = SWA_HEADS // SWA_KV_HEADS
    n = pl.program_id(1)
    bias = _band_bias(n, QB - (SWA_WINDOW - 1), QB)
    bias4 = jnp.concatenate([bias] * (G // 2), axis=0)
    k2 = jnp.concatenate([kvp_ref[:, :LANES], kvo_ref[:, :LANES]], axis=0).astype(F32)
    v2 = jnp.concatenate([kvp_ref[:, LANES:], kvo_ref[:, LANES:]], axis=0).astype(F32)
    k2r = pltpu.roll(k2, SWA_DH, 1)
    v2r = pltpu.roll(v2, SWA_DH, 1)
    lane = lax.broadcasted_iota(jnp.int32, (2 * QB, LANES), 1)
    lo = lane < SWA_DH
    lane_q = lax.broadcasted_iota(jnp.int32, (QB, LANES), 1)
    lo_q = lane_q < SWA_DH
    vs, ss = [], []
    for kv in range(SWA_KV_HEADS):
        ka, kb_ = (k2, k2r) if kv == 0 else (k2r, k2)
        va, vb_ = (v2, v2r) if kv == 0 else (v2r, v2)
        k_lo = jnp.where(lo, ka, 0.0).astype(BF16)
        k_hi = jnp.where(lo, 0.0, kb_).astype(BF16)
        vs.append((jnp.where(lo, va, 0.0).astype(BF16), jnp.where(lo, 0.0, vb_).astype(BF16)))
        base = kv * (G // 2)
        q4 = jnp.concatenate([q_ref[:, (base + p) * LANES:(base + p + 1) * LANES] for p in range(G // 2)],
                             axis=0)
        ss.append((_qk(q4, k_lo) + bias4, _qk(q4, k_hi) + bias4))
    stats = []
    for s_lo, s_hi in ss:
        m_lo = jnp.max(s_lo, axis=1, keepdims=True)
        m_hi = jnp.max(s_hi, axis=1, keepdims=True)
        p_lo = jnp.exp(s_lo - m_lo)
        p_hi = jnp.exp(s_hi - m_hi)
        stats.append((m_lo, m_hi, p_lo, p_hi, jnp.sum(p_lo, axis=1, keepdims=True),
                      jnp.sum(p_hi, axis=1, keepdims=True)))
    for kv in range(SWA_KV_HEADS):
        m_lo, m_hi, p_lo, p_hi, l_lo, l_hi = stats[kv]
        v_lo, v_hi = vs[kv]
        base = kv * (G // 2)
        o = (jnp.dot(p_lo.astype(BF16), v_lo, preferred_element_type=F32)
             + jnp.dot(p_hi.astype(BF16), v_hi, preferred_element_type=F32))
        lse_lo = m_lo + jnp.log(l_lo)
        lse_hi = m_hi + jnp.log(l_hi)
        for p in range(G // 2):
            r = slice(p * QB, (p + 1) * QB)
            c = slice((base + p) * LANES, (base + p + 1) * LANES)
            lse = jnp.where(lo_q, lse_lo[r], lse_hi[r])
            l = jnp.where(lo_q, l_lo[r], l_hi[r])
            w = _sigmoid(lse - sink_ref[:, c])
            y = (o[r] / l) * w * _silu(gate_ref[:, c].astype(F32))
            y_ref[:, c] = y.astype(y_ref.dtype)


def _swa_attn(qc, kvc, gate, sink_row):
    nb = SEQ // Q_BLOCK
    w = SWA_HEADS * SWA_DH
    return pl.pallas_call(
        _swa_kernel,
        out_shape=jax.ShapeDtypeStruct((N_TOK, w), BF16),
        grid=(BATCH, nb),
        in_specs=[pl.BlockSpec((Q_BLOCK, w), lambda b, n: (b * nb + n, 0)),
                  pl.BlockSpec((Q_BLOCK, 2 * LANES), lambda b, n: (jnp.maximum(b * nb + n - 1, 0), 0)),
                  pl.BlockSpec((Q_BLOCK, 2 * LANES), lambda b, n: (b * nb + n, 0)),
                  pl.BlockSpec((Q_BLOCK, w), lambda b, n: (b * nb + n, 0)),
                  pl.BlockSpec((1, w), lambda b, n: (0, 0))],
        out_specs=pl.BlockSpec((Q_BLOCK, w), lambda b, n: (b * nb + n, 0)),
        compiler_params=_cparams(("parallel", "parallel")),
        name="swa_attn",
    )(qc, kvc, kvc, gate, sink_row)


DIL_BLOCKS_PER_STEP = 2


def _dil_kernel(q_ref, kp_ref, ko_ref, vp_ref, vo_ref, o_ref, lse_ref):
    QB = Q_BLOCK
    R = DIL_BLOCKS_PER_STEP
    n = pl.program_id(2)
    lane = lax.broadcasted_iota(jnp.int32, (QB, LANES), 1)
    tasks = []
    for u in range(R):
        rows = slice(u * QB, (u + 1) * QB)
        bias = _band_bias(n if u == 0 else 1, 0, QB)
        for h in range(DIL_HEADS):
            c = slice(h * DIL_DH, (h + 1) * DIL_DH)
            k_prev = kp_ref[:, c] if u == 0 else ko_ref[(u - 1) * QB:u * QB, c]
            v_prev = vp_ref[:, c] if u == 0 else vo_ref[(u - 1) * QB:u * QB, c]
            k = jnp.concatenate([k_prev, ko_ref[rows, c]], axis=0)
            v = jnp.concatenate([v_prev, vo_ref[rows, c]], axis=0)
            tasks.append((u, h, _qk(q_ref[rows, c], k) + bias, v))
    soft = []
    for u, h, s, v in tasks:
        m = jnp.max(s, axis=1, keepdims=True)
        p = jnp.exp(s - m)
        soft.append((m, p, jnp.sum(p, axis=1, keepdims=True)))
    lse_all = [jnp.zeros((QB, LANES), F32) for _ in range(R)]
    for (u, h, _, v), (m, p, l) in zip(tasks, soft):
        o_ref[u * QB:(u + 1) * QB, h * DIL_DH:(h + 1) * DIL_DH] = (
            jnp.dot(p.astype(BF16), v, preferred_element_type=F32) / l).astype(o_ref.dtype)
        lse_all[u] = jnp.where(lane == h, m + jnp.log(l), lse_all[u])
    for u in range(R):
        lse_ref[u * QB:(u + 1) * QB, :] = lse_all[u]


def _dil_attn(qv, kv, vv, dil):
    w = DIL_HEADS * DIL_DH
    R = DIL_BLOCKS_PER_STEP
    L = SEQ // dil
    ns = L // (R * Q_BLOCK)
    cur = lambda b, r, n: (b * ns + n, r)
    prev = lambda b, r, n: (jnp.maximum((b * ns + n) * R - 1, 0), r)
    blk = (R * Q_BLOCK, w)
    pblk = (Q_BLOCK, w)
    return pl.pallas_call(
        _dil_kernel,
        out_shape=[jax.ShapeDtypeStruct((BATCH * L, dil * w), BF16),
                   jax.ShapeDtypeStruct((BATCH * L, dil * LANES), F32)],
        grid=(BATCH, dil, ns),
        in_specs=[pl.BlockSpec(blk, cur), pl.BlockSpec(pblk, prev), pl.BlockSpec(blk, cur),
                  pl.BlockSpec(pblk, prev), pl.BlockSpec(blk, cur)],
        out_specs=[pl.BlockSpec(blk, cur), pl.BlockSpec((R * Q_BLOCK, LANES), cur)],
        compiler_params=_cparams(("parallel", "parallel", "parallel")),
        name=f"dil_attn_d{dil}",
    )(qv, kv, kv, vv, vv)


def _dil_combine_kernel(tm, *refs):
    np_ = len(DIL_PATTERNS)
    o_refs, l_refs = refs[:np_], refs[np_:2 * np_]
    gate_ref, y_ref = refs[2 * np_], refs[2 * np_ + 1]
    stage = refs[2 * np_ + 2:]
    w = DIL_HEADS * DIL_DH
    os_, ls = [], []
    k = 0
    for (_, d), o_ref, l_ref in zip(DIL_PATTERNS, o_refs, l_refs):
        if d == 1:
            os_.append(lambda h, o_ref=o_ref: o_ref[:, h * DIL_DH:(h + 1) * DIL_DH].astype(F32))
            ls.append(l_ref[...])
            continue
        so, sl = stage[2 * k], stage[2 * k + 1]
        k += 1
        for r in range(d):
            for h in range(DIL_HEADS):
                a = r * w + h * DIL_DH
                so[h, pl.ds(r, tm // d, stride=d), :] = o_ref[:, a:a + DIL_DH].astype(F32)
            sl[pl.ds(r, tm // d, stride=d), :] = l_ref[:, r * LANES:(r + 1) * LANES]
        os_.append(lambda h, so=so: so[h])
        ls.append(sl[...])
    mx = functools.reduce(jnp.maximum, ls)
    es = [jnp.exp(x - mx) for x in ls]
    den = functools.reduce(lambda a, b: a + b, es)
    ws = [e / den for e in es]
    lane = lax.broadcasted_iota(jnp.int32, ws[0].shape, 1)
    for h in range(DIL_HEADS):
        c = slice(h * DIL_DH, (h + 1) * DIL_DH)
        wh = [jnp.sum(jnp.where(lane == h, wgt, 0.0), axis=1, keepdims=True) for wgt in ws]
        o = functools.reduce(lambda a, b: a + b, [wh[p] * os_[p](h) for p in range(np_)])
        y_ref[:, c] = (o * _silu(gate_ref[:, c].astype(F32))).astype(y_ref.dtype)


def _dil_combine(outs, lses, gate, tm):
    w = DIL_HEADS * DIL_DH
    in_specs = [pl.BlockSpec((tm // d, d * w), lambda i: (i, 0)) for _, d in DIL_PATTERNS]
    in_specs += [pl.BlockSpec((tm // d, d * LANES), lambda i: (i, 0)) for _, d in DIL_PATTERNS]
    in_specs.append(pl.BlockSpec((tm, w), lambda i: (i, 0)))
    scratch = []
    for _, d in DIL_PATTERNS:
        if d > 1:
            scratch += [pltpu.VMEM((DIL_HEADS, tm, DIL_DH), F32), pltpu.VMEM((tm, LANES), F32)]
    return pl.pallas_call(
        functools.partial(_dil_combine_kernel, tm),
        out_shape=jax.ShapeDtypeStruct((N_TOK, w), BF16),
        grid=(N_TOK // tm,),
        in_specs=in_specs,
        out_specs=pl.BlockSpec((tm, w), lambda i: (i, 0)),
        scratch_shapes=scratch,
        compiler_params=_cparams(("parallel",)),
        name="dil_combine",
    )(*outs, *lses, gate)


def _outproj_kernel(nparts, final, *refs):
    ys, ws = refs[:nparts], refs[nparts:2 * nparts]
    x_ref = refs[2 * nparts]
    o_ref = refs[-1]
    acc = x_ref[...]
    for y, w in zip(ys, ws):
        acc = acc + jnp.dot(y[...], w[...], preferred_element_type=F32)
    if final:
        acc = _rms(acc, refs[2 * nparts + 1][...])
    o_ref[...] = acc


def _outproj(ys, ws, x, final_g, tm):
    n_tok, d = x.shape
    in_specs = [pl.BlockSpec((tm, y.shape[1]), lambda i: (i, 0)) for y in ys]
    in_specs += [pl.BlockSpec(w.shape, lambda i: (0, 0)) for w in ws]
    in_specs.append(pl.BlockSpec((tm, d), lambda i: (i, 0)))
    args = [*ys, *ws, x]
    if final_g is not None:
        in_specs.append(pl.BlockSpec((1, d), lambda i: (0, 0)))
        args.append(final_g.reshape(1, d))
    return pl.pallas_call(
        functools.partial(_outproj_kernel, len(ys), final_g is not None),
        out_shape=jax.ShapeDtypeStruct((n_tok, d), F32),
        grid=(n_tok // tm,),
        in_specs=in_specs,
        out_specs=pl.BlockSpec((tm, d), lambda i: (i, 0)),
        compiler_params=_cparams(("parallel",)),
        name="outproj",
    )(*args)


PROJ_TM = 512
ATTN_TQ = 512
MLA_HEADS_PER_STEP = 2
MOBA_HEADS_PER_STEP = 4


def _even_layer(x, mem2d, norm_g, w_in, q_norm_g, w_uq, kv_norm_g, w_ukv, mem_norm_g, w_mem_kv, w_out):
    n_lat = MLA_Q_RANK + MLA_KV_RANK + MLA_ROPE
    wlat = jnp.pad(w_in[:, :n_lat], ((0, 0), (0, LANES - MLA_ROPE))).astype(BF16)
    n_bqkv = n_lat + 3 * MOBA_HEADS * MOBA_DH
    w_bqkv = w_in[:, n_lat:n_bqkv].astype(BF16)
    w_rest = w_in[:, n_bqkv:].astype(BF16)
    wuq = w_uq.reshape(MLA_Q_RANK, MLA_HEADS, MLA_NOPE + MLA_ROPE)
    wuq = jnp.pad(wuq, ((0, 0), (0, 0), (0, MLA_HW - MLA_NOPE - MLA_ROPE)))
    wuq = wuq.reshape(MLA_Q_RANK, MLA_HEADS * MLA_HW).astype(BF16)
    wukv = w_ukv.reshape(MLA_KV_RANK, MLA_HEADS, MLA_NOPE + MLA_V)
    wukv = jnp.concatenate([wukv[:, :, :MLA_NOPE].reshape(MLA_KV_RANK, -1),
                            wukv[:, :, MLA_NOPE:].reshape(MLA_KV_RANK, -1)], axis=1).astype(BF16)

    cos_a, sin_a = _rope_tables(MLA_ROPE, LANES)
    qa, ka, va = _mla_prep(x, norm_g, wlat, q_norm_g, wuq, kv_norm_g, wukv, cos_a, sin_a, ATTN_TQ)

    rot = MOBA_DH // ROT_FRAC
    cos_b, sin_b = _rope_tables(rot, MOBA_DH)
    sc = MOBA_DH ** -0.5 * LOG2E
    wb = MOBA_HEADS * MOBA_DH
    plain = [("plain",)]
    groups_qkv = [
        (wb, [("rope", 0, rot // 2, MOBA_DH, sc)] * MOBA_HEADS, "rows"),
        (wb, [("rope", 0, rot // 2, MOBA_DH, 1.0)] * MOBA_HEADS, "rows"),
        (wb, plain * MOBA_HEADS, ("vT", MOBA_BLOCK)),
    ]
    groups_rest = [
        (MEMQ_W, [("scale", MEM_DH ** -0.5)] * MEM_HEADS, "rows"),
        (MLA_HEADS * MLA_V, plain * MLA_HEADS, "rows"),
        (wb, plain * MOBA_HEADS, "rows"),
        (MEMQ_W, plain * MEM_HEADS, "rows"),
    ]
    qb, kb, vb = _inproj(x, norm_g, w_bqkv, groups_qkv, [cos_b, sin_b], PROJ_TM)
    qm, gate_a, gate_b, gate_m = _inproj(x, norm_g, w_rest, groups_rest, [], PROJ_TM)

    y_a = _mla_attn(qa, ka, va, gate_a, ATTN_TQ, MLA_HEADS_PER_STEP)
    y_b = _moba_attn(qb, kb, vb, gate_b, MOBA_HEADS_PER_STEP)
    kvm = _mem_prep(mem2d, mem_norm_g, w_mem_kv.astype(BF16))
    y_m = _mem_attn(qm, kvm, gate_m, ATTN_TQ)

    a, b = MLA_HEADS * MLA_V, MLA_HEADS * MLA_V + MOBA_HEADS * MOBA_DH
    return [y_a, y_b, y_m], [w_out[:a].astype(BF16), w_out[a:b].astype(BF16), w_out[b:].astype(BF16)]


def _odd_layer(x, mem2d, norm_g, w_in, sinks, mem_norm_g, w_mem_kv, w_out):
    rot_c = SWA_DH // ROT_FRAC
    rot_d = DIL_DH // ROT_FRAC
    cos_c, sin_c = _rope_tables(rot_c, SWA_DH)
    cos_d, sin_d = _rope_tables(rot_d, DIL_DH)
    wq = SWA_HEADS * SWA_DH
    wd = DIL_HEADS * DIL_DH
    wkv = 2 * SWA_KV_HEADS * SWA_DH
    rc = lambda s: ("rope", 0, rot_c // 2, SWA_DH, s)
    rd = lambda s: ("rope", 1, rot_d // 2, DIL_DH, s)
    plain = [("plain",)]
    groups_qkv = [
        (wq, [rc(SWA_DH ** -0.5)] * (wq // LANES), "rows"),
        (wkv, [rc(1.0), ("plain",)], "rows"),
        (wd, [rd(DIL_DH ** -0.5)] * DIL_HEADS, "dilated"),
        (wd, [rd(1.0)] * DIL_HEADS, "dilated"),
        (wd, plain * DIL_HEADS, "dilated"),
    ]
    groups_rest = [
        (MEMQ_W, [("scale", MEM_DH ** -0.5)] * MEM_HEADS, "rows"),
        (wq, plain * (wq // LANES), "rows"),
        (wd, plain * DIL_HEADS, "rows"),
        (MEMQ_W, plain * MEM_HEADS, "rows"),
    ]
    n_qkv = wq + wkv + 3 * wd
    res = _inproj(x, norm_g, w_in[:, :n_qkv].astype(BF16), groups_qkv, [cos_c, sin_c, cos_d, sin_d], PROJ_TM)
    qc, kvc = res[0], res[1]
    nv = 1 + len(DIL_VIEWS)
    qd, kd, vd = (res[2 + i * nv:2 + (i + 1) * nv] for i in range(3))
    qm, gate_c, gate_d, gate_m = _inproj(x, norm_g, w_in[:, n_qkv:].astype(BF16), groups_rest, [], PROJ_TM)

    sink_row = jnp.repeat(sinks.astype(F32), SWA_DH).reshape(1, wq)
    y_c = _swa_attn(qc, kvc, gate_c, sink_row)
    view_of = {1: 0, **{d: 1 + i for i, d in enumerate(DIL_VIEWS)}}
    outs, lses = zip(*[_dil_attn(qd[view_of[dil]], kd[view_of[dil]], vd[view_of[dil]], dil)
                       for _, dil in DIL_PATTERNS])
    y_d = _dil_combine(outs, lses, gate_d, PROJ_TM)
    kvm = _mem_prep(mem2d, mem_norm_g, w_mem_kv.astype(BF16))
    y_m = _mem_attn(qm, kvm, gate_m, ATTN_TQ)

    return [y_c, y_d, y_m], [w_out[:wq].astype(BF16), w_out[wq:wq + wd].astype(BF16),
                             w_out[wq + wd:].astype(BF16)]


def kernel(x, mem, ev_norm_g, ev_w_in, ev_q_norm_g, ev_w_uq, ev_kv_norm_g, ev_w_ukv, ev_mem_norm_g,
           ev_w_mem_kv, ev_w_out, od_norm_g, od_w_in, od_sinks, od_mem_norm_g, od_w_mem_kv, od_w_out,
           final_norm_g):
    x2 = x.reshape(N_TOK, D_MODEL)
    mem2d = mem.reshape(BATCH * N_MEM, D_MODEL)
    ys, ws = _even_layer(x2, mem2d, ev_norm_g[0], ev_w_in[0], ev_q_norm_g[0], ev_w_uq[0], ev_kv_norm_g[0],
                         ev_w_ukv[0], ev_mem_norm_g[0], ev_w_mem_kv[0], ev_w_out[0])
    x2 = _outproj(ys, ws, x2, None, PROJ_TM)
    ys, ws = _odd_layer(x2, mem2d, od_norm_g[0], od_w_in[0], od_sinks[0], od_mem_norm_g[0],
                        od_w_mem_kv[0], od_w_out[0])
    x2 = _outproj(ys, ws, x2, final_norm_g, PROJ_TM)
    return x2.reshape(BATCH, SEQ, D_MODEL)
```

```python
import functools

import numpy as np
import jax
import jax.numpy as jnp
from jax import lax
from jax.experimental import pallas as pl
from jax.experimental.pallas import tpu as pltpu

D_MODEL = 2048
BATCH = 4
SEQ = 4096
N_TOK = BATCH * SEQ
N_MEM = 256
ROPE_THETA = 500000.0
ROT_FRAC = 4
EPS = 1e-6

MLA_HEADS = 8
MLA_Q_RANK = 512
MLA_KV_RANK = 256
MLA_NOPE = 128
MLA_ROPE = 64
MLA_V = 128
MOBA_HEADS = 8
MOBA_DH = 128
MOBA_BLOCK = 256
MOBA_TOPK = 3
SWA_HEADS = 16
SWA_KV_HEADS = 2
SWA_DH = 64
SWA_WINDOW = 128
DIL_HEADS = 6
DIL_DH = 128
DIL_PATTERNS = ((128, 1), (512, 4), (2048, 16))
MEM_HEADS = 4
MEM_DH = 128
MEMQ_W = MEM_HEADS * MEM_DH
Q_BLOCK = 128

LANES = 128
VMEM_LIMIT = 52 * 1024 * 1024

NEG = -1e30
BF16 = jnp.bfloat16
F32 = jnp.float32


def _cparams(sem):
    return pltpu.CompilerParams(dimension_semantics=sem, vmem_limit_bytes=VMEM_LIMIT)


def _rms(x, g):
    ms = jnp.mean(x * x, axis=-1, keepdims=True)
    return (x * lax.rsqrt(ms + EPS)) * g


def _silu(g):
    return g / (1.0 + jnp.exp(-g))


def _sigmoid(z):
    return 1.0 / (1.0 + jnp.exp(-z))


def _rope_slab(x, cos_t, sin_t, half, period):
    lane = lax.broadcasted_iota(jnp.int32, x.shape, 1)
    up = pltpu.roll(x, LANES - half, 1)
    dn = pltpu.roll(x, half, 1)
    sw = jnp.where((lane & (period - 1)) < half, up, dn)
    return x * cos_t + sw * sin_t


def _qk(q, k):
    return lax.dot_general(q, k, (((1,), (1,)), ((), ())), preferred_element_type=F32)


def _rope_tables(rot_dim, period):
    half = rot_dim // 2
    inv = 1.0 / (ROPE_THETA ** (jnp.arange(0, rot_dim, 2, dtype=F32) / rot_dim))
    ang = jnp.arange(SEQ, dtype=F32)[:, None] * inv[None, :]
    c, s = jnp.cos(ang), jnp.sin(ang)
    d = np.arange(LANES) % period
    idx = d % half
    cos_t = jnp.where(d < rot_dim, c[:, idx], 1.0)
    sin_t = jnp.where(d < half, -s[:, idx], jnp.where(d < rot_dim, s[:, idx], 0.0))
    return cos_t.astype(F32), sin_t.astype(F32)


PROJ_CHUNK = 512
DIL_VIEWS = tuple(d for _, d in DIL_PATTERNS if d > 1)


def _inproj_kernel(groups, ntab, tm, x_ref, g_ref, *rest):
    tabs = rest[:2 * ntab]
    w_ref = rest[2 * ntab]
    ndil = sum(layout == "dilated" for _, _, layout in groups)
    refs = list(rest[2 * ntab + 1:len(rest) - ndil])
    stage = list(rest[len(rest) - ndil:])
    xn = _rms(x_ref[...], g_ref[...]).astype(BF16)
    c0 = 0
    for ncols, kinds, layout in groups:
        o_ref = refs.pop(0)
        views = [refs.pop(0) for _ in DIL_VIEWS] if layout == "dilated" else []
        st_ref = stage.pop(0) if layout == "dilated" else None
        for s0 in range(0, ncols, PROJ_CHUNK):
            n = min(PROJ_CHUNK, ncols - s0)
            acc = jnp.dot(xn, w_ref[:, c0 + s0:c0 + s0 + n], preferred_element_type=F32)
            for c in range(n // LANES):
                col = s0 + c * LANES
                kind = kinds[col // LANES]
                piece = acc[:, c * LANES:(c + 1) * LANES]
                if kind[0] == "rope":
                    _, ti, half, period, sc = kind
                    piece = _rope_slab(piece, tabs[2 * ti][...], tabs[2 * ti + 1][...], half, period)
                    if sc != 1.0:
                        piece = piece * sc
                elif kind[0] == "scale":
                    piece = piece * kind[1]
                if layout == "rows" or layout == "dilated":
                    o_ref[:, col:col + LANES] = piece.astype(o_ref.dtype)
                    if layout == "dilated":
                        st_ref[col // LANES] = piece
                else:
                    tb = layout[1]
                    for u in range(tm // tb):
                        o_ref[col // LANES, u] = _vt_block(piece[u * tb:(u + 1) * tb, :])
        for d, v_ref in zip(DIL_VIEWS, views):
            for r in range(d):
                for c in range(ncols // LANES):
                    a = r * ncols + c * LANES
                    v_ref[:, a:a + LANES] = st_ref[c, pl.ds(r, tm // d, stride=d), :].astype(v_ref.dtype)
        c0 += ncols


def _inproj(x, g, w, groups, tables, tm):
    n_tok, d = x.shape
    assert sum(nc for nc, _, _ in groups) == w.shape[1]
    seq_tiles = SEQ // tm
    in_specs = [pl.BlockSpec((tm, d), lambda i: (i, 0)),
                pl.BlockSpec((1, d), lambda i: (0, 0))]
    args = [x, g.reshape(1, d)]
    for t in tables:
        in_specs.append(pl.BlockSpec((tm, LANES), lambda i: (i % seq_tiles, 0)))
        args.append(t)
    in_specs.append(pl.BlockSpec(w.shape, lambda i: (0, 0), pipeline_mode=pl.Buffered(1)))
    args.append(w)
    out_specs, out_shapes = [], []
    for nc, _, layout in groups:
        if layout == "rows" or layout == "dilated":
            out_specs.append(pl.BlockSpec((tm, nc), lambda i: (i, 0)))
            out_shapes.append(jax.ShapeDtypeStruct((n_tok, nc), BF16))
            if layout == "dilated":
                for dil in DIL_VIEWS:
                    out_specs.append(pl.BlockSpec((tm // dil, dil * nc), lambda i: (i, 0)))
                    out_shapes.append(jax.ShapeDtypeStruct((n_tok // dil, dil * nc), BF16))
        else:
            tb = layout[1]
            out_specs.append(pl.BlockSpec((nc // LANES, tm // tb, VT_ROWS, tb), lambda i: (0, i, 0, 0)))
            out_shapes.append(jax.ShapeDtypeStruct((nc // LANES, n_tok // tb, VT_ROWS, tb), BF16))
    return pl.pallas_call(
        functools.partial(_inproj_kernel, groups, len(tables) // 2, tm),
        out_shape=out_shapes,
        grid=(n_tok // tm,),
        in_specs=in_specs,
        out_specs=out_specs,
        scratch_shapes=[pltpu.VMEM((nc // LANES, tm, LANES), F32) for nc, _, layout in groups
                        if layout == "dilated"],
        compiler_params=_cparams(("parallel",)),
        name="inproj",
    )(*args)


MLA_HW = 256


def _mla_prep_kernel(scale, x_ref, g_ref, wlat_ref, qg_ref, wuq_ref, kvg_ref, wukv_ref,
                     cos_ref, sin_ref, qa_ref, ka_ref, va_ref):
    xn = _rms(x_ref[...], g_ref[...]).astype(BF16)
    lat = jnp.dot(xn, wlat_ref[...], preferred_element_type=F32)
    cq = _rms(lat[:, :MLA_Q_RANK], qg_ref[...]).astype(BF16)
    ckv = _rms(lat[:, MLA_Q_RANK:MLA_Q_RANK + MLA_KV_RANK], kvg_ref[...]).astype(BF16)
    kpe = lat[:, MLA_Q_RANK + MLA_KV_RANK:]
    cos_t, sin_t = cos_ref[...], sin_ref[...]
    kpe = _rope_slab(kpe, cos_t, sin_t, MLA_ROPE // 2, LANES).astype(BF16)
    q = jnp.dot(cq, wuq_ref[...], preferred_element_type=F32)
    kv = jnp.dot(ckv, wukv_ref[...], preferred_element_type=F32)
    for h in range(MLA_HEADS):
        a = h * MLA_HW
        qa_ref[:, a:a + LANES] = (q[:, a:a + LANES] * scale).astype(BF16)
        qr = _rope_slab(q[:, a + LANES:a + 2 * LANES], cos_t, sin_t, MLA_ROPE // 2, LANES)
        qa_ref[:, a + LANES:a + 2 * LANES] = (qr * scale).astype(BF16)
        ka_ref[:, a:a + LANES] = kv[:, h * LANES:(h + 1) * LANES].astype(BF16)
        ka_ref[:, a + LANES:a + 2 * LANES] = kpe
        v0 = MLA_HEADS * MLA_NOPE + h * MLA_V
        va_ref[h, 0] = _vt_block(kv[:, v0:v0 + MLA_V])


def _mla_prep(x, g, wlat, qg, wuq, kvg, wukv, cos_t, sin_t, tm):
    n_tok, d = x.shape
    seq_tiles = SEQ // tm
    full = lambda a: pl.BlockSpec(a.shape, lambda i: (0, 0))
    qg2, kvg2, g2 = qg.reshape(1, -1), kvg.reshape(1, -1), g.reshape(1, d)
    scale = (MLA_NOPE + MLA_ROPE) ** -0.5 * LOG2E
    return pl.pallas_call(
        functools.partial(_mla_prep_kernel, scale),
        out_shape=[jax.ShapeDtypeStruct((n_tok, MLA_HEADS * MLA_HW), BF16),
                   jax.ShapeDtypeStruct((n_tok, MLA_HEADS * MLA_HW), BF16),
                   jax.ShapeDtypeStruct((MLA_HEADS, n_tok // tm, VT_ROWS, tm), BF16)],
        grid=(n_tok // tm,),
        in_specs=[pl.BlockSpec((tm, d), lambda i: (i, 0)), full(g2), full(wlat), full(qg2), full(wuq),
                  full(kvg2), full(wukv),
                  pl.BlockSpec((tm, LANES), lambda i: (i % seq_tiles, 0)),
                  pl.BlockSpec((tm, LANES), lambda i: (i % seq_tiles, 0))],
        out_specs=[pl.BlockSpec((tm, MLA_HEADS * MLA_HW), lambda i: (i, 0)),
                   pl.BlockSpec((tm, MLA_HEADS * MLA_HW), lambda i: (i, 0)),
                   pl.BlockSpec((MLA_HEADS, 1, VT_ROWS, tm), lambda i: (0, i, 0, 0))],
        compiler_params=_cparams(("parallel",)),
        name="mla_prep",
    )(x, g2, wlat, qg2, wuq, kvg2, wukv, cos_t, sin_t)


LOG2E = 1.4426950408889634
VT_TAIL = 16
VT_ROWS = LANES + VT_TAIL


def _vt_block(v):
    tb = v.shape[0]
    row = lax.broadcasted_iota(jnp.int32, (VT_TAIL, tb), 0)
    tail = jnp.where(row == 0, 1.0, 0.0).astype(F32)
    return jnp.concatenate([v.T, tail], axis=0).astype(BF16)


def _flash_finish_all(nh, n, tq, dv, acc_ref, gate_ref, y_ref):
    def body(c, _):
        rows = pl.ds(pl.multiple_of(c * tq, tq), tq)
        for g in range(nh):
            cols = slice(g * dv, (g + 1) * dv)
            a = acc_ref.at[g, c]
            o = (a[:LANES, :] / a[LANES:LANES + 1, :]).T
            y_ref[rows, cols] = (o * _silu(gate_ref[rows, cols].astype(F32))).astype(y_ref.dtype)
        return 0

    lax.fori_loop(0, n, body, 0)


def _flash_flat_t(nh, n, tq, score_fn, row_bias, vt_fn, s_ref, p_ref, acc_ref, m_ref):
    n_off = n * (n - 1) // 2
    assert n % 2 == 0 and n_off % 2 == 0
    acc_ref[...] = jnp.zeros_like(acc_ref)
    p_ref[...] = jnp.zeros_like(p_ref)
    m_ref[...] = jnp.full(m_ref.shape, NEG, F32)
    key = lax.broadcasted_iota(jnp.int32, (tq, tq), 0)
    qry = lax.broadcasted_iota(jnp.int32, (tq, tq), 1)

    def diag_scores(c):
        return lambda g: jnp.where(key <= qry, score_fn(g, c, c), NEG)

    def off_scores(j, c):
        return lambda g: score_fn(g, j, jnp.minimum(c, n - 1))

    def step(rd, wr, nxt, cur, prev, alphas):
        (jc, cc), (jp, cp) = cur, prev
        for g in range(nh):
            s_ref[wr, g] = nxt(g)
        pend = [jnp.dot(vt_fn(g, jp), p_ref[wr, g], preferred_element_type=F32) for g in range(nh)]
        out = []
        for g in range(nh):
            s = s_ref[rd, g]
            m_prev = m_ref[g, cc]
            if row_bias is None:
                m_new = jnp.maximum(m_prev, jnp.max(s, axis=0, keepdims=True))
                p = jnp.exp2((s - m_new).astype(BF16))
            else:
                nbs = row_bias(g, jc, cc)
                hk = tq // len(nbs)
                subs = [s[u * hk:(u + 1) * hk, :] for u in range(len(nbs))]
                m_new = m_prev
                for sub, nb in zip(subs, nbs):
                    m_new = jnp.maximum(m_new, jnp.max(sub, axis=0, keepdims=True) + nb)
                p = jnp.concatenate([jnp.exp2((sub - (m_new - nb)).astype(BF16))
                                     for sub, nb in zip(subs, nbs)], axis=0)
            m_ref[g, cc] = m_new
            acc_ref[g, cp] = alphas[g] * acc_ref[g, cp] + pend[g]
            p_ref[rd, g] = p
            out.append(jnp.exp2(m_prev - m_new))
        return tuple(out)

    def advance(j, c):
        wrap = j + 1 >= c
        return jnp.where(wrap, 0, j + 1), jnp.where(wrap, c + 1, c)

    for g in range(nh):
        s_ref[0, g] = diag_scores(0)(g)
    alphas = (jnp.ones((1, tq), F32),) * nh

    def diag_pair(u, alphas):
        i = 2 * u
        ip = jnp.maximum(i - 1, 0)
        alphas = step(0, 1, diag_scores(i + 1), (i, i), (ip, ip), alphas)
        return step(1, 0, diag_scores(i + 2), (i + 1, i + 1), (i, i), alphas)

    alphas = lax.fori_loop(0, (n - 2) // 2, diag_pair, alphas)
    alphas = step(0, 1, diag_scores(n - 1), (n - 2, n - 2), (n - 3, n - 3), alphas)
    alphas = step(1, 0, off_scores(0, 1), (n - 1, n - 1), (n - 2, n - 2), alphas)

    def off_pair(u, carry):
        jc, cc, jp, cp = carry[:4]
        jn, cn = advance(jc, cc)
        alphas = step(0, 1, off_scores(jn, cn), (jc, cc), (jp, cp), carry[4:])
        jn2, cn2 = advance(jn, cn)
        alphas = step(1, 0, off_scores(jn2, cn2), (jn, cn), (jc, cc), alphas)
        return (jn2, cn2, jn, cn) + tuple(alphas)

    i32 = jnp.int32
    carry = lax.fori_loop(0, n_off // 2, off_pair, (i32(0), i32(1), i32(n - 1), i32(n - 1)) + tuple(alphas))
    jl, cl = carry[2], carry[3]
    for g in range(nh):
        pend = jnp.dot(vt_fn(g, jl), p_ref[1, g], preferred_element_type=F32)
        acc_ref[g, cl] = carry[4 + g] * acc_ref[g, cl] + pend


def _rows(i, t):
    return pl.ds(i * t, t) if isinstance(i, int) else pl.ds(pl.multiple_of(i * t, t), t)


def _flash_scratch(nh, n, tq):
    return [pltpu.VMEM((2, nh, tq, tq), F32), pltpu.VMEM((2, nh, tq, tq), BF16),
            pltpu.VMEM((nh, n, VT_ROWS, tq), F32), pltpu.VMEM((nh, n, 1, tq), F32)]


def _mla_kernel(tq, nh, q_ref, k_ref, vt_ref, gate_ref, y_ref, s_ref, p_ref, acc_ref, m_ref):
    n = SEQ // tq

    def scores(g, j, c):
        cols = slice(g * MLA_HW, (g + 1) * MLA_HW)
        return _qk(k_ref[_rows(j, tq), cols], q_ref[_rows(c, tq), cols])

    _flash_flat_t(nh, n, tq, scores, None, lambda g, j: vt_ref[g, j], s_ref, p_ref, acc_ref, m_ref)
    _flash_finish_all(nh, n, tq, MLA_V, acc_ref, gate_ref, y_ref)


def _mla_attn(qa, ka, vat, gate, tq, nh):
    nq = SEQ // tq
    return pl.pallas_call(
        functools.partial(_mla_kernel, tq, nh),
        out_shape=jax.ShapeDtypeStruct((N_TOK, MLA_HEADS * MLA_V), BF16),
        grid=(BATCH, MLA_HEADS // nh),
        in_specs=[pl.BlockSpec((SEQ, nh * MLA_HW), lambda b, h: (b, h)),
                  pl.BlockSpec((SEQ, nh * MLA_HW), lambda b, h: (b, h)),
                  pl.BlockSpec((nh, nq, VT_ROWS, tq), lambda b, h: (h, b, 0, 0)),
                  pl.BlockSpec((SEQ, nh * MLA_V), lambda b, h: (b, h))],
        out_specs=pl.BlockSpec((SEQ, nh * MLA_V), lambda b, h: (b, h)),
        scratch_shapes=_flash_scratch(nh, nq, tq),
        compiler_params=_cparams(("parallel", "parallel")),
        name="mla_attn",
    )(qa, ka, vat, gate)


MOBA_TILE = 2 * MOBA_BLOCK


def _moba_flat_kernel(nh, q_ref, k_ref, vt_ref, gate_ref, y_ref, km_ref, nb_ref, s_ref, p_ref, acc_ref, m_ref):
    L = MOBA_BLOCK
    T = MOBA_TILE
    D = MOBA_DH
    nblk = SEQ // L
    ntile = SEQ // T
    cols = [slice(g * D, (g + 1) * D) for g in range(nh)]

    for g in range(nh):
        for j in range(nblk):
            blk = k_ref[j * L:(j + 1) * L, cols[g]].astype(F32)
            km_ref[g, j:j + 1, :] = jnp.sum(blk, axis=0, keepdims=True) * (1.0 / L)

    blk_i = lax.broadcasted_iota(jnp.int32, (nblk, T), 0)
    blk_f = blk_i.astype(F32)
    second = lax.broadcasted_iota(jnp.int32, (nblk, T), 1) >= L

    def select(c, carry):
        own = 2 * c + second.astype(jnp.int32)
        past = blk_i < own
        for g in range(nh):
            gt = jnp.where(past, _qk(km_ref[g].astype(BF16), q_ref[_rows(c, T), cols[g]]), -jnp.inf)
            sel = jnp.zeros((nblk, T), jnp.bool_)
            for _ in range(MOBA_TOPK):
                mx = jnp.max(gt, axis=0, keepdims=True)
                first = jnp.min(jnp.where(gt == mx, blk_f, float(nblk)), axis=0, keepdims=True)
                pick = blk_f == first
                sel = jnp.logical_or(sel, pick)
                gt = jnp.where(pick, -jnp.inf, gt)
            keep = jnp.logical_or(jnp.logical_and(sel, past), blk_i == own)
            nb_ref[g, c] = jnp.where(keep, 0.0, NEG)
        return carry

    lax.fori_loop(0, ntile, select, 0)

    def scores(g, j, c):
        return _qk(k_ref[_rows(j, T), cols[g]], q_ref[_rows(c, T), cols[g]])

    def bias(g, j, c):
        return [nb_ref[g, c, pl.ds(2 * j, 1), :], nb_ref[g, c, pl.ds(2 * j + 1, 1), :]]

    _flash_flat_t(nh, ntile, T, scores, bias, lambda g, j: vt_ref[g, j], s_ref, p_ref, acc_ref, m_ref)
    _flash_finish_all(nh, ntile, T, D, acc_ref, gate_ref, y_ref)


def _moba_flat_attn(qb, kb, vbt, gate, nh):
    T = MOBA_TILE
    D = MOBA_DH
    nblk = SEQ // MOBA_BLOCK
    ntile = SEQ // T
    return pl.pallas_call(
        functools.partial(_moba_flat_kernel, nh),
        out_shape=jax.ShapeDtypeStruct((N_TOK, MOBA_HEADS * D), BF16),
        grid=(BATCH, MOBA_HEADS // nh),
        in_specs=[pl.BlockSpec((SEQ, nh * D), lambda b, h: (b, h)),
                  pl.BlockSpec((SEQ, nh * D), lambda b, h: (b, h)),
                  pl.BlockSpec((nh, ntile, VT_ROWS, T), lambda b, h: (h, b, 0, 0)),
                  pl.BlockSpec((SEQ, nh * D), lambda b, h: (b, h))],
        out_specs=pl.BlockSpec((SEQ, nh * D), lambda b, h: (b, h)),
        scratch_shapes=[pltpu.VMEM((nh, nblk, D), F32), pltpu.VMEM((nh, ntile, nblk, T), F32)]
        + _flash_scratch(nh, ntile, T),
        compiler_params=_cparams(("parallel", "parallel")),
        name="moba_attn",
    )(qb, kb, vbt, gate)


def _mem_prep_kernel(mem_ref, g_ref, w_ref, kv_ref):
    mn = _rms(mem_ref[...], g_ref[...]).astype(BF16)
    kv_ref[...] = jnp.dot(mn, w_ref[...], preferred_element_type=F32).astype(BF16)


def _mem_prep(mem2d, g, w):
    g2 = g.reshape(1, -1)
    return pl.pallas_call(
        _mem_prep_kernel,
        out_shape=jax.ShapeDtypeStruct((BATCH * N_MEM, 2 * MEMQ_W), BF16),
        grid=(BATCH,),
        in_specs=[pl.BlockSpec((N_MEM, D_MODEL), lambda b: (b, 0)),
                  pl.BlockSpec(g2.shape, lambda b: (0, 0)),
                  pl.BlockSpec(w.shape, lambda b: (0, 0))],
        out_specs=pl.BlockSpec((N_MEM, 2 * MEMQ_W), lambda b: (b, 0)),
        compiler_params=_cparams(("parallel",)),
        name="mem_prep",
    )(mem2d, g2, w)


def _mem_attn_kernel(q_ref, kv_ref, gate_ref, y_ref):
    cols = [slice(h * MEM_DH, (h + 1) * MEM_DH) for h in range(MEM_HEADS)]
    ss = [_qk(q_ref[:, c], kv_ref[:, c]) for c in cols]
    ps = [jnp.exp(s - jnp.max(s, axis=1, keepdims=True)) for s in ss]
    for h, (c, p) in enumerate(zip(cols, ps)):
        l = jnp.sum(p, axis=1, keepdims=True)
        a = MEMQ_W + h * MEM_DH
        o = jnp.dot(p.astype(BF16), kv_ref[:, a:a + MEM_DH], preferred_element_type=F32) / l
        y_ref[:, c] = (o * _silu(gate_ref[:, c].astype(F32))).astype(y_ref.dtype)


def _mem_attn(qm, kvm, gate, tq):
    nq = SEQ // tq
    return pl.pallas_call(
        _mem_attn_kernel,
        out_shape=jax.ShapeDtypeStruct((N_TOK, MEMQ_W), BF16),
        grid=(BATCH, nq),
        in_specs=[pl.BlockSpec((tq, MEMQ_W), lambda b, i: (b * nq + i, 0)),
                  pl.BlockSpec((N_MEM, 2 * MEMQ_W), lambda b, i: (b, 0)),
                  pl.BlockSpec((tq, MEMQ_W), lambda b, i: (b * nq + i, 0))],
        out_specs=pl.BlockSpec((tq, MEMQ_W), lambda b, i: (b * nq + i, 0)),
        compiler_params=_cparams(("parallel", "parallel")),
        name="mem_attn",
    )(qm, kvm, gate)


def _band_bias(n, lo_off, hi_off):
    i = lax.broadcasted_iota(jnp.int32, (Q_BLOCK, 2 * Q_BLOCK), 0)
    c = lax.broadcasted_iota(jnp.int32, (Q_BLOCK, 2 * Q_BLOCK), 1)
    vis = (c - i >= lo_off) & (c - i <= hi_off) & ((c >= Q_BLOCK) | (n > 0))
    return jnp.where(vis, 0.0, NEG).astype(F32)


def _swa_kernel(q_ref, kvp_ref, kvo_ref, gate_ref, sink_ref, y_ref):
    QB = Q_BLOCK
    G = SWA_HEADS // SWA_KV_HEADS
    n = pl.program_id(1)
    bias = _band_bias(n, QB - (SWA_WINDOW - 1), QB)
    bias4 = jnp.concatenate([bias] * (G // 2), axis=0)
    k2 = jnp.concatenate([kvp_ref[:, :LANES], kvo_ref[:, :LANES]], axis=0).astype(F32)
    v2 = jnp.concatenate([kvp_ref[:, LANES:], kvo_ref[:, LANES:]], axis=0).astype(F32)
    k2r = pltpu.roll(k2, SWA_DH, 1)
    v2r = pltpu.roll(v2, SWA_DH, 1)
    lane = lax.broadcasted_iota(jnp.int32, (2 * QB, LANES), 1)
    lo = lane < SWA_DH
    lane_q = lax.broadcasted_iota(jnp.int32, (QB, LANES), 1)
    lo_q = lane_q < SWA_DH
    vs, ss = [], []
    for kv in range(SWA_KV_HEADS):
        ka, kb_ = (k2, k2r) if kv == 0 else (k2r, k2)
        va, vb_ = (v2, v2r) if kv == 0 else (v2r, v2)
        k_lo = jnp.where(lo, ka, 0.0).astype(BF16)
        k_hi = jnp.where(lo, 0.0, kb_).astype(BF16)
        vs.append((jnp.where(lo, va, 0.0).astype(BF16), jnp.where(lo, 0.0, vb_).astype(BF16)))
        base = kv * (G // 2)
        q4 = jnp.concatenate([q_ref[:, (base + p) * LANES:(base + p + 1) * LANES] for p in range(G // 2)],
                             axis=0)
        ss.append((_qk(q4, k_lo) + bias4, _qk(q4, k_hi) + bias4))
    stats = []
    for s_lo, s_hi in ss:
        m_lo = jnp.max(s_lo, axis=1, keepdims=True)
        m_hi = jnp.max(s_hi, axis=1, keepdims=True)
        p_lo = jnp.exp(s_lo - m_lo)
        p_hi = jnp.exp(s_hi - m_hi)
        stats.append((m_lo, m_hi, p_lo, p_hi, jnp.sum(p_lo, axis=1, keepdims=True),
                      jnp.sum(p_hi, axis=1, keepdims=True)))
    for kv in range(SWA_KV_HEADS):
        m_lo, m_hi, p_lo, p_hi, l_lo, l_hi = stats[kv]
        v_lo, v_hi = vs[kv]
        base = kv * (G // 2)
        o = (jnp.dot(p_lo.astype(BF16), v_lo, preferred_element_type=F32)
             + jnp.dot(p_hi.astype(BF16), v_hi, preferred_element_type=F32))
        lse_lo = m_lo + jnp.log(l_lo)
        lse_hi = m_hi + jnp.log(l_hi)
        for p in range(G // 2):
            r = slice(p * QB, (p + 1) * QB)
            c = slice((base + p) * LANES, (base + p + 1) * LANES)
            lse = jnp.where(lo_q, lse_lo[r], lse_hi[r])
            l = jnp.where(lo_q, l_lo[r], l_hi[r])
            w = _sigmoid(lse - sink_ref[:, c])
            y = (o[r] / l) * w * _silu(gate_ref[:, c].astype(F32))
            y_ref[:, c] = y.astype(y_ref.dtype)


def _swa_attn(qc, kvc, gate, sink_row):
    nb = SEQ // Q_BLOCK
    w = SWA_HEADS * SWA_DH
    return pl.pallas_call(
        _swa_kernel,
        out_shape=jax.ShapeDtypeStruct((N_TOK, w), BF16),
        grid=(BATCH, nb),
        in_specs=[pl.BlockSpec((Q_BLOCK, w), lambda b, n: (b * nb + n, 0)),
                  pl.BlockSpec((Q_BLOCK, 2 * LANES), lambda b, n: (jnp.maximum(b * nb + n - 1, 0), 0)),
                  pl.BlockSpec((Q_BLOCK, 2 * LANES), lambda b, n: (b * nb + n, 0)),
                  pl.BlockSpec((Q_BLOCK, w), lambda b, n: (b * nb + n, 0)),
                  pl.BlockSpec((1, w), lambda b, n: (0, 0))],
        out_specs=pl.BlockSpec((Q_BLOCK, w), lambda b, n: (b * nb + n, 0)),
        compiler_params=_cparams(("parallel", "parallel")),
        name="swa_attn",
    )(qc, kvc, kvc, gate, sink_row)


DIL_BLOCKS_PER_STEP = 2


def _dil_kernel(q_ref, kp_ref, ko_ref, vp_ref, vo_ref, o_ref, lse_ref):
    QB = Q_BLOCK
    R = DIL_BLOCKS_PER_STEP
    n = pl.program_id(2)
    lane = lax.broadcasted_iota(jnp.int32, (QB, LANES), 1)
    tasks = []
    for u in range(R):
        rows = slice(u * QB, (u + 1) * QB)
        bias = _band_bias(n if u == 0 else 1, 0, QB)
        for h in range(DIL_HEADS):
            c = slice(h * DIL_DH, (h + 1) * DIL_DH)
            k_prev = kp_ref[:, c] if u == 0 else ko_ref[(u - 1) * QB:u * QB, c]
            v_prev = vp_ref[:, c] if u == 0 else vo_ref[(u - 1) * QB:u * QB, c]
            k = jnp.concatenate([k_prev, ko_ref[rows, c]], axis=0)
            v = jnp.concatenate([v_prev, vo_ref[rows, c]], axis=0)
            tasks.append((u, h, _qk(q_ref[rows, c], k) + bias, v))
    soft = []
    for u, h, s, v in tasks:
        m = jnp.max(s, axis=1, keepdims=True)
        p = jnp.exp(s - m)
        soft.append((m, p, jnp.sum(p, axis=1, keepdims=True)))
    lse_all = [jnp.zeros((QB, LANES), F32) for _ in range(R)]
    for (u, h, _, v), (m, p, l) in zip(tasks, soft):
        o_ref[u * QB:(u + 1) * QB, h * DIL_DH:(h + 1) * DIL_DH] = (
            jnp.dot(p.astype(BF16), v, preferred_element_type=F32) / l).astype(o_ref.dtype)
        lse_all[u] = jnp.where(lane == h, m + jnp.log(l), lse_all[u])
    for u in range(R):
        lse_ref[u * QB:(u + 1) * QB, :] = lse_all[u]


def _dil_attn(qv, kv, vv, dil):
    w = DIL_HEADS * DIL_DH
    R = DIL_BLOCKS_PER_STEP
    L = SEQ // dil
    ns = L // (R * Q_BLOCK)
    cur = lambda b, r, n: (b * ns + n, r)
    prev = lambda b, r, n: (jnp.maximum((b * ns + n) * R - 1, 0), r)
    blk = (R * Q_BLOCK, w)
    pblk = (Q_BLOCK, w)
    return pl.pallas_call(
        _dil_kernel,
        out_shape=[jax.ShapeDtypeStruct((BATCH * L, dil * w), BF16),
                   jax.ShapeDtypeStruct((BATCH * L, dil * LANES), F32)],
        grid=(BATCH, dil, ns),
        in_specs=[pl.BlockSpec(blk, cur), pl.BlockSpec(pblk, prev), pl.BlockSpec(blk, cur),
                  pl.BlockSpec(pblk, prev), pl.BlockSpec(blk, cur)],
        out_specs=[pl.BlockSpec(blk, cur), pl.BlockSpec((R * Q_BLOCK, LANES), cur)],
        compiler_params=_cparams(("parallel", "parallel", "parallel")),
        name=f"dil_attn_d{dil}",
    )(qv, kv, kv, vv, vv)


def _dil_combine_kernel(tm, *refs):
    np_ = len(DIL_PATTERNS)
    o_refs, l_refs = refs[:np_], refs[np_:2 * np_]
    gate_ref, y_ref = refs[2 * np_], refs[2 * np_ + 1]
    stage = refs[2 * np_ + 2:]
    w = DIL_HEADS * DIL_DH
    os_, ls = [], []
    k = 0
    for (_, d), o_ref, l_ref in zip(DIL_PATTERNS, o_refs, l_refs):
        if d == 1:
            os_.append(lambda h, o_ref=o_ref: o_ref[:, h * DIL_DH:(h + 1) * DIL_DH].astype(F32))
            ls.append(l_ref[...])
            continue
        so, sl = stage[2 * k], stage[2 * k + 1]
        k += 1
        for r in range(d):
            for h in range(DIL_HEADS):
                a = r * w + h * DIL_DH
                so[h, pl.ds(r, tm // d, stride=d), :] = o_ref[:, a:a + DIL_DH].astype(F32)
            sl[pl.ds(r, tm // d, stride=d), :] = l_ref[:, r * LANES:(r + 1) * LANES]
        os_.append(lambda h, so=so: so[h])
        ls.append(sl[...])
    mx = functools.reduce(jnp.maximum, ls)
    es = [jnp.exp(x - mx) for x in ls]
    den = functools.reduce(lambda a, b: a + b, es)
    ws = [e / den for e in es]
    lane = lax.broadcasted_iota(jnp.int32, ws[0].shape, 1)
    for h in range(DIL_HEADS):
        c = slice(h * DIL_DH, (h + 1) * DIL_DH)
        wh = [jnp.sum(jnp.where(lane == h, wgt, 0.0), axis=1, keepdims=True) for wgt in ws]
        o = functools.reduce(lambda a, b: a + b, [wh[p] * os_[p](h) for p in range(np_)])
        y_ref[:, c] = (o * _silu(gate_ref[:, c].astype(F32))).astype(y_ref.dtype)


def _dil_combine(outs, lses, gate, tm):
    w = DIL_HEADS * DIL_DH
    in_specs = [pl.BlockSpec((tm // d, d * w), lambda i: (i, 0)) for _, d in DIL_PATTERNS]
    in_specs += [pl.BlockSpec((tm // d, d * LANES), lambda i: (i, 0)) for _, d in DIL_PATTERNS]
    in_specs.append(pl.BlockSpec((tm, w), lambda i: (i, 0)))
    scratch = []
    for _, d in DIL_PATTERNS:
        if d > 1:
            scratch += [pltpu.VMEM((DIL_HEADS, tm, DIL_DH), F32), pltpu.VMEM((tm, LANES), F32)]
    return pl.pallas_call(
        functools.partial(_dil_combine_kernel, tm),
        out_shape=jax.ShapeDtypeStruct((N_TOK, w), BF16),
        grid=(N_TOK // tm,),
        in_specs=in_specs,
        out_specs=pl.BlockSpec((tm, w), lambda i: (i, 0)),
        scratch_shapes=scratch,
        compiler_params=_cparams(("parallel",)),
        name="dil_combine",
    )(*outs, *lses, gate)


def _outproj_kernel(nparts, final, *refs):
    ys, ws = refs[:nparts], refs[nparts:2 * nparts]
    x_ref = refs[2 * nparts]
    o_ref = refs[-1]
    acc = x_ref[...]
    for y, w in zip(ys, ws):
        acc = acc + jnp.dot(y[...], w[...], preferred_element_type=F32)
    if final:
        acc = _rms(acc, refs[2 * nparts + 1][...])
    o_ref[...] = acc


def _outproj(ys, ws, x, final_g, tm):
    n_tok, d = x.shape
    in_specs = [pl.BlockSpec((tm, y.shape[1]), lambda i: (i, 0)) for y in ys]
    in_specs += [pl.BlockSpec(w.shape, lambda i: (0, 0)) for w in ws]
    in_specs.append(pl.BlockSpec((tm, d), lambda i: (i, 0)))
    args = [*ys, *ws, x]
    if final_g is not None:
        in_specs.append(pl.BlockSpec((1, d), lambda i: (0, 0)))
        args.append(final_g.reshape(1, d))
    return pl.pallas_call(
        functools.partial(_outproj_kernel, len(ys), final_g is not None),
        out_shape=jax.ShapeDtypeStruct((n_tok, d), F32),
        grid=(n_tok // tm,),
        in_specs=in_specs,
        out_specs=pl.BlockSpec((tm, d), lambda i: (i, 0)),
        compiler_params=_cparams(("parallel",)),
        name="outproj",
    )(*args)


PROJ_TM = 512
ATTN_TQ = 512
MLA_HEADS_PER_STEP = 2
MOBA_HEADS_PER_STEP = 2


def _even_layer(x, mem2d, norm_g, w_in, q_norm_g, w_uq, kv_norm_g, w_ukv, mem_norm_g, w_mem_kv, w_out):
    n_lat = MLA_Q_RANK + MLA_KV_RANK + MLA_ROPE
    wlat = jnp.pad(w_in[:, :n_lat], ((0, 0), (0, LANES - MLA_ROPE))).astype(BF16)
    n_bqkv = n_lat + 3 * MOBA_HEADS * MOBA_DH
    w_bqkv = w_in[:, n_lat:n_bqkv].astype(BF16)
    w_rest = w_in[:, n_bqkv:].astype(BF16)
    wuq = w_uq.reshape(MLA_Q_RANK, MLA_HEADS, MLA_NOPE + MLA_ROPE)
    wuq = jnp.pad(wuq, ((0, 0), (0, 0), (0, MLA_HW - MLA_NOPE - MLA_ROPE)))
    wuq = wuq.reshape(MLA_Q_RANK, MLA_HEADS * MLA_HW).astype(BF16)
    wukv = w_ukv.reshape(MLA_KV_RANK, MLA_HEADS, MLA_NOPE + MLA_V)
    wukv = jnp.concatenate([wukv[:, :, :MLA_NOPE].reshape(MLA_KV_RANK, -1),
                            wukv[:, :, MLA_NOPE:].reshape(MLA_KV_RANK, -1)], axis=1).astype(BF16)

    cos_a, sin_a = _rope_tables(MLA_ROPE, LANES)
    qa, ka, va = _mla_prep(x, norm_g, wlat, q_norm_g, wuq, kv_norm_g, wukv, cos_a, sin_a, ATTN_TQ)

    rot = MOBA_DH // ROT_FRAC
    cos_b, sin_b = _rope_tables(rot, MOBA_DH)
    sc = MOBA_DH ** -0.5 * LOG2E
    wb = MOBA_HEADS * MOBA_DH
    plain = [("plain",)]
    groups_qkv = [
        (wb, [("rope", 0, rot // 2, MOBA_DH, sc)] * MOBA_HEADS, "rows"),
        (wb, [("rope", 0, rot // 2, MOBA_DH, 1.0)] * MOBA_HEADS, "rows"),
        (wb, plain * MOBA_HEADS, ("vT", MOBA_TILE)),
    ]
    groups_rest = [
        (MEMQ_W, [("scale", MEM_DH ** -0.5)] * MEM_HEADS, "rows"),
        (MLA_HEADS * MLA_V, plain * MLA_HEADS, "rows"),
        (wb, plain * MOBA_HEADS, "rows"),
        (MEMQ_W, plain * MEM_HEADS, "rows"),
    ]
    qb, kb, vb = _inproj(x, norm_g, w_bqkv, groups_qkv, [cos_b, sin_b], PROJ_TM)
    qm, gate_a, gate_b, gate_m = _inproj(x, norm_g, w_rest, groups_rest, [], PROJ_TM)

    y_a = _mla_attn(qa, ka, va, gate_a, ATTN_TQ, MLA_HEADS_PER_STEP)
    y_b = _moba_flat_attn(qb, kb, vb, gate_b, MOBA_HEADS_PER_STEP)
    kvm = _mem_prep(mem2d, mem_norm_g, w_mem_kv.astype(BF16))
    y_m = _mem_attn(qm, kvm, gate_m, ATTN_TQ)

    a, b = MLA_HEADS * MLA_V, MLA_HEADS * MLA_V + MOBA_HEADS * MOBA_DH
    return [y_a, y_b, y_m], [w_out[:a].astype(BF16), w_out[a:b].astype(BF16), w_out[b:].astype(BF16)]


def _odd_layer(x, mem2d, norm_g, w_in, sinks, mem_norm_g, w_mem_kv, w_out):
    rot_c = SWA_DH // ROT_FRAC
    rot_d = DIL_DH // ROT_FRAC
    cos_c, sin_c = _rope_tables(rot_c, SWA_DH)
    cos_d, sin_d = _rope_tables(rot_d, DIL_DH)
    wq = SWA_HEADS * SWA_DH
    wd = DIL_HEADS * DIL_DH
    wkv = 2 * SWA_KV_HEADS * SWA_DH
    rc = lambda s: ("rope", 0, rot_c // 2, SWA_DH, s)
    rd = lambda s: ("rope", 1, rot_d // 2, DIL_DH, s)
    plain = [("plain",)]
    groups_qkv = [
        (wq, [rc(SWA_DH ** -0.5)] * (wq // LANES), "rows"),
        (wkv, [rc(1.0), ("plain",)], "rows"),
        (wd, [rd(DIL_DH ** -0.5)] * DIL_HEADS, "dilated"),
        (wd, [rd(1.0)] * DIL_HEADS, "dilated"),
        (wd, plain * DIL_HEADS, "dilated"),
    ]
    groups_rest = [
        (MEMQ_W, [("scale", MEM_DH ** -0.5)] * MEM_HEADS, "rows"),
        (wq, plain * (wq // LANES), "rows"),
        (wd, plain * DIL_HEADS, "rows"),
        (MEMQ_W, plain * MEM_HEADS, "rows"),
    ]
    n_qkv = wq + wkv + 3 * wd
    res = _inproj(x, norm_g, w_in[:, :n_qkv].astype(BF16), groups_qkv, [cos_c, sin_c, cos_d, sin_d], PROJ_TM)
    qc, kvc = res[0], res[1]
    nv = 1 + len(DIL_VIEWS)
    qd, kd, vd = (res[2 + i * nv:2 + (i + 1) * nv] for i in range(3))
    qm, gate_c, gate_d, gate_m = _inproj(x, norm_g, w_in[:, n_qkv:].astype(BF16), groups_rest, [], PROJ_TM)

    sink_row = jnp.repeat(sinks.astype(F32), SWA_DH).reshape(1, wq)
    y_c = _swa_attn(qc, kvc, gate_c, sink_row)
    view_of = {1: 0, **{d: 1 + i for i, d in enumerate(DIL_VIEWS)}}
    outs, lses = zip(*[_dil_attn(qd[view_of[dil]], kd[view_of[dil]], vd[view_of[dil]], dil)
                       for _, dil in DIL_PATTERNS])
    y_d = _dil_combine(outs, lses, gate_d, PROJ_TM)
    kvm = _mem_prep(mem2d, mem_norm_g, w_mem_kv.astype(BF16))
    y_m = _mem_attn(qm, kvm, gate_m, ATTN_TQ)

    return [y_c, y_d, y_m], [w_out[:wq].astype(BF16), w_out[wq:wq + wd].astype(BF16),
                             w_out[wq + wd:].astype(BF16)]


def kernel(x, mem, ev_norm_g, ev_w_in, ev_q_norm_g, ev_w_uq, ev_kv_norm_g, ev_w_ukv, ev_mem_norm_g,
           ev_w_mem_kv, ev_w_out, od_norm_g, od_w_in, od_sinks, od_mem_norm_g, od_w_mem_kv, od_w_out,
           final_norm_g):
    x2 = x.reshape(N_TOK, D_MODEL)
    mem2d = mem.reshape(BATCH * N_MEM, D_MODEL)
    ys, ws = _even_layer(x2, mem2d, ev_norm_g[0], ev_w_in[0], ev_q_norm_g[0], ev_w_uq[0], ev_kv_norm_g[0],
                         ev_w_ukv[0], ev_mem_norm_g[0], ev_w_mem_kv[0], ev_w_out[0])
    x2 = _outproj(ys, ws, x2, None, PROJ_TM)
    ys, ws = _odd_layer(x2, mem2d, od_norm_g[0], od_w_in[0], od_sinks[0], od_mem_norm_g[0],
                        od_w_mem_kv[0], od_w_out[0])
    x2 = _outproj(ys, ws, x2, final_norm_g, PROJ_TM)
    return x2.reshape(BATCH, SEQ, D_MODEL)
```

```python
import functools

import numpy as np
import jax
import jax.numpy as jnp
from jax import lax
from jax.experimental import pallas as pl
from jax.experimental.pallas import tpu as pltpu

D_MODEL = 2048
BATCH = 4
SEQ = 4096
N_TOK = BATCH * SEQ
N_MEM = 256
ROPE_THETA = 500000.0
ROT_FRAC = 4
EPS = 1e-6

MLA_HEADS = 8
MLA_Q_RANK = 512
MLA_KV_RANK = 256
MLA_NOPE = 128
MLA_ROPE = 64
MLA_V = 128
MOBA_HEADS = 8
MOBA_DH = 128
MOBA_BLOCK = 256
MOBA_TOPK = 3
SWA_HEADS = 16
SWA_KV_HEADS = 2
SWA_DH = 64
SWA_WINDOW = 128
DIL_HEADS = 6
DIL_DH = 128
DIL_PATTERNS = ((128, 1), (512, 4), (2048, 16))
MEM_HEADS = 4
MEM_DH = 128
MEMQ_W = MEM_HEADS * MEM_DH
Q_BLOCK = 128

LANES = 128
VMEM_LIMIT = 52 * 1024 * 1024

NEG = -1e30
BF16 = jnp.bfloat16
F32 = jnp.float32


def _cparams(sem):
    return pltpu.CompilerParams(dimension_semantics=sem, vmem_limit_bytes=VMEM_LIMIT)


def _rms(x, g):
    ms = jnp.mean(x * x, axis=-1, keepdims=True)
    return (x * lax.rsqrt(ms + EPS)) * g


def _silu(g):
    return g / (1.0 + jnp.exp(-g))


def _sigmoid(z):
    return 1.0 / (1.0 + jnp.exp(-z))


def _rope_slab(x, cos_t, sin_t, half, period):
    lane = lax.broadcasted_iota(jnp.int32, x.shape, 1)
    up = pltpu.roll(x, LANES - half, 1)
    dn = pltpu.roll(x, half, 1)
    sw = jnp.where((lane & (period - 1)) < half, up, dn)
    return x * cos_t + sw * sin_t


def _qk(q, k):
    return lax.dot_general(q, k, (((1,), (1,)), ((), ())), preferred_element_type=F32)


def _rope_tables(rot_dim, period):
    half = rot_dim // 2
    inv = 1.0 / (ROPE_THETA ** (jnp.arange(0, rot_dim, 2, dtype=F32) / rot_dim))
    ang = jnp.arange(SEQ, dtype=F32)[:, None] * inv[None, :]
    c, s = jnp.cos(ang), jnp.sin(ang)
    d = np.arange(LANES) % period
    idx = d % half
    cos_t = jnp.where(d < rot_dim, c[:, idx], 1.0)
    sin_t = jnp.where(d < half, -s[:, idx], jnp.where(d < rot_dim, s[:, idx], 0.0))
    return cos_t.astype(F32), sin_t.astype(F32)


PROJ_CHUNK = 512
DIL_VIEWS = tuple(d for _, d in DIL_PATTERNS if d > 1)


def _inproj_kernel(groups, ntab, tm, x_ref, g_ref, *rest):
    tabs = rest[:2 * ntab]
    w_ref = rest[2 * ntab]
    ndil = sum(layout == "dilated" for _, _, layout in groups)
    refs = list(rest[2 * ntab + 1:len(rest) - ndil])
    stage = list(rest[len(rest) - ndil:])
    xn = _rms(x_ref[...], g_ref[...]).astype(BF16)
    c0 = 0
    for ncols, kinds, layout in groups:
        o_ref = refs.pop(0)
        views = [refs.pop(0) for _ in DIL_VIEWS] if layout == "dilated" else []
        st_ref = stage.pop(0) if layout == "dilated" else None
        for s0 in range(0, ncols, PROJ_CHUNK):
            n = min(PROJ_CHUNK, ncols - s0)
            acc = jnp.dot(xn, w_ref[:, c0 + s0:c0 + s0 + n], preferred_element_type=F32)
            for c in range(n // LANES):
                col = s0 + c * LANES
                kind = kinds[col // LANES]
                piece = acc[:, c * LANES:(c + 1) * LANES]
                if kind[0] == "rope":
                    _, ti, half, period, sc = kind
                    piece = _rope_slab(piece, tabs[2 * ti][...], tabs[2 * ti + 1][...], half, period)
                    if sc != 1.0:
                        piece = piece * sc
                elif kind[0] == "scale":
                    piece = piece * kind[1]
                if layout == "rows" or layout == "dilated":
                    o_ref[:, col:col + LANES] = piece.astype(o_ref.dtype)
                    if layout == "dilated":
                        st_ref[col // LANES] = piece
                else:
                    tb = layout[1]
                    for u in range(tm // tb):
                        o_ref[col // LANES, u] = _vt_block(piece[u * tb:(u + 1) * tb, :])
        for d, v_ref in zip(DIL_VIEWS, views):
            for r in range(d):
                for c in range(ncols // LANES):
                    a = r * ncols + c * LANES
                    v_ref[:, a:a + LANES] = st_ref[c, pl.ds(r, tm // d, stride=d), :].astype(v_ref.dtype)
        c0 += ncols


def _inproj(x, g, w, groups, tables, tm):
    n_tok, d = x.shape
    assert sum(nc for nc, _, _ in groups) == w.shape[1]
    seq_tiles = SEQ // tm
    in_specs = [pl.BlockSpec((tm, d), lambda i: (i, 0)),
                pl.BlockSpec((1, d), lambda i: (0, 0))]
    args = [x, g.reshape(1, d)]
    for t in tables:
        in_specs.append(pl.BlockSpec((tm, LANES), lambda i: (i % seq_tiles, 0)))
        args.append(t)
    in_specs.append(pl.BlockSpec(w.shape, lambda i: (0, 0), pipeline_mode=pl.Buffered(1)))
    args.append(w)
    out_specs, out_shapes = [], []
    for nc, _, layout in groups:
        if layout == "rows" or layout == "dilated":
            out_specs.append(pl.BlockSpec((tm, nc), lambda i: (i, 0)))
            out_shapes.append(jax.ShapeDtypeStruct((n_tok, nc), BF16))
            if layout == "dilated":
                for dil in DIL_VIEWS:
                    out_specs.append(pl.BlockSpec((tm // dil, dil * nc), lambda i: (i, 0)))
                    out_shapes.append(jax.ShapeDtypeStruct((n_tok // dil, dil * nc), BF16))
        else:
            tb = layout[1]
            out_specs.append(pl.BlockSpec((nc // LANES, tm // tb, VT_ROWS, tb), lambda i: (0, i, 0, 0)))
            out_shapes.append(jax.ShapeDtypeStruct((nc // LANES, n_tok // tb, VT_ROWS, tb), BF16))
    return pl.pallas_call(
        functools.partial(_inproj_kernel, groups, len(tables) // 2, tm),
        out_shape=out_shapes,
        grid=(n_tok // tm,),
        in_specs=in_specs,
        out_specs=out_specs,
        scratch_shapes=[pltpu.VMEM((nc // LANES, tm, LANES), F32) for nc, _, layout in groups
                        if layout == "dilated"],
        compiler_params=_cparams(("parallel",)),
        name="inproj",
    )(*args)


MLA_HW = 256


def _mla_prep_kernel(scale, x_ref, g_ref, wlat_ref, qg_ref, wuq_ref, kvg_ref, wukv_ref,
                     cos_ref, sin_ref, qa_ref, ka_ref, va_ref):
    tm = x_ref.shape[0]
    halves = [slice(u * (tm // 2), (u + 1) * (tm // 2)) for u in range(2)]
    xns = [_rms(x_ref[r, :], g_ref[...]).astype(BF16) for r in halves]
    lats = [jnp.dot(xn, wlat_ref[...], preferred_element_type=F32) for xn in xns]
    cqs = [_rms(lat[:, :MLA_Q_RANK], qg_ref[...]).astype(BF16) for lat in lats]
    ckvs = [_rms(lat[:, MLA_Q_RANK:MLA_Q_RANK + MLA_KV_RANK], kvg_ref[...]).astype(BF16) for lat in lats]
    qs = [jnp.dot(cq, wuq_ref[...], preferred_element_type=F32) for cq in cqs]
    kvs = [jnp.dot(ckv, wukv_ref[...], preferred_element_type=F32) for ckv in ckvs]
    for r, lat, q, kv in zip(halves, lats, qs, kvs):
        cos_t, sin_t = cos_ref[r, :], sin_ref[r, :]
        kpe = lat[:, MLA_Q_RANK + MLA_KV_RANK:]
        kpe = _rope_slab(kpe, cos_t, sin_t, MLA_ROPE // 2, LANES).astype(BF16)
        for h in range(MLA_HEADS):
            a = h * MLA_HW
            qa_ref[r, a:a + LANES] = (q[:, a:a + LANES] * scale).astype(BF16)
            qr = _rope_slab(q[:, a + LANES:a + 2 * LANES], cos_t, sin_t, MLA_ROPE // 2, LANES)
            qa_ref[r, a + LANES:a + 2 * LANES] = (qr * scale).astype(BF16)
            ka_ref[r, a:a + LANES] = kv[:, h * LANES:(h + 1) * LANES].astype(BF16)
            ka_ref[r, a + LANES:a + 2 * LANES] = kpe
            v0 = MLA_HEADS * MLA_NOPE + h * MLA_V
            va_ref[h, 0, :, r] = _vt_block(kv[:, v0:v0 + MLA_V])


def _mla_prep(x, g, wlat, qg, wuq, kvg, wukv, cos_t, sin_t, tm):
    n_tok, d = x.shape
    seq_tiles = SEQ // tm
    full = lambda a: pl.BlockSpec(a.shape, lambda i: (0, 0))
    qg2, kvg2, g2 = qg.reshape(1, -1), kvg.reshape(1, -1), g.reshape(1, d)
    scale = (MLA_NOPE + MLA_ROPE) ** -0.5 * LOG2E
    return pl.pallas_call(
        functools.partial(_mla_prep_kernel, scale),
        out_shape=[jax.ShapeDtypeStruct((n_tok, MLA_HEADS * MLA_HW), BF16),
                   jax.ShapeDtypeStruct((n_tok, MLA_HEADS * MLA_HW), BF16),
                   jax.ShapeDtypeStruct((MLA_HEADS, n_tok // tm, VT_ROWS, tm), BF16)],
        grid=(n_tok // tm,),
        in_specs=[pl.BlockSpec((tm, d), lambda i: (i, 0)), full(g2), full(wlat), full(qg2), full(wuq),
                  full(kvg2), full(wukv),
                  pl.BlockSpec((tm, LANES), lambda i: (i % seq_tiles, 0)),
                  pl.BlockSpec((tm, LANES), lambda i: (i % seq_tiles, 0))],
        out_specs=[pl.BlockSpec((tm, MLA_HEADS * MLA_HW), lambda i: (i, 0)),
                   pl.BlockSpec((tm, MLA_HEADS * MLA_HW), lambda i: (i, 0)),
                   pl.BlockSpec((MLA_HEADS, 1, VT_ROWS, tm), lambda i: (0, i, 0, 0))],
        compiler_params=_cparams(("parallel",)),
        name="mla_prep",
    )(x, g2, wlat, qg2, wuq, kvg2, wukv, cos_t, sin_t)


LOG2E = 1.4426950408889634
VT_TAIL = 16
VT_ROWS = LANES + VT_TAIL


def _vt_block(v):
    tb = v.shape[0]
    row = lax.broadcasted_iota(jnp.int32, (VT_TAIL, tb), 0)
    tail = jnp.where(row == 0, 1.0, 0.0).astype(F32)
    return jnp.concatenate([v.T, tail], axis=0).astype(BF16)


def _flash_finish_all(nh, n, tq, dv, acc_ref, gate_ref, y_ref):
    def body(c, _):
        rows = pl.ds(pl.multiple_of(c * tq, tq), tq)
        for g in range(nh):
            cols = slice(g * dv, (g + 1) * dv)
            a = acc_ref.at[g, c]
            o = (a[:LANES, :] / a[LANES:LANES + 1, :]).T
            y_ref[rows, cols] = (o * _silu(gate_ref[rows, cols].astype(F32))).astype(y_ref.dtype)
        return 0

    lax.fori_loop(0, n, body, 0)


def _flash_flat_t(nh, n, tq, score_fn, row_bias, vt_fn, s_ref, p_ref, acc_ref, m_ref):
    n_off = n * (n - 1) // 2
    assert n % 2 == 0 and n_off % 2 == 0
    acc_ref[...] = jnp.zeros_like(acc_ref)
    p_ref[...] = jnp.zeros_like(p_ref)
    m_ref[...] = jnp.full(m_ref.shape, NEG, F32)
    key = lax.broadcasted_iota(jnp.int32, (tq, tq), 0)
    qry = lax.broadcasted_iota(jnp.int32, (tq, tq), 1)

    def diag_scores(c):
        return lambda g: jnp.where(key <= qry, score_fn(g, c, c), NEG)

    def off_scores(j, c):
        return lambda g: score_fn(g, j, jnp.minimum(c, n - 1))

    def step(rd, wr, nxt, cur, prev, alphas):
        (jc, cc), (jp, cp) = cur, prev
        for g in range(nh):
            s_ref[wr, g] = nxt(g)
        pend = [jnp.dot(vt_fn(g, jp), p_ref[wr, g], preferred_element_type=F32) for g in range(nh)]
        out = []
        for g in range(nh):
            s = s_ref[rd, g]
            m_prev = m_ref[g, cc]
            if row_bias is None:
                m_new = jnp.maximum(m_prev, jnp.max(s, axis=0, keepdims=True))
                p = jnp.exp2((s - m_new).astype(BF16))
            else:
                nbs = row_bias(g, jc, cc)
                hk = tq // len(nbs)
                subs = [s[u * hk:(u + 1) * hk, :] for u in range(len(nbs))]
                m_new = m_prev
                for sub, nb in zip(subs, nbs):
                    m_new = jnp.maximum(m_new, jnp.max(sub, axis=0, keepdims=True) + nb)
                p = jnp.concatenate([jnp.exp2((sub - (m_new - nb)).astype(BF16))
                                     for sub, nb in zip(subs, nbs)], axis=0)
            m_ref[g, cc] = m_new
            acc_ref[g, cp] = alphas[g] * acc_ref[g, cp] + pend[g]
            p_ref[rd, g] = p
            out.append(jnp.exp2(m_prev - m_new))
        return tuple(out)

    def advance(j, c):
        wrap = j + 1 >= c
        return jnp.where(wrap, 0, j + 1), jnp.where(wrap, c + 1, c)

    for g in range(nh):
        s_ref[0, g] = diag_scores(0)(g)
    alphas = (jnp.ones((1, tq), F32),) * nh

    def diag_pair(u, alphas):
        i = 2 * u
        ip = jnp.maximum(i - 1, 0)
        alphas = step(0, 1, diag_scores(i + 1), (i, i), (ip, ip), alphas)
        return step(1, 0, diag_scores(i + 2), (i + 1, i + 1), (i, i), alphas)

    alphas = lax.fori_loop(0, (n - 2) // 2, diag_pair, alphas)
    alphas = step(0, 1, diag_scores(n - 1), (n - 2, n - 2), (n - 3, n - 3), alphas)
    alphas = step(1, 0, off_scores(0, 1), (n - 1, n - 1), (n - 2, n - 2), alphas)

    def off_pair(u, carry):
        jc, cc, jp, cp = carry[:4]
        jn, cn = advance(jc, cc)
        alphas = step(0, 1, off_scores(jn, cn), (jc, cc), (jp, cp), carry[4:])
        jn2, cn2 = advance(jn, cn)
        alphas = step(1, 0, off_scores(jn2, cn2), (jn, cn), (jc, cc), alphas)
        return (jn2, cn2, jn, cn) + tuple(alphas)

    i32 = jnp.int32
    carry = lax.fori_loop(0, n_off // 2, off_pair, (i32(0), i32(1), i32(n - 1), i32(n - 1)) + tuple(alphas))
    jl, cl = carry[2], carry[3]
    for g in range(nh):
        pend = jnp.dot(vt_fn(g, jl), p_ref[1, g], preferred_element_type=F32)
        acc_ref[g, cl] = carry[4 + g] * acc_ref[g, cl] + pend


def _rows(i, t):
    return pl.ds(i * t, t) if isinstance(i, int) else pl.ds(pl.multiple_of(i * t, t), t)


def _flash_scratch(nh, n, tq):
    return [pltpu.VMEM((2, nh, tq, tq), F32), pltpu.VMEM((2, nh, tq, tq), BF16),
            pltpu.VMEM((nh, n, VT_ROWS, tq), F32), pltpu.VMEM((nh, n, 1, tq), F32)]


def _mla_kernel(tq, nh, q_ref, k_ref, vt_ref, gate_ref, y_ref, s_ref, p_ref, acc_ref, m_ref):
    n = SEQ // tq

    def scores(g, j, c):
        cols = slice(g * MLA_HW, (g + 1) * MLA_HW)
        return _qk(k_ref[_rows(j, tq), cols], q_ref[_rows(c, tq), cols])

    _flash_flat_t(nh, n, tq, scores, None, lambda g, j: vt_ref[g, j], s_ref, p_ref, acc_ref, m_ref)
    _flash_finish_all(nh, n, tq, MLA_V, acc_ref, gate_ref, y_ref)


def _mla_attn(qa, ka, vat, gate, tq, nh):
    nq = SEQ // tq
    return pl.pallas_call(
        functools.partial(_mla_kernel, tq, nh),
        out_shape=jax.ShapeDtypeStruct((N_TOK, MLA_HEADS * MLA_V), BF16),
        grid=(BATCH, MLA_HEADS // nh),
        in_specs=[pl.BlockSpec((SEQ, nh * MLA_HW), lambda b, h: (b, h)),
                  pl.BlockSpec((SEQ, nh * MLA_HW), lambda b, h: (b, h)),
                  pl.BlockSpec((nh, nq, VT_ROWS, tq), lambda b, h: (h, b, 0, 0)),
                  pl.BlockSpec((SEQ, nh * MLA_V), lambda b, h: (b, h))],
        out_specs=pl.BlockSpec((SEQ, nh * MLA_V), lambda b, h: (b, h)),
        scratch_shapes=_flash_scratch(nh, nq, tq),
        compiler_params=_cparams(("parallel", "parallel")),
        name="mla_attn",
    )(qa, ka, vat, gate)


MOBA_TILE = 2 * MOBA_BLOCK


def _moba_flat_kernel(nh, q_ref, k_ref, vt_ref, gate_ref, y_ref, km_ref, nb_ref, s_ref, p_ref, acc_ref, m_ref):
    L = MOBA_BLOCK
    T = MOBA_TILE
    D = MOBA_DH
    nblk = SEQ // L
    ntile = SEQ // T
    cols = [slice(g * D, (g + 1) * D) for g in range(nh)]

    for g in range(nh):
        for j in range(nblk):
            blk = k_ref[j * L:(j + 1) * L, cols[g]].astype(F32)
            km_ref[g, j:j + 1, :] = jnp.sum(blk, axis=0, keepdims=True) * (1.0 / L)

    blk_i = lax.broadcasted_iota(jnp.int32, (nblk, T), 0)
    blk_f = blk_i.astype(F32)
    second = lax.broadcasted_iota(jnp.int32, (nblk, T), 1) >= L

    def select(c, carry):
        own = 2 * c + second.astype(jnp.int32)
        past = blk_i < own
        for g in range(nh):
            gt = jnp.where(past, _qk(km_ref[g].astype(BF16), q_ref[_rows(c, T), cols[g]]), -jnp.inf)
            sel = jnp.zeros((nblk, T), jnp.bool_)
            for _ in range(MOBA_TOPK):
                mx = jnp.max(gt, axis=0, keepdims=True)
                first = jnp.min(jnp.where(gt == mx, blk_f, float(nblk)), axis=0, keepdims=True)
                pick = blk_f == first
                sel = jnp.logical_or(sel, pick)
                gt = jnp.where(pick, -jnp.inf, gt)
            keep = jnp.logical_or(jnp.logical_and(sel, past), blk_i == own)
            nb_ref[g, c] = jnp.where(keep, 0.0, NEG)
        return carry

    lax.fori_loop(0, ntile, select, 0)

    def scores(g, j, c):
        return _qk(k_ref[_rows(j, T), cols[g]], q_ref[_rows(c, T), cols[g]])

    def bias(g, j, c):
        return [nb_ref[g, c, pl.ds(2 * j, 1), :], nb_ref[g, c, pl.ds(2 * j + 1, 1), :]]

    _flash_flat_t(nh, ntile, T, scores, bias, lambda g, j: vt_ref[g, j], s_ref, p_ref, acc_ref, m_ref)
    _flash_finish_all(nh, ntile, T, D, acc_ref, gate_ref, y_ref)


def _moba_flat_attn(qb, kb, vbt, gate, nh):
    T = MOBA_TILE
    D = MOBA_DH
    nblk = SEQ // MOBA_BLOCK
    ntile = SEQ // T
    return pl.pallas_call(
        functools.partial(_moba_flat_kernel, nh),
        out_shape=jax.ShapeDtypeStruct((N_TOK, MOBA_HEADS * D), BF16),
        grid=(BATCH, MOBA_HEADS // nh),
        in_specs=[pl.BlockSpec((SEQ, nh * D), lambda b, h: (b, h)),
                  pl.BlockSpec((SEQ, nh * D), lambda b, h: (b, h)),
                  pl.BlockSpec((nh, ntile, VT_ROWS, T), lambda b, h: (h, b, 0, 0)),
                  pl.BlockSpec((SEQ, nh * D), lambda b, h: (b, h))],
        out_specs=pl.BlockSpec((SEQ, nh * D), lambda b, h: (b, h)),
        scratch_shapes=[pltpu.VMEM((nh, nblk, D), F32), pltpu.VMEM((nh, ntile, nblk, T), F32)]
        + _flash_scratch(nh, ntile, T),
        compiler_params=_cparams(("parallel", "parallel")),
        name="moba_attn",
    )(qb, kb, vbt, gate)


def _mem_prep_kernel(mem_ref, g_ref, w_ref, kv_ref):
    mn = _rms(mem_ref[...], g_ref[...]).astype(BF16)
    kv_ref[...] = jnp.dot(mn, w_ref[...], preferred_element_type=F32).astype(BF16)


def _mem_prep(mem2d, g, w):
    g2 = g.reshape(1, -1)
    return pl.pallas_call(
        _mem_prep_kernel,
        out_shape=jax.ShapeDtypeStruct((BATCH * N_MEM, 2 * MEMQ_W), BF16),
        grid=(BATCH,),
        in_specs=[pl.BlockSpec((N_MEM, D_MODEL), lambda b: (b, 0)),
                  pl.BlockSpec(g2.shape, lambda b: (0, 0)),
                  pl.BlockSpec(w.shape, lambda b: (0, 0))],
        out_specs=pl.BlockSpec((N_MEM, 2 * MEMQ_W), lambda b: (b, 0)),
        compiler_params=_cparams(("parallel",)),
        name="mem_prep",
    )(mem2d, g2, w)


def _mem_attn_kernel(q_ref, kv_ref, gate_ref, y_ref):
    cols = [slice(h * MEM_DH, (h + 1) * MEM_DH) for h in range(MEM_HEADS)]
    ss = [_qk(q_ref[:, c], kv_ref[:, c]) for c in cols]
    ps = [jnp.exp2(s - jnp.max(s, axis=1, keepdims=True)) for s in ss]
    for h, (c, p) in enumerate(zip(cols, ps)):
        l = jnp.sum(p, axis=1, keepdims=True)
        a = MEMQ_W + h * MEM_DH
        o = jnp.dot(p.astype(BF16), kv_ref[:, a:a + MEM_DH], preferred_element_type=F32) / l
        y_ref[:, c] = (o * _silu(gate_ref[:, c].astype(F32))).astype(y_ref.dtype)


def _mem_attn(qm, kvm, gate, tq):
    nq = SEQ // tq
    return pl.pallas_call(
        _mem_attn_kernel,
        out_shape=jax.ShapeDtypeStruct((N_TOK, MEMQ_W), BF16),
        grid=(BATCH, nq),
        in_specs=[pl.BlockSpec((tq, MEMQ_W), lambda b, i: (b * nq + i, 0)),
                  pl.BlockSpec((N_MEM, 2 * MEMQ_W), lambda b, i: (b, 0)),
                  pl.BlockSpec((tq, MEMQ_W), lambda b, i: (b * nq + i, 0))],
        out_specs=pl.BlockSpec((tq, MEMQ_W), lambda b, i: (b * nq + i, 0)),
        compiler_params=_cparams(("parallel", "parallel")),
        name="mem_attn",
    )(qm, kvm, gate)


def _band_bias(n, lo_off, hi_off):
    i = lax.broadcasted_iota(jnp.int32, (Q_BLOCK, 2 * Q_BLOCK), 0)
    c = lax.broadcasted_iota(jnp.int32, (Q_BLOCK, 2 * Q_BLOCK), 1)
    vis = (c - i >= lo_off) & (c - i <= hi_off) & ((c >= Q_BLOCK) | (n > 0))
    return jnp.where(vis, 0.0, NEG).astype(F32)


def _swa_kernel(q_ref, kvp_ref, kvo_ref, gate_ref, sink_ref, y_ref):
    QB = Q_BLOCK
    G = SWA_HEADS // SWA_KV_HEADS
    P = G // 2
    n = pl.program_id(1)
    c = lax.broadcasted_iota(jnp.int32, (2 * QB, QB), 0)
    i = lax.broadcasted_iota(jnp.int32, (2 * QB, QB), 1)
    vis = (c - i >= QB - (SWA_WINDOW - 1)) & (c - i <= QB) & ((c >= QB) | (n > 0))
    bias = jnp.where(vis, 0.0, NEG).astype(F32)
    bias4 = jnp.concatenate([bias] * P, axis=1)
    k2 = jnp.concatenate([kvp_ref[:, :LANES], kvo_ref[:, :LANES]], axis=0).astype(F32)
    v2 = jnp.concatenate([kvp_ref[:, LANES:], kvo_ref[:, LANES:]], axis=0).astype(F32)
    k2r = pltpu.roll(k2, SWA_DH, 1)
    v2r = pltpu.roll(v2, SWA_DH, 1)
    lane = lax.broadcasted_iota(jnp.int32, (2 * QB, LANES), 1)
    lo = lane < SWA_DH
    top = lax.broadcasted_iota(jnp.int32, (LANES, P * QB), 0) < SWA_DH
    vts, ss = [], []
    for kv in range(SWA_KV_HEADS):
        ka, kb_ = (k2, k2r) if kv == 0 else (k2r, k2)
        va, vb_ = (v2, v2r) if kv == 0 else (v2r, v2)
        k_lo = jnp.where(lo, ka, 0.0).astype(BF16)
        k_hi = jnp.where(lo, 0.0, kb_).astype(BF16)
        vts.append((jnp.where(lo, va, 0.0).T.astype(BF16), jnp.where(lo, 0.0, vb_).T.astype(BF16)))
        base = kv * P
        q4 = jnp.concatenate([q_ref[:, (base + p) * LANES:(base + p + 1) * LANES] for p in range(P)],
                             axis=0)
        ss.append((_qk(k_lo, q4) + bias4, _qk(k_hi, q4) + bias4))
    stats = []
    for pair in ss:
        st = []
        for s in pair:
            m = jnp.max(s, axis=0, keepdims=True)
            p = jnp.exp2(s - m)
            st.append((m, p, jnp.sum(p, axis=0, keepdims=True)))
        stats.append(st)
    for kv in range(SWA_KV_HEADS):
        (m_lo, p_lo, l_lo), (m_hi, p_hi, l_hi) = stats[kv]
        vt_lo, vt_hi = vts[kv]
        base = kv * P
        ot = (jnp.dot(vt_lo, p_lo.astype(BF16), preferred_element_type=F32)
              + jnp.dot(vt_hi, p_hi.astype(BF16), preferred_element_type=F32))
        cols = slice(base * LANES, (base + P) * LANES)
        f_lo = 1.0 / ((1.0 + jnp.exp2(sink_ref[0:1, cols] * LOG2E - (m_lo + jnp.log2(l_lo)))) * l_lo)
        f_hi = 1.0 / ((1.0 + jnp.exp2(sink_ref[1:2, cols] * LOG2E - (m_hi + jnp.log2(l_hi)))) * l_hi)
        ot = ot * jnp.where(top, f_lo, f_hi)
        for p in range(P):
            cc = slice((base + p) * LANES, (base + p + 1) * LANES)
            o = ot[:, p * QB:(p + 1) * QB].T
            y_ref[:, cc] = (o * _silu(gate_ref[:, cc].astype(F32))).astype(y_ref.dtype)


def _swa_attn(qc, kvc, gate, sink_rows):
    nb = SEQ // Q_BLOCK
    w = SWA_HEADS * SWA_DH
    return pl.pallas_call(
        _swa_kernel,
        out_shape=jax.ShapeDtypeStruct((N_TOK, w), BF16),
        grid=(BATCH, nb),
        in_specs=[pl.BlockSpec((Q_BLOCK, w), lambda b, n: (b * nb + n, 0)),
                  pl.BlockSpec((Q_BLOCK, 2 * LANES), lambda b, n: (jnp.maximum(b * nb + n - 1, 0), 0)),
                  pl.BlockSpec((Q_BLOCK, 2 * LANES), lambda b, n: (b * nb + n, 0)),
                  pl.BlockSpec((Q_BLOCK, w), lambda b, n: (b * nb + n, 0)),
                  pl.BlockSpec((2, w), lambda b, n: (0, 0))],
        out_specs=pl.BlockSpec((Q_BLOCK, w), lambda b, n: (b * nb + n, 0)),
        compiler_params=_cparams(("parallel", "parallel")),
        name="swa_attn",
    )(qc, kvc, kvc, gate, sink_rows)


DIL_BLOCKS_PER_STEP = 2


def _dil_kernel(q_ref, kp_ref, ko_ref, vp_ref, vo_ref, o_ref, lse_ref):
    QB = Q_BLOCK
    R = DIL_BLOCKS_PER_STEP
    n = pl.program_id(2)
    lane = lax.broadcasted_iota(jnp.int32, (QB, LANES), 1)
    tasks = []
    for u in range(R):
        rows = slice(u * QB, (u + 1) * QB)
        bias = _band_bias(n if u == 0 else 1, 0, QB)
        for h in range(DIL_HEADS):
            c = slice(h * DIL_DH, (h + 1) * DIL_DH)
            k_prev = kp_ref[:, c] if u == 0 else ko_ref[(u - 1) * QB:u * QB, c]
            v_prev = vp_ref[:, c] if u == 0 else vo_ref[(u - 1) * QB:u * QB, c]
            k = jnp.concatenate([k_prev, ko_ref[rows, c]], axis=0)
            v = jnp.concatenate([v_prev, vo_ref[rows, c]], axis=0)
            tasks.append((u, h, _qk(q_ref[rows, c], k) + bias, v))
    soft = []
    for u, h, s, v in tasks:
        m = jnp.max(s, axis=1, keepdims=True)
        p = jnp.exp2(s - m)
        soft.append((m, p, jnp.sum(p, axis=1, keepdims=True)))
    lse_all = [jnp.zeros((QB, LANES), F32) for _ in range(R)]
    for (u, h, _, v), (m, p, l) in zip(tasks, soft):
        o_ref[u * QB:(u + 1) * QB, h * DIL_DH:(h + 1) * DIL_DH] = (
            jnp.dot(p.astype(BF16), v, preferred_element_type=F32) / l).astype(o_ref.dtype)
        lse_all[u] = jnp.where(lane == h, m + jnp.log2(l), lse_all[u])
    for u in range(R):
        lse_ref[u * QB:(u + 1) * QB, :] = lse_all[u]


def _dil_attn(qv, kv, vv, dil):
    w = DIL_HEADS * DIL_DH
    R = DIL_BLOCKS_PER_STEP
    L = SEQ // dil
    ns = L // (R * Q_BLOCK)
    cur = lambda b, r, n: (b * ns + n, r)
    prev = lambda b, r, n: (jnp.maximum((b * ns + n) * R - 1, 0), r)
    blk = (R * Q_BLOCK, w)
    pblk = (Q_BLOCK, w)
    return pl.pallas_call(
        _dil_kernel,
        out_shape=[jax.ShapeDtypeStruct((BATCH * L, dil * w), BF16),
                   jax.ShapeDtypeStruct((BATCH * L, dil * LANES), F32)],
        grid=(BATCH, dil, ns),
        in_specs=[pl.BlockSpec(blk, cur), pl.BlockSpec(pblk, prev), pl.BlockSpec(blk, cur),
                  pl.BlockSpec(pblk, prev), pl.BlockSpec(blk, cur)],
        out_specs=[pl.BlockSpec(blk, cur), pl.BlockSpec((R * Q_BLOCK, LANES), cur)],
        compiler_params=_cparams(("parallel", "parallel", "parallel")),
        name=f"dil_attn_d{dil}",
    )(qv, kv, kv, vv, vv)


def _dil_combine_kernel(tm, *refs):
    np_ = len(DIL_PATTERNS)
    o_refs, l_refs = refs[:np_], refs[np_:2 * np_]
    gate_ref, y_ref = refs[2 * np_], refs[2 * np_ + 1]
    stage = refs[2 * np_ + 2:]
    w = DIL_HEADS * DIL_DH
    os_, ls = [], []
    k = 0
    for (_, d), o_ref, l_ref in zip(DIL_PATTERNS, o_refs, l_refs):
        if d == 1:
            os_.append(lambda h, o_ref=o_ref: o_ref[:, h * DIL_DH:(h + 1) * DIL_DH].astype(F32))
            ls.append(l_ref[...])
            continue
        so, sl = stage[2 * k], stage[2 * k + 1]
        k += 1
        for r in range(d):
            for h in range(DIL_HEADS):
                a = r * w + h * DIL_DH
                so[h, pl.ds(r, tm // d, stride=d), :] = o_ref[:, a:a + DIL_DH].astype(F32)
            sl[pl.ds(r, tm // d, stride=d), :] = l_ref[:, r * LANES:(r + 1) * LANES]
        os_.append(lambda h, so=so: so[h])
        ls.append(sl[...])
    mx = functools.reduce(jnp.maximum, ls)
    es = [jnp.exp2(x - mx) for x in ls]
    den = functools.reduce(lambda a, b: a + b, es)
    ws = [e / den for e in es]
    lane = lax.broadcasted_iota(jnp.int32, ws[0].shape, 1)
    for h in range(DIL_HEADS):
        c = slice(h * DIL_DH, (h + 1) * DIL_DH)
        wh = [jnp.sum(jnp.where(lane == h, wgt, 0.0), axis=1, keepdims=True) for wgt in ws]
        o = functools.reduce(lambda a, b: a + b, [wh[p] * os_[p](h) for p in range(np_)])
        y_ref[:, c] = (o * _silu(gate_ref[:, c].astype(F32))).astype(y_ref.dtype)


def _dil_combine(outs, lses, gate, tm):
    w = DIL_HEADS * DIL_DH
    in_specs = [pl.BlockSpec((tm // d, d * w), lambda i: (i, 0)) for _, d in DIL_PATTERNS]
    in_specs += [pl.BlockSpec((tm // d, d * LANES), lambda i: (i, 0)) for _, d in DIL_PATTERNS]
    in_specs.append(pl.BlockSpec((tm, w), lambda i: (i, 0)))
    scratch = []
    for _, d in DIL_PATTERNS:
        if d > 1:
            scratch += [pltpu.VMEM((DIL_HEADS, tm, DIL_DH), F32), pltpu.VMEM((tm, LANES), F32)]
    return pl.pallas_call(
        functools.partial(_dil_combine_kernel, tm),
        out_shape=jax.ShapeDtypeStruct((N_TOK, w), BF16),
        grid=(N_TOK // tm,),
        in_specs=in_specs,
        out_specs=pl.BlockSpec((tm, w), lambda i: (i, 0)),
        scratch_shapes=scratch,
        compiler_params=_cparams(("parallel",)),
        name="dil_combine",
    )(*outs, *lses, gate)


def _outproj_kernel(nparts, final, *refs):
    ys, ws = refs[:nparts], refs[nparts:2 * nparts]
    x_ref = refs[2 * nparts]
    o_ref = refs[-1]
    acc = x_ref[...]
    for y, w in zip(ys, ws):
        acc = acc + jnp.dot(y[...], w[...], preferred_element_type=F32)
    if final:
        acc = _rms(acc, refs[2 * nparts + 1][...])
    o_ref[...] = acc


def _outproj(ys, ws, x, final_g, tm):
    n_tok, d = x.shape
    in_specs = [pl.BlockSpec((tm, y.shape[1]), lambda i: (i, 0)) for y in ys]
    in_specs += [pl.BlockSpec(w.shape, lambda i: (0, 0)) for w in ws]
    in_specs.append(pl.BlockSpec((tm, d), lambda i: (i, 0)))
    args = [*ys, *ws, x]
    if final_g is not None:
        in_specs.append(pl.BlockSpec((1, d), lambda i: (0, 0)))
        args.append(final_g.reshape(1, d))
    return pl.pallas_call(
        functools.partial(_outproj_kernel, len(ys), final_g is not None),
        out_shape=jax.ShapeDtypeStruct((n_tok, d), F32),
        grid=(n_tok // tm,),
        in_specs=in_specs,
        out_specs=pl.BlockSpec((tm, d), lambda i: (i, 0)),
        compiler_params=_cparams(("parallel",)),
        name="outproj",
    )(*args)


PROJ_TM = 512
ATTN_TQ = 512
MLA_HEADS_PER_STEP = 2
MOBA_HEADS_PER_STEP = 2


def _even_layer(x, mem2d, norm_g, w_in, q_norm_g, w_uq, kv_norm_g, w_ukv, mem_norm_g, w_mem_kv, w_out):
    n_lat = MLA_Q_RANK + MLA_KV_RANK + MLA_ROPE
    wlat = jnp.pad(w_in[:, :n_lat], ((0, 0), (0, LANES - MLA_ROPE))).astype(BF16)
    n_bqkv = n_lat + 3 * MOBA_HEADS * MOBA_DH
    w_bqkv = w_in[:, n_lat:n_bqkv].astype(BF16)
    w_rest = w_in[:, n_bqkv:].astype(BF16)
    wuq = w_uq.reshape(MLA_Q_RANK, MLA_HEADS, MLA_NOPE + MLA_ROPE)
    wuq = jnp.pad(wuq, ((0, 0), (0, 0), (0, MLA_HW - MLA_NOPE - MLA_ROPE)))
    wuq = wuq.reshape(MLA_Q_RANK, MLA_HEADS * MLA_HW).astype(BF16)
    wukv = w_ukv.reshape(MLA_KV_RANK, MLA_HEADS, MLA_NOPE + MLA_V)
    wukv = jnp.concatenate([wukv[:, :, :MLA_NOPE].reshape(MLA_KV_RANK, -1),
                            wukv[:, :, MLA_NOPE:].reshape(MLA_KV_RANK, -1)], axis=1).astype(BF16)

    cos_a, sin_a = _rope_tables(MLA_ROPE, LANES)
    qa, ka, va = _mla_prep(x, norm_g, wlat, q_norm_g, wuq, kv_norm_g, wukv, cos_a, sin_a, ATTN_TQ)

    rot = MOBA_DH // ROT_FRAC
    cos_b, sin_b = _rope_tables(rot, MOBA_DH)
    sc = MOBA_DH ** -0.5 * LOG2E
    wb = MOBA_HEADS * MOBA_DH
    plain = [("plain",)]
    groups_qkv = [
        (wb, [("rope", 0, rot // 2, MOBA_DH, sc)] * MOBA_HEADS, "rows"),
        (wb, [("rope", 0, rot // 2, MOBA_DH, 1.0)] * MOBA_HEADS, "rows"),
        (wb, plain * MOBA_HEADS, ("vT", MOBA_TILE)),
    ]
    groups_rest = [
        (MEMQ_W, [("scale", MEM_DH ** -0.5 * LOG2E)] * MEM_HEADS, "rows"),
        (MLA_HEADS * MLA_V, plain * MLA_HEADS, "rows"),
        (wb, plain * MOBA_HEADS, "rows"),
        (MEMQ_W, plain * MEM_HEADS, "rows"),
    ]
    qb, kb, vb = _inproj(x, norm_g, w_bqkv, groups_qkv, [cos_b, sin_b], PROJ_TM)
    qm, gate_a, gate_b, gate_m = _inproj(x, norm_g, w_rest, groups_rest, [], PROJ_TM)

    y_a = _mla_attn(qa, ka, va, gate_a, ATTN_TQ, MLA_HEADS_PER_STEP)
    y_b = _moba_flat_attn(qb, kb, vb, gate_b, MOBA_HEADS_PER_STEP)
    kvm = _mem_prep(mem2d, mem_norm_g, w_mem_kv.astype(BF16))
    y_m = _mem_attn(qm, kvm, gate_m, ATTN_TQ)

    a, b = MLA_HEADS * MLA_V, MLA_HEADS * MLA_V + MOBA_HEADS * MOBA_DH
    return [y_a, y_b, y_m], [w_out[:a].astype(BF16), w_out[a:b].astype(BF16), w_out[b:].astype(BF16)]


def _odd_layer(x, mem2d, norm_g, w_in, sinks, mem_norm_g, w_mem_kv, w_out):
    rot_c = SWA_DH // ROT_FRAC
    rot_d = DIL_DH // ROT_FRAC
    cos_c, sin_c = _rope_tables(rot_c, SWA_DH)
    cos_d, sin_d = _rope_tables(rot_d, DIL_DH)
    wq = SWA_HEADS * SWA_DH
    wd = DIL_HEADS * DIL_DH
    wkv = 2 * SWA_KV_HEADS * SWA_DH
    rc = lambda s: ("rope", 0, rot_c // 2, SWA_DH, s)
    rd = lambda s: ("rope", 1, rot_d // 2, DIL_DH, s)
    plain = [("plain",)]
    groups_qkv = [
        (wq, [rc(SWA_DH ** -0.5 * LOG2E)] * (wq // LANES), "rows"),
        (wkv, [rc(1.0), ("plain",)], "rows"),
        (wd, [rd(DIL_DH ** -0.5 * LOG2E)] * DIL_HEADS, "dilated"),
        (wd, [rd(1.0)] * DIL_HEADS, "dilated"),
        (wd, plain * DIL_HEADS, "dilated"),
    ]
    groups_rest = [
        (MEMQ_W, [("scale", MEM_DH ** -0.5 * LOG2E)] * MEM_HEADS, "rows"),
        (wq, plain * (wq // LANES), "rows"),
        (wd, plain * DIL_HEADS, "rows"),
        (MEMQ_W, plain * MEM_HEADS, "rows"),
    ]
    n_qkv = wq + wkv + 3 * wd
    res = _inproj(x, norm_g, w_in[:, :n_qkv].astype(BF16), groups_qkv, [cos_c, sin_c, cos_d, sin_d], PROJ_TM)
    qc, kvc = res[0], res[1]
    nv = 1 + len(DIL_VIEWS)
    qd, kd, vd = (res[2 + i * nv:2 + (i + 1) * nv] for i in range(3))
    qm, gate_c, gate_d, gate_m = _inproj(x, norm_g, w_in[:, n_qkv:].astype(BF16), groups_rest, [], PROJ_TM)

    sink_rows = jnp.repeat(sinks.astype(F32).reshape(SWA_HEADS // 2, 2).T, LANES, axis=1)
    y_c = _swa_attn(qc, kvc, gate_c, sink_rows)
    view_of = {1: 0, **{d: 1 + i for i, d in enumerate(DIL_VIEWS)}}
    outs, lses = zip(*[_dil_attn(qd[view_of[dil]], kd[view_of[dil]], vd[view_of[dil]], dil)
                       for _, dil in DIL_PATTERNS])
    y_d = _dil_combine(outs, lses, gate_d, PROJ_TM)
    kvm = _mem_prep(mem2d, mem_norm_g, w_mem_kv.astype(BF16))
    y_m = _mem_attn(qm, kvm, gate_m, ATTN_TQ)

    return [y_c, y_d, y_m], [w_out[:wq].astype(BF16), w_out[wq:wq + wd].astype(BF16),
                             w_out[wq + wd:].astype(BF16)]


def kernel(x, mem, ev_norm_g, ev_w_in, ev_q_norm_g, ev_w_uq, ev_kv_norm_g, ev_w_ukv, ev_mem_norm_g,
           ev_w_mem_kv, ev_w_out, od_norm_g, od_w_in, od_sinks, od_mem_norm_g, od_w_mem_kv, od_w_out,
           final_norm_g):
    x2 = x.reshape(N_TOK, D_MODEL)
    mem2d = mem.reshape(BATCH * N_MEM, D_MODEL)
    ys, ws = _even_layer(x2, mem2d, ev_norm_g[0], ev_w_in[0], ev_q_norm_g[0], ev_w_uq[0], ev_kv_norm_g[0],
                         ev_w_ukv[0], ev_mem_norm_g[0], ev_w_mem_kv[0], ev_w_out[0])
    x2 = _outproj(ys, ws, x2, None, PROJ_TM)
    ys, ws = _odd_layer(x2, mem2d, od_norm_g[0], od_w_in[0], od_sinks[0], od_mem_norm_g[0],
                        od_w_mem_kv[0], od_w_out[0])
    x2 = _outproj(ys, ws, x2, final_norm_g, PROJ_TM)
    return x2.reshape(BATCH, SEQ, D_MODEL)
```

```python
import functools

import numpy as np
import jax
import jax.numpy as jnp
from jax import lax
from jax.experimental import pallas as pl
from jax.experimental.pallas import tpu as pltpu

D_MODEL = 2048
BATCH = 4
SEQ = 4096
N_TOK = BATCH * SEQ
N_MEM = 256
ROPE_THETA = 500000.0
ROT_FRAC = 4
EPS = 1e-6

MLA_HEADS = 8
MLA_Q_RANK = 512
MLA_KV_RANK = 256
MLA_NOPE = 128
MLA_ROPE = 64
MLA_V = 128
MOBA_HEADS = 8
MOBA_DH = 128
MOBA_BLOCK = 256
MOBA_TOPK = 3
SWA_HEADS = 16
SWA_KV_HEADS = 2
SWA_DH = 64
SWA_WINDOW = 128
DIL_HEADS = 6
DIL_DH = 128
DIL_PATTERNS = ((128, 1), (512, 4), (2048, 16))
MEM_HEADS = 4
MEM_DH = 128
MEMQ_W = MEM_HEADS * MEM_DH
Q_BLOCK = 128

LANES = 128
VMEM_LIMIT = 52 * 1024 * 1024

NEG = -1e30
BF16 = jnp.bfloat16
F32 = jnp.float32


def _cparams(sem):
    return pltpu.CompilerParams(dimension_semantics=sem, vmem_limit_bytes=VMEM_LIMIT)


def _rms(x, g):
    ms = jnp.mean(x * x, axis=-1, keepdims=True)
    return (x * lax.rsqrt(ms + EPS)) * g


def _silu(g):
    return g / (1.0 + jnp.exp(-g))


def _sigmoid(z):
    return 1.0 / (1.0 + jnp.exp(-z))


def _rope_slab(x, cos_t, sin_t, half, period):
    lane = lax.broadcasted_iota(jnp.int32, x.shape, 1)
    up = pltpu.roll(x, LANES - half, 1)
    dn = pltpu.roll(x, half, 1)
    sw = jnp.where((lane & (period - 1)) < half, up, dn)
    return x * cos_t + sw * sin_t


def _qk(q, k):
    return lax.dot_general(q, k, (((1,), (1,)), ((), ())), preferred_element_type=F32)


def _rope_tables(rot_dim, period):
    half = rot_dim // 2
    inv = 1.0 / (ROPE_THETA ** (jnp.arange(0, rot_dim, 2, dtype=F32) / rot_dim))
    ang = jnp.arange(SEQ, dtype=F32)[:, None] * inv[None, :]
    c, s = jnp.cos(ang), jnp.sin(ang)
    d = np.arange(LANES) % period
    idx = d % half
    cos_t = jnp.where(d < rot_dim, c[:, idx], 1.0)
    sin_t = jnp.where(d < half, -s[:, idx], jnp.where(d < rot_dim, s[:, idx], 0.0))
    return cos_t.astype(F32), sin_t.astype(F32)


PROJ_CHUNK = 512
DIL_VIEWS = tuple(d for _, d in DIL_PATTERNS if d > 1)


def _inproj_kernel(groups, ntab, tm, x_ref, g_ref, *rest):
    tabs = rest[:2 * ntab]
    w_ref = rest[2 * ntab]
    ndil = sum(layout == "dilated" for _, _, layout in groups)
    refs = list(rest[2 * ntab + 1:len(rest) - ndil])
    stage = list(rest[len(rest) - ndil:])
    xn = _rms(x_ref[...], g_ref[...]).astype(BF16)
    c0 = 0
    for ncols, kinds, layout in groups:
        o_ref = refs.pop(0)
        views = [refs.pop(0) for _ in DIL_VIEWS] if layout == "dilated" else []
        st_ref = stage.pop(0) if layout == "dilated" else None
        for s0 in range(0, ncols, PROJ_CHUNK):
            n = min(PROJ_CHUNK, ncols - s0)
            acc = jnp.dot(xn, w_ref[:, c0 + s0:c0 + s0 + n], preferred_element_type=F32)
            for c in range(n // LANES):
                col = s0 + c * LANES
                kind = kinds[col // LANES]
                piece = acc[:, c * LANES:(c + 1) * LANES]
                if kind[0] == "rope":
                    _, ti, half, period, sc = kind
                    piece = _rope_slab(piece, tabs[2 * ti][...], tabs[2 * ti + 1][...], half, period)
                    if sc != 1.0:
                        piece = piece * sc
                elif kind[0] == "scale":
                    piece = piece * kind[1]
                if layout == "rows" or layout == "dilated":
                    o_ref[:, col:col + LANES] = piece.astype(o_ref.dtype)
                    if layout == "dilated":
                        st_ref[col // LANES] = piece
                else:
                    tb = layout[1]
                    for u in range(tm // tb):
                        o_ref[col // LANES, u] = _vt_block(piece[u * tb:(u + 1) * tb, :])
        for d, v_ref in zip(DIL_VIEWS, views):
            for r in range(d):
                for c in range(ncols // LANES):
                    a = r * ncols + c * LANES
                    v_ref[:, a:a + LANES] = st_ref[c, pl.ds(r, tm // d, stride=d), :].astype(v_ref.dtype)
        c0 += ncols


def _inproj(x, g, w, groups, tables, tm):
    n_tok, d = x.shape
    assert sum(nc for nc, _, _ in groups) == w.shape[1]
    seq_tiles = SEQ // tm
    in_specs = [pl.BlockSpec((tm, d), lambda i: (i, 0)),
                pl.BlockSpec((1, d), lambda i: (0, 0))]
    args = [x, g.reshape(1, d)]
    for t in tables:
        in_specs.append(pl.BlockSpec((tm, LANES), lambda i: (i % seq_tiles, 0)))
        args.append(t)
    in_specs.append(pl.BlockSpec(w.shape, lambda i: (0, 0), pipeline_mode=pl.Buffered(1)))
    args.append(w)
    out_specs, out_shapes = [], []
    for nc, _, layout in groups:
        if layout == "rows" or layout == "dilated":
            out_specs.append(pl.BlockSpec((tm, nc), lambda i: (i, 0)))
            out_shapes.append(jax.ShapeDtypeStruct((n_tok, nc), BF16))
            if layout == "dilated":
                for dil in DIL_VIEWS:
                    out_specs.append(pl.BlockSpec((tm // dil, dil * nc), lambda i: (i, 0)))
                    out_shapes.append(jax.ShapeDtypeStruct((n_tok // dil, dil * nc), BF16))
        else:
            tb = layout[1]
            out_specs.append(pl.BlockSpec((nc // LANES, tm // tb, VT_ROWS, tb), lambda i: (0, i, 0, 0)))
            out_shapes.append(jax.ShapeDtypeStruct((nc // LANES, n_tok // tb, VT_ROWS, tb), BF16))
    return pl.pallas_call(
        functools.partial(_inproj_kernel, groups, len(tables) // 2, tm),
        out_shape=out_shapes,
        grid=(n_tok // tm,),
        in_specs=in_specs,
        out_specs=out_specs,
        scratch_shapes=[pltpu.VMEM((nc // LANES, tm, LANES), F32) for nc, _, layout in groups
                        if layout == "dilated"],
        compiler_params=_cparams(("parallel",)),
        name="inproj",
    )(*args)


MLA_HW = 256


def _mla_prep_kernel(scale, x_ref, g_ref, wlat_ref, qg_ref, wuq_ref, kvg_ref, wukv_ref,
                     cos_ref, sin_ref, qa_ref, ka_ref, va_ref):
    tm = x_ref.shape[0]
    halves = [slice(u * (tm // 2), (u + 1) * (tm // 2)) for u in range(2)]
    xns = [_rms(x_ref[r, :], g_ref[...]).astype(BF16) for r in halves]
    lats = [jnp.dot(xn, wlat_ref[...], preferred_element_type=F32) for xn in xns]
    cqs = [_rms(lat[:, :MLA_Q_RANK], qg_ref[...]).astype(BF16) for lat in lats]
    ckvs = [_rms(lat[:, MLA_Q_RANK:MLA_Q_RANK + MLA_KV_RANK], kvg_ref[...]).astype(BF16) for lat in lats]
    qs = [jnp.dot(cq, wuq_ref[...], preferred_element_type=F32) for cq in cqs]
    kvs = [jnp.dot(ckv, wukv_ref[...], preferred_element_type=F32) for ckv in ckvs]
    for r, lat, q, kv in zip(halves, lats, qs, kvs):
        cos_t, sin_t = cos_ref[r, :], sin_ref[r, :]
        kpe = lat[:, MLA_Q_RANK + MLA_KV_RANK:]
        kpe = _rope_slab(kpe, cos_t, sin_t, MLA_ROPE // 2, LANES).astype(BF16)
        for h in range(MLA_HEADS):
            a = h * MLA_HW
            qa_ref[r, a:a + LANES] = (q[:, a:a + LANES] * scale).astype(BF16)
            qr = _rope_slab(q[:, a + LANES:a + 2 * LANES], cos_t, sin_t, MLA_ROPE // 2, LANES)
            qa_ref[r, a + LANES:a + 2 * LANES] = (qr * scale).astype(BF16)
            ka_ref[r, a:a + LANES] = kv[:, h * LANES:(h + 1) * LANES].astype(BF16)
            ka_ref[r, a + LANES:a + 2 * LANES] = kpe
            v0 = MLA_HEADS * MLA_NOPE + h * MLA_V
            va_ref[h, 0, :, r] = _vt_block(kv[:, v0:v0 + MLA_V])


def _mla_prep(x, g, wlat, qg, wuq, kvg, wukv, cos_t, sin_t, tm):
    n_tok, d = x.shape
    seq_tiles = SEQ // tm
    full = lambda a: pl.BlockSpec(a.shape, lambda i: (0, 0))
    qg2, kvg2, g2 = qg.reshape(1, -1), kvg.reshape(1, -1), g.reshape(1, d)
    scale = (MLA_NOPE + MLA_ROPE) ** -0.5 * LOG2E
    return pl.pallas_call(
        functools.partial(_mla_prep_kernel, scale),
        out_shape=[jax.ShapeDtypeStruct((n_tok, MLA_HEADS * MLA_HW), BF16),
                   jax.ShapeDtypeStruct((n_tok, MLA_HEADS * MLA_HW), BF16),
                   jax.ShapeDtypeStruct((MLA_HEADS, n_tok // tm, VT_ROWS, tm), BF16)],
        grid=(n_tok // tm,),
        in_specs=[pl.BlockSpec((tm, d), lambda i: (i, 0)), full(g2), full(wlat), full(qg2), full(wuq),
                  full(kvg2), full(wukv),
                  pl.BlockSpec((tm, LANES), lambda i: (i % seq_tiles, 0)),
                  pl.BlockSpec((tm, LANES), lambda i: (i % seq_tiles, 0))],
        out_specs=[pl.BlockSpec((tm, MLA_HEADS * MLA_HW), lambda i: (i, 0)),
                   pl.BlockSpec((tm, MLA_HEADS * MLA_HW), lambda i: (i, 0)),
                   pl.BlockSpec((MLA_HEADS, 1, VT_ROWS, tm), lambda i: (0, i, 0, 0))],
        compiler_params=_cparams(("parallel",)),
        name="mla_prep",
    )(x, g2, wlat, qg2, wuq, kvg2, wukv, cos_t, sin_t)


LOG2E = 1.4426950408889634
VT_TAIL = 16
VT_ROWS = LANES + VT_TAIL
FLASH_CHUNK = 128


def _vt_block(v):
    tb = v.shape[0]
    row = lax.broadcasted_iota(jnp.int32, (VT_TAIL, tb), 0)
    tail = jnp.where(row == 0, 1.0, 0.0).astype(F32)
    return jnp.concatenate([v.T, tail], axis=0).astype(BF16)


def _flash_finish_all(nh, n, tq, dv, acc_ref, gate_ref, y_ref):
    def body(c, _):
        rows = pl.ds(pl.multiple_of(c * tq, tq), tq)
        for g in range(nh):
            cols = slice(g * dv, (g + 1) * dv)
            a = acc_ref.at[g, c]
            o = (a[:LANES, :] / a[LANES:LANES + 1, :]).T
            y_ref[rows, cols] = (o * _silu(gate_ref[rows, cols].astype(F32))).astype(y_ref.dtype)
        return 0

    lax.fori_loop(0, n, body, 0)


def _flash_flat_t(nh, n, tq, score_fn, row_bias, vt_fn, s_ref, p_ref, acc_ref, m_ref):
    n_off = n * (n - 1) // 2
    assert n % 2 == 0 and n_off % 2 == 0
    acc_ref[...] = jnp.zeros_like(acc_ref)
    p_ref[...] = jnp.zeros_like(p_ref)
    m_ref[...] = jnp.full(m_ref.shape, NEG, F32)
    key = lax.broadcasted_iota(jnp.int32, (tq, tq), 0)
    qry = lax.broadcasted_iota(jnp.int32, (tq, tq), 1)

    def diag_scores(c):
        return lambda g: jnp.where(key <= qry, score_fn(g, c, c), NEG)

    def off_scores(j, c):
        return lambda g: score_fn(g, j, jnp.minimum(c, n - 1))

    def step(rd, wr, nxt, cur, prev, alphas):
        (jc, cc), (jp, cp) = cur, prev
        for g in range(nh):
            s_ref[wr, g] = nxt(g)
        pend = [jnp.dot(vt_fn(g, jp), p_ref[wr, g], preferred_element_type=F32) for g in range(nh)]
        out = []
        for g in range(nh):
            s_view = s_ref.at[rd, g]
            m_prev = m_ref[g, cc]
            nbs = [None] if row_bias is None else row_bias(g, jc, cc)
            hk = tq // len(nbs)
            m_new = m_prev
            for u, nb in enumerate(nbs):
                mb = functools.reduce(jnp.maximum, [
                    jnp.max(s_view[r0:r0 + FLASH_CHUNK, :], axis=0, keepdims=True)
                    for r0 in range(u * hk, (u + 1) * hk, FLASH_CHUNK)])
                m_new = jnp.maximum(m_new, mb if nb is None else mb + nb)
            m_ref[g, cc] = m_new
            acc_ref[g, cp] = alphas[g] * acc_ref[g, cp] + pend[g]
            for u, nb in enumerate(nbs):
                shift = m_new if nb is None else m_new - nb
                for r0 in range(u * hk, (u + 1) * hk, FLASH_CHUNK):
                    rows = slice(r0, r0 + FLASH_CHUNK)
                    p_ref[rd, g, rows, :] = jnp.exp2((s_view[rows, :] - shift).astype(BF16))
            out.append(jnp.exp2(m_prev - m_new))
        return tuple(out)

    def advance(j, c):
        wrap = j + 1 >= c
        return jnp.where(wrap, 0, j + 1), jnp.where(wrap, c + 1, c)

    for g in range(nh):
        s_ref[0, g] = diag_scores(0)(g)
    alphas = (jnp.ones((1, tq), F32),) * nh

    def diag_pair(u, alphas):
        i = 2 * u
        ip = jnp.maximum(i - 1, 0)
        alphas = step(0, 1, diag_scores(i + 1), (i, i), (ip, ip), alphas)
        return step(1, 0, diag_scores(i + 2), (i + 1, i + 1), (i, i), alphas)

    alphas = lax.fori_loop(0, (n - 2) // 2, diag_pair, alphas)
    alphas = step(0, 1, diag_scores(n - 1), (n - 2, n - 2), (n - 3, n - 3), alphas)
    alphas = step(1, 0, off_scores(0, 1), (n - 1, n - 1), (n - 2, n - 2), alphas)

    def off_pair(u, carry):
        jc, cc, jp, cp = carry[:4]
        jn, cn = advance(jc, cc)
        alphas = step(0, 1, off_scores(jn, cn), (jc, cc), (jp, cp), carry[4:])
        jn2, cn2 = advance(jn, cn)
        alphas = step(1, 0, off_scores(jn2, cn2), (jn, cn), (jc, cc), alphas)
        return (jn2, cn2, jn, cn) + tuple(alphas)

    i32 = jnp.int32
    carry = lax.fori_loop(0, n_off // 2, off_pair, (i32(0), i32(1), i32(n - 1), i32(n - 1)) + tuple(alphas))
    jl, cl = carry[2], carry[3]
    for g in range(nh):
        pend = jnp.dot(vt_fn(g, jl), p_ref[1, g], preferred_element_type=F32)
        acc_ref[g, cl] = carry[4 + g] * acc_ref[g, cl] + pend


def _rows(i, t):
    return pl.ds(i * t, t) if isinstance(i, int) else pl.ds(pl.multiple_of(i * t, t), t)


def _flash_scratch(nh, n, tq):
    return [pltpu.VMEM((2, nh, tq, tq), F32), pltpu.VMEM((2, nh, tq, tq), BF16),
            pltpu.VMEM((nh, n, VT_ROWS, tq), F32), pltpu.VMEM((nh, n, 1, tq), F32)]


def _mla_kernel(tq, nh, q_ref, k_ref, vt_ref, gate_ref, y_ref, s_ref, p_ref, acc_ref, m_ref):
    n = SEQ // tq

    def scores(g, j, c):
        cols = slice(g * MLA_HW, (g + 1) * MLA_HW)
        return _qk(k_ref[_rows(j, tq), cols], q_ref[_rows(c, tq), cols])

    _flash_flat_t(nh, n, tq, scores, None, lambda g, j: vt_ref[g, j], s_ref, p_ref, acc_ref, m_ref)
    _flash_finish_all(nh, n, tq, MLA_V, acc_ref, gate_ref, y_ref)


def _mla_attn(qa, ka, vat, gate, tq, nh):
    nq = SEQ // tq
    return pl.pallas_call(
        functools.partial(_mla_kernel, tq, nh),
        out_shape=jax.ShapeDtypeStruct((N_TOK, MLA_HEADS * MLA_V), BF16),
        grid=(BATCH, MLA_HEADS // nh),
        in_specs=[pl.BlockSpec((SEQ, nh * MLA_HW), lambda b, h: (b, h)),
                  pl.BlockSpec((SEQ, nh * MLA_HW), lambda b, h: (b, h)),
                  pl.BlockSpec((nh, nq, VT_ROWS, tq), lambda b, h: (h, b, 0, 0)),
                  pl.BlockSpec((SEQ, nh * MLA_V), lambda b, h: (b, h))],
        out_specs=pl.BlockSpec((SEQ, nh * MLA_V), lambda b, h: (b, h)),
        scratch_shapes=_flash_scratch(nh, nq, tq),
        compiler_params=_cparams(("parallel", "parallel")),
        name="mla_attn",
    )(qa, ka, vat, gate)


MOBA_TILE = 2 * MOBA_BLOCK


def _moba_flat_kernel(nh, q_ref, k_ref, vt_ref, gate_ref, y_ref, km_ref, nb_ref, s_ref, p_ref, acc_ref, m_ref):
    L = MOBA_BLOCK
    T = MOBA_TILE
    D = MOBA_DH
    nblk = SEQ // L
    ntile = SEQ // T
    cols = [slice(g * D, (g + 1) * D) for g in range(nh)]

    for g in range(nh):
        for j in range(nblk):
            blk = k_ref[j * L:(j + 1) * L, cols[g]].astype(F32)
            km_ref[g, j:j + 1, :] = jnp.sum(blk, axis=0, keepdims=True) * (1.0 / L)

    blk_i = lax.broadcasted_iota(jnp.int32, (nblk, T), 0)
    blk_f = blk_i.astype(F32)
    second = lax.broadcasted_iota(jnp.int32, (nblk, T), 1) >= L

    def select(c, carry):
        own = 2 * c + second.astype(jnp.int32)
        past = blk_i < own
        for g in range(nh):
            gt = jnp.where(past, _qk(km_ref[g].astype(BF16), q_ref[_rows(c, T), cols[g]]), -jnp.inf)
            sel = jnp.zeros((nblk, T), jnp.bool_)
            for _ in range(MOBA_TOPK):
                mx = jnp.max(gt, axis=0, keepdims=True)
                first = jnp.min(jnp.where(gt == mx, blk_f, float(nblk)), axis=0, keepdims=True)
                pick = blk_f == first
                sel = jnp.logical_or(sel, pick)
                gt = jnp.where(pick, -jnp.inf, gt)
            keep = jnp.logical_or(jnp.logical_and(sel, past), blk_i == own)
            nb_ref[g, c] = jnp.where(keep, 0.0, NEG)
        return carry

    lax.fori_loop(0, ntile, select, 0)

    def scores(g, j, c):
        return _qk(k_ref[_rows(j, T), cols[g]], q_ref[_rows(c, T), cols[g]])

    def bias(g, j, c):
        return [nb_ref[g, c, pl.ds(2 * j, 1), :], nb_ref[g, c, pl.ds(2 * j + 1, 1), :]]

    _flash_flat_t(nh, ntile, T, scores, bias, lambda g, j: vt_ref[g, j], s_ref, p_ref, acc_ref, m_ref)
    _flash_finish_all(nh, ntile, T, D, acc_ref, gate_ref, y_ref)


def _moba_flat_attn(qb, kb, vbt, gate, nh):
    T = MOBA_TILE
    D = MOBA_DH
    nblk = SEQ // MOBA_BLOCK
    ntile = SEQ // T
    return pl.pallas_call(
        functools.partial(_moba_flat_kernel, nh),
        out_shape=jax.ShapeDtypeStruct((N_TOK, MOBA_HEADS * D), BF16),
        grid=(BATCH, MOBA_HEADS // nh),
        in_specs=[pl.BlockSpec((SEQ, nh * D), lambda b, h: (b, h)),
                  pl.BlockSpec((SEQ, nh * D), lambda b, h: (b, h)),
                  pl.BlockSpec((nh, ntile, VT_ROWS, T), lambda b, h: (h, b, 0, 0)),
                  pl.BlockSpec((SEQ, nh * D), lambda b, h: (b, h))],
        out_specs=pl.BlockSpec((SEQ, nh * D), lambda b, h: (b, h)),
        scratch_shapes=[pltpu.VMEM((nh, nblk, D), F32), pltpu.VMEM((nh, ntile, nblk, T), F32)]
        + _flash_scratch(nh, ntile, T),
        compiler_params=_cparams(("parallel", "parallel")),
        name="moba_attn",
    )(qb, kb, vbt, gate)


def _mem_prep_kernel(mem_ref, g_ref, w_ref, kv_ref):
    mn = _rms(mem_ref[...], g_ref[...]).astype(BF16)
    kv_ref[...] = jnp.dot(mn, w_ref[...], preferred_element_type=F32).astype(BF16)


def _mem_prep(mem2d, g, w):
    g2 = g.reshape(1, -1)
    return pl.pallas_call(
        _mem_prep_kernel,
        out_shape=jax.ShapeDtypeStruct((BATCH * N_MEM, 2 * MEMQ_W), BF16),
        grid=(BATCH,),
        in_specs=[pl.BlockSpec((N_MEM, D_MODEL), lambda b: (b, 0)),
                  pl.BlockSpec(g2.shape, lambda b: (0, 0)),
                  pl.BlockSpec(w.shape, lambda b: (0, 0))],
        out_specs=pl.BlockSpec((N_MEM, 2 * MEMQ_W), lambda b: (b, 0)),
        compiler_params=_cparams(("parallel",)),
        name="mem_prep",
    )(mem2d, g2, w)


def _mem_attn_kernel(q_ref, kv_ref, gate_ref, y_ref):
    cols = [slice(h * MEM_DH, (h + 1) * MEM_DH) for h in range(MEM_HEADS)]
    ss = [_qk(q_ref[:, c], kv_ref[:, c]) for c in cols]
    ps = [jnp.exp2(s - jnp.max(s, axis=1, keepdims=True)) for s in ss]
    for h, (c, p) in enumerate(zip(cols, ps)):
        l = jnp.sum(p, axis=1, keepdims=True)
        a = MEMQ_W + h * MEM_DH
        o = jnp.dot(p.astype(BF16), kv_ref[:, a:a + MEM_DH], preferred_element_type=F32) / l
        y_ref[:, c] = (o * _silu(gate_ref[:, c].astype(F32))).astype(y_ref.dtype)


def _mem_attn(qm, kvm, gate, tq):
    nq = SEQ // tq
    return pl.pallas_call(
        _mem_attn_kernel,
        out_shape=jax.ShapeDtypeStruct((N_TOK, MEMQ_W), BF16),
        grid=(BATCH, nq),
        in_specs=[pl.BlockSpec((tq, MEMQ_W), lambda b, i: (b * nq + i, 0)),
                  pl.BlockSpec((N_MEM, 2 * MEMQ_W), lambda b, i: (b, 0)),
                  pl.BlockSpec((tq, MEMQ_W), lambda b, i: (b * nq + i, 0))],
        out_specs=pl.BlockSpec((tq, MEMQ_W), lambda b, i: (b * nq + i, 0)),
        compiler_params=_cparams(("parallel", "parallel")),
        name="mem_attn",
    )(qm, kvm, gate)


def _band_bias(n, lo_off, hi_off):
    i = lax.broadcasted_iota(jnp.int32, (Q_BLOCK, 2 * Q_BLOCK), 0)
    c = lax.broadcasted_iota(jnp.int32, (Q_BLOCK, 2 * Q_BLOCK), 1)
    vis = (c - i >= lo_off) & (c - i <= hi_off) & ((c >= Q_BLOCK) | (n > 0))
    return jnp.where(vis, 0.0, NEG).astype(F32)


def _swa_kernel(q_ref, kvp_ref, kvo_ref, gate_ref, sink_ref, y_ref):
    QB = Q_BLOCK
    G = SWA_HEADS // SWA_KV_HEADS
    P = G // 2
    n = pl.program_id(1)
    c = lax.broadcasted_iota(jnp.int32, (2 * QB, QB), 0)
    i = lax.broadcasted_iota(jnp.int32, (2 * QB, QB), 1)
    vis = (c - i >= QB - (SWA_WINDOW - 1)) & (c - i <= QB) & ((c >= QB) | (n > 0))
    bias = jnp.where(vis, 0.0, NEG).astype(F32)
    bias4 = jnp.concatenate([bias] * P, axis=1)
    k2 = jnp.concatenate([kvp_ref[:, :LANES], kvo_ref[:, :LANES]], axis=0).astype(F32)
    v2 = jnp.concatenate([kvp_ref[:, LANES:], kvo_ref[:, LANES:]], axis=0).astype(F32)
    k2r = pltpu.roll(k2, SWA_DH, 1)
    v2r = pltpu.roll(v2, SWA_DH, 1)
    lane = lax.broadcasted_iota(jnp.int32, (2 * QB, LANES), 1)
    lo = lane < SWA_DH
    top = lax.broadcasted_iota(jnp.int32, (LANES, P * QB), 0) < SWA_DH
    vts, ss = [], []
    for kv in range(SWA_KV_HEADS):
        ka, kb_ = (k2, k2r) if kv == 0 else (k2r, k2)
        va, vb_ = (v2, v2r) if kv == 0 else (v2r, v2)
        k_lo = jnp.where(lo, ka, 0.0).astype(BF16)
        k_hi = jnp.where(lo, 0.0, kb_).astype(BF16)
        vts.append((jnp.where(lo, va, 0.0).T.astype(BF16), jnp.where(lo, 0.0, vb_).T.astype(BF16)))
        base = kv * P
        q4 = jnp.concatenate([q_ref[:, (base + p) * LANES:(base + p + 1) * LANES] for p in range(P)],
                             axis=0)
        ss.append((_qk(k_lo, q4) + bias4, _qk(k_hi, q4) + bias4))
    stats = []
    for pair in ss:
        st = []
        for s in pair:
            m = jnp.max(s, axis=0, keepdims=True)
            p = jnp.exp2(s - m)
            st.append((m, p, jnp.sum(p, axis=0, keepdims=True)))
        stats.append(st)
    for kv in range(SWA_KV_HEADS):
        (m_lo, p_lo, l_lo), (m_hi, p_hi, l_hi) = stats[kv]
        vt_lo, vt_hi = vts[kv]
        base = kv * P
        ot = (jnp.dot(vt_lo, p_lo.astype(BF16), preferred_element_type=F32)
              + jnp.dot(vt_hi, p_hi.astype(BF16), preferred_element_type=F32))
        cols = slice(base * LANES, (base + P) * LANES)
        f_lo = 1.0 / ((1.0 + jnp.exp2(sink_ref[0:1, cols] * LOG2E - (m_lo + jnp.log2(l_lo)))) * l_lo)
        f_hi = 1.0 / ((1.0 + jnp.exp2(sink_ref[1:2, cols] * LOG2E - (m_hi + jnp.log2(l_hi)))) * l_hi)
        ot = ot * jnp.where(top, f_lo, f_hi)
        for p in range(P):
            cc = slice((base + p) * LANES, (base + p + 1) * LANES)
            o = ot[:, p * QB:(p + 1) * QB].T
            y_ref[:, cc] = (o * _silu(gate_ref[:, cc].astype(F32))).astype(y_ref.dtype)


def _swa_attn(qc, kvc, gate, sink_rows):
    nb = SEQ // Q_BLOCK
    w = SWA_HEADS * SWA_DH
    return pl.pallas_call(
        _swa_kernel,
        out_shape=jax.ShapeDtypeStruct((N_TOK, w), BF16),
        grid=(BATCH, nb),
        in_specs=[pl.BlockSpec((Q_BLOCK, w), lambda b, n: (b * nb + n, 0)),
                  pl.BlockSpec((Q_BLOCK, 2 * LANES), lambda b, n: (jnp.maximum(b * nb + n - 1, 0), 0)),
                  pl.BlockSpec((Q_BLOCK, 2 * LANES), lambda b, n: (b * nb + n, 0)),
                  pl.BlockSpec((Q_BLOCK, w), lambda b, n: (b * nb + n, 0)),
                  pl.BlockSpec((2, w), lambda b, n: (0, 0))],
        out_specs=pl.BlockSpec((Q_BLOCK, w), lambda b, n: (b * nb + n, 0)),
        compiler_params=_cparams(("parallel", "parallel")),
        name="swa_attn",
    )(qc, kvc, kvc, gate, sink_rows)


DIL_BLOCKS_PER_STEP = 2


def _dil_kernel(q_ref, kp_ref, ko_ref, vp_ref, vo_ref, o_ref, lse_ref):
    QB = Q_BLOCK
    R = DIL_BLOCKS_PER_STEP
    n = pl.program_id(2)
    lane = lax.broadcasted_iota(jnp.int32, (QB, LANES), 1)
    tasks = []
    for u in range(R):
        rows = slice(u * QB, (u + 1) * QB)
        bias = _band_bias(n if u == 0 else 1, 0, QB)
        for h in range(DIL_HEADS):
            c = slice(h * DIL_DH, (h + 1) * DIL_DH)
            k_prev = kp_ref[:, c] if u == 0 else ko_ref[(u - 1) * QB:u * QB, c]
            v_prev = vp_ref[:, c] if u == 0 else vo_ref[(u - 1) * QB:u * QB, c]
            k = jnp.concatenate([k_prev, ko_ref[rows, c]], axis=0)
            v = jnp.concatenate([v_prev, vo_ref[rows, c]], axis=0)
            tasks.append((u, h, _qk(q_ref[rows, c], k) + bias, v))
    soft = []
    for u, h, s, v in tasks:
        m = jnp.max(s, axis=1, keepdims=True)
        p = jnp.exp2(s - m)
        soft.append((m, p, jnp.sum(p, axis=1, keepdims=True)))
    lse_all = [jnp.zeros((QB, LANES), F32) for _ in range(R)]
    for (u, h, _, v), (m, p, l) in zip(tasks, soft):
        o_ref[u * QB:(u + 1) * QB, h * DIL_DH:(h + 1) * DIL_DH] = (
            jnp.dot(p.astype(BF16), v, preferred_element_type=F32) / l).astype(o_ref.dtype)
        lse_all[u] = jnp.where(lane == h, m + jnp.log2(l), lse_all[u])
    for u in range(R):
        lse_ref[u * QB:(u + 1) * QB, :] = lse_all[u]


def _dil_attn(qv, kv, vv, dil):
    w = DIL_HEADS * DIL_DH
    R = DIL_BLOCKS_PER_STEP
    L = SEQ // dil
    ns = L // (R * Q_BLOCK)
    cur = lambda b, r, n: (b * ns + n, r)
    prev = lambda b, r, n: (jnp.maximum((b * ns + n) * R - 1, 0), r)
    blk = (R * Q_BLOCK, w)
    pblk = (Q_BLOCK, w)
    return pl.pallas_call(
        _dil_kernel,
        out_shape=[jax.ShapeDtypeStruct((BATCH * L, dil * w), BF16),
                   jax.ShapeDtypeStruct((BATCH * L, dil * LANES), F32)],
        grid=(BATCH, dil, ns),
        in_specs=[pl.BlockSpec(blk, cur), pl.BlockSpec(pblk, prev), pl.BlockSpec(blk, cur),
                  pl.BlockSpec(pblk, prev), pl.BlockSpec(blk, cur)],
        out_specs=[pl.BlockSpec(blk, cur), pl.BlockSpec((R * Q_BLOCK, LANES), cur)],
        compiler_params=_cparams(("parallel", "parallel", "parallel")),
        name=f"dil_attn_d{dil}",
    )(qv, kv, kv, vv, vv)


def _dil_combine_kernel(tm, *refs):
    np_ = len(DIL_PATTERNS)
    o_refs, l_refs = refs[:np_], refs[np_:2 * np_]
    gate_ref, y_ref = refs[2 * np_], refs[2 * np_ + 1]
    stage = refs[2 * np_ + 2:]
    w = DIL_HEADS * DIL_DH
    os_, ls = [], []
    k = 0
    for (_, d), o_ref, l_ref in zip(DIL_PATTERNS, o_refs, l_refs):
        if d == 1:
            os_.append(lambda h, o_ref=o_ref: o_ref[:, h * DIL_DH:(h + 1) * DIL_DH].astype(F32))
            ls.append(l_ref[...])
            continue
        so, sl = stage[2 * k], stage[2 * k + 1]
        k += 1
        for r in range(d):
            for h in range(DIL_HEADS):
                a = r * w + h * DIL_DH
                so[h, pl.ds(r, tm // d, stride=d), :] = o_ref[:, a:a + DIL_DH].astype(F32)
            sl[pl.ds(r, tm // d, stride=d), :] = l_ref[:, r * LANES:(r + 1) * LANES]
        os_.append(lambda h, so=so: so[h])
        ls.append(sl[...])
    mx = functools.reduce(jnp.maximum, ls)
    es = [jnp.exp2(x - mx) for x in ls]
    den = functools.reduce(lambda a, b: a + b, es)
    ws = [e / den for e in es]
    lane = lax.broadcasted_iota(jnp.int32, ws[0].shape, 1)
    for h in range(DIL_HEADS):
        c = slice(h * DIL_DH, (h + 1) * DIL_DH)
        wh = [jnp.sum(jnp.where(lane == h, wgt, 0.0), axis=1, keepdims=True) for wgt in ws]
        o = functools.reduce(lambda a, b: a + b, [wh[p] * os_[p](h) for p in range(np_)])
        y_ref[:, c] = (o * _silu(gate_ref[:, c].astype(F32))).astype(y_ref.dtype)


def _dil_combine(outs, lses, gate, tm):
    w = DIL_HEADS * DIL_DH
    in_specs = [pl.BlockSpec((tm // d, d * w), lambda i: (i, 0)) for _, d in DIL_PATTERNS]
    in_specs += [pl.BlockSpec((tm // d, d * LANES), lambda i: (i, 0)) for _, d in DIL_PATTERNS]
    in_specs.append(pl.BlockSpec((tm, w), lambda i: (i, 0)))
    scratch = []
    for _, d in DIL_PATTERNS:
        if d > 1:
            scratch += [pltpu.VMEM((DIL_HEADS, tm, DIL_DH), F32), pltpu.VMEM((tm, LANES), F32)]
    return pl.pallas_call(
        functools.partial(_dil_combine_kernel, tm),
        out_shape=jax.ShapeDtypeStruct((N_TOK, w), BF16),
        grid=(N_TOK // tm,),
        in_specs=in_specs,
        out_specs=pl.BlockSpec((tm, w), lambda i: (i, 0)),
        scratch_shapes=scratch,
        compiler_params=_cparams(("parallel",)),
        name="dil_combine",
    )(*outs, *lses, gate)


def _outproj_kernel(nparts, final, *refs):
    ys, ws = refs[:nparts], refs[nparts:2 * nparts]
    x_ref = refs[2 * nparts]
    o_ref = refs[-1]
    acc = x_ref[...]
    for y, w in zip(ys, ws):
        acc = acc + jnp.dot(y[...], w[...], preferred_element_type=F32)
    if final:
        acc = _rms(acc, refs[2 * nparts + 1][...])
    o_ref[...] = acc


def _outproj(ys, ws, x, final_g, tm):
    n_tok, d = x.shape
    in_specs = [pl.BlockSpec((tm, y.shape[1]), lambda i: (i, 0)) for y in ys]
    in_specs += [pl.BlockSpec(w.shape, lambda i: (0, 0)) for w in ws]
    in_specs.append(pl.BlockSpec((tm, d), lambda i: (i, 0)))
    args = [*ys, *ws, x]
    if final_g is not None:
        in_specs.append(pl.BlockSpec((1, d), lambda i: (0, 0)))
        args.append(final_g.reshape(1, d))
    return pl.pallas_call(
        functools.partial(_outproj_kernel, len(ys), final_g is not None),
        out_shape=jax.ShapeDtypeStruct((n_tok, d), F32),
        grid=(n_tok // tm,),
        in_specs=in_specs,
        out_specs=pl.BlockSpec((tm, d), lambda i: (i, 0)),
        compiler_params=_cparams(("parallel",)),
        name="outproj",
    )(*args)


PROJ_TM = 512
ATTN_TQ = 512
MLA_HEADS_PER_STEP = 2
MOBA_HEADS_PER_STEP = 2


def _even_layer(x, mem2d, norm_g, w_in, q_norm_g, w_uq, kv_norm_g, w_ukv, mem_norm_g, w_mem_kv, w_out):
    n_lat = MLA_Q_RANK + MLA_KV_RANK + MLA_ROPE
    wlat = jnp.pad(w_in[:, :n_lat], ((0, 0), (0, LANES - MLA_ROPE))).astype(BF16)
    n_bqkv = n_lat + 3 * MOBA_HEADS * MOBA_DH
    w_bqkv = w_in[:, n_lat:n_bqkv].astype(BF16)
    w_rest = w_in[:, n_bqkv:].astype(BF16)
    wuq = w_uq.reshape(MLA_Q_RANK, MLA_HEADS, MLA_NOPE + MLA_ROPE)
    wuq = jnp.pad(wuq, ((0, 0), (0, 0), (0, MLA_HW - MLA_NOPE - MLA_ROPE)))
    wuq = wuq.reshape(MLA_Q_RANK, MLA_HEADS * MLA_HW).astype(BF16)
    wukv = w_ukv.reshape(MLA_KV_RANK, MLA_HEADS, MLA_NOPE + MLA_V)
    wukv = jnp.concatenate([wukv[:, :, :MLA_NOPE].reshape(MLA_KV_RANK, -1),
                            wukv[:, :, MLA_NOPE:].reshape(MLA_KV_RANK, -1)], axis=1).astype(BF16)

    cos_a, sin_a = _rope_tables(MLA_ROPE, LANES)
    qa, ka, va = _mla_prep(x, norm_g, wlat, q_norm_g, wuq, kv_norm_g, wukv, cos_a, sin_a, ATTN_TQ)

    rot = MOBA_DH // ROT_FRAC
    cos_b, sin_b = _rope_tables(rot, MOBA_DH)
    sc = MOBA_DH ** -0.5 * LOG2E
    wb = MOBA_HEADS * MOBA_DH
    plain = [("plain",)]
    groups_qkv = [
        (wb, [("rope", 0, rot // 2, MOBA_DH, sc)] * MOBA_HEADS, "rows"),
        (wb, [("rope", 0, rot // 2, MOBA_DH, 1.0)] * MOBA_HEADS, "rows"),
        (wb, plain * MOBA_HEADS, ("vT", MOBA_TILE)),
    ]
    groups_rest = [
        (MEMQ_W, [("scale", MEM_DH ** -0.5 * LOG2E)] * MEM_HEADS, "rows"),
        (MLA_HEADS * MLA_V, plain * MLA_HEADS, "rows"),
        (wb, plain * MOBA_HEADS, "rows"),
        (MEMQ_W, plain * MEM_HEADS, "rows"),
    ]
    qb, kb, vb = _inproj(x, norm_g, w_bqkv, groups_qkv, [cos_b, sin_b], PROJ_TM)
    qm, gate_a, gate_b, gate_m = _inproj(x, norm_g, w_rest, groups_rest, [], PROJ_TM)

    y_a = _mla_attn(qa, ka, va, gate_a, ATTN_TQ, MLA_HEADS_PER_STEP)
    y_b = _moba_flat_attn(qb, kb, vb, gate_b, MOBA_HEADS_PER_STEP)
    kvm = _mem_prep(mem2d, mem_norm_g, w_mem_kv.astype(BF16))
    y_m = _mem_attn(qm, kvm, gate_m, ATTN_TQ)

    a, b = MLA_HEADS * MLA_V, MLA_HEADS * MLA_V + MOBA_HEADS * MOBA_DH
    return [y_a, y_b, y_m], [w_out[:a].astype(BF16), w_out[a:b].astype(BF16), w_out[b:].astype(BF16)]


def _odd_layer(x, mem2d, norm_g, w_in, sinks, mem_norm_g, w_mem_kv, w_out):
    rot_c = SWA_DH // ROT_FRAC
    rot_d = DIL_DH // ROT_FRAC
    cos_c, sin_c = _rope_tables(rot_c, SWA_DH)
    cos_d, sin_d = _rope_tables(rot_d, DIL_DH)
    wq = SWA_HEADS * SWA_DH
    wd = DIL_HEADS * DIL_DH
    wkv = 2 * SWA_KV_HEADS * SWA_DH
    rc = lambda s: ("rope", 0, rot_c // 2, SWA_DH, s)
    rd = lambda s: ("rope", 1, rot_d // 2, DIL_DH, s)
    plain = [("plain",)]
    groups_qkv = [
        (wq, [rc(SWA_DH ** -0.5 * LOG2E)] * (wq // LANES), "rows"),
        (wkv, [rc(1.0), ("plain",)], "rows"),
        (wd, [rd(DIL_DH ** -0.5 * LOG2E)] * DIL_HEADS, "dilated"),
        (wd, [rd(1.0)] * DIL_HEADS, "dilated"),
        (wd, plain * DIL_HEADS, "dilated"),
    ]
    groups_rest = [
        (MEMQ_W, [("scale", MEM_DH ** -0.5 * LOG2E)] * MEM_HEADS, "rows"),
        (wq, plain * (wq // LANES), "rows"),
        (wd, plain * DIL_HEADS, "rows"),
        (MEMQ_W, plain * MEM_HEADS, "rows"),
    ]
    n_qkv = wq + wkv + 3 * wd
    res = _inproj(x, norm_g, w_in[:, :n_qkv].astype(BF16), groups_qkv, [cos_c, sin_c, cos_d, sin_d], PROJ_TM)
    qc, kvc = res[0], res[1]
    nv = 1 + len(DIL_VIEWS)
    qd, kd, vd = (res[2 + i * nv:2 + (i + 1) * nv] for i in range(3))
    qm, gate_c, gate_d, gate_m = _inproj(x, norm_g, w_in[:, n_qkv:].astype(BF16), groups_rest, [], PROJ_TM)

    sink_rows = jnp.repeat(sinks.astype(F32).reshape(SWA_HEADS // 2, 2).T, LANES, axis=1)
    y_c = _swa_attn(qc, kvc, gate_c, sink_rows)
    view_of = {1: 0, **{d: 1 + i for i, d in enumerate(DIL_VIEWS)}}
    outs, lses = zip(*[_dil_attn(qd[view_of[dil]], kd[view_of[dil]], vd[view_of[dil]], dil)
                       for _, dil in DIL_PATTERNS])
    y_d = _dil_combine(outs, lses, gate_d, PROJ_TM)
    kvm = _mem_prep(mem2d, mem_norm_g, w_mem_kv.astype(BF16))
    y_m = _mem_attn(qm, kvm, gate_m, ATTN_TQ)

    return [y_c, y_d, y_m], [w_out[:wq].astype(BF16), w_out[wq:wq + wd].astype(BF16),
                             w_out[wq + wd:].astype(BF16)]


def kernel(x, mem, ev_norm_g, ev_w_in, ev_q_norm_g, ev_w_uq, ev_kv_norm_g, ev_w_ukv, ev_mem_norm_g,
           ev_w_mem_kv, ev_w_out, od_norm_g, od_w_in, od_sinks, od_mem_norm_g, od_w_mem_kv, od_w_out,
           final_norm_g):
    x2 = x.reshape(N_TOK, D_MODEL)
    mem2d = mem.reshape(BATCH * N_MEM, D_MODEL)
    ys, ws = _even_layer(x2, mem2d, ev_norm_g[0], ev_w_in[0], ev_q_norm_g[0], ev_w_uq[0], ev_kv_norm_g[0],
                         ev_w_ukv[0], ev_mem_norm_g[0], ev_w_mem_kv[0], ev_w_out[0])
    x2 = _outproj(ys, ws, x2, None, PROJ_TM)
    ys, ws = _odd_layer(x2, mem2d, od_norm_g[0], od_w_in[0], od_sinks[0], od_mem_norm_g[0],
                        od_w_mem_kv[0], od_w_out[0])
    x2 = _outproj(ys, ws, x2, final_norm_g, PROJ_TM)
    return x2.reshape(BATCH, SEQ, D_MODEL)
```

```python
import functools

import numpy as np
import jax
import jax.numpy as jnp
from jax import lax
from jax.experimental import pallas as pl
from jax.experimental.pallas import tpu as pltpu

D_MODEL = 2048
BATCH = 4
SEQ = 4096
N_TOK = BATCH * SEQ
N_MEM = 256
ROPE_THETA = 500000.0
ROT_FRAC = 4
EPS = 1e-6

MLA_HEADS = 8
MLA_Q_RANK = 512
MLA_KV_RANK = 256
MLA_NOPE = 128
MLA_ROPE = 64
MLA_V = 128
MOBA_HEADS = 8
MOBA_DH = 128
MOBA_BLOCK = 256
MOBA_TOPK = 3
SWA_HEADS = 16
SWA_KV_HEADS = 2
SWA_DH = 64
SWA_WINDOW = 128
DIL_HEADS = 6
DIL_DH = 128
DIL_PATTERNS = ((128, 1), (512, 4), (2048, 16))
MEM_HEADS = 4
MEM_DH = 128
MEMQ_W = MEM_HEADS * MEM_DH
Q_BLOCK = 128

LANES = 128
VMEM_LIMIT = 52 * 1024 * 1024

NEG = -1e30
BF16 = jnp.bfloat16
F32 = jnp.float32


def _cparams(sem):
    return pltpu.CompilerParams(dimension_semantics=sem, vmem_limit_bytes=VMEM_LIMIT)


def _rms(x, g):
    ms = jnp.mean(x * x, axis=-1, keepdims=True)
    return (x * lax.rsqrt(ms + EPS)) * g


def _silu(g):
    return g / (1.0 + jnp.exp(-g))


def _sigmoid(z):
    return 1.0 / (1.0 + jnp.exp(-z))


def _rope_slab(x, cos_t, sin_t, half, period):
    lane = lax.broadcasted_iota(jnp.int32, x.shape, 1)
    up = pltpu.roll(x, LANES - half, 1)
    dn = pltpu.roll(x, half, 1)
    sw = jnp.where((lane & (period - 1)) < half, up, dn)
    return x * cos_t + sw * sin_t


def _qk(q, k):
    return lax.dot_general(q, k, (((1,), (1,)), ((), ())), preferred_element_type=F32)


def _rope_tables(rot_dim, period):
    half = rot_dim // 2
    inv = 1.0 / (ROPE_THETA ** (jnp.arange(0, rot_dim, 2, dtype=F32) / rot_dim))
    ang = jnp.arange(SEQ, dtype=F32)[:, None] * inv[None, :]
    c, s = jnp.cos(ang), jnp.sin(ang)
    d = np.arange(LANES) % period
    idx = d % half
    cos_t = jnp.where(d < rot_dim, c[:, idx], 1.0)
    sin_t = jnp.where(d < half, -s[:, idx], jnp.where(d < rot_dim, s[:, idx], 0.0))
    return cos_t.astype(F32), sin_t.astype(F32)


PROJ_CHUNK = 512
DIL_VIEWS = tuple(d for _, d in DIL_PATTERNS if d > 1)


def _inproj_kernel(groups, ntab, tm, x_ref, g_ref, *rest):
    tabs = rest[:2 * ntab]
    w_ref = rest[2 * ntab]
    ndil = sum(layout == "dilated" for _, _, layout in groups)
    nscr = ndil + (1 if ndil else 0)
    refs = list(rest[2 * ntab + 1:len(rest) - nscr])
    stage = list(rest[len(rest) - nscr:])
    chain_ref = stage.pop() if ndil else None
    xn = _rms(x_ref[...], g_ref[...]).astype(BF16)
    c0 = 0
    for ncols, kinds, layout in groups:
        o_ref = refs.pop(0)
        views = [refs.pop(0) for _ in DIL_VIEWS] if layout == "dilated" else []
        st_ref = stage.pop(0) if layout == "dilated" else None
        for s0 in range(0, ncols, PROJ_CHUNK):
            n = min(PROJ_CHUNK, ncols - s0)
            acc = jnp.dot(xn, w_ref[:, c0 + s0:c0 + s0 + n], preferred_element_type=F32)
            for c in range(n // LANES):
                col = s0 + c * LANES
                kind = kinds[col // LANES]
                piece = acc[:, c * LANES:(c + 1) * LANES]
                if kind[0] == "rope":
                    _, ti, half, period, sc = kind
                    piece = _rope_slab(piece, tabs[2 * ti][...], tabs[2 * ti + 1][...], half, period)
                    if sc != 1.0:
                        piece = piece * sc
                elif kind[0] == "scale":
                    piece = piece * kind[1]
                if layout == "rows" or layout == "dilated":
                    o_ref[:, col:col + LANES] = piece.astype(o_ref.dtype)
                    if layout == "dilated":
                        st_ref[col // LANES] = piece
                else:
                    tb = layout[1]
                    for u in range(tm // tb):
                        o_ref[col // LANES, u] = _vt_block(piece[u * tb:(u + 1) * tb, :])
        src, src_d = st_ref, 1
        for vi, (d, v_ref) in enumerate(zip(DIL_VIEWS, views)):
            keep = chain_ref if vi + 1 < len(views) else None
            rows_src, rows_dst = tm // src_d, tm // d
            for r in range(d):
                start = (r % src_d) * rows_src + r // src_d
                for c in range(ncols // LANES):
                    blk = src[c, pl.ds(start, rows_dst, stride=d // src_d), :]
                    a = r * ncols + c * LANES
                    v_ref[:, a:a + LANES] = blk.astype(v_ref.dtype)
                    if keep is not None:
                        keep[c, r * rows_dst:(r + 1) * rows_dst, :] = blk
            src, src_d = keep, d
        c0 += ncols


def _inproj(x, g, w, groups, tables, tm):
    n_tok, d = x.shape
    assert sum(nc for nc, _, _ in groups) == w.shape[1]
    seq_tiles = SEQ // tm
    in_specs = [pl.BlockSpec((tm, d), lambda i: (i, 0)),
                pl.BlockSpec((1, d), lambda i: (0, 0))]
    args = [x, g.reshape(1, d)]
    for t in tables:
        in_specs.append(pl.BlockSpec((tm, LANES), lambda i: (i % seq_tiles, 0)))
        args.append(t)
    in_specs.append(pl.BlockSpec(w.shape, lambda i: (0, 0), pipeline_mode=pl.Buffered(1)))
    args.append(w)
    out_specs, out_shapes = [], []
    for nc, _, layout in groups:
        if layout == "rows" or layout == "dilated":
            out_specs.append(pl.BlockSpec((tm, nc), lambda i: (i, 0)))
            out_shapes.append(jax.ShapeDtypeStruct((n_tok, nc), BF16))
            if layout == "dilated":
                for dil in DIL_VIEWS:
                    out_specs.append(pl.BlockSpec((tm // dil, dil * nc), lambda i: (i, 0)))
                    out_shapes.append(jax.ShapeDtypeStruct((n_tok // dil, dil * nc), BF16))
        else:
            tb = layout[1]
            out_specs.append(pl.BlockSpec((nc // LANES, tm // tb, VT_ROWS, tb), lambda i: (0, i, 0, 0)))
            out_shapes.append(jax.ShapeDtypeStruct((nc // LANES, n_tok // tb, VT_ROWS, tb), BF16))
    return pl.pallas_call(
        functools.partial(_inproj_kernel, groups, len(tables) // 2, tm),
        out_shape=out_shapes,
        grid=(n_tok // tm,),
        in_specs=in_specs,
        out_specs=out_specs,
        scratch_shapes=(lambda dil: dil + dil[:1])(
            [pltpu.VMEM((nc // LANES, tm, LANES), F32) for nc, _, layout in groups if layout == "dilated"]),
        compiler_params=_cparams(("parallel",)),
        name="inproj",
    )(*args)


MLA_HW = 256


def _mla_prep_kernel(scale, x_ref, g_ref, wlat_ref, qg_ref, wuq_ref, kvg_ref, wukv_ref,
                     cos_ref, sin_ref, qa_ref, ka_ref, va_ref):
    tm = x_ref.shape[0]
    halves = [slice(u * (tm // 2), (u + 1) * (tm // 2)) for u in range(2)]
    xns = [_rms(x_ref[r, :], g_ref[...]).astype(BF16) for r in halves]
    lats = [jnp.dot(xn, wlat_ref[...], preferred_element_type=F32) for xn in xns]
    cqs = [_rms(lat[:, :MLA_Q_RANK], qg_ref[...]).astype(BF16) for lat in lats]
    ckvs = [_rms(lat[:, MLA_Q_RANK:MLA_Q_RANK + MLA_KV_RANK], kvg_ref[...]).astype(BF16) for lat in lats]
    qs = [jnp.dot(cq, wuq_ref[...], preferred_element_type=F32) for cq in cqs]
    kvs = [jnp.dot(ckv, wukv_ref[...], preferred_element_type=F32) for ckv in ckvs]
    for r, lat, q, kv in zip(halves, lats, qs, kvs):
        cos_t, sin_t = cos_ref[r, :], sin_ref[r, :]
        kpe = lat[:, MLA_Q_RANK + MLA_KV_RANK:]
        kpe = _rope_slab(kpe, cos_t, sin_t, MLA_ROPE // 2, LANES).astype(BF16)
        for h in range(MLA_HEADS):
            a = h * MLA_HW
            qa_ref[r, a:a + LANES] = (q[:, a:a + LANES] * scale).astype(BF16)
            qr = _rope_slab(q[:, a + LANES:a + 2 * LANES], cos_t, sin_t, MLA_ROPE // 2, LANES)
            qa_ref[r, a + LANES:a + 2 * LANES] = (qr * scale).astype(BF16)
            ka_ref[r, a:a + LANES] = kv[:, h * LANES:(h + 1) * LANES].astype(BF16)
            ka_ref[r, a + LANES:a + 2 * LANES] = kpe
            v0 = MLA_HEADS * MLA_NOPE + h * MLA_V
            va_ref[h, 0, :, r] = _vt_block(kv[:, v0:v0 + MLA_V])


def _mla_prep(x, g, wlat, qg, wuq, kvg, wukv, cos_t, sin_t, tm):
    n_tok, d = x.shape
    seq_tiles = SEQ // tm
    full = lambda a: pl.BlockSpec(a.shape, lambda i: (0, 0))
    qg2, kvg2, g2 = qg.reshape(1, -1), kvg.reshape(1, -1), g.reshape(1, d)
    scale = (MLA_NOPE + MLA_ROPE) ** -0.5 * LOG2E
    return pl.pallas_call(
        functools.partial(_mla_prep_kernel, scale),
        out_shape=[jax.ShapeDtypeStruct((n_tok, MLA_HEADS * MLA_HW), BF16),
                   jax.ShapeDtypeStruct((n_tok, MLA_HEADS * MLA_HW), BF16),
                   jax.ShapeDtypeStruct((MLA_HEADS, n_tok // tm, VT_ROWS, tm), BF16)],
        grid=(n_tok // tm,),
        in_specs=[pl.BlockSpec((tm, d), lambda i: (i, 0)), full(g2), full(wlat), full(qg2), full(wuq),
                  full(kvg2), full(wukv),
                  pl.BlockSpec((tm, LANES), lambda i: (i % seq_tiles, 0)),
                  pl.BlockSpec((tm, LANES), lambda i: (i % seq_tiles, 0))],
        out_specs=[pl.BlockSpec((tm, MLA_HEADS * MLA_HW), lambda i: (i, 0)),
                   pl.BlockSpec((tm, MLA_HEADS * MLA_HW), lambda i: (i, 0)),
                   pl.BlockSpec((MLA_HEADS, 1, VT_ROWS, tm), lambda i: (0, i, 0, 0))],
        compiler_params=_cparams(("parallel",)),
        name="mla_prep",
    )(x, g2, wlat, qg2, wuq, kvg2, wukv, cos_t, sin_t)


LOG2E = 1.4426950408889634
VT_TAIL = 16
VT_ROWS = LANES + VT_TAIL
FLASH_CHUNK = 128


def _vt_block(v):
    tb = v.shape[0]
    row = lax.broadcasted_iota(jnp.int32, (VT_TAIL, tb), 0)
    tail = jnp.where(row == 0, 1.0, 0.0).astype(F32)
    return jnp.concatenate([v.T, tail], axis=0).astype(BF16)


def _flash_finish_all(nh, n, tq, dv, acc_ref, gate_ref, y_ref):
    def body(c, _):
        rows = pl.ds(pl.multiple_of(c * tq, tq), tq)
        for g in range(nh):
            cols = slice(g * dv, (g + 1) * dv)
            a = acc_ref.at[g, c]
            o = (a[:LANES, :] / a[LANES:LANES + 1, :]).T
            y_ref[rows, cols] = (o * _silu(gate_ref[rows, cols].astype(F32))).astype(y_ref.dtype)
        return 0

    lax.fori_loop(0, n, body, 0)


def _flash_flat_t(nh, n, tq, score_fn, row_bias, vt_fn, s_ref, p_ref, acc_ref, m_ref):
    n_off = n * (n - 1) // 2
    assert n % 2 == 0 and n_off % 2 == 0
    acc_ref[...] = jnp.zeros_like(acc_ref)
    p_ref[...] = jnp.zeros_like(p_ref)
    m_ref[...] = jnp.full(m_ref.shape, NEG, F32)
    key = lax.broadcasted_iota(jnp.int32, (tq, tq), 0)
    qry = lax.broadcasted_iota(jnp.int32, (tq, tq), 1)

    def diag_scores(c):
        return lambda g: jnp.where(key <= qry, score_fn(g, c, c), NEG)

    def off_scores(j, c):
        return lambda g: score_fn(g, j, jnp.minimum(c, n - 1))

    def step(rd, wr, nxt, cur, prev, alphas):
        (jc, cc), (jp, cp) = cur, prev
        for g in range(nh):
            s_ref[wr, g] = nxt(g)
        pend = [jnp.dot(vt_fn(g, jp), p_ref[wr, g], preferred_element_type=F32) for g in range(nh)]
        out = []
        for g in range(nh):
            s_view = s_ref.at[rd, g]
            m_prev = m_ref[g, cc]
            nbs = [None] if row_bias is None else row_bias(g, jc, cc)
            hk = tq // len(nbs)
            m_new = m_prev
            for u, nb in enumerate(nbs):
                mb = functools.reduce(jnp.maximum, [
                    jnp.max(s_view[r0:r0 + FLASH_CHUNK, :], axis=0, keepdims=True)
                    for r0 in range(u * hk, (u + 1) * hk, FLASH_CHUNK)])
                m_new = jnp.maximum(m_new, mb if nb is None else mb + nb)
            m_ref[g, cc] = m_new
            acc_ref[g, cp] = alphas[g] * acc_ref[g, cp] + pend[g]
            for u, nb in enumerate(nbs):
                shift = m_new if nb is None else m_new - nb
                for r0 in range(u * hk, (u + 1) * hk, FLASH_CHUNK):
                    rows = slice(r0, r0 + FLASH_CHUNK)
                    p_ref[rd, g, rows, :] = jnp.exp2((s_view[rows, :] - shift).astype(BF16))
            out.append(jnp.exp2(m_prev - m_new))
        return tuple(out)

    def advance(j, c):
        wrap = j + 1 >= c
        return jnp.where(wrap, 0, j + 1), jnp.where(wrap, c + 1, c)

    for g in range(nh):
        s_ref[0, g] = diag_scores(0)(g)
    alphas = (jnp.ones((1, tq), F32),) * nh

    def diag_pair(u, alphas):
        i = 2 * u
        ip = jnp.maximum(i - 1, 0)
        alphas = step(0, 1, diag_scores(i + 1), (i, i), (ip, ip), alphas)
        return step(1, 0, diag_scores(i + 2), (i + 1, i + 1), (i, i), alphas)

    alphas = lax.fori_loop(0, (n - 2) // 2, diag_pair, alphas)
    alphas = step(0, 1, diag_scores(n - 1), (n - 2, n - 2), (n - 3, n - 3), alphas)
    alphas = step(1, 0, off_scores(0, 1), (n - 1, n - 1), (n - 2, n - 2), alphas)

    def off_pair(u, carry):
        jc, cc, jp, cp = carry[:4]
        jn, cn = advance(jc, cc)
        alphas = step(0, 1, off_scores(jn, cn), (jc, cc), (jp, cp), carry[4:])
        jn2, cn2 = advance(jn, cn)
        alphas = step(1, 0, off_scores(jn2, cn2), (jn, cn), (jc, cc), alphas)
        return (jn2, cn2, jn, cn) + tuple(alphas)

    i32 = jnp.int32
    carry = lax.fori_loop(0, n_off // 2, off_pair, (i32(0), i32(1), i32(n - 1), i32(n - 1)) + tuple(alphas))
    jl, cl = carry[2], carry[3]
    for g in range(nh):
        pend = jnp.dot(vt_fn(g, jl), p_ref[1, g], preferred_element_type=F32)
        acc_ref[g, cl] = carry[4 + g] * acc_ref[g, cl] + pend


def _rows(i, t):
    return pl.ds(i * t, t) if isinstance(i, int) else pl.ds(pl.multiple_of(i * t, t), t)


def _flash_scratch(nh, n, tq):
    return [pltpu.VMEM((2, nh, tq, tq), F32), pltpu.VMEM((2, nh, tq, tq), BF16),
            pltpu.VMEM((nh, n, VT_ROWS, tq), F32), pltpu.VMEM((nh, n, 1, tq), F32)]


def _mla_kernel(tq, nh, q_ref, k_ref, vt_ref, gate_ref, y_ref, s_ref, p_ref, acc_ref, m_ref):
    n = SEQ // tq

    def scores(g, j, c):
        cols = slice(g * MLA_HW, (g + 1) * MLA_HW)
        return _qk(k_ref[_rows(j, tq), cols], q_ref[_rows(c, tq), cols])

    _flash_flat_t(nh, n, tq, scores, None, lambda g, j: vt_ref[g, j], s_ref, p_ref, acc_ref, m_ref)
    _flash_finish_all(nh, n, tq, MLA_V, acc_ref, gate_ref, y_ref)


def _mla_attn(qa, ka, vat, gate, tq, nh):
    nq = SEQ // tq
    return pl.pallas_call(
        functools.partial(_mla_kernel, tq, nh),
        out_shape=jax.ShapeDtypeStruct((N_TOK, MLA_HEADS * MLA_V), BF16),
        grid=(BATCH, MLA_HEADS // nh),
        in_specs=[pl.BlockSpec((SEQ, nh * MLA_HW), lambda b, h: (b, h)),
                  pl.BlockSpec((SEQ, nh * MLA_HW), lambda b, h: (b, h)),
                  pl.BlockSpec((nh, nq, VT_ROWS, tq), lambda b, h: (h, b, 0, 0)),
                  pl.BlockSpec((SEQ, nh * MLA_V), lambda b, h: (b, h))],
        out_specs=pl.BlockSpec((SEQ, nh * MLA_V), lambda b, h: (b, h)),
        scratch_shapes=_flash_scratch(nh, nq, tq),
        compiler_params=_cparams(("parallel", "parallel")),
        name="mla_attn",
    )(qa, ka, vat, gate)


MOBA_TILE = 2 * MOBA_BLOCK


def _moba_flat_kernel(nh, q_ref, k_ref, vt_ref, gate_ref, y_ref, km_ref, nb_ref, s_ref, p_ref, acc_ref, m_ref):
    L = MOBA_BLOCK
    T = MOBA_TILE
    D = MOBA_DH
    nblk = SEQ // L
    ntile = SEQ // T
    cols = [slice(g * D, (g + 1) * D) for g in range(nh)]

    for g in range(nh):
        for j in range(nblk):
            blk = k_ref[j * L:(j + 1) * L, cols[g]].astype(F32)
            km_ref[g, j:j + 1, :] = jnp.sum(blk, axis=0, keepdims=True) * (1.0 / L)

    blk_i = lax.broadcasted_iota(jnp.int32, (nblk, T), 0)
    blk_f = blk_i.astype(F32)
    second = lax.broadcasted_iota(jnp.int32, (nblk, T), 1) >= L

    def select(c, carry):
        own = 2 * c + second.astype(jnp.int32)
        past = blk_i < own
        for g in range(nh):
            gt = jnp.where(past, _qk(km_ref[g].astype(BF16), q_ref[_rows(c, T), cols[g]]), -jnp.inf)
            sel = jnp.zeros((nblk, T), jnp.bool_)
            for _ in range(MOBA_TOPK):
                mx = jnp.max(gt, axis=0, keepdims=True)
                first = jnp.min(jnp.where(gt == mx, blk_f, float(nblk)), axis=0, keepdims=True)
                pick = blk_f == first
                sel = jnp.logical_or(sel, pick)
                gt = jnp.where(pick, -jnp.inf, gt)
            keep = jnp.logical_or(jnp.logical_and(sel, past), blk_i == own)
            nb_ref[g, c] = jnp.where(keep, 0.0, NEG)
        return carry

    lax.fori_loop(0, ntile, select, 0)

    def scores(g, j, c):
        return _qk(k_ref[_rows(j, T), cols[g]], q_ref[_rows(c, T), cols[g]])

    def bias(g, j, c):
        return [nb_ref[g, c, pl.ds(2 * j, 1), :], nb_ref[g, c, pl.ds(2 * j + 1, 1), :]]

    _flash_flat_t(nh, ntile, T, scores, bias, lambda g, j: vt_ref[g, j], s_ref, p_ref, acc_ref, m_ref)
    _flash_finish_all(nh, ntile, T, D, acc_ref, gate_ref, y_ref)


def _moba_flat_attn(qb, kb, vbt, gate, nh):
    T = MOBA_TILE
    D = MOBA_DH
    nblk = SEQ // MOBA_BLOCK
    ntile = SEQ // T
    return pl.pallas_call(
        functools.partial(_moba_flat_kernel, nh),
        out_shape=jax.ShapeDtypeStruct((N_TOK, MOBA_HEADS * D), BF16),
        grid=(BATCH, MOBA_HEADS // nh),
        in_specs=[pl.BlockSpec((SEQ, nh * D), lambda b, h: (b, h)),
                  pl.BlockSpec((SEQ, nh * D), lambda b, h: (b, h)),
                  pl.BlockSpec((nh, ntile, VT_ROWS, T), lambda b, h: (h, b, 0, 0)),
                  pl.BlockSpec((SEQ, nh * D), lambda b, h: (b, h))],
        out_specs=pl.BlockSpec((SEQ, nh * D), lambda b, h: (b, h)),
        scratch_shapes=[pltpu.VMEM((nh, nblk, D), F32), pltpu.VMEM((nh, ntile, nblk, T), F32)]
        + _flash_scratch(nh, ntile, T),
        compiler_params=_cparams(("parallel", "parallel")),
        name="moba_attn",
    )(qb, kb, vbt, gate)


def _mem_prep_kernel(mem_ref, g_ref, w_ref, kv_ref):
    mn = _rms(mem_ref[...], g_ref[...]).astype(BF16)
    kv_ref[...] = jnp.dot(mn, w_ref[...], preferred_element_type=F32).astype(BF16)


def _mem_prep(mem2d, g, w):
    g2 = g.reshape(1, -1)
    return pl.pallas_call(
        _mem_prep_kernel,
        out_shape=jax.ShapeDtypeStruct((BATCH * N_MEM, 2 * MEMQ_W), BF16),
        grid=(BATCH,),
        in_specs=[pl.BlockSpec((N_MEM, D_MODEL), lambda b: (b, 0)),
                  pl.BlockSpec(g2.shape, lambda b: (0, 0)),
                  pl.BlockSpec(w.shape, lambda b: (0, 0))],
        out_specs=pl.BlockSpec((N_MEM, 2 * MEMQ_W), lambda b: (b, 0)),
        compiler_params=_cparams(("parallel",)),
        name="mem_prep",
    )(mem2d, g2, w)


def _mem_attn_kernel(q_ref, kv_ref, gate_ref, y_ref):
    cols = [slice(h * MEM_DH, (h + 1) * MEM_DH) for h in range(MEM_HEADS)]
    ss = [_qk(q_ref[:, c], kv_ref[:, c]) for c in cols]
    ps = [jnp.exp2(s - jnp.max(s, axis=1, keepdims=True)) for s in ss]
    for h, (c, p) in enumerate(zip(cols, ps)):
        l = jnp.sum(p, axis=1, keepdims=True)
        a = MEMQ_W + h * MEM_DH
        o = jnp.dot(p.astype(BF16), kv_ref[:, a:a + MEM_DH], preferred_element_type=F32) / l
        y_ref[:, c] = (o * _silu(gate_ref[:, c].astype(F32))).astype(y_ref.dtype)


def _mem_attn(qm, kvm, gate, tq):
    nq = SEQ // tq
    return pl.pallas_call(
        _mem_attn_kernel,
        out_shape=jax.ShapeDtypeStruct((N_TOK, MEMQ_W), BF16),
        grid=(BATCH, nq),
        in_specs=[pl.BlockSpec((tq, MEMQ_W), lambda b, i: (b * nq + i, 0)),
                  pl.BlockSpec((N_MEM, 2 * MEMQ_W), lambda b, i: (b, 0)),
                  pl.BlockSpec((tq, MEMQ_W), lambda b, i: (b * nq + i, 0))],
        out_specs=pl.BlockSpec((tq, MEMQ_W), lambda b, i: (b * nq + i, 0)),
        compiler_params=_cparams(("parallel", "parallel")),
        name="mem_attn",
    )(qm, kvm, gate)


def _band_bias(n, lo_off, hi_off):
    i = lax.broadcasted_iota(jnp.int32, (Q_BLOCK, 2 * Q_BLOCK), 0)
    c = lax.broadcasted_iota(jnp.int32, (Q_BLOCK, 2 * Q_BLOCK), 1)
    vis = (c - i >= lo_off) & (c - i <= hi_off) & ((c >= Q_BLOCK) | (n > 0))
    return jnp.where(vis, 0.0, NEG).astype(F32)


def _swa_kernel(q_ref, kvp_ref, kvo_ref, gate_ref, sink_ref, y_ref):
    QB = Q_BLOCK
    G = SWA_HEADS // SWA_KV_HEADS
    P = G // 2
    n = pl.program_id(1)
    c = lax.broadcasted_iota(jnp.int32, (2 * QB, QB), 0)
    i = lax.broadcasted_iota(jnp.int32, (2 * QB, QB), 1)
    vis = (c - i >= QB - (SWA_WINDOW - 1)) & (c - i <= QB) & ((c >= QB) | (n > 0))
    bias = jnp.where(vis, 0.0, NEG).astype(F32)
    bias4 = jnp.concatenate([bias] * P, axis=1)
    k2 = jnp.concatenate([kvp_ref[:, :LANES], kvo_ref[:, :LANES]], axis=0).astype(F32)
    v2 = jnp.concatenate([kvp_ref[:, LANES:], kvo_ref[:, LANES:]], axis=0).astype(F32)
    k2r = pltpu.roll(k2, SWA_DH, 1)
    v2r = pltpu.roll(v2, SWA_DH, 1)
    lane = lax.broadcasted_iota(jnp.int32, (2 * QB, LANES), 1)
    lo = lane < SWA_DH
    top = lax.broadcasted_iota(jnp.int32, (LANES, P * QB), 0) < SWA_DH
    vts, ss = [], []
    for kv in range(SWA_KV_HEADS):
        ka, kb_ = (k2, k2r) if kv == 0 else (k2r, k2)
        va, vb_ = (v2, v2r) if kv == 0 else (v2r, v2)
        k_lo = jnp.where(lo, ka, 0.0).astype(BF16)
        k_hi = jnp.where(lo, 0.0, kb_).astype(BF16)
        vts.append((jnp.where(lo, va, 0.0).T.astype(BF16), jnp.where(lo, 0.0, vb_).T.astype(BF16)))
        base = kv * P
        q4 = jnp.concatenate([q_ref[:, (base + p) * LANES:(base + p + 1) * LANES] for p in range(P)],
                             axis=0)
        ss.append((_qk(k_lo, q4) + bias4, _qk(k_hi, q4) + bias4))
    stats = []
    for pair in ss:
        st = []
        for s in pair:
            m = jnp.max(s, axis=0, keepdims=True)
            p = jnp.exp2(s - m)
            st.append((m, p, jnp.sum(p, axis=0, keepdims=True)))
        stats.append(st)
    for kv in range(SWA_KV_HEADS):
        (m_lo, p_lo, l_lo), (m_hi, p_hi, l_hi) = stats[kv]
        vt_lo, vt_hi = vts[kv]
        base = kv * P
        ot = (jnp.dot(vt_lo, p_lo.astype(BF16), preferred_element_type=F32)
              + jnp.dot(vt_hi, p_hi.astype(BF16), preferred_element_type=F32))
        cols = slice(base * LANES, (base + P) * LANES)
        f_lo = 1.0 / ((1.0 + jnp.exp2(sink_ref[0:1, cols] * LOG2E - (m_lo + jnp.log2(l_lo)))) * l_lo)
        f_hi = 1.0 / ((1.0 + jnp.exp2(sink_ref[1:2, cols] * LOG2E - (m_hi + jnp.log2(l_hi)))) * l_hi)
        ot = ot * jnp.where(top, f_lo, f_hi)
        for p in range(P):
            cc = slice((base + p) * LANES, (base + p + 1) * LANES)
            o = ot[:, p * QB:(p + 1) * QB].T
            y_ref[:, cc] = (o * _silu(gate_ref[:, cc].astype(F32))).astype(y_ref.dtype)


def _swa_attn(qc, kvc, gate, sink_rows):
    nb = SEQ // Q_BLOCK
    w = SWA_HEADS * SWA_DH
    return pl.pallas_call(
        _swa_kernel,
        out_shape=jax.ShapeDtypeStruct((N_TOK, w), BF16),
        grid=(BATCH, nb),
        in_specs=[pl.BlockSpec((Q_BLOCK, w), lambda b, n: (b * nb + n, 0)),
                  pl.BlockSpec((Q_BLOCK, 2 * LANES), lambda b, n: (jnp.maximum(b * nb + n - 1, 0), 0)),
                  pl.BlockSpec((Q_BLOCK, 2 * LANES), lambda b, n: (b * nb + n, 0)),
                  pl.BlockSpec((Q_BLOCK, w), lambda b, n: (b * nb + n, 0)),
                  pl.BlockSpec((2, w), lambda b, n: (0, 0))],
        out_specs=pl.BlockSpec((Q_BLOCK, w), lambda b, n: (b * nb + n, 0)),
        compiler_params=_cparams(("parallel", "parallel")),
        name="swa_attn",
    )(qc, kvc, kvc, gate, sink_rows)


DIL_BLOCKS_PER_STEP = 2
MAX_STRIDE = 4


def _dil_kernel(q_ref, kp_ref, ko_ref, vp_ref, vo_ref, o_ref, lse_ref):
    QB = Q_BLOCK
    R = DIL_BLOCKS_PER_STEP
    n = pl.program_id(2)
    lane = lax.broadcasted_iota(jnp.int32, (QB, LANES), 1)
    tasks = []
    for u in range(R):
        rows = slice(u * QB, (u + 1) * QB)
        bias = _band_bias(n if u == 0 else 1, 0, QB)
        for h in range(DIL_HEADS):
            c = slice(h * DIL_DH, (h + 1) * DIL_DH)
            k_prev = kp_ref[:, c] if u == 0 else ko_ref[(u - 1) * QB:u * QB, c]
            v_prev = vp_ref[:, c] if u == 0 else vo_ref[(u - 1) * QB:u * QB, c]
            k = jnp.concatenate([k_prev, ko_ref[rows, c]], axis=0)
            v = jnp.concatenate([v_prev, vo_ref[rows, c]], axis=0)
            tasks.append((u, h, _qk(q_ref[rows, c], k) + bias, v))
    soft = []
    for u, h, s, v in tasks:
        m = jnp.max(s, axis=1, keepdims=True)
        p = jnp.exp2(s - m)
        soft.append((m, p, jnp.sum(p, axis=1, keepdims=True)))
    lse_all = [jnp.zeros((QB, LANES), F32) for _ in range(R)]
    for (u, h, _, v), (m, p, l) in zip(tasks, soft):
        o_ref[u * QB:(u + 1) * QB, h * DIL_DH:(h + 1) * DIL_DH] = (
            jnp.dot(p.astype(BF16), v, preferred_element_type=F32) / l).astype(o_ref.dtype)
        lse_all[u] = jnp.where(lane == h, m + jnp.log2(l), lse_all[u])
    for u in range(R):
        lse_ref[u * QB:(u + 1) * QB, :] = lse_all[u]


def _dil_attn(qv, kv, vv, dil):
    w = DIL_HEADS * DIL_DH
    R = DIL_BLOCKS_PER_STEP
    L = SEQ // dil
    ns = L // (R * Q_BLOCK)
    cur = lambda b, r, n: (b * ns + n, r)
    prev = lambda b, r, n: (jnp.maximum((b * ns + n) * R - 1, 0), r)
    blk = (R * Q_BLOCK, w)
    pblk = (Q_BLOCK, w)
    return pl.pallas_call(
        _dil_kernel,
        out_shape=[jax.ShapeDtypeStruct((BATCH * L, dil * w), BF16),
                   jax.ShapeDtypeStruct((BATCH * L, dil * LANES), F32)],
        grid=(BATCH, dil, ns),
        in_specs=[pl.BlockSpec(blk, cur), pl.BlockSpec(pblk, prev), pl.BlockSpec(blk, cur),
                  pl.BlockSpec(pblk, prev), pl.BlockSpec(blk, cur)],
        out_specs=[pl.BlockSpec(blk, cur), pl.BlockSpec((R * Q_BLOCK, LANES), cur)],
        compiler_params=_cparams(("parallel", "parallel", "parallel")),
        name=f"dil_attn_d{dil}",
    )(qv, kv, kv, vv, vv)


def _dil_combine_kernel(tm, *refs):
    np_ = len(DIL_PATTERNS)
    o_refs, l_refs = refs[:np_], refs[np_:2 * np_]
    gate_ref, y_ref = refs[2 * np_], refs[2 * np_ + 1]
    stage = refs[2 * np_ + 2:]
    w = DIL_HEADS * DIL_DH
    os_, ls = [], []
    k = 0
    for (_, d), o_ref, l_ref in zip(DIL_PATTERNS, o_refs, l_refs):
        if d == 1:
            os_.append(lambda h, o_ref=o_ref: o_ref[:, h * DIL_DH:(h + 1) * DIL_DH].astype(F32))
            ls.append(l_ref[...])
            continue
        so, sl = stage[2 * k], stage[2 * k + 1]
        k += 1
        if d <= MAX_STRIDE:
            hops = [(so, sl, d, lambda r: r)]
        else:
            assert d == MAX_STRIDE * MAX_STRIDE
            co, cl = stage[-2], stage[-1]
            q4 = tm // MAX_STRIDE
            hops = [(co, cl, MAX_STRIDE, lambda r: (r % MAX_STRIDE) * q4 + r // MAX_STRIDE)]
        dst_o, dst_l, stride, start = hops[0]
        for r in range(d):
            rows = pl.ds(start(r), tm // d, stride=stride)
            for h in range(DIL_HEADS):
                a = r * w + h * DIL_DH
                dst_o[h, rows, :] = o_ref[:, a:a + DIL_DH].astype(F32)
            dst_l[rows, :] = l_ref[:, r * LANES:(r + 1) * LANES]
        if d > MAX_STRIDE:
            for r in range(MAX_STRIDE):
                rows = pl.ds(r, q4, stride=MAX_STRIDE)
                blk = slice(r * q4, (r + 1) * q4)
                for h in range(DIL_HEADS):
                    so[h, rows, :] = co[h, blk, :]
                sl[rows, :] = cl[blk, :]
        os_.append(lambda h, so=so: so[h])
        ls.append(sl[...])
    mx = functools.reduce(jnp.maximum, ls)
    es = [jnp.exp2(x - mx) for x in ls]
    den = functools.reduce(lambda a, b: a + b, es)
    ws = [e / den for e in es]
    lane = lax.broadcasted_iota(jnp.int32, ws[0].shape, 1)
    for h in range(DIL_HEADS):
        c = slice(h * DIL_DH, (h + 1) * DIL_DH)
        wh = [jnp.sum(jnp.where(lane == h, wgt, 0.0), axis=1, keepdims=True) for wgt in ws]
        o = functools.reduce(lambda a, b: a + b, [wh[p] * os_[p](h) for p in range(np_)])
        y_ref[:, c] = (o * _silu(gate_ref[:, c].astype(F32))).astype(y_ref.dtype)


def _dil_combine(outs, lses, gate, tm):
    w = DIL_HEADS * DIL_DH
    in_specs = [pl.BlockSpec((tm // d, d * w), lambda i: (i, 0)) for _, d in DIL_PATTERNS]
    in_specs += [pl.BlockSpec((tm // d, d * LANES), lambda i: (i, 0)) for _, d in DIL_PATTERNS]
    in_specs.append(pl.BlockSpec((tm, w), lambda i: (i, 0)))
    scratch = []
    for _, d in DIL_PATTERNS:
        if d > 1:
            scratch += [pltpu.VMEM((DIL_HEADS, tm, DIL_DH), F32), pltpu.VMEM((tm, LANES), F32)]
    if any(d > MAX_STRIDE for _, d in DIL_PATTERNS):
        scratch += [pltpu.VMEM((DIL_HEADS, tm, DIL_DH), F32), pltpu.VMEM((tm, LANES), F32)]
    return pl.pallas_call(
        functools.partial(_dil_combine_kernel, tm),
        out_shape=jax.ShapeDtypeStruct((N_TOK, w), BF16),
        grid=(N_TOK // tm,),
        in_specs=in_specs,
        out_specs=pl.BlockSpec((tm, w), lambda i: (i, 0)),
        scratch_shapes=scratch,
        compiler_params=_cparams(("parallel",)),
        name="dil_combine",
    )(*outs, *lses, gate)


def _outproj_kernel(nparts, final, *refs):
    ys, ws = refs[:nparts], refs[nparts:2 * nparts]
    x_ref = refs[2 * nparts]
    o_ref = refs[-1]
    acc = x_ref[...]
    for y, w in zip(ys, ws):
        acc = acc + jnp.dot(y[...], w[...], preferred_element_type=F32)
    if final:
        acc = _rms(acc, refs[2 * nparts + 1][...])
    o_ref[...] = acc


def _outproj(ys, ws, x, final_g, tm):
    n_tok, d = x.shape
    in_specs = [pl.BlockSpec((tm, y.shape[1]), lambda i: (i, 0)) for y in ys]
    in_specs += [pl.BlockSpec(w.shape, lambda i: (0, 0)) for w in ws]
    in_specs.append(pl.BlockSpec((tm, d), lambda i: (i, 0)))
    args = [*ys, *ws, x]
    if final_g is not None:
        in_specs.append(pl.BlockSpec((1, d), lambda i: (0, 0)))
        args.append(final_g.reshape(1, d))
    return pl.pallas_call(
        functools.partial(_outproj_kernel, len(ys), final_g is not None),
        out_shape=jax.ShapeDtypeStruct((n_tok, d), F32),
        grid=(n_tok // tm,),
        in_specs=in_specs,
        out_specs=pl.BlockSpec((tm, d), lambda i: (i, 0)),
        compiler_params=_cparams(("parallel",)),
        name="outproj",
    )(*args)


PROJ_TM = 512
ATTN_TQ = 512
MLA_HEADS_PER_STEP = 2
MOBA_HEADS_PER_STEP = 2


def _even_layer(x, mem2d, norm_g, w_in, q_norm_g, w_uq, kv_norm_g, w_ukv, mem_norm_g, w_mem_kv, w_out):
    n_lat = MLA_Q_RANK + MLA_KV_RANK + MLA_ROPE
    wlat = jnp.pad(w_in[:, :n_lat], ((0, 0), (0, LANES - MLA_ROPE))).astype(BF16)
    n_bqkv = n_lat + 3 * MOBA_HEADS * MOBA_DH
    w_bqkv = w_in[:, n_lat:n_bqkv].astype(BF16)
    w_rest = w_in[:, n_bqkv:].astype(BF16)
    wuq = w_uq.reshape(MLA_Q_RANK, MLA_HEADS, MLA_NOPE + MLA_ROPE)
    wuq = jnp.pad(wuq, ((0, 0), (0, 0), (0, MLA_HW - MLA_NOPE - MLA_ROPE)))
    wuq = wuq.reshape(MLA_Q_RANK, MLA_HEADS * MLA_HW).astype(BF16)
    wukv = w_ukv.reshape(MLA_KV_RANK, MLA_HEADS, MLA_NOPE + MLA_V)
    wukv = jnp.concatenate([wukv[:, :, :MLA_NOPE].reshape(MLA_KV_RANK, -1),
                            wukv[:, :, MLA_NOPE:].reshape(MLA_KV_RANK, -1)], axis=1).astype(BF16)

    cos_a, sin_a = _rope_tables(MLA_ROPE, LANES)
    qa, ka, va = _mla_prep(x, norm_g, wlat, q_norm_g, wuq, kv_norm_g, wukv, cos_a, sin_a, ATTN_TQ)

    rot = MOBA_DH // ROT_FRAC
    cos_b, sin_b = _rope_tables(rot, MOBA_DH)
    sc = MOBA_DH ** -0.5 * LOG2E
    wb = MOBA_HEADS * MOBA_DH
    plain = [("plain",)]
    groups_qkv = [
        (wb, [("rope", 0, rot // 2, MOBA_DH, sc)] * MOBA_HEADS, "rows"),
        (wb, [("rope", 0, rot // 2, MOBA_DH, 1.0)] * MOBA_HEADS, "rows"),
        (wb, plain * MOBA_HEADS, ("vT", MOBA_TILE)),
    ]
    groups_rest = [
        (MEMQ_W, [("scale", MEM_DH ** -0.5 * LOG2E)] * MEM_HEADS, "rows"),
        (MLA_HEADS * MLA_V, plain * MLA_HEADS, "rows"),
        (wb, plain * MOBA_HEADS, "rows"),
        (MEMQ_W, plain * MEM_HEADS, "rows"),
    ]
    qb, kb, vb = _inproj(x, norm_g, w_bqkv, groups_qkv, [cos_b, sin_b], PROJ_TM)
    qm, gate_a, gate_b, gate_m = _inproj(x, norm_g, w_rest, groups_rest, [], PROJ_TM)

    y_a = _mla_attn(qa, ka, va, gate_a, ATTN_TQ, MLA_HEADS_PER_STEP)
    y_b = _moba_flat_attn(qb, kb, vb, gate_b, MOBA_HEADS_PER_STEP)
    kvm = _mem_prep(mem2d, mem_norm_g, w_mem_kv.astype(BF16))
    y_m = _mem_attn(qm, kvm, gate_m, ATTN_TQ)

    a, b = MLA_HEADS * MLA_V, MLA_HEADS * MLA_V + MOBA_HEADS * MOBA_DH
    return [y_a, y_b, y_m], [w_out[:a].astype(BF16), w_out[a:b].astype(BF16), w_out[b:].astype(BF16)]


def _odd_layer(x, mem2d, norm_g, w_in, sinks, mem_norm_g, w_mem_kv, w_out):
    rot_c = SWA_DH // ROT_FRAC
    rot_d = DIL_DH // ROT_FRAC
    cos_c, sin_c = _rope_tables(rot_c, SWA_DH)
    cos_d, sin_d = _rope_tables(rot_d, DIL_DH)
    wq = SWA_HEADS * SWA_DH
    wd = DIL_HEADS * DIL_DH
    wkv = 2 * SWA_KV_HEADS * SWA_DH
    rc = lambda s: ("rope", 0, rot_c // 2, SWA_DH, s)
    rd = lambda s: ("rope", 1, rot_d // 2, DIL_DH, s)
    plain = [("plain",)]
    groups_qkv = [
        (wq, [rc(SWA_DH ** -0.5 * LOG2E)] * (wq // LANES), "rows"),
        (wkv, [rc(1.0), ("plain",)], "rows"),
        (wd, [rd(DIL_DH ** -0.5 * LOG2E)] * DIL_HEADS, "dilated"),
        (wd, [rd(1.0)] * DIL_HEADS, "dilated"),
        (wd, plain * DIL_HEADS, "dilated"),
    ]
    groups_rest = [
        (MEMQ_W, [("scale", MEM_DH ** -0.5 * LOG2E)] * MEM_HEADS, "rows"),
        (wq, plain * (wq // LANES), "rows"),
        (wd, plain * DIL_HEADS, "rows"),
        (MEMQ_W, plain * MEM_HEADS, "rows"),
    ]
    n_qkv = wq + wkv + 3 * wd
    res = _inproj(x, norm_g, w_in[:, :n_qkv].astype(BF16), groups_qkv, [cos_c, sin_c, cos_d, sin_d], PROJ_TM)
    qc, kvc = res[0], res[1]
    nv = 1 + len(DIL_VIEWS)
    qd, kd, vd = (res[2 + i * nv:2 + (i + 1) * nv] for i in range(3))
    qm, gate_c, gate_d, gate_m = _inproj(x, norm_g, w_in[:, n_qkv:].astype(BF16), groups_rest, [], PROJ_TM)

    sink_rows = jnp.repeat(sinks.astype(F32).reshape(SWA_HEADS // 2, 2).T, LANES, axis=1)
    y_c = _swa_attn(qc, kvc, gate_c, sink_rows)
    view_of = {1: 0, **{d: 1 + i for i, d in enumerate(DIL_VIEWS)}}
    outs, lses = zip(*[_dil_attn(qd[view_of[dil]], kd[view_of[dil]], vd[view_of[dil]], dil)
                       for _, dil in DIL_PATTERNS])
    y_d = _dil_combine(outs, lses, gate_d, PROJ_TM)
    kvm = _mem_prep(mem2d, mem_norm_g, w_mem_kv.astype(BF16))
    y_m = _mem_attn(qm, kvm, gate_m, ATTN_TQ)

    return [y_c, y_d, y_m], [w_out[:wq].astype(BF16), w_out[wq:wq + wd].astype(BF16),
                             w_out[wq + wd:].astype(BF16)]


def kernel(x, mem, ev_norm_g, ev_w_in, ev_q_norm_g, ev_w_uq, ev_kv_norm_g, ev_w_ukv, ev_mem_norm_g,
           ev_w_mem_kv, ev_w_out, od_norm_g, od_w_in, od_sinks, od_mem_norm_g, od_w_mem_kv, od_w_out,
           final_norm_g):
    x2 = x.reshape(N_TOK, D_MODEL)
    mem2d = mem.reshape(BATCH * N_MEM, D_MODEL)
    ys, ws = _even_layer(x2, mem2d, ev_norm_g[0], ev_w_in[0], ev_q_norm_g[0], ev_w_uq[0], ev_kv_norm_g[0],
                         ev_w_ukv[0], ev_mem_norm_g[0], ev_w_mem_kv[0], ev_w_out[0])
    x2 = _outproj(ys, ws, x2, None, PROJ_TM)
    ys, ws = _odd_layer(x2, mem2d, od_norm_g[0], od_w_in[0], od_sinks[0], od_mem_norm_g[0],
                        od_w_mem_kv[0], od_w_out[0])
    x2 = _outproj(ys, ws, x2, final_norm_g, PROJ_TM)
    return x2.reshape(BATCH, SEQ, D_MODEL)
```

```python
import functools

import numpy as np
import jax
import jax.numpy as jnp
from jax import lax
from jax.experimental import pallas as pl
from jax.experimental.pallas import tpu as pltpu

D_MODEL = 2048
BATCH = 4
SEQ = 4096
N_TOK = BATCH * SEQ
N_MEM = 256
ROPE_THETA = 500000.0
ROT_FRAC = 4
EPS = 1e-6

MLA_HEADS = 8
MLA_Q_RANK = 512
MLA_KV_RANK = 256
MLA_NOPE = 128
MLA_ROPE = 64
MLA_V = 128
MOBA_HEADS = 8
MOBA_DH = 128
MOBA_BLOCK = 256
MOBA_TOPK = 3
SWA_HEADS = 16
SWA_KV_HEADS = 2
SWA_DH = 64
SWA_WINDOW = 128
DIL_HEADS = 6
DIL_DH = 128
DIL_PATTERNS = ((128, 1), (512, 4), (2048, 16))
MEM_HEADS = 4
MEM_DH = 128
MEMQ_W = MEM_HEADS * MEM_DH
Q_BLOCK = 128

LANES = 128
VMEM_LIMIT = 52 * 1024 * 1024

NEG = -1e30
BF16 = jnp.bfloat16
F32 = jnp.float32


def _cparams(sem):
    return pltpu.CompilerParams(dimension_semantics=sem, vmem_limit_bytes=VMEM_LIMIT)


def _rms(x, g):
    ms = jnp.mean(x * x, axis=-1, keepdims=True)
    return (x * lax.rsqrt(ms + EPS)) * g


def _silu(g):
    return g / (1.0 + jnp.exp(-g))


def _sigmoid(z):
    return 1.0 / (1.0 + jnp.exp(-z))


def _rope_slab(x, cos_t, sin_t, half, period):
    lane = lax.broadcasted_iota(jnp.int32, x.shape, 1)
    up = pltpu.roll(x, LANES - half, 1)
    dn = pltpu.roll(x, half, 1)
    sw = jnp.where((lane & (period - 1)) < half, up, dn)
    return x * cos_t + sw * sin_t


def _qk(q, k):
    return lax.dot_general(q, k, (((1,), (1,)), ((), ())), preferred_element_type=F32)


def _rope_tables(rot_dim, period):
    half = rot_dim // 2
    inv = 1.0 / (ROPE_THETA ** (jnp.arange(0, rot_dim, 2, dtype=F32) / rot_dim))
    ang = jnp.arange(SEQ, dtype=F32)[:, None] * inv[None, :]
    c, s = jnp.cos(ang), jnp.sin(ang)
    d = np.arange(LANES) % period
    idx = d % half
    cos_t = jnp.where(d < rot_dim, c[:, idx], 1.0)
    sin_t = jnp.where(d < half, -s[:, idx], jnp.where(d < rot_dim, s[:, idx], 0.0))
    return cos_t.astype(F32), sin_t.astype(F32)


PROJ_CHUNK = 512
DIL_VIEWS = tuple(d for _, d in DIL_PATTERNS if d > 1)


def _inproj_kernel(groups, ntab, tm, x_ref, g_ref, *rest):
    tabs = rest[:2 * ntab]
    w_ref = rest[2 * ntab]
    ndil = sum(layout == "dilated" for _, _, layout in groups)
    nscr = ndil + (1 if ndil else 0)
    refs = list(rest[2 * ntab + 1:len(rest) - nscr])
    stage = list(rest[len(rest) - nscr:])
    chain_ref = stage.pop() if ndil else None
    xn = _rms(x_ref[...], g_ref[...]).astype(BF16)
    c0 = 0
    for ncols, kinds, layout in groups:
        o_ref = refs.pop(0)
        views = [refs.pop(0) for _ in DIL_VIEWS] if layout == "dilated" else []
        st_ref = stage.pop(0) if layout == "dilated" else None
        for s0 in range(0, ncols, PROJ_CHUNK):
            n = min(PROJ_CHUNK, ncols - s0)
            acc = jnp.dot(xn, w_ref[:, c0 + s0:c0 + s0 + n], preferred_element_type=F32)
            for c in range(n // LANES):
                col = s0 + c * LANES
                kind = kinds[col // LANES]
                piece = acc[:, c * LANES:(c + 1) * LANES]
                if kind[0] == "rope":
                    _, ti, half, period, sc = kind
                    piece = _rope_slab(piece, tabs[2 * ti][...], tabs[2 * ti + 1][...], half, period)
                    if sc != 1.0:
                        piece = piece * sc
                elif kind[0] == "scale":
                    piece = piece * kind[1]
                if layout == "rows" or layout == "dilated":
                    o_ref[:, col:col + LANES] = piece.astype(o_ref.dtype)
                    if layout == "dilated":
                        st_ref[col // LANES] = piece
                else:
                    tb = layout[1]
                    for u in range(tm // tb):
                        o_ref[col // LANES, u] = _vt_block(piece[u * tb:(u + 1) * tb, :])
        src, src_d = st_ref, 1
        for vi, (d, v_ref) in enumerate(zip(DIL_VIEWS, views)):
            keep = chain_ref if vi + 1 < len(views) else None
            rows_src, rows_dst = tm // src_d, tm // d
            for r in range(d):
                start = (r % src_d) * rows_src + r // src_d
                for c in range(ncols // LANES):
                    blk = src[c, pl.ds(start, rows_dst, stride=d // src_d), :]
                    a = r * ncols + c * LANES
                    v_ref[:, a:a + LANES] = blk.astype(v_ref.dtype)
                    if keep is not None:
                        keep[c, r * rows_dst:(r + 1) * rows_dst, :] = blk
            src, src_d = keep, d
        c0 += ncols


def _inproj(x, g, w, groups, tables, tm):
    n_tok, d = x.shape
    assert sum(nc for nc, _, _ in groups) == w.shape[1]
    seq_tiles = SEQ // tm
    in_specs = [pl.BlockSpec((tm, d), lambda i: (i, 0)),
                pl.BlockSpec((1, d), lambda i: (0, 0))]
    args = [x, g.reshape(1, d)]
    for t in tables:
        in_specs.append(pl.BlockSpec((tm, LANES), lambda i: (i % seq_tiles, 0)))
        args.append(t)
    in_specs.append(pl.BlockSpec(w.shape, lambda i: (0, 0), pipeline_mode=pl.Buffered(1)))
    args.append(w)
    out_specs, out_shapes = [], []
    for nc, _, layout in groups:
        if layout == "rows" or layout == "dilated":
            out_specs.append(pl.BlockSpec((tm, nc), lambda i: (i, 0)))
            out_shapes.append(jax.ShapeDtypeStruct((n_tok, nc), BF16))
            if layout == "dilated":
                for dil in DIL_VIEWS:
                    out_specs.append(pl.BlockSpec((tm // dil, dil * nc), lambda i: (i, 0)))
                    out_shapes.append(jax.ShapeDtypeStruct((n_tok // dil, dil * nc), BF16))
        else:
            tb = layout[1]
            out_specs.append(pl.BlockSpec((nc // LANES, tm // tb, VT_ROWS, tb), lambda i: (0, i, 0, 0)))
            out_shapes.append(jax.ShapeDtypeStruct((nc // LANES, n_tok // tb, VT_ROWS, tb), BF16))
    return pl.pallas_call(
        functools.partial(_inproj_kernel, groups, len(tables) // 2, tm),
        out_shape=out_shapes,
        grid=(n_tok // tm,),
        in_specs=in_specs,
        out_specs=out_specs,
        scratch_shapes=(lambda dil: dil + dil[:1])(
            [pltpu.VMEM((nc // LANES, tm, LANES), F32) for nc, _, layout in groups if layout == "dilated"]),
        compiler_params=_cparams(("parallel",)),
        name="inproj",
    )(*args)


MLA_HW = 256


def _mla_prep_kernel(scale, x_ref, g_ref, wlat_ref, qg_ref, wuq_ref, kvg_ref, wukv_ref,
                     cos_ref, sin_ref, qa_ref, ka_ref, va_ref):
    tm = x_ref.shape[0]
    halves = [slice(u * (tm // 2), (u + 1) * (tm // 2)) for u in range(2)]
    xns = [_rms(x_ref[r, :], g_ref[...]).astype(BF16) for r in halves]
    lats = [jnp.dot(xn, wlat_ref[...], preferred_element_type=F32) for xn in xns]
    cqs = [_rms(lat[:, :MLA_Q_RANK], qg_ref[...]).astype(BF16) for lat in lats]
    ckvs = [_rms(lat[:, MLA_Q_RANK:MLA_Q_RANK + MLA_KV_RANK], kvg_ref[...]).astype(BF16) for lat in lats]
    qs = [jnp.dot(cq, wuq_ref[...], preferred_element_type=F32) for cq in cqs]
    kvs = [jnp.dot(ckv, wukv_ref[...], preferred_element_type=F32) for ckv in ckvs]
    for r, lat, q, kv in zip(halves, lats, qs, kvs):
        cos_t, sin_t = cos_ref[r, :], sin_ref[r, :]
        kpe = lat[:, MLA_Q_RANK + MLA_KV_RANK:]
        kpe = _rope_slab(kpe, cos_t, sin_t, MLA_ROPE // 2, LANES).astype(BF16)
        for h in range(MLA_HEADS):
            a = h * MLA_HW
            qa_ref[r, a:a + LANES] = (q[:, a:a + LANES] * scale).astype(BF16)
            qr = _rope_slab(q[:, a + LANES:a + 2 * LANES], cos_t, sin_t, MLA_ROPE // 2, LANES)
            qa_ref[r, a + LANES:a + 2 * LANES] = (qr * scale).astype(BF16)
            ka_ref[r, a:a + LANES] = kv[:, h * LANES:(h + 1) * LANES].astype(BF16)
            ka_ref[r, a + LANES:a + 2 * LANES] = kpe
            v0 = MLA_HEADS * MLA_NOPE + h * MLA_V
            va_ref[h, 0, :, r] = _vt_block(kv[:, v0:v0 + MLA_V])


def _mla_prep(x, g, wlat, qg, wuq, kvg, wukv, cos_t, sin_t, tm):
    n_tok, d = x.shape
    seq_tiles = SEQ // tm
    full = lambda a: pl.BlockSpec(a.shape, lambda i: (0, 0))
    qg2, kvg2, g2 = qg.reshape(1, -1), kvg.reshape(1, -1), g.reshape(1, d)
    scale = (MLA_NOPE + MLA_ROPE) ** -0.5 * LOG2E
    return pl.pallas_call(
        functools.partial(_mla_prep_kernel, scale),
        out_shape=[jax.ShapeDtypeStruct((n_tok, MLA_HEADS * MLA_HW), BF16),
                   jax.ShapeDtypeStruct((n_tok, MLA_HEADS * MLA_HW), BF16),
                   jax.ShapeDtypeStruct((MLA_HEADS, n_tok // tm, VT_ROWS, tm), BF16)],
        grid=(n_tok // tm,),
        in_specs=[pl.BlockSpec((tm, d), lambda i: (i, 0)), full(g2), full(wlat), full(qg2), full(wuq),
                  full(kvg2), full(wukv),
                  pl.BlockSpec((tm, LANES), lambda i: (i % seq_tiles, 0)),
                  pl.BlockSpec((tm, LANES), lambda i: (i % seq_tiles, 0))],
        out_specs=[pl.BlockSpec((tm, MLA_HEADS * MLA_HW), lambda i: (i, 0)),
                   pl.BlockSpec((tm, MLA_HEADS * MLA_HW), lambda i: (i, 0)),
                   pl.BlockSpec((MLA_HEADS, 1, VT_ROWS, tm), lambda i: (0, i, 0, 0))],
        compiler_params=_cparams(("parallel",)),
        name="mla_prep",
    )(x, g2, wlat, qg2, wuq, kvg2, wukv, cos_t, sin_t)


LOG2E = 1.4426950408889634
VT_TAIL = 16
VT_ROWS = LANES + VT_TAIL
FLASH_CHUNK = 128
FLASH_UNROLL = 2


def _vt_block(v):
    tb = v.shape[0]
    row = lax.broadcasted_iota(jnp.int32, (VT_TAIL, tb), 0)
    tail = jnp.where(row == 0, 1.0, 0.0).astype(F32)
    return jnp.concatenate([v.T, tail], axis=0).astype(BF16)


def _flash_finish_all(nh, n, tq, dv, acc_ref, gate_ref, y_ref):
    def body(c, _):
        rows = pl.ds(pl.multiple_of(c * tq, tq), tq)
        for g in range(nh):
            cols = slice(g * dv, (g + 1) * dv)
            a = acc_ref.at[g, c]
            o = (a[:LANES, :] / a[LANES:LANES + 1, :]).T
            y_ref[rows, cols] = (o * _silu(gate_ref[rows, cols].astype(F32))).astype(y_ref.dtype)
        return 0

    lax.fori_loop(0, n, body, 0)


def _flash_flat_t(nh, n, tq, score_fn, row_bias, vt_fn, s_ref, p_ref, acc_ref, m_ref):
    n_off = n * (n - 1) // 2
    assert n % 2 == 0 and n_off % 2 == 0
    acc_ref[...] = jnp.zeros_like(acc_ref)
    p_ref[...] = jnp.zeros_like(p_ref)
    m_ref[...] = jnp.full(m_ref.shape, NEG, F32)
    key = lax.broadcasted_iota(jnp.int32, (tq, tq), 0)
    qry = lax.broadcasted_iota(jnp.int32, (tq, tq), 1)

    def diag_scores(c):
        return lambda g: jnp.where(key <= qry, score_fn(g, c, c), NEG)

    def off_scores(j, c):
        return lambda g: score_fn(g, j, jnp.minimum(c, n - 1))

    def step(rd, wr, nxt, cur, prev, alphas):
        (jc, cc), (jp, cp) = cur, prev
        for g in range(nh):
            s_ref[wr, g] = nxt(g)
        pend = [jnp.dot(vt_fn(g, jp), p_ref[wr, g], preferred_element_type=F32) for g in range(nh)]
        out = []
        for g in range(nh):
            s_view = s_ref.at[rd, g]
            m_prev = m_ref[g, cc]
            nbs = [None] if row_bias is None else row_bias(g, jc, cc)
            hk = tq // len(nbs)
            m_new = m_prev
            for u, nb in enumerate(nbs):
                mb = functools.reduce(jnp.maximum, [
                    jnp.max(s_view[r0:r0 + FLASH_CHUNK, :], axis=0, keepdims=True)
                    for r0 in range(u * hk, (u + 1) * hk, FLASH_CHUNK)])
                m_new = jnp.maximum(m_new, mb if nb is None else mb + nb)
            m_ref[g, cc] = m_new
            acc_ref[g, cp] = alphas[g] * acc_ref[g, cp] + pend[g]
            for u, nb in enumerate(nbs):
                shift = m_new if nb is None else m_new - nb
                for r0 in range(u * hk, (u + 1) * hk, FLASH_CHUNK):
                    rows = slice(r0, r0 + FLASH_CHUNK)
                    p_ref[rd, g, rows, :] = jnp.exp2((s_view[rows, :] - shift).astype(BF16))
            out.append(jnp.exp2(m_prev - m_new))
        return tuple(out)

    def advance(j, c):
        wrap = j + 1 >= c
        return jnp.where(wrap, 0, j + 1), jnp.where(wrap, c + 1, c)

    for g in range(nh):
        s_ref[0, g] = diag_scores(0)(g)
    alphas = (jnp.ones((1, tq), F32),) * nh

    def diag_pair(u, alphas):
        i = 2 * u
        ip = jnp.maximum(i - 1, 0)
        alphas = step(0, 1, diag_scores(i + 1), (i, i), (ip, ip), alphas)
        return step(1, 0, diag_scores(i + 2), (i + 1, i + 1), (i, i), alphas)

    alphas = lax.fori_loop(0, (n - 2) // 2, diag_pair, alphas)
    alphas = step(0, 1, diag_scores(n - 1), (n - 2, n - 2), (n - 3, n - 3), alphas)
    alphas = step(1, 0, off_scores(0, 1), (n - 1, n - 1), (n - 2, n - 2), alphas)

    def off_steps(u, carry):
        cur, prev, alphas = carry[0:2], carry[2:4], carry[4:]
        for k in range(FLASH_UNROLL):
            nxt = advance(*cur)
            alphas = step(k % 2, 1 - k % 2, off_scores(*nxt), cur, prev, alphas)
            cur, prev = nxt, cur
        return tuple(cur) + tuple(prev) + tuple(alphas)

    assert n_off % FLASH_UNROLL == 0 and FLASH_UNROLL % 2 == 0
    i32 = jnp.int32
    carry = lax.fori_loop(0, n_off // FLASH_UNROLL, off_steps,
                          (i32(0), i32(1), i32(n - 1), i32(n - 1)) + tuple(alphas))
    jl, cl = carry[2], carry[3]
    for g in range(nh):
        pend = jnp.dot(vt_fn(g, jl), p_ref[1, g], preferred_element_type=F32)
        acc_ref[g, cl] = carry[4 + g] * acc_ref[g, cl] + pend


def _rows(i, t):
    return pl.ds(i * t, t) if isinstance(i, int) else pl.ds(pl.multiple_of(i * t, t), t)


def _flash_scratch(nh, n, tq):
    return [pltpu.VMEM((2, nh, tq, tq), F32), pltpu.VMEM((2, nh, tq, tq), BF16),
            pltpu.VMEM((nh, n, VT_ROWS, tq), F32), pltpu.VMEM((nh, n, 1, tq), F32)]


def _mla_kernel(tq, nh, q_ref, k_ref, vt_ref, gate_ref, y_ref, s_ref, p_ref, acc_ref, m_ref):
    n = SEQ // tq

    def scores(g, j, c):
        cols = slice(g * MLA_HW, (g + 1) * MLA_HW)
        return _qk(k_ref[_rows(j, tq), cols], q_ref[_rows(c, tq), cols])

    _flash_flat_t(nh, n, tq, scores, None, lambda g, j: vt_ref[g, j], s_ref, p_ref, acc_ref, m_ref)
    _flash_finish_all(nh, n, tq, MLA_V, acc_ref, gate_ref, y_ref)


def _mla_attn(qa, ka, vat, gate, tq, nh):
    nq = SEQ // tq
    return pl.pallas_call(
        functools.partial(_mla_kernel, tq, nh),
        out_shape=jax.ShapeDtypeStruct((N_TOK, MLA_HEADS * MLA_V), BF16),
        grid=(BATCH, MLA_HEADS // nh),
        in_specs=[pl.BlockSpec((SEQ, nh * MLA_HW), lambda b, h: (b, h)),
                  pl.BlockSpec((SEQ, nh * MLA_HW), lambda b, h: (b, h)),
                  pl.BlockSpec((nh, nq, VT_ROWS, tq), lambda b, h: (h, b, 0, 0)),
                  pl.BlockSpec((SEQ, nh * MLA_V), lambda b, h: (b, h))],
        out_specs=pl.BlockSpec((SEQ, nh * MLA_V), lambda b, h: (b, h)),
        scratch_shapes=_flash_scratch(nh, nq, tq),
        compiler_params=_cparams(("parallel", "parallel")),
        name="mla_attn",
    )(qa, ka, vat, gate)


MOBA_TILE = 2 * MOBA_BLOCK


def _moba_flat_kernel(nh, q_ref, k_ref, vt_ref, gate_ref, y_ref, km_ref, nb_ref, s_ref, p_ref, acc_ref, m_ref):
    L = MOBA_BLOCK
    T = MOBA_TILE
    D = MOBA_DH
    nblk = SEQ // L
    ntile = SEQ // T
    cols = [slice(g * D, (g + 1) * D) for g in range(nh)]

    for g in range(nh):
        for j in range(nblk):
            blk = k_ref[j * L:(j + 1) * L, cols[g]].astype(F32)
            km_ref[g, j:j + 1, :] = jnp.sum(blk, axis=0, keepdims=True) * (1.0 / L)

    blk_i = lax.broadcasted_iota(jnp.int32, (nblk, T), 0)
    blk_f = blk_i.astype(F32)
    second = lax.broadcasted_iota(jnp.int32, (nblk, T), 1) >= L

    def select(c, carry):
        own = 2 * c + second.astype(jnp.int32)
        past = blk_i < own
        for g in range(nh):
            gt = jnp.where(past, _qk(km_ref[g].astype(BF16), q_ref[_rows(c, T), cols[g]]), -jnp.inf)
            sel = jnp.zeros((nblk, T), jnp.bool_)
            for _ in range(MOBA_TOPK):
                mx = jnp.max(gt, axis=0, keepdims=True)
                first = jnp.min(jnp.where(gt == mx, blk_f, float(nblk)), axis=0, keepdims=True)
                pick = blk_f == first
                sel = jnp.logical_or(sel, pick)
                gt = jnp.where(pick, -jnp.inf, gt)
            keep = jnp.logical_or(jnp.logical_and(sel, past), blk_i == own)
            nb_ref[g, c] = jnp.where(keep, 0.0, NEG)
        return carry

    lax.fori_loop(0, ntile, select, 0)

    def scores(g, j, c):
        return _qk(k_ref[_rows(j, T), cols[g]], q_ref[_rows(c, T), cols[g]])

    def bias(g, j, c):
        return [nb_ref[g, c, pl.ds(2 * j, 1), :], nb_ref[g, c, pl.ds(2 * j + 1, 1), :]]

    _flash_flat_t(nh, ntile, T, scores, bias, lambda g, j: vt_ref[g, j], s_ref, p_ref, acc_ref, m_ref)
    _flash_finish_all(nh, ntile, T, D, acc_ref, gate_ref, y_ref)


def _moba_flat_attn(qb, kb, vbt, gate, nh):
    T = MOBA_TILE
    D = MOBA_DH
    nblk = SEQ // MOBA_BLOCK
    ntile = SEQ // T
    return pl.pallas_call(
        functools.partial(_moba_flat_kernel, nh),
        out_shape=jax.ShapeDtypeStruct((N_TOK, MOBA_HEADS * D), BF16),
        grid=(BATCH, MOBA_HEADS // nh),
        in_specs=[pl.BlockSpec((SEQ, nh * D), lambda b, h: (b, h)),
                  pl.BlockSpec((SEQ, nh * D), lambda b, h: (b, h)),
                  pl.BlockSpec((nh, ntile, VT_ROWS, T), lambda b, h: (h, b, 0, 0)),
                  pl.BlockSpec((SEQ, nh * D), lambda b, h: (b, h))],
        out_specs=pl.BlockSpec((SEQ, nh * D), lambda b, h: (b, h)),
        scratch_shapes=[pltpu.VMEM((nh, nblk, D), F32), pltpu.VMEM((nh, ntile, nblk, T), F32)]
        + _flash_scratch(nh, ntile, T),
        compiler_params=_cparams(("parallel", "parallel")),
        name="moba_attn",
    )(qb, kb, vbt, gate)


def _mem_prep_kernel(mem_ref, g_ref, w_ref, kv_ref):
    mn = _rms(mem_ref[...], g_ref[...]).astype(BF16)
    kv_ref[...] = jnp.dot(mn, w_ref[...], preferred_element_type=F32).astype(BF16)


def _mem_prep(mem2d, g, w):
    g2 = g.reshape(1, -1)
    return pl.pallas_call(
        _mem_prep_kernel,
        out_shape=jax.ShapeDtypeStruct((BATCH * N_MEM, 2 * MEMQ_W), BF16),
        grid=(BATCH,),
        in_specs=[pl.BlockSpec((N_MEM, D_MODEL), lambda b: (b, 0)),
                  pl.BlockSpec(g2.shape, lambda b: (0, 0)),
                  pl.BlockSpec(w.shape, lambda b: (0, 0))],
        out_specs=pl.BlockSpec((N_MEM, 2 * MEMQ_W), lambda b: (b, 0)),
        compiler_params=_cparams(("parallel",)),
        name="mem_prep",
    )(mem2d, g2, w)


def _mem_attn_kernel(q_ref, kv_ref, gate_ref, y_ref):
    cols = [slice(h * MEM_DH, (h + 1) * MEM_DH) for h in range(MEM_HEADS)]
    ss = [_qk(q_ref[:, c], kv_ref[:, c]) for c in cols]
    ps = [jnp.exp2(s - jnp.max(s, axis=1, keepdims=True)) for s in ss]
    for h, (c, p) in enumerate(zip(cols, ps)):
        l = jnp.sum(p, axis=1, keepdims=True)
        a = MEMQ_W + h * MEM_DH
        o = jnp.dot(p.astype(BF16), kv_ref[:, a:a + MEM_DH], preferred_element_type=F32) / l
        y_ref[:, c] = (o * _silu(gate_ref[:, c].astype(F32))).astype(y_ref.dtype)


def _mem_attn(qm, kvm, gate, tq):
    nq = SEQ // tq
    return pl.pallas_call(
        _mem_attn_kernel,
        out_shape=jax.ShapeDtypeStruct((N_TOK, MEMQ_W), BF16),
        grid=(BATCH, nq),
        in_specs=[pl.BlockSpec((tq, MEMQ_W), lambda b, i: (b * nq + i, 0)),
                  pl.BlockSpec((N_MEM, 2 * MEMQ_W), lambda b, i: (b, 0)),
                  pl.BlockSpec((tq, MEMQ_W), lambda b, i: (b * nq + i, 0))],
        out_specs=pl.BlockSpec((tq, MEMQ_W), lambda b, i: (b * nq + i, 0)),
        compiler_params=_cparams(("parallel", "parallel")),
        name="mem_attn",
    )(qm, kvm, gate)


def _band_bias(n, lo_off, hi_off):
    i = lax.broadcasted_iota(jnp.int32, (Q_BLOCK, 2 * Q_BLOCK), 0)
    c = lax.broadcasted_iota(jnp.int32, (Q_BLOCK, 2 * Q_BLOCK), 1)
    vis = (c - i >= lo_off) & (c - i <= hi_off) & ((c >= Q_BLOCK) | (n > 0))
    return jnp.where(vis, 0.0, NEG).astype(F32)


SWA_BLOCKS_PER_STEP = 2


def _swa_kernel(q_ref, kvp_ref, kvo_ref, gate_ref, sink_ref, y_ref):
    n = pl.program_id(1)
    for u in range(SWA_BLOCKS_PER_STEP):
        rows = slice(u * Q_BLOCK, (u + 1) * Q_BLOCK)
        prev = kvp_ref if u == 0 else kvo_ref.at[(u - 1) * Q_BLOCK:u * Q_BLOCK, :]
        _swa_block(n if u == 0 else 1, q_ref.at[rows, :], prev, kvo_ref.at[rows, :], gate_ref.at[rows, :],
                   sink_ref, y_ref.at[rows, :])


def _swa_block(n, q_ref, kvp_ref, kvo_ref, gate_ref, sink_ref, y_ref):
    QB = Q_BLOCK
    G = SWA_HEADS // SWA_KV_HEADS
    P = G // 2
    c = lax.broadcasted_iota(jnp.int32, (2 * QB, QB), 0)
    i = lax.broadcasted_iota(jnp.int32, (2 * QB, QB), 1)
    vis = (c - i >= QB - (SWA_WINDOW - 1)) & (c - i <= QB) & ((c >= QB) | (n > 0))
    bias = jnp.where(vis, 0.0, NEG).astype(F32)
    bias4 = jnp.concatenate([bias] * P, axis=1)
    k2 = jnp.concatenate([kvp_ref[:, :LANES], kvo_ref[:, :LANES]], axis=0).astype(F32)
    v2 = jnp.concatenate([kvp_ref[:, LANES:], kvo_ref[:, LANES:]], axis=0).astype(F32)
    k2r = pltpu.roll(k2, SWA_DH, 1)
    v2r = pltpu.roll(v2, SWA_DH, 1)
    lane = lax.broadcasted_iota(jnp.int32, (2 * QB, LANES), 1)
    lo = lane < SWA_DH
    top = lax.broadcasted_iota(jnp.int32, (LANES, P * QB), 0) < SWA_DH
    vts, ss = [], []
    for kv in range(SWA_KV_HEADS):
        ka, kb_ = (k2, k2r) if kv == 0 else (k2r, k2)
        va, vb_ = (v2, v2r) if kv == 0 else (v2r, v2)
        k_lo = jnp.where(lo, ka, 0.0).astype(BF16)
        k_hi = jnp.where(lo, 0.0, kb_).astype(BF16)
        vts.append((jnp.where(lo, va, 0.0).T.astype(BF16), jnp.where(lo, 0.0, vb_).T.astype(BF16)))
        base = kv * P
        q4 = jnp.concatenate([q_ref[:, (base + p) * LANES:(base + p + 1) * LANES] for p in range(P)],
                             axis=0)
        ss.append((_qk(k_lo, q4) + bias4, _qk(k_hi, q4) + bias4))
    stats = []
    for pair in ss:
        st = []
        for s in pair:
            m = jnp.max(s, axis=0, keepdims=True)
            p = jnp.exp2(s - m)
            st.append((m, p, jnp.sum(p, axis=0, keepdims=True)))
        stats.append(st)
    for kv in range(SWA_KV_HEADS):
        (m_lo, p_lo, l_lo), (m_hi, p_hi, l_hi) = stats[kv]
        vt_lo, vt_hi = vts[kv]
        base = kv * P
        ot = (jnp.dot(vt_lo, p_lo.astype(BF16), preferred_element_type=F32)
              + jnp.dot(vt_hi, p_hi.astype(BF16), preferred_element_type=F32))
        cols = slice(base * LANES, (base + P) * LANES)
        f_lo = 1.0 / ((1.0 + jnp.exp2(sink_ref[0:1, cols] * LOG2E - (m_lo + jnp.log2(l_lo)))) * l_lo)
        f_hi = 1.0 / ((1.0 + jnp.exp2(sink_ref[1:2, cols] * LOG2E - (m_hi + jnp.log2(l_hi)))) * l_hi)
        ot = ot * jnp.where(top, f_lo, f_hi)
        for p in range(P):
            cc = slice((base + p) * LANES, (base + p + 1) * LANES)
            o = ot[:, p * QB:(p + 1) * QB].T
            y_ref[:, cc] = (o * _silu(gate_ref[:, cc].astype(F32))).astype(y_ref.dtype)


def _swa_attn(qc, kvc, gate, sink_rows):
    R = SWA_BLOCKS_PER_STEP
    ns = SEQ // (R * Q_BLOCK)
    w = SWA_HEADS * SWA_DH
    return pl.pallas_call(
        _swa_kernel,
        out_shape=jax.ShapeDtypeStruct((N_TOK, w), BF16),
        grid=(BATCH, ns),
        in_specs=[pl.BlockSpec((R * Q_BLOCK, w), lambda b, n: (b * ns + n, 0)),
                  pl.BlockSpec((Q_BLOCK, 2 * LANES), lambda b, n: (jnp.maximum((b * ns + n) * R - 1, 0), 0)),
                  pl.BlockSpec((R * Q_BLOCK, 2 * LANES), lambda b, n: (b * ns + n, 0)),
                  pl.BlockSpec((R * Q_BLOCK, w), lambda b, n: (b * ns + n, 0)),
                  pl.BlockSpec((2, w), lambda b, n: (0, 0))],
        out_specs=pl.BlockSpec((R * Q_BLOCK, w), lambda b, n: (b * ns + n, 0)),
        compiler_params=_cparams(("parallel", "parallel")),
        name="swa_attn",
    )(qc, kvc, kvc, gate, sink_rows)


DIL_BLOCKS_PER_STEP = 4
DIL_GROUP = 2
MAX_STRIDE = 4


def _dil_kernel(R, q_ref, kp_ref, ko_ref, vp_ref, vo_ref, o_ref, lse_ref):
    QB = Q_BLOCK
    n = pl.program_id(2)
    lane = lax.broadcasted_iota(jnp.int32, (QB, LANES), 1)
    for u0 in range(0, R, DIL_GROUP):
        tasks = []
        for u in range(u0, u0 + DIL_GROUP):
            rows = slice(u * QB, (u + 1) * QB)
            bias = _band_bias(n if u == 0 else 1, 0, QB)
            for h in range(DIL_HEADS):
                c = slice(h * DIL_DH, (h + 1) * DIL_DH)
                k_prev = kp_ref[:, c] if u == 0 else ko_ref[(u - 1) * QB:u * QB, c]
                v_prev = vp_ref[:, c] if u == 0 else vo_ref[(u - 1) * QB:u * QB, c]
                k = jnp.concatenate([k_prev, ko_ref[rows, c]], axis=0)
                v = jnp.concatenate([v_prev, vo_ref[rows, c]], axis=0)
                tasks.append((u, h, _qk(q_ref[rows, c], k) + bias, v))
        soft = []
        for u, h, s, v in tasks:
            m = jnp.max(s, axis=1, keepdims=True)
            p = jnp.exp2(s - m)
            soft.append((m, p, jnp.sum(p, axis=1, keepdims=True)))
        lse_all = {u: jnp.zeros((QB, LANES), F32) for u in range(u0, u0 + DIL_GROUP)}
        for (u, h, _, v), (m, p, l) in zip(tasks, soft):
            o_ref[u * QB:(u + 1) * QB, h * DIL_DH:(h + 1) * DIL_DH] = (
                jnp.dot(p.astype(BF16), v, preferred_element_type=F32) / l).astype(o_ref.dtype)
            lse_all[u] = jnp.where(lane == h, m + jnp.log2(l), lse_all[u])
        for u, val in lse_all.items():
            lse_ref[u * QB:(u + 1) * QB, :] = val


def _dil_attn(qv, kv, vv, dil):
    w = DIL_HEADS * DIL_DH
    L = SEQ // dil
    R = min(DIL_BLOCKS_PER_STEP, L // Q_BLOCK)
    ns = L // (R * Q_BLOCK)
    cur = lambda b, r, n: (b * ns + n, r)
    prev = lambda b, r, n: (jnp.maximum((b * ns + n) * R - 1, 0), r)
    blk = (R * Q_BLOCK, w)
    pblk = (Q_BLOCK, w)
    return pl.pallas_call(
        functools.partial(_dil_kernel, R),
        out_shape=[jax.ShapeDtypeStruct((BATCH * L, dil * w), BF16),
                   jax.ShapeDtypeStruct((BATCH * L, dil * LANES), F32)],
        grid=(BATCH, dil, ns),
        in_specs=[pl.BlockSpec(blk, cur), pl.BlockSpec(pblk, prev), pl.BlockSpec(blk, cur),
                  pl.BlockSpec(pblk, prev), pl.BlockSpec(blk, cur)],
        out_specs=[pl.BlockSpec(blk, cur), pl.BlockSpec((R * Q_BLOCK, LANES), cur)],
        compiler_params=_cparams(("parallel", "parallel", "parallel")),
        name=f"dil_attn_d{dil}",
    )(qv, kv, kv, vv, vv)


def _dil_combine_kernel(tm, *refs):
    np_ = len(DIL_PATTERNS)
    o_refs, l_refs = refs[:np_], refs[np_:2 * np_]
    gate_ref, y_ref = refs[2 * np_], refs[2 * np_ + 1]
    stage = refs[2 * np_ + 2:]
    w = DIL_HEADS * DIL_DH
    os_, ls = [], []
    k = 0
    for (_, d), o_ref, l_ref in zip(DIL_PATTERNS, o_refs, l_refs):
        if d == 1:
            os_.append(lambda h, o_ref=o_ref: o_ref[:, h * DIL_DH:(h + 1) * DIL_DH].astype(F32))
            ls.append(l_ref[...])
            continue
        so, sl = stage[2 * k], stage[2 * k + 1]
        k += 1
        if d <= MAX_STRIDE:
            hops = [(so, sl, d, lambda r: r)]
        else:
            assert d == MAX_STRIDE * MAX_STRIDE
            co, cl = stage[-2], stage[-1]
            q4 = tm // MAX_STRIDE
            hops = [(co, cl, MAX_STRIDE, lambda r: (r % MAX_STRIDE) * q4 + r // MAX_STRIDE)]
        dst_o, dst_l, stride, start = hops[0]
        for r in range(d):
            rows = pl.ds(start(r), tm // d, stride=stride)
            for h in range(DIL_HEADS):
                a = r * w + h * DIL_DH
                dst_o[h, rows, :] = o_ref[:, a:a + DIL_DH].astype(F32)
            dst_l[rows, :] = l_ref[:, r * LANES:(r + 1) * LANES]
        if d > MAX_STRIDE:
            for r in range(MAX_STRIDE):
                rows = pl.ds(r, q4, stride=MAX_STRIDE)
                blk = slice(r * q4, (r + 1) * q4)
                for h in range(DIL_HEADS):
                    so[h, rows, :] = co[h, blk, :]
                sl[rows, :] = cl[blk, :]
        os_.append(lambda h, so=so: so[h])
        ls.append(sl[...])
    mx = functools.reduce(jnp.maximum, ls)
    es = [jnp.exp2(x - mx) for x in ls]
    den = functools.reduce(lambda a, b: a + b, es)
    ws = [e / den for e in es]
    lane = lax.broadcasted_iota(jnp.int32, ws[0].shape, 1)
    for h in range(DIL_HEADS):
        c = slice(h * DIL_DH, (h + 1) * DIL_DH)
        wh = [jnp.sum(jnp.where(lane == h, wgt, 0.0), axis=1, keepdims=True) for wgt in ws]
        o = functools.reduce(lambda a, b: a + b, [wh[p] * os_[p](h) for p in range(np_)])
        y_ref[:, c] = (o * _silu(gate_ref[:, c].astype(F32))).astype(y_ref.dtype)


def _dil_combine(outs, lses, gate, tm):
    w = DIL_HEADS * DIL_DH
    in_specs = [pl.BlockSpec((tm // d, d * w), lambda i: (i, 0)) for _, d in DIL_PATTERNS]
    in_specs += [pl.BlockSpec((tm // d, d * LANES), lambda i: (i, 0)) for _, d in DIL_PATTERNS]
    in_specs.append(pl.BlockSpec((tm, w), lambda i: (i, 0)))
    scratch = []
    for _, d in DIL_PATTERNS:
        if d > 1:
            scratch += [pltpu.VMEM((DIL_HEADS, tm, DIL_DH), F32), pltpu.VMEM((tm, LANES), F32)]
    if any(d > MAX_STRIDE for _, d in DIL_PATTERNS):
        scratch += [pltpu.VMEM((DIL_HEADS, tm, DIL_DH), F32), pltpu.VMEM((tm, LANES), F32)]
    return pl.pallas_call(
        functools.partial(_dil_combine_kernel, tm),
        out_shape=jax.ShapeDtypeStruct((N_TOK, w), BF16),
        grid=(N_TOK // tm,),
        in_specs=in_specs,
        out_specs=pl.BlockSpec((tm, w), lambda i: (i, 0)),
        scratch_shapes=scratch,
        compiler_params=_cparams(("parallel",)),
        name="dil_combine",
    )(*outs, *lses, gate)


def _outproj_kernel(nparts, final, *refs):
    ys, ws = refs[:nparts], refs[nparts:2 * nparts]
    x_ref = refs[2 * nparts]
    o_ref = refs[-1]
    acc = x_ref[...]
    for y, w in zip(ys, ws):
        acc = acc + jnp.dot(y[...], w[...], preferred_element_type=F32)
    if final:
        acc = _rms(acc, refs[2 * nparts + 1][...])
    o_ref[...] = acc


def _outproj(ys, ws, x, final_g, tm):
    n_tok, d = x.shape
    in_specs = [pl.BlockSpec((tm, y.shape[1]), lambda i: (i, 0)) for y in ys]
    in_specs += [pl.BlockSpec(w.shape, lambda i: (0, 0)) for w in ws]
    in_specs.append(pl.BlockSpec((tm, d), lambda i: (i, 0)))
    args = [*ys, *ws, x]
    if final_g is not None:
        in_specs.append(pl.BlockSpec((1, d), lambda i: (0, 0)))
        args.append(final_g.reshape(1, d))
    return pl.pallas_call(
        functools.partial(_outproj_kernel, len(ys), final_g is not None),
        out_shape=jax.ShapeDtypeStruct((n_tok, d), F32),
        grid=(n_tok // tm,),
        in_specs=in_specs,
        out_specs=pl.BlockSpec((tm, d), lambda i: (i, 0)),
        compiler_params=_cparams(("parallel",)),
        name="outproj",
    )(*args)


PROJ_TM = 512
ATTN_TQ = 512
MLA_HEADS_PER_STEP = 2
MOBA_HEADS_PER_STEP = 2


def _even_layer(x, mem2d, norm_g, w_in, q_norm_g, w_uq, kv_norm_g, w_ukv, mem_norm_g, w_mem_kv, w_out):
    n_lat = MLA_Q_RANK + MLA_KV_RANK + MLA_ROPE
    wlat = jnp.pad(w_in[:, :n_lat], ((0, 0), (0, LANES - MLA_ROPE))).astype(BF16)
    n_bqkv = n_lat + 3 * MOBA_HEADS * MOBA_DH
    w_bqkv = w_in[:, n_lat:n_bqkv].astype(BF16)
    w_rest = w_in[:, n_bqkv:].astype(BF16)
    wuq = w_uq.reshape(MLA_Q_RANK, MLA_HEADS, MLA_NOPE + MLA_ROPE)
    wuq = jnp.pad(wuq, ((0, 0), (0, 0), (0, MLA_HW - MLA_NOPE - MLA_ROPE)))
    wuq = wuq.reshape(MLA_Q_RANK, MLA_HEADS * MLA_HW).astype(BF16)
    wukv = w_ukv.reshape(MLA_KV_RANK, MLA_HEADS, MLA_NOPE + MLA_V)
    wukv = jnp.concatenate([wukv[:, :, :MLA_NOPE].reshape(MLA_KV_RANK, -1),
                            wukv[:, :, MLA_NOPE:].reshape(MLA_KV_RANK, -1)], axis=1).astype(BF16)

    cos_a, sin_a = _rope_tables(MLA_ROPE, LANES)
    qa, ka, va = _mla_prep(x, norm_g, wlat, q_norm_g, wuq, kv_norm_g, wukv, cos_a, sin_a, ATTN_TQ)

    rot = MOBA_DH // ROT_FRAC
    cos_b, sin_b = _rope_tables(rot, MOBA_DH)
    sc = MOBA_DH ** -0.5 * LOG2E
    wb = MOBA_HEADS * MOBA_DH
    plain = [("plain",)]
    groups_qkv = [
        (wb, [("rope", 0, rot // 2, MOBA_DH, sc)] * MOBA_HEADS, "rows"),
        (wb, [("rope", 0, rot // 2, MOBA_DH, 1.0)] * MOBA_HEADS, "rows"),
        (wb, plain * MOBA_HEADS, ("vT", MOBA_TILE)),
    ]
    groups_rest = [
        (MEMQ_W, [("scale", MEM_DH ** -0.5 * LOG2E)] * MEM_HEADS, "rows"),
        (MLA_HEADS * MLA_V, plain * MLA_HEADS, "rows"),
        (wb, plain * MOBA_HEADS, "rows"),
        (MEMQ_W, plain * MEM_HEADS, "rows"),
    ]
    qb, kb, vb = _inproj(x, norm_g, w_bqkv, groups_qkv, [cos_b, sin_b], PROJ_TM)
    qm, gate_a, gate_b, gate_m = _inproj(x, norm_g, w_rest, groups_rest, [], PROJ_TM)

    y_a = _mla_attn(qa, ka, va, gate_a, ATTN_TQ, MLA_HEADS_PER_STEP)
    y_b = _moba_flat_attn(qb, kb, vb, gate_b, MOBA_HEADS_PER_STEP)
    kvm = _mem_prep(mem2d, mem_norm_g, w_mem_kv.astype(BF16))
    y_m = _mem_attn(qm, kvm, gate_m, ATTN_TQ)

    a, b = MLA_HEADS * MLA_V, MLA_HEADS * MLA_V + MOBA_HEADS * MOBA_DH
    return [y_a, y_b, y_m], [w_out[:a].astype(BF16), w_out[a:b].astype(BF16), w_out[b:].astype(BF16)]


def _odd_layer(x, mem2d, norm_g, w_in, sinks, mem_norm_g, w_mem_kv, w_out):
    rot_c = SWA_DH // ROT_FRAC
    rot_d = DIL_DH // ROT_FRAC
    cos_c, sin_c = _rope_tables(rot_c, SWA_DH)
    cos_d, sin_d = _rope_tables(rot_d, DIL_DH)
    wq = SWA_HEADS * SWA_DH
    wd = DIL_HEADS * DIL_DH
    wkv = 2 * SWA_KV_HEADS * SWA_DH
    rc = lambda s: ("rope", 0, rot_c // 2, SWA_DH, s)
    rd = lambda s: ("rope", 1, rot_d // 2, DIL_DH, s)
    plain = [("plain",)]
    groups_qkv = [
        (wq, [rc(SWA_DH ** -0.5 * LOG2E)] * (wq // LANES), "rows"),
        (wkv, [rc(1.0), ("plain",)], "rows"),
        (wd, [rd(DIL_DH ** -0.5 * LOG2E)] * DIL_HEADS, "dilated"),
        (wd, [rd(1.0)] * DIL_HEADS, "dilated"),
        (wd, plain * DIL_HEADS, "dilated"),
    ]
    groups_rest = [
        (MEMQ_W, [("scale", MEM_DH ** -0.5 * LOG2E)] * MEM_HEADS, "rows"),
        (wq, plain * (wq // LANES), "rows"),
        (wd, plain * DIL_HEADS, "rows"),
        (MEMQ_W, plain * MEM_HEADS, "rows"),
    ]
    n_qkv = wq + wkv + 3 * wd
    res = _inproj(x, norm_g, w_in[:, :n_qkv].astype(BF16), groups_qkv, [cos_c, sin_c, cos_d, sin_d], PROJ_TM)
    qc, kvc = res[0], res[1]
    nv = 1 + len(DIL_VIEWS)
    qd, kd, vd = (res[2 + i * nv:2 + (i + 1) * nv] for i in range(3))
    qm, gate_c, gate_d, gate_m = _inproj(x, norm_g, w_in[:, n_qkv:].astype(BF16), groups_rest, [], PROJ_TM)

    sink_rows = jnp.repeat(sinks.astype(F32).reshape(SWA_HEADS // 2, 2).T, LANES, axis=1)
    y_c = _swa_attn(qc, kvc, gate_c, sink_rows)
    view_of = {1: 0, **{d: 1 + i for i, d in enumerate(DIL_VIEWS)}}
    outs, lses = zip(*[_dil_attn(qd[view_of[dil]], kd[view_of[dil]], vd[view_of[dil]], dil)
                       for _, dil in DIL_PATTERNS])
    y_d = _dil_combine(outs, lses, gate_d, PROJ_TM)
    kvm = _mem_prep(mem2d, mem_norm_g, w_mem_kv.astype(BF16))
    y_m = _mem_attn(qm, kvm, gate_m, ATTN_TQ)

    return [y_c, y_d, y_m], [w_out[:wq].astype(BF16), w_out[wq:wq + wd].astype(BF16),
                             w_out[wq + wd:].astype(BF16)]


def kernel(x, mem, ev_norm_g, ev_w_in, ev_q_norm_g, ev_w_uq, ev_kv_norm_g, ev_w_ukv, ev_mem_norm_g,
           ev_w_mem_kv, ev_w_out, od_norm_g, od_w_in, od_sinks, od_mem_norm_g, od_w_mem_kv, od_w_out,
           final_norm_g):
    x2 = x.reshape(N_TOK, D_MODEL)
    mem2d = mem.reshape(BATCH * N_MEM, D_MODEL)
    ys, ws = _even_layer(x2, mem2d, ev_norm_g[0], ev_w_in[0], ev_q_norm_g[0], ev_w_uq[0], ev_kv_norm_g[0],
                         ev_w_ukv[0], ev_mem_norm_g[0], ev_w_mem_kv[0], ev_w_out[0])
    x2 = _outproj(ys, ws, x2, None, PROJ_TM)
    ys, ws = _odd_layer(x2, mem2d, od_norm_g[0], od_w_in[0], od_sinks[0], od_mem_norm_g[0],
                        od_w_mem_kv[0], od_w_out[0])
    x2 = _outproj(ys, ws, x2, final_norm_g, PROJ_TM)
    return x2.reshape(BATCH, SEQ, D_MODEL)
```

```python
import functools

import numpy as np
import jax
import jax.numpy as jnp
from jax import lax
from jax.experimental import pallas as pl
from jax.experimental.pallas import tpu as pltpu

D_MODEL = 2048
BATCH = 4
SEQ = 4096
N_TOK = BATCH * SEQ
N_MEM = 256
ROPE_THETA = 500000.0
ROT_FRAC = 4
EPS = 1e-6

MLA_HEADS = 8
MLA_Q_RANK = 512
MLA_KV_RANK = 256
MLA_NOPE = 128
MLA_ROPE = 64
MLA_V = 128
MOBA_HEADS = 8
MOBA_DH = 128
MOBA_BLOCK = 256
MOBA_TOPK = 3
SWA_HEADS = 16
SWA_KV_HEADS = 2
SWA_DH = 64
SWA_WINDOW = 128
DIL_HEADS = 6
DIL_DH = 128
DIL_PATTERNS = ((128, 1), (512, 4), (2048, 16))
MEM_HEADS = 4
MEM_DH = 128
MEMQ_W = MEM_HEADS * MEM_DH
Q_BLOCK = 128

LANES = 128
VMEM_LIMIT = 52 * 1024 * 1024

NEG = -1e30
BF16 = jnp.bfloat16
F32 = jnp.float32


def _cparams(sem):
    return pltpu.CompilerParams(dimension_semantics=sem, vmem_limit_bytes=VMEM_LIMIT)


def _rms(x, g):
    ms = jnp.mean(x * x, axis=-1, keepdims=True)
    return (x * lax.rsqrt(ms + EPS)) * g


def _silu(g):
    return g / (1.0 + jnp.exp(-g))


def _sigmoid(z):
    return 1.0 / (1.0 + jnp.exp(-z))


def _rope_slab(x, cos_t, sin_t, half, period):
    lane = lax.broadcasted_iota(jnp.int32, x.shape, 1)
    up = pltpu.roll(x, LANES - half, 1)
    dn = pltpu.roll(x, half, 1)
    sw = jnp.where((lane & (period - 1)) < half, up, dn)
    return x * cos_t + sw * sin_t


def _qk(q, k):
    return lax.dot_general(q, k, (((1,), (1,)), ((), ())), preferred_element_type=F32)


def _rope_tables(rot_dim, period):
    half = rot_dim // 2
    inv = 1.0 / (ROPE_THETA ** (jnp.arange(0, rot_dim, 2, dtype=F32) / rot_dim))
    ang = jnp.arange(SEQ, dtype=F32)[:, None] * inv[None, :]
    c, s = jnp.cos(ang), jnp.sin(ang)
    d = np.arange(LANES) % period
    idx = d % half
    cos_t = jnp.where(d < rot_dim, c[:, idx], 1.0)
    sin_t = jnp.where(d < half, -s[:, idx], jnp.where(d < rot_dim, s[:, idx], 0.0))
    return cos_t.astype(F32), sin_t.astype(F32)


PROJ_CHUNK = 512
DIL_VIEWS = tuple(d for _, d in DIL_PATTERNS if d > 1)


def _inproj_kernel(groups, ntab, tm, x_ref, g_ref, *rest):
    tabs = rest[:2 * ntab]
    w_ref = rest[2 * ntab]
    ndil = sum(layout == "dilated" for _, _, layout in groups)
    nscr = ndil + (1 if ndil else 0)
    refs = list(rest[2 * ntab + 1:len(rest) - nscr])
    stage = list(rest[len(rest) - nscr:])
    chain_ref = stage.pop() if ndil else None
    xn = _rms(x_ref[...], g_ref[...]).astype(BF16)
    c0 = 0
    for ncols, kinds, layout in groups:
        o_ref = refs.pop(0)
        views = [refs.pop(0) for _ in DIL_VIEWS] if layout == "dilated" else []
        st_ref = stage.pop(0) if layout == "dilated" else None
        for s0 in range(0, ncols, PROJ_CHUNK):
            n = min(PROJ_CHUNK, ncols - s0)
            acc = jnp.dot(xn, w_ref[:, c0 + s0:c0 + s0 + n], preferred_element_type=F32)
            for c in range(n // LANES):
                col = s0 + c * LANES
                kind = kinds[col // LANES]
                piece = acc[:, c * LANES:(c + 1) * LANES]
                if kind[0] == "rope":
                    _, ti, half, period, sc = kind
                    piece = _rope_slab(piece, tabs[2 * ti][...], tabs[2 * ti + 1][...], half, period)
                    if sc != 1.0:
                        piece = piece * sc
                elif kind[0] == "scale":
                    piece = piece * kind[1]
                if layout == "rows" or layout == "dilated":
                    o_ref[:, col:col + LANES] = piece.astype(o_ref.dtype)
                    if layout == "dilated":
                        st_ref[col // LANES] = piece
                else:
                    tb = layout[1]
                    for u in range(tm // tb):
                        o_ref[col // LANES, u] = _vt_block(piece[u * tb:(u + 1) * tb, :])
        src, src_d = st_ref, 1
        for vi, (d, v_ref) in enumerate(zip(DIL_VIEWS, views)):
            keep = chain_ref if vi + 1 < len(views) else None
            rows_src, rows_dst = tm // src_d, tm // d
            for r in range(d):
                start = (r % src_d) * rows_src + r // src_d
                for c in range(ncols // LANES):
                    blk = src[c, pl.ds(start, rows_dst, stride=d // src_d), :]
                    a = r * ncols + c * LANES
                    v_ref[:, a:a + LANES] = blk.astype(v_ref.dtype)
                    if keep is not None:
                        keep[c, r * rows_dst:(r + 1) * rows_dst, :] = blk
            src, src_d = keep, d
        c0 += ncols


def _inproj(x, g, w, groups, tables, tm):
    n_tok, d = x.shape
    assert sum(nc for nc, _, _ in groups) == w.shape[1]
    seq_tiles = SEQ // tm
    in_specs = [pl.BlockSpec((tm, d), lambda i: (i, 0)),
                pl.BlockSpec((1, d), lambda i: (0, 0))]
    args = [x, g.reshape(1, d)]
    for t in tables:
        in_specs.append(pl.BlockSpec((tm, LANES), lambda i: (i % seq_tiles, 0)))
        args.append(t)
    in_specs.append(pl.BlockSpec(w.shape, lambda i: (0, 0), pipeline_mode=pl.Buffered(1)))
    args.append(w)
    out_specs, out_shapes = [], []
    for nc, _, layout in groups:
        if layout == "rows" or layout == "dilated":
            out_specs.append(pl.BlockSpec((tm, nc), lambda i: (i, 0)))
            out_shapes.append(jax.ShapeDtypeStruct((n_tok, nc), BF16))
            if layout == "dilated":
                for dil in DIL_VIEWS:
                    out_specs.append(pl.BlockSpec((tm // dil, dil * nc), lambda i: (i, 0)))
                    out_shapes.append(jax.ShapeDtypeStruct((n_tok // dil, dil * nc), BF16))
        else:
            tb = layout[1]
            out_specs.append(pl.BlockSpec((nc // LANES, tm // tb, VT_ROWS, tb), lambda i: (0, i, 0, 0)))
            out_shapes.append(jax.ShapeDtypeStruct((nc // LANES, n_tok // tb, VT_ROWS, tb), BF16))
    return pl.pallas_call(
        functools.partial(_inproj_kernel, groups, len(tables) // 2, tm),
        out_shape=out_shapes,
        grid=(n_tok // tm,),
        in_specs=in_specs,
        out_specs=out_specs,
        scratch_shapes=(lambda dil: dil + dil[:1])(
            [pltpu.VMEM((nc // LANES, tm, LANES), F32) for nc, _, layout in groups if layout == "dilated"]),
        compiler_params=_cparams(("parallel",)),
        name="inproj",
    )(*args)


MLA_HW = 256


def _mla_prep_kernel(scale, x_ref, g_ref, wlat_ref, qg_ref, wuq_ref, kvg_ref, wukv_ref,
                     cos_ref, sin_ref, qa_ref, ka_ref, va_ref):
    tm = x_ref.shape[0]
    halves = [slice(u * (tm // 2), (u + 1) * (tm // 2)) for u in range(2)]
    xns = [_rms(x_ref[r, :], g_ref[...]).astype(BF16) for r in halves]
    lats = [jnp.dot(xn, wlat_ref[...], preferred_element_type=F32) for xn in xns]
    cqs = [_rms(lat[:, :MLA_Q_RANK], qg_ref[...]).astype(BF16) for lat in lats]
    ckvs = [_rms(lat[:, MLA_Q_RANK:MLA_Q_RANK + MLA_KV_RANK], kvg_ref[...]).astype(BF16) for lat in lats]
    qs = [jnp.dot(cq, wuq_ref[...], preferred_element_type=F32) for cq in cqs]
    kvs = [jnp.dot(ckv, wukv_ref[...], preferred_element_type=F32) for ckv in ckvs]
    for r, lat, q, kv in zip(halves, lats, qs, kvs):
        cos_t, sin_t = cos_ref[r, :], sin_ref[r, :]
        kpe = lat[:, MLA_Q_RANK + MLA_KV_RANK:]
        kpe = _rope_slab(kpe, cos_t, sin_t, MLA_ROPE // 2, LANES).astype(BF16)
        for h in range(MLA_HEADS):
            a = h * MLA_HW
            qa_ref[r, a:a + LANES] = (q[:, a:a + LANES] * scale).astype(BF16)
            qr = _rope_slab(q[:, a + LANES:a + 2 * LANES], cos_t, sin_t, MLA_ROPE // 2, LANES)
            qa_ref[r, a + LANES:a + 2 * LANES] = (qr * scale).astype(BF16)
            ka_ref[r, a:a + LANES] = kv[:, h * LANES:(h + 1) * LANES].astype(BF16)
            ka_ref[r, a + LANES:a + 2 * LANES] = kpe
            v0 = MLA_HEADS * MLA_NOPE + h * MLA_V
            va_ref[h, 0, :, r] = _vt_block(kv[:, v0:v0 + MLA_V])


def _mla_prep(x, g, wlat, qg, wuq, kvg, wukv, cos_t, sin_t, tm):
    n_tok, d = x.shape
    seq_tiles = SEQ // tm
    full = lambda a: pl.BlockSpec(a.shape, lambda i: (0, 0))
    qg2, kvg2, g2 = qg.reshape(1, -1), kvg.reshape(1, -1), g.reshape(1, d)
    scale = (MLA_NOPE + MLA_ROPE) ** -0.5 * LOG2E
    return pl.pallas_call(
        functools.partial(_mla_prep_kernel, scale),
        out_shape=[jax.ShapeDtypeStruct((n_tok, MLA_HEADS * MLA_HW), BF16),
                   jax.ShapeDtypeStruct((n_tok, MLA_HEADS * MLA_HW), BF16),
                   jax.ShapeDtypeStruct((MLA_HEADS, n_tok // tm, VT_ROWS, tm), BF16)],
        grid=(n_tok // tm,),
        in_specs=[pl.BlockSpec((tm, d), lambda i: (i, 0)), full(g2), full(wlat), full(qg2), full(wuq),
                  full(kvg2), full(wukv),
                  pl.BlockSpec((tm, LANES), lambda i: (i % seq_tiles, 0)),
                  pl.BlockSpec((tm, LANES), lambda i: (i % seq_tiles, 0))],
        out_specs=[pl.BlockSpec((tm, MLA_HEADS * MLA_HW), lambda i: (i, 0)),
                   pl.BlockSpec((tm, MLA_HEADS * MLA_HW), lambda i: (i, 0)),
                   pl.BlockSpec((MLA_HEADS, 1, VT_ROWS, tm), lambda i: (0, i, 0, 0))],
        compiler_params=_cparams(("parallel",)),
        name="mla_prep",
    )(x, g2, wlat, qg2, wuq, kvg2, wukv, cos_t, sin_t)


LOG2E = 1.4426950408889634
VT_TAIL = 16
VT_ROWS = LANES + VT_TAIL
FLASH_CHUNK = 128
FLASH_UNROLL = 2


def _vt_block(v):
    tb = v.shape[0]
    row = lax.broadcasted_iota(jnp.int32, (VT_TAIL, tb), 0)
    tail = jnp.where(row == 0, 1.0, 0.0).astype(F32)
    return jnp.concatenate([v.T, tail], axis=0).astype(BF16)


def _flash_finish_all(nh, n, tq, dv, acc_ref, gate_ref, y_ref):
    def body(c, _):
        rows = pl.ds(pl.multiple_of(c * tq, tq), tq)
        for g in range(nh):
            cols = slice(g * dv, (g + 1) * dv)
            a = acc_ref.at[g, c]
            o = (a[:LANES, :] / a[LANES:LANES + 1, :]).T
            y_ref[rows, cols] = (o * _silu(gate_ref[rows, cols].astype(F32))).astype(y_ref.dtype)
        return 0

    lax.fori_loop(0, n, body, 0)


def _flash_flat_t(nh, n, tq, score_fn, row_bias, vt_fn, s_ref, p_ref, acc_ref, m_ref):
    n_off = n * (n - 1) // 2
    assert n % 2 == 0 and n_off % 2 == 0
    acc_ref[...] = jnp.zeros_like(acc_ref)
    p_ref[...] = jnp.zeros_like(p_ref)
    m_ref[...] = jnp.full(m_ref.shape, NEG, F32)
    key = lax.broadcasted_iota(jnp.int32, (tq, tq), 0)
    qry = lax.broadcasted_iota(jnp.int32, (tq, tq), 1)

    def diag_scores(c):
        return lambda g: jnp.where(key <= qry, score_fn(g, c, c), NEG)

    def off_scores(j, c):
        return lambda g: score_fn(g, j, jnp.minimum(c, n - 1))

    def step(rd, wr, nxt, cur, prev, alphas):
        (jc, cc), (jp, cp) = cur, prev
        for g in range(nh):
            s_ref[wr, g] = nxt(g)
        pend = [jnp.dot(vt_fn(g, jp), p_ref[wr, g], preferred_element_type=F32) for g in range(nh)]
        out = []
        for g in range(nh):
            s_view = s_ref.at[rd, g]
            m_prev = m_ref[g, cc]
            nbs = [None] if row_bias is None else row_bias(g, jc, cc)
            hk = tq // len(nbs)
            m_new = m_prev
            for u, nb in enumerate(nbs):
                mb = functools.reduce(jnp.maximum, [
                    jnp.max(s_view[r0:r0 + FLASH_CHUNK, :], axis=0, keepdims=True)
                    for r0 in range(u * hk, (u + 1) * hk, FLASH_CHUNK)])
                m_new = jnp.maximum(m_new, mb if nb is None else mb + nb)
            m_ref[g, cc] = m_new
            acc_ref[g, cp] = alphas[g] * acc_ref[g, cp] + pend[g]
            for u, nb in enumerate(nbs):
                shift = m_new if nb is None else m_new - nb
                for r0 in range(u * hk, (u + 1) * hk, FLASH_CHUNK):
                    rows = slice(r0, r0 + FLASH_CHUNK)
                    p_ref[rd, g, rows, :] = jnp.exp2((s_view[rows, :] - shift).astype(BF16))
            out.append(jnp.exp2(m_prev - m_new))
        return tuple(out)

    def advance(j, c):
        wrap = j + 1 >= c
        return jnp.where(wrap, 0, j + 1), jnp.where(wrap, c + 1, c)

    for g in range(nh):
        s_ref[0, g] = diag_scores(0)(g)
    alphas = (jnp.ones((1, tq), F32),) * nh

    def diag_pair(u, alphas):
        i = 2 * u
        ip = jnp.maximum(i - 1, 0)
        alphas = step(0, 1, diag_scores(i + 1), (i, i), (ip, ip), alphas)
        return step(1, 0, diag_scores(i + 2), (i + 1, i + 1), (i, i), alphas)

    alphas = lax.fori_loop(0, (n - 2) // 2, diag_pair, alphas)
    alphas = step(0, 1, diag_scores(n - 1), (n - 2, n - 2), (n - 3, n - 3), alphas)
    alphas = step(1, 0, off_scores(0, 1), (n - 1, n - 1), (n - 2, n - 2), alphas)

    def off_steps(u, carry):
        cur, prev, alphas = carry[0:2], carry[2:4], carry[4:]
        for k in range(FLASH_UNROLL):
            nxt = advance(*cur)
            alphas = step(k % 2, 1 - k % 2, off_scores(*nxt), cur, prev, alphas)
            cur, prev = nxt, cur
        return tuple(cur) + tuple(prev) + tuple(alphas)

    assert n_off % FLASH_UNROLL == 0 and FLASH_UNROLL % 2 == 0
    i32 = jnp.int32
    carry = lax.fori_loop(0, n_off // FLASH_UNROLL, off_steps,
                          (i32(0), i32(1), i32(n - 1), i32(n - 1)) + tuple(alphas))
    jl, cl = carry[2], carry[3]
    for g in range(nh):
        pend = jnp.dot(vt_fn(g, jl), p_ref[1, g], preferred_element_type=F32)
        acc_ref[g, cl] = carry[4 + g] * acc_ref[g, cl] + pend


def _rows(i, t):
    return pl.ds(i * t, t) if isinstance(i, int) else pl.ds(pl.multiple_of(i * t, t), t)


def _flash_scratch(nh, n, tq):
    return [pltpu.VMEM((2, nh, tq, tq), F32), pltpu.VMEM((2, nh, tq, tq), BF16),
            pltpu.VMEM((nh, n, VT_ROWS, tq), F32), pltpu.VMEM((nh, n, 1, tq), F32)]


def _mla_kernel(tq, nh, q_ref, k_ref, vt_ref, gate_ref, y_ref, s_ref, p_ref, acc_ref, m_ref):
    n = SEQ // tq

    def scores(g, j, c):
        cols = slice(g * MLA_HW, (g + 1) * MLA_HW)
        return _qk(k_ref[_rows(j, tq), cols], q_ref[_rows(c, tq), cols])

    _flash_flat_t(nh, n, tq, scores, None, lambda g, j: vt_ref[g, j], s_ref, p_ref, acc_ref, m_ref)
    _flash_finish_all(nh, n, tq, MLA_V, acc_ref, gate_ref, y_ref)


def _mla_attn(qa, ka, vat, gate, tq, nh):
    nq = SEQ // tq
    return pl.pallas_call(
        functools.partial(_mla_kernel, tq, nh),
        out_shape=jax.ShapeDtypeStruct((N_TOK, MLA_HEADS * MLA_V), BF16),
        grid=(BATCH, MLA_HEADS // nh),
        in_specs=[pl.BlockSpec((SEQ, nh * MLA_HW), lambda b, h: (b, h)),
                  pl.BlockSpec((SEQ, nh * MLA_HW), lambda b, h: (b, h)),
                  pl.BlockSpec((nh, nq, VT_ROWS, tq), lambda b, h: (h, b, 0, 0)),
                  pl.BlockSpec((SEQ, nh * MLA_V), lambda b, h: (b, h))],
        out_specs=pl.BlockSpec((SEQ, nh * MLA_V), lambda b, h: (b, h)),
        scratch_shapes=_flash_scratch(nh, nq, tq),
        compiler_params=_cparams(("parallel", "parallel")),
        name="mla_attn",
    )(qa, ka, vat, gate)


MOBA_TILE = 2 * MOBA_BLOCK


def _moba_flat_kernel(nh, q_ref, k_ref, vt_ref, gate_ref, y_ref, km_ref, nb_ref, s_ref, p_ref, acc_ref, m_ref):
    L = MOBA_BLOCK
    T = MOBA_TILE
    D = MOBA_DH
    nblk = SEQ // L
    ntile = SEQ // T
    cols = [slice(g * D, (g + 1) * D) for g in range(nh)]

    for g in range(nh):
        for j in range(nblk):
            blk = k_ref[j * L:(j + 1) * L, cols[g]].astype(F32)
            km_ref[g, j:j + 1, :] = jnp.sum(blk, axis=0, keepdims=True) * (1.0 / L)

    blk_i = lax.broadcasted_iota(jnp.int32, (nblk, T), 0)
    blk_f = blk_i.astype(F32)
    second = lax.broadcasted_iota(jnp.int32, (nblk, T), 1) >= L

    def select(c, carry):
        own = 2 * c + second.astype(jnp.int32)
        past = blk_i < own
        for g in range(nh):
            gt = jnp.where(past, _qk(km_ref[g].astype(BF16), q_ref[_rows(c, T), cols[g]]), -jnp.inf)
            sel = jnp.zeros((nblk, T), jnp.bool_)
            for _ in range(MOBA_TOPK):
                mx = jnp.max(gt, axis=0, keepdims=True)
                first = jnp.min(jnp.where(gt == mx, blk_f, float(nblk)), axis=0, keepdims=True)
                pick = blk_f == first
                sel = jnp.logical_or(sel, pick)
                gt = jnp.where(pick, -jnp.inf, gt)
            keep = jnp.logical_or(jnp.logical_and(sel, past), blk_i == own)
            nb_ref[g, c] = jnp.where(keep, 0.0, NEG)
        return carry

    lax.fori_loop(0, ntile, select, 0)

    def scores(g, j, c):
        return _qk(k_ref[_rows(j, T), cols[g]], q_ref[_rows(c, T), cols[g]])

    def bias(g, j, c):
        return [nb_ref[g, c, pl.ds(2 * j, 1), :], nb_ref[g, c, pl.ds(2 * j + 1, 1), :]]

    _flash_flat_t(nh, ntile, T, scores, bias, lambda g, j: vt_ref[g, j], s_ref, p_ref, acc_ref, m_ref)
    _flash_finish_all(nh, ntile, T, D, acc_ref, gate_ref, y_ref)


def _moba_flat_attn(qb, kb, vbt, gate, nh):
    T = MOBA_TILE
    D = MOBA_DH
    nblk = SEQ // MOBA_BLOCK
    ntile = SEQ // T
    return pl.pallas_call(
        functools.partial(_moba_flat_kernel, nh),
        out_shape=jax.ShapeDtypeStruct((N_TOK, MOBA_HEADS * D), BF16),
        grid=(BATCH, MOBA_HEADS // nh),
        in_specs=[pl.BlockSpec((SEQ, nh * D), lambda b, h: (b, h)),
                  pl.BlockSpec((SEQ, nh * D), lambda b, h: (b, h)),
                  pl.BlockSpec((nh, ntile, VT_ROWS, T), lambda b, h: (h, b, 0, 0)),
                  pl.BlockSpec((SEQ, nh * D), lambda b, h: (b, h))],
        out_specs=pl.BlockSpec((SEQ, nh * D), lambda b, h: (b, h)),
        scratch_shapes=[pltpu.VMEM((nh, nblk, D), F32), pltpu.VMEM((nh, ntile, nblk, T), F32)]
        + _flash_scratch(nh, ntile, T),
        compiler_params=_cparams(("parallel", "parallel")),
        name="moba_attn",
    )(qb, kb, vbt, gate)


def _mem_prep_kernel(mem_ref, g_ref, w_ref, kv_ref):
    mn = _rms(mem_ref[...], g_ref[...]).astype(BF16)
    kv_ref[...] = jnp.dot(mn, w_ref[...], preferred_element_type=F32).astype(BF16)


def _mem_prep(mem2d, g, w):
    g2 = g.reshape(1, -1)
    return pl.pallas_call(
        _mem_prep_kernel,
        out_shape=jax.ShapeDtypeStruct((BATCH * N_MEM, 2 * MEMQ_W), BF16),
        grid=(BATCH,),
        in_specs=[pl.BlockSpec((N_MEM, D_MODEL), lambda b: (b, 0)),
                  pl.BlockSpec(g2.shape, lambda b: (0, 0)),
                  pl.BlockSpec(w.shape, lambda b: (0, 0))],
        out_specs=pl.BlockSpec((N_MEM, 2 * MEMQ_W), lambda b: (b, 0)),
        compiler_params=_cparams(("parallel",)),
        name="mem_prep",
    )(mem2d, g2, w)


def _mem_attn_kernel(q_ref, kv_ref, gate_ref, y_ref):
    cols = [slice(h * MEM_DH, (h + 1) * MEM_DH) for h in range(MEM_HEADS)]
    ss = [_qk(q_ref[:, c], kv_ref[:, c]) for c in cols]
    ps = [jnp.exp2(s - jnp.max(s, axis=1, keepdims=True)) for s in ss]
    for h, (c, p) in enumerate(zip(cols, ps)):
        l = jnp.sum(p, axis=1, keepdims=True)
        a = MEMQ_W + h * MEM_DH
        o = jnp.dot(p.astype(BF16), kv_ref[:, a:a + MEM_DH], preferred_element_type=F32) / l
        y_ref[:, c] = (o * _silu(gate_ref[:, c].astype(F32))).astype(y_ref.dtype)


def _mem_attn(qm, kvm, gate, tq):
    nq = SEQ // tq
    return pl.pallas_call(
        _mem_attn_kernel,
        out_shape=jax.ShapeDtypeStruct((N_TOK, MEMQ_W), BF16),
        grid=(BATCH, nq),
        in_specs=[pl.BlockSpec((tq, MEMQ_W), lambda b, i: (b * nq + i, 0)),
                  pl.BlockSpec((N_MEM, 2 * MEMQ_W), lambda b, i: (b, 0)),
                  pl.BlockSpec((tq, MEMQ_W), lambda b, i: (b * nq + i, 0))],
        out_specs=pl.BlockSpec((tq, MEMQ_W), lambda b, i: (b * nq + i, 0)),
        compiler_params=_cparams(("parallel", "parallel")),
        name="mem_attn",
    )(qm, kvm, gate)


def _band_bias(n, lo_off, hi_off):
    i = lax.broadcasted_iota(jnp.int32, (Q_BLOCK, 2 * Q_BLOCK), 0)
    c = lax.broadcasted_iota(jnp.int32, (Q_BLOCK, 2 * Q_BLOCK), 1)
    vis = (c - i >= lo_off) & (c - i <= hi_off) & ((c >= Q_BLOCK) | (n > 0))
    return jnp.where(vis, 0.0, NEG).astype(F32)


SWA_BLOCKS_PER_STEP = 4


def _swa_kernel(q_ref, kvp_ref, kvo_ref, gate_ref, sink_ref, y_ref):
    n = pl.program_id(1)
    for u in range(SWA_BLOCKS_PER_STEP):
        rows = slice(u * Q_BLOCK, (u + 1) * Q_BLOCK)
        prev = kvp_ref if u == 0 else kvo_ref.at[(u - 1) * Q_BLOCK:u * Q_BLOCK, :]
        _swa_block(n if u == 0 else 1, q_ref.at[rows, :], prev, kvo_ref.at[rows, :], gate_ref.at[rows, :],
                   sink_ref, y_ref.at[rows, :])


def _swa_block(n, q_ref, kvp_ref, kvo_ref, gate_ref, sink_ref, y_ref):
    QB = Q_BLOCK
    G = SWA_HEADS // SWA_KV_HEADS
    P = G // 2
    c = lax.broadcasted_iota(jnp.int32, (2 * QB, QB), 0)
    i = lax.broadcasted_iota(jnp.int32, (2 * QB, QB), 1)
    vis = (c - i >= QB - (SWA_WINDOW - 1)) & (c - i <= QB) & ((c >= QB) | (n > 0))
    bias = jnp.where(vis, 0.0, NEG).astype(F32)
    bias4 = jnp.concatenate([bias] * P, axis=1)
    k2 = jnp.concatenate([kvp_ref[:, :LANES], kvo_ref[:, :LANES]], axis=0).astype(F32)
    v2 = jnp.concatenate([kvp_ref[:, LANES:], kvo_ref[:, LANES:]], axis=0).astype(F32)
    k2r = pltpu.roll(k2, SWA_DH, 1)
    v2r = pltpu.roll(v2, SWA_DH, 1)
    lane = lax.broadcasted_iota(jnp.int32, (2 * QB, LANES), 1)
    lo = lane < SWA_DH
    top = lax.broadcasted_iota(jnp.int32, (LANES, P * QB), 0) < SWA_DH
    vts, ss = [], []
    for kv in range(SWA_KV_HEADS):
        ka, kb_ = (k2, k2r) if kv == 0 else (k2r, k2)
        va, vb_ = (v2, v2r) if kv == 0 else (v2r, v2)
        k_lo = jnp.where(lo, ka, 0.0).astype(BF16)
        k_hi = jnp.where(lo, 0.0, kb_).astype(BF16)
        vts.append((jnp.where(lo, va, 0.0).T.astype(BF16), jnp.where(lo, 0.0, vb_).T.astype(BF16)))
        base = kv * P
        q4 = jnp.concatenate([q_ref[:, (base + p) * LANES:(base + p + 1) * LANES] for p in range(P)],
                             axis=0)
        ss.append((_qk(k_lo, q4) + bias4, _qk(k_hi, q4) + bias4))
    stats = []
    for pair in ss:
        st = []
        for s in pair:
            m = jnp.max(s, axis=0, keepdims=True)
            p = jnp.exp2(s - m)
            st.append((m, p, jnp.sum(p, axis=0, keepdims=True)))
        stats.append(st)
    for kv in range(SWA_KV_HEADS):
        (m_lo, p_lo, l_lo), (m_hi, p_hi, l_hi) = stats[kv]
        vt_lo, vt_hi = vts[kv]
        base = kv * P
        ot = (jnp.dot(vt_lo, p_lo.astype(BF16), preferred_element_type=F32)
              + jnp.dot(vt_hi, p_hi.astype(BF16), preferred_element_type=F32))
        cols = slice(base * LANES, (base + P) * LANES)
        f_lo = 1.0 / ((1.0 + jnp.exp2(sink_ref[0:1, cols] * LOG2E - (m_lo + jnp.log2(l_lo)))) * l_lo)
        f_hi = 1.0 / ((1.0 + jnp.exp2(sink_ref[1:2, cols] * LOG2E - (m_hi + jnp.log2(l_hi)))) * l_hi)
        ot = ot * jnp.where(top, f_lo, f_hi)
        for p in range(P):
            cc = slice((base + p) * LANES, (base + p + 1) * LANES)
            o = ot[:, p * QB:(p + 1) * QB].T
            y_ref[:, cc] = (o * _silu(gate_ref[:, cc].astype(F32))).astype(y_ref.dtype)


def _swa_attn(qc, kvc, gate, sink_rows):
    R = SWA_BLOCKS_PER_STEP
    ns = SEQ // (R * Q_BLOCK)
    w = SWA_HEADS * SWA_DH
    return pl.pallas_call(
        _swa_kernel,
        out_shape=jax.ShapeDtypeStruct((N_TOK, w), BF16),
        grid=(BATCH, ns),
        in_specs=[pl.BlockSpec((R * Q_BLOCK, w), lambda b, n: (b * ns + n, 0)),
                  pl.BlockSpec((Q_BLOCK, 2 * LANES), lambda b, n: (jnp.maximum((b * ns + n) * R - 1, 0), 0)),
                  pl.BlockSpec((R * Q_BLOCK, 2 * LANES), lambda b, n: (b * ns + n, 0)),
                  pl.BlockSpec((R * Q_BLOCK, w), lambda b, n: (b * ns + n, 0)),
                  pl.BlockSpec((2, w), lambda b, n: (0, 0))],
        out_specs=pl.BlockSpec((R * Q_BLOCK, w), lambda b, n: (b * ns + n, 0)),
        compiler_params=_cparams(("parallel", "parallel")),
        name="swa_attn",
    )(qc, kvc, kvc, gate, sink_rows)


DIL_BLOCKS_PER_STEP = 8
DIL_GROUP = 2
MAX_STRIDE = 4


def _dil_kernel(R, q_ref, kp_ref, ko_ref, vp_ref, vo_ref, o_ref, lse_ref):
    QB = Q_BLOCK
    n = pl.program_id(2)
    lane = lax.broadcasted_iota(jnp.int32, (QB, LANES), 1)
    for u0 in range(0, R, DIL_GROUP):
        tasks = []
        for u in range(u0, u0 + DIL_GROUP):
            rows = slice(u * QB, (u + 1) * QB)
            bias = _band_bias(n if u == 0 else 1, 0, QB)
            for h in range(DIL_HEADS):
                c = slice(h * DIL_DH, (h + 1) * DIL_DH)
                k_prev = kp_ref[:, c] if u == 0 else ko_ref[(u - 1) * QB:u * QB, c]
                v_prev = vp_ref[:, c] if u == 0 else vo_ref[(u - 1) * QB:u * QB, c]
                k = jnp.concatenate([k_prev, ko_ref[rows, c]], axis=0)
                v = jnp.concatenate([v_prev, vo_ref[rows, c]], axis=0)
                tasks.append((u, h, _qk(q_ref[rows, c], k) + bias, v))
        soft = []
        for u, h, s, v in tasks:
            m = jnp.max(s, axis=1, keepdims=True)
            p = jnp.exp2(s - m)
            soft.append((m, p, jnp.sum(p, axis=1, keepdims=True)))
        lse_all = {u: jnp.zeros((QB, LANES), F32) for u in range(u0, u0 + DIL_GROUP)}
        for (u, h, _, v), (m, p, l) in zip(tasks, soft):
            o_ref[u * QB:(u + 1) * QB, h * DIL_DH:(h + 1) * DIL_DH] = (
                jnp.dot(p.astype(BF16), v, preferred_element_type=F32) / l).astype(o_ref.dtype)
            lse_all[u] = jnp.where(lane == h, m + jnp.log2(l), lse_all[u])
        for u, val in lse_all.items():
            lse_ref[u * QB:(u + 1) * QB, :] = val


def _dil_attn(qv, kv, vv, dil):
    w = DIL_HEADS * DIL_DH
    L = SEQ // dil
    R = min(DIL_BLOCKS_PER_STEP, L // Q_BLOCK)
    ns = L // (R * Q_BLOCK)
    cur = lambda b, r, n: (b * ns + n, r)
    prev = lambda b, r, n: (jnp.maximum((b * ns + n) * R - 1, 0), r)
    blk = (R * Q_BLOCK, w)
    pblk = (Q_BLOCK, w)
    return pl.pallas_call(
        functools.partial(_dil_kernel, R),
        out_shape=[jax.ShapeDtypeStruct((BATCH * L, dil * w), BF16),
                   jax.ShapeDtypeStruct((BATCH * L, dil * LANES), F32)],
        grid=(BATCH, dil, ns),
        in_specs=[pl.BlockSpec(blk, cur), pl.BlockSpec(pblk, prev), pl.BlockSpec(blk, cur),
                  pl.BlockSpec(pblk, prev), pl.BlockSpec(blk, cur)],
        out_specs=[pl.BlockSpec(blk, cur), pl.BlockSpec((R * Q_BLOCK, LANES), cur)],
        compiler_params=_cparams(("parallel", "parallel", "parallel")),
        name=f"dil_attn_d{dil}",
    )(qv, kv, kv, vv, vv)


def _dil_combine_kernel(tm, *refs):
    np_ = len(DIL_PATTERNS)
    o_refs, l_refs = refs[:np_], refs[np_:2 * np_]
    gate_ref, y_ref = refs[2 * np_], refs[2 * np_ + 1]
    stage = refs[2 * np_ + 2:]
    w = DIL_HEADS * DIL_DH
    os_, ls = [], []
    k = 0
    for (_, d), o_ref, l_ref in zip(DIL_PATTERNS, o_refs, l_refs):
        if d == 1:
            os_.append(lambda h, o_ref=o_ref: o_ref[:, h * DIL_DH:(h + 1) * DIL_DH].astype(F32))
            ls.append(l_ref[...])
            continue
        so, sl = stage[2 * k], stage[2 * k + 1]
        k += 1
        if d <= MAX_STRIDE:
            hops = [(so, sl, d, lambda r: r)]
        else:
            assert d == MAX_STRIDE * MAX_STRIDE
            co, cl = stage[-2], stage[-1]
            q4 = tm // MAX_STRIDE
            hops = [(co, cl, MAX_STRIDE, lambda r: (r % MAX_STRIDE) * q4 + r // MAX_STRIDE)]
        dst_o, dst_l, stride, start = hops[0]
        for r in range(d):
            rows = pl.ds(start(r), tm // d, stride=stride)
            for h in range(DIL_HEADS):
                a = r * w + h * DIL_DH
                dst_o[h, rows, :] = o_ref[:, a:a + DIL_DH].astype(F32)
            dst_l[rows, :] = l_ref[:, r * LANES:(r + 1) * LANES]
        if d > MAX_STRIDE:
            for r in range(MAX_STRIDE):
                rows = pl.ds(r, q4, stride=MAX_STRIDE)
                blk = slice(r * q4, (r + 1) * q4)
                for h in range(DIL_HEADS):
                    so[h, rows, :] = co[h, blk, :]
                sl[rows, :] = cl[blk, :]
        os_.append(lambda h, so=so: so[h])
        ls.append(sl[...])
    mx = functools.reduce(jnp.maximum, ls)
    es = [jnp.exp2(x - mx) for x in ls]
    den = functools.reduce(lambda a, b: a + b, es)
    ws = [e / den for e in es]
    lane = lax.broadcasted_iota(jnp.int32, ws[0].shape, 1)
    for h in range(DIL_HEADS):
        c = slice(h * DIL_DH, (h + 1) * DIL_DH)
        wh = [jnp.sum(jnp.where(lane == h, wgt, 0.0), axis=1, keepdims=True) for wgt in ws]
        o = functools.reduce(lambda a, b: a + b, [wh[p] * os_[p](h) for p in range(np_)])
        y_ref[:, c] = (o * _silu(gate_ref[:, c].astype(F32))).astype(y_ref.dtype)


def _dil_combine(outs, lses, gate, tm):
    w = DIL_HEADS * DIL_DH
    in_specs = [pl.BlockSpec((tm // d, d * w), lambda i: (i, 0)) for _, d in DIL_PATTERNS]
    in_specs += [pl.BlockSpec((tm // d, d * LANES), lambda i: (i, 0)) for _, d in DIL_PATTERNS]
    in_specs.append(pl.BlockSpec((tm, w), lambda i: (i, 0)))
    scratch = []
    for _, d in DIL_PATTERNS:
        if d > 1:
            scratch += [pltpu.VMEM((DIL_HEADS, tm, DIL_DH), F32), pltpu.VMEM((tm, LANES), F32)]
    if any(d > MAX_STRIDE for _, d in DIL_PATTERNS):
        scratch += [pltpu.VMEM((DIL_HEADS, tm, DIL_DH), F32), pltpu.VMEM((tm, LANES), F32)]
    return pl.pallas_call(
        functools.partial(_dil_combine_kernel, tm),
        out_shape=jax.ShapeDtypeStruct((N_TOK, w), BF16),
        grid=(N_TOK // tm,),
        in_specs=in_specs,
        out_specs=pl.BlockSpec((tm, w), lambda i: (i, 0)),
        scratch_shapes=scratch,
        compiler_params=_cparams(("parallel",)),
        name="dil_combine",
    )(*outs, *lses, gate)


def _outproj_kernel(nparts, final, *refs):
    ys, ws = refs[:nparts], refs[nparts:2 * nparts]
    x_ref = refs[2 * nparts]
    o_ref = refs[-1]
    acc = x_ref[...]
    for y, w in zip(ys, ws):
        acc = acc + jnp.dot(y[...], w[...], preferred_element_type=F32)
    if final:
        acc = _rms(acc, refs[2 * nparts + 1][...])
    o_ref[...] = acc


def _outproj(ys, ws, x, final_g, tm):
    n_tok, d = x.shape
    in_specs = [pl.BlockSpec((tm, y.shape[1]), lambda i: (i, 0)) for y in ys]
    in_specs += [pl.BlockSpec(w.shape, lambda i: (0, 0)) for w in ws]
    in_specs.append(pl.BlockSpec((tm, d), lambda i: (i, 0)))
    args = [*ys, *ws, x]
    if final_g is not None:
        in_specs.append(pl.BlockSpec((1, d), lambda i: (0, 0)))
        args.append(final_g.reshape(1, d))
    return pl.pallas_call(
        functools.partial(_outproj_kernel, len(ys), final_g is not None),
        out_shape=jax.ShapeDtypeStruct((n_tok, d), F32),
        grid=(n_tok // tm,),
        in_specs=in_specs,
        out_specs=pl.BlockSpec((tm, d), lambda i: (i, 0)),
        compiler_params=_cparams(("parallel",)),
        name="outproj",
    )(*args)


PROJ_TM = 512
PROJ_TM_WIDE = 1024
MEM_TQ = 1024
ATTN_TQ = 512
MLA_HEADS_PER_STEP = 2
MOBA_HEADS_PER_STEP = 2


def _even_layer(x, mem2d, norm_g, w_in, q_norm_g, w_uq, kv_norm_g, w_ukv, mem_norm_g, w_mem_kv, w_out):
    n_lat = MLA_Q_RANK + MLA_KV_RANK + MLA_ROPE
    wlat = jnp.pad(w_in[:, :n_lat], ((0, 0), (0, LANES - MLA_ROPE))).astype(BF16)
    n_bqkv = n_lat + 3 * MOBA_HEADS * MOBA_DH
    w_bqkv = w_in[:, n_lat:n_bqkv].astype(BF16)
    w_rest = w_in[:, n_bqkv:].astype(BF16)
    wuq = w_uq.reshape(MLA_Q_RANK, MLA_HEADS, MLA_NOPE + MLA_ROPE)
    wuq = jnp.pad(wuq, ((0, 0), (0, 0), (0, MLA_HW - MLA_NOPE - MLA_ROPE)))
    wuq = wuq.reshape(MLA_Q_RANK, MLA_HEADS * MLA_HW).astype(BF16)
    wukv = w_ukv.reshape(MLA_KV_RANK, MLA_HEADS, MLA_NOPE + MLA_V)
    wukv = jnp.concatenate([wukv[:, :, :MLA_NOPE].reshape(MLA_KV_RANK, -1),
                            wukv[:, :, MLA_NOPE:].reshape(MLA_KV_RANK, -1)], axis=1).astype(BF16)

    cos_a, sin_a = _rope_tables(MLA_ROPE, LANES)
    qa, ka, va = _mla_prep(x, norm_g, wlat, q_norm_g, wuq, kv_norm_g, wukv, cos_a, sin_a, ATTN_TQ)

    rot = MOBA_DH // ROT_FRAC
    cos_b, sin_b = _rope_tables(rot, MOBA_DH)
    sc = MOBA_DH ** -0.5 * LOG2E
    wb = MOBA_HEADS * MOBA_DH
    plain = [("plain",)]
    groups_qkv = [
        (wb, [("rope", 0, rot // 2, MOBA_DH, sc)] * MOBA_HEADS, "rows"),
        (wb, [("rope", 0, rot // 2, MOBA_DH, 1.0)] * MOBA_HEADS, "rows"),
        (wb, plain * MOBA_HEADS, ("vT", MOBA_TILE)),
    ]
    groups_rest = [
        (MEMQ_W, [("scale", MEM_DH ** -0.5 * LOG2E)] * MEM_HEADS, "rows"),
        (MLA_HEADS * MLA_V, plain * MLA_HEADS, "rows"),
        (wb, plain * MOBA_HEADS, "rows"),
        (MEMQ_W, plain * MEM_HEADS, "rows"),
    ]
    qb, kb, vb = _inproj(x, norm_g, w_bqkv, groups_qkv, [cos_b, sin_b], PROJ_TM)
    qm, gate_a, gate_b, gate_m = _inproj(x, norm_g, w_rest, groups_rest, [], PROJ_TM_WIDE)

    y_a = _mla_attn(qa, ka, va, gate_a, ATTN_TQ, MLA_HEADS_PER_STEP)
    y_b = _moba_flat_attn(qb, kb, vb, gate_b, MOBA_HEADS_PER_STEP)
    kvm = _mem_prep(mem2d, mem_norm_g, w_mem_kv.astype(BF16))
    y_m = _mem_attn(qm, kvm, gate_m, MEM_TQ)

    a, b = MLA_HEADS * MLA_V, MLA_HEADS * MLA_V + MOBA_HEADS * MOBA_DH
    return [y_a, y_b, y_m], [w_out[:a].astype(BF16), w_out[a:b].astype(BF16), w_out[b:].astype(BF16)]


def _odd_layer(x, mem2d, norm_g, w_in, sinks, mem_norm_g, w_mem_kv, w_out):
    rot_c = SWA_DH // ROT_FRAC
    rot_d = DIL_DH // ROT_FRAC
    cos_c, sin_c = _rope_tables(rot_c, SWA_DH)
    cos_d, sin_d = _rope_tables(rot_d, DIL_DH)
    wq = SWA_HEADS * SWA_DH
    wd = DIL_HEADS * DIL_DH
    wkv = 2 * SWA_KV_HEADS * SWA_DH
    rc = lambda s: ("rope", 0, rot_c // 2, SWA_DH, s)
    rd = lambda s: ("rope", 1, rot_d // 2, DIL_DH, s)
    plain = [("plain",)]
    groups_qkv = [
        (wq, [rc(SWA_DH ** -0.5 * LOG2E)] * (wq // LANES), "rows"),
        (wkv, [rc(1.0), ("plain",)], "rows"),
        (wd, [rd(DIL_DH ** -0.5 * LOG2E)] * DIL_HEADS, "dilated"),
        (wd, [rd(1.0)] * DIL_HEADS, "dilated"),
        (wd, plain * DIL_HEADS, "dilated"),
    ]
    groups_rest = [
        (MEMQ_W, [("scale", MEM_DH ** -0.5 * LOG2E)] * MEM_HEADS, "rows"),
        (wq, plain * (wq // LANES), "rows"),
        (wd, plain * DIL_HEADS, "rows"),
        (MEMQ_W, plain * MEM_HEADS, "rows"),
    ]
    n_qkv = wq + wkv + 3 * wd
    res = _inproj(x, norm_g, w_in[:, :n_qkv].astype(BF16), groups_qkv, [cos_c, sin_c, cos_d, sin_d], PROJ_TM)
    qc, kvc = res[0], res[1]
    nv = 1 + len(DIL_VIEWS)
    qd, kd, vd = (res[2 + i * nv:2 + (i + 1) * nv] for i in range(3))
    qm, gate_c, gate_d, gate_m = _inproj(x, norm_g, w_in[:, n_qkv:].astype(BF16), groups_rest, [], PROJ_TM_WIDE)

    sink_rows = jnp.repeat(sinks.astype(F32).reshape(SWA_HEADS // 2, 2).T, LANES, axis=1)
    y_c = _swa_attn(qc, kvc, gate_c, sink_rows)
    view_of = {1: 0, **{d: 1 + i for i, d in enumerate(DIL_VIEWS)}}
    outs, lses = zip(*[_dil_attn(qd[view_of[dil]], kd[view_of[dil]], vd[view_of[dil]], dil)
                       for _, dil in DIL_PATTERNS])
    y_d = _dil_combine(outs, lses, gate_d, PROJ_TM)
    kvm = _mem_prep(mem2d, mem_norm_g, w_mem_kv.astype(BF16))
    y_m = _mem_attn(qm, kvm, gate_m, MEM_TQ)

    return [y_c, y_d, y_m], [w_out[:wq].astype(BF16), w_out[wq:wq + wd].astype(BF16),
                             w_out[wq + wd:].astype(BF16)]


def kernel(x, mem, ev_norm_g, ev_w_in, ev_q_norm_g, ev_w_uq, ev_kv_norm_g, ev_w_ukv, ev_mem_norm_g,
           ev_w_mem_kv, ev_w_out, od_norm_g, od_w_in, od_sinks, od_mem_norm_g, od_w_mem_kv, od_w_out,
           final_norm_g):
    x2 = x.reshape(N_TOK, D_MODEL)
    mem2d = mem.reshape(BATCH * N_MEM, D_MODEL)
    ys, ws = _even_layer(x2, mem2d, ev_norm_g[0], ev_w_in[0], ev_q_norm_g[0], ev_w_uq[0], ev_kv_norm_g[0],
                         ev_w_ukv[0], ev_mem_norm_g[0], ev_w_mem_kv[0], ev_w_out[0])
    x2 = _outproj(ys, ws, x2, None, PROJ_TM)
    ys, ws = _odd_layer(x2, mem2d, od_norm_g[0], od_w_in[0], od_sinks[0], od_mem_norm_g[0],
                        od_w_mem_kv[0], od_w_out[0])
    x2 = _outproj(ys, ws, x2, final_norm_g, PROJ_TM)
    return x2.reshape(BATCH, SEQ, D_MODEL)
```

```python
import functools

import numpy as np
import jax
import jax.numpy as jnp
from jax import lax
from jax.experimental import pallas as pl
from jax.experimental.pallas import tpu as pltpu

D_MODEL = 2048
BATCH = 4
SEQ = 4096
N_TOK = BATCH * SEQ
N_MEM = 256
ROPE_THETA = 500000.0
ROT_FRAC = 4
EPS = 1e-6

MLA_HEADS = 8
MLA_Q_RANK = 512
MLA_KV_RANK = 256
MLA_NOPE = 128
MLA_ROPE = 64
MLA_V = 128
MOBA_HEADS = 8
MOBA_DH = 128
MOBA_BLOCK = 256
MOBA_TOPK = 3
SWA_HEADS = 16
SWA_KV_HEADS = 2
SWA_DH = 64
SWA_WINDOW = 128
DIL_HEADS = 6
DIL_DH = 128
DIL_PATTERNS = ((128, 1), (512, 4), (2048, 16))
MEM_HEADS = 4
MEM_DH = 128
MEMQ_W = MEM_HEADS * MEM_DH
Q_BLOCK = 128

LANES = 128
VMEM_LIMIT = 52 * 1024 * 1024

NEG = -1e30
BF16 = jnp.bfloat16
F32 = jnp.float32


def _cparams(sem):
    return pltpu.CompilerParams(dimension_semantics=sem, vmem_limit_bytes=VMEM_LIMIT)


def _rms(x, g):
    ms = jnp.mean(x * x, axis=-1, keepdims=True)
    return (x * lax.rsqrt(ms + EPS)) * g


def _silu(g):
    return g / (1.0 + jnp.exp(-g))


def _rope_slab(x, cos_t, sin_t, half, period):
    lane = lax.broadcasted_iota(jnp.int32, x.shape, 1)
    up = pltpu.roll(x, LANES - half, 1)
    dn = pltpu.roll(x, half, 1)
    sw = jnp.where((lane & (period - 1)) < half, up, dn)
    return x * cos_t + sw * sin_t


def _qk(q, k):
    return lax.dot_general(q, k, (((1,), (1,)), ((), ())), preferred_element_type=F32)


def _rope_tables(rot_dim, period):
    half = rot_dim // 2
    inv = 1.0 / (ROPE_THETA ** (jnp.arange(0, rot_dim, 2, dtype=F32) / rot_dim))
    ang = jnp.arange(SEQ, dtype=F32)[:, None] * inv[None, :]
    c, s = jnp.cos(ang), jnp.sin(ang)
    d = np.arange(LANES) % period
    idx = d % half
    cos_t = jnp.where(d < rot_dim, c[:, idx], 1.0)
    sin_t = jnp.where(d < half, -s[:, idx], jnp.where(d < rot_dim, s[:, idx], 0.0))
    return cos_t.astype(F32), sin_t.astype(F32)


PROJ_CHUNK = 512
DIL_VIEWS = tuple(d for _, d in DIL_PATTERNS if d > 1)


def _inproj_kernel(groups, ntab, tm, x_ref, g_ref, *rest):
    tabs = rest[:2 * ntab]
    w_ref = rest[2 * ntab]
    ndil = sum(layout == "dilated" for _, _, layout in groups)
    nscr = ndil + (1 if ndil else 0)
    refs = list(rest[2 * ntab + 1:len(rest) - nscr])
    stage = list(rest[len(rest) - nscr:])
    chain_ref = stage.pop() if ndil else None
    xn = _rms(x_ref[...], g_ref[...]).astype(BF16)
    c0 = 0
    for ncols, kinds, layout in groups:
        o_ref = refs.pop(0)
        views = [refs.pop(0) for _ in DIL_VIEWS] if layout == "dilated" else []
        st_ref = stage.pop(0) if layout == "dilated" else None
        for s0 in range(0, ncols, PROJ_CHUNK):
            n = min(PROJ_CHUNK, ncols - s0)
            acc = jnp.dot(xn, w_ref[:, c0 + s0:c0 + s0 + n], preferred_element_type=F32)
            for c in range(n // LANES):
                col = s0 + c * LANES
                kind = kinds[col // LANES]
                piece = acc[:, c * LANES:(c + 1) * LANES]
                if kind[0] == "rope":
                    _, ti, half, period, sc = kind
                    piece = _rope_slab(piece, tabs[2 * ti][...], tabs[2 * ti + 1][...], half, period)
                    if sc != 1.0:
                        piece = piece * sc
                elif kind[0] == "scale":
                    piece = piece * kind[1]
                if layout == "rows" or layout == "dilated":
                    o_ref[:, col:col + LANES] = piece.astype(o_ref.dtype)
                    if layout == "dilated":
                        st_ref[col // LANES] = piece
                else:
                    tb = layout[1]
                    for u in range(tm // tb):
                        o_ref[col // LANES, u] = _vt_block(piece[u * tb:(u + 1) * tb, :])
        src, src_d = st_ref, 1
        for vi, (d, v_ref) in enumerate(zip(DIL_VIEWS, views)):
            keep = chain_ref if vi + 1 < len(views) else None
            rows_src, rows_dst = tm // src_d, tm // d
            for r in range(d):
                start = (r % src_d) * rows_src + r // src_d
                for c in range(ncols // LANES):
                    blk = src[c, pl.ds(start, rows_dst, stride=d // src_d), :]
                    a = r * ncols + c * LANES
                    v_ref[:, a:a + LANES] = blk.astype(v_ref.dtype)
                    if keep is not None:
                        keep[c, r * rows_dst:(r + 1) * rows_dst, :] = blk
            src, src_d = keep, d
        c0 += ncols


def _inproj(x, g, w, groups, tables, tm):
    n_tok, d = x.shape
    assert sum(nc for nc, _, _ in groups) == w.shape[1]
    seq_tiles = SEQ // tm
    in_specs = [pl.BlockSpec((tm, d), lambda i: (i, 0)),
                pl.BlockSpec((1, d), lambda i: (0, 0))]
    args = [x, g.reshape(1, d)]
    for t in tables:
        in_specs.append(pl.BlockSpec((tm, LANES), lambda i: (i % seq_tiles, 0)))
        args.append(t)
    in_specs.append(pl.BlockSpec(w.shape, lambda i: (0, 0), pipeline_mode=pl.Buffered(1)))
    args.append(w)
    out_specs, out_shapes = [], []
    for nc, _, layout in groups:
        if layout == "rows" or layout == "dilated":
            out_specs.append(pl.BlockSpec((tm, nc), lambda i: (i, 0)))
            out_shapes.append(jax.ShapeDtypeStruct((n_tok, nc), BF16))
            if layout == "dilated":
                for dil in DIL_VIEWS:
                    out_specs.append(pl.BlockSpec((tm // dil, dil * nc), lambda i: (i, 0)))
                    out_shapes.append(jax.ShapeDtypeStruct((n_tok // dil, dil * nc), BF16))
        else:
            tb = layout[1]
            out_specs.append(pl.BlockSpec((nc // LANES, tm // tb, VT_ROWS, tb), lambda i: (0, i, 0, 0)))
            out_shapes.append(jax.ShapeDtypeStruct((nc // LANES, n_tok // tb, VT_ROWS, tb), BF16))
    return pl.pallas_call(
        functools.partial(_inproj_kernel, groups, len(tables) // 2, tm),
        out_shape=out_shapes,
        grid=(n_tok // tm,),
        in_specs=in_specs,
        out_specs=out_specs,
        scratch_shapes=(lambda dil: dil + dil[:1])(
            [pltpu.VMEM((nc // LANES, tm, LANES), F32) for nc, _, layout in groups if layout == "dilated"]),
        compiler_params=_cparams(("parallel",)),
        name="inproj",
    )(*args)


MLA_HW = 256


def _mla_prep_kernel(scale, x_ref, g_ref, wlat_ref, qg_ref, wuq_ref, kvg_ref, wukv_ref,
                     cos_ref, sin_ref, qa_ref, ka_ref, va_ref):
    tm = x_ref.shape[0]
    halves = [slice(u * (tm // 2), (u + 1) * (tm // 2)) for u in range(2)]
    xns = [_rms(x_ref[r, :], g_ref[...]).astype(BF16) for r in halves]
    lats = [jnp.dot(xn, wlat_ref[...], preferred_element_type=F32) for xn in xns]
    cqs = [_rms(lat[:, :MLA_Q_RANK], qg_ref[...]).astype(BF16) for lat in lats]
    ckvs = [_rms(lat[:, MLA_Q_RANK:MLA_Q_RANK + MLA_KV_RANK], kvg_ref[...]).astype(BF16) for lat in lats]
    qs = [jnp.dot(cq, wuq_ref[...], preferred_element_type=F32) for cq in cqs]
    kvs = [jnp.dot(ckv, wukv_ref[...], preferred_element_type=F32) for ckv in ckvs]
    for r, lat, q, kv in zip(halves, lats, qs, kvs):
        cos_t, sin_t = cos_ref[r, :], sin_ref[r, :]
        kpe = lat[:, MLA_Q_RANK + MLA_KV_RANK:]
        kpe = _rope_slab(kpe, cos_t, sin_t, MLA_ROPE // 2, LANES).astype(BF16)
        for h in range(MLA_HEADS):
            a = h * MLA_HW
            qa_ref[r, a:a + LANES] = (q[:, a:a + LANES] * scale).astype(BF16)
            qr = _rope_slab(q[:, a + LANES:a + 2 * LANES], cos_t, sin_t, MLA_ROPE // 2, LANES)
            qa_ref[r, a + LANES:a + 2 * LANES] = (qr * scale).astype(BF16)
            ka_ref[r, a:a + LANES] = kv[:, h * LANES:(h + 1) * LANES].astype(BF16)
            ka_ref[r, a + LANES:a + 2 * LANES] = kpe
            v0 = MLA_HEADS * MLA_NOPE + h * MLA_V
            va_ref[h, 0, :, r] = _vt_block(kv[:, v0:v0 + MLA_V])


def _mla_prep(x, g, wlat, qg, wuq, kvg, wukv, cos_t, sin_t, tm):
    n_tok, d = x.shape
    seq_tiles = SEQ // tm
    full = lambda a: pl.BlockSpec(a.shape, lambda i: (0, 0))
    qg2, kvg2, g2 = qg.reshape(1, -1), kvg.reshape(1, -1), g.reshape(1, d)
    scale = (MLA_NOPE + MLA_ROPE) ** -0.5 * LOG2E
    return pl.pallas_call(
        functools.partial(_mla_prep_kernel, scale),
        out_shape=[jax.ShapeDtypeStruct((n_tok, MLA_HEADS * MLA_HW), BF16),
                   jax.ShapeDtypeStruct((n_tok, MLA_HEADS * MLA_HW), BF16),
                   jax.ShapeDtypeStruct((MLA_HEADS, n_tok // tm, VT_ROWS, tm), BF16)],
        grid=(n_tok // tm,),
        in_specs=[pl.BlockSpec((tm, d), lambda i: (i, 0)), full(g2), full(wlat), full(qg2), full(wuq),
                  full(kvg2), full(wukv),
                  pl.BlockSpec((tm, LANES), lambda i: (i % seq_tiles, 0)),
                  pl.BlockSpec((tm, LANES), lambda i: (i % seq_tiles, 0))],
        out_specs=[pl.BlockSpec((tm, MLA_HEADS * MLA_HW), lambda i: (i, 0)),
                   pl.BlockSpec((tm, MLA_HEADS * MLA_HW), lambda i: (i, 0)),
                   pl.BlockSpec((MLA_HEADS, 1, VT_ROWS, tm), lambda i: (0, i, 0, 0))],
        compiler_params=_cparams(("parallel",)),
        name="mla_prep",
    )(x, g2, wlat, qg2, wuq, kvg2, wukv, cos_t, sin_t)


LOG2E = 1.4426950408889634
VT_TAIL = 16
VT_ROWS = LANES + VT_TAIL
FLASH_CHUNK = 128
FLASH_UNROLL = 2


def _vt_block(v):
    tb = v.shape[0]
    row = lax.broadcasted_iota(jnp.int32, (VT_TAIL, tb), 0)
    tail = jnp.where(row == 0, 1.0, 0.0).astype(F32)
    return jnp.concatenate([v.T, tail], axis=0).astype(BF16)


def _flash_finish_all(nh, n, tq, dv, acc_ref, gate_ref, y_ref):
    def body(c, _):
        rows = pl.ds(pl.multiple_of(c * tq, tq), tq)
        for g in range(nh):
            cols = slice(g * dv, (g + 1) * dv)
            a = acc_ref.at[g, c]
            o = (a[:LANES, :] / a[LANES:LANES + 1, :]).T
            y_ref[rows, cols] = (o * _silu(gate_ref[rows, cols].astype(F32))).astype(y_ref.dtype)
        return 0

    lax.fori_loop(0, n, body, 0)


def _flash_flat_t(nh, n, tq, score_fn, row_bias, vt_fn, s_ref, p_ref, acc_ref, m_ref):
    n_off = n * (n - 1) // 2
    assert n % 2 == 0 and n_off % 2 == 0
    acc_ref[...] = jnp.zeros_like(acc_ref)
    p_ref[...] = jnp.zeros_like(p_ref)
    m_ref[...] = jnp.full(m_ref.shape, NEG, F32)
    key = lax.broadcasted_iota(jnp.int32, (tq, tq), 0)
    qry = lax.broadcasted_iota(jnp.int32, (tq, tq), 1)

    def diag_scores(c):
        return lambda g: jnp.where(key <= qry, score_fn(g, c, c), NEG)

    def off_scores(j, c):
        return lambda g: score_fn(g, j, jnp.minimum(c, n - 1))

    def step(rd, wr, nxt, cur, prev, alphas):
        (jc, cc), (jp, cp) = cur, prev
        for g in range(nh):
            s_ref[wr, g] = nxt(g)
        pend = [jnp.dot(vt_fn(g, jp), p_ref[wr, g], preferred_element_type=F32) for g in range(nh)]
        out = []
        for g in range(nh):
            s_view = s_ref.at[rd, g]
            m_prev = m_ref[g, cc]
            nbs = [None] if row_bias is None else row_bias(g, jc, cc)
            hk = tq // len(nbs)
            m_new = m_prev
            for u, nb in enumerate(nbs):
                mb = functools.reduce(jnp.maximum, [
                    jnp.max(s_view[r0:r0 + FLASH_CHUNK, :], axis=0, keepdims=True)
                    for r0 in range(u * hk, (u + 1) * hk, FLASH_CHUNK)])
                m_new = jnp.maximum(m_new, mb if nb is None else mb + nb)
            m_ref[g, cc] = m_new
            acc_ref[g, cp] = alphas[g] * acc_ref[g, cp] + pend[g]
            for u, nb in enumerate(nbs):
                shift = m_new if nb is None else m_new - nb
                for r0 in range(u * hk, (u + 1) * hk, FLASH_CHUNK):
                    rows = slice(r0, r0 + FLASH_CHUNK)
                    p_ref[rd, g, rows, :] = jnp.exp2((s_view[rows, :] - shift).astype(BF16))
            out.append(jnp.exp2(m_prev - m_new))
        return tuple(out)

    def advance(j, c):
        wrap = j + 1 >= c
        return jnp.where(wrap, 0, j + 1), jnp.where(wrap, c + 1, c)

    for g in range(nh):
        s_ref[0, g] = diag_scores(0)(g)
    alphas = (jnp.ones((1, tq), F32),) * nh

    def diag_pair(u, alphas):
        i = 2 * u
        ip = jnp.maximum(i - 1, 0)
        alphas = step(0, 1, diag_scores(i + 1), (i, i), (ip, ip), alphas)
        return step(1, 0, diag_scores(i + 2), (i + 1, i + 1), (i, i), alphas)

    alphas = lax.fori_loop(0, (n - 2) // 2, diag_pair, alphas)
    alphas = step(0, 1, diag_scores(n - 1), (n - 2, n - 2), (n - 3, n - 3), alphas)
    alphas = step(1, 0, off_scores(0, 1), (n - 1, n - 1), (n - 2, n - 2), alphas)

    def off_steps(u, carry):
        cur, prev, alphas = carry[0:2], carry[2:4], carry[4:]
        for k in range(FLASH_UNROLL):
            nxt = advance(*cur)
            alphas = step(k % 2, 1 - k % 2, off_scores(*nxt), cur, prev, alphas)
            cur, prev = nxt, cur
        return tuple(cur) + tuple(prev) + tuple(alphas)

    assert n_off % FLASH_UNROLL == 0 and FLASH_UNROLL % 2 == 0
    i32 = jnp.int32
    carry = lax.fori_loop(0, n_off // FLASH_UNROLL, off_steps,
                          (i32(0), i32(1), i32(n - 1), i32(n - 1)) + tuple(alphas))
    jl, cl = carry[2], carry[3]
    for g in range(nh):
        pend = jnp.dot(vt_fn(g, jl), p_ref[1, g], preferred_element_type=F32)
        acc_ref[g, cl] = carry[4 + g] * acc_ref[g, cl] + pend


def _rows(i, t):
    return pl.ds(i * t, t) if isinstance(i, int) else pl.ds(pl.multiple_of(i * t, t), t)


def _flash_scratch(nh, n, tq):
    return [pltpu.VMEM((2, nh, tq, tq), F32), pltpu.VMEM((2, nh, tq, tq), BF16),
            pltpu.VMEM((nh, n, VT_ROWS, tq), F32), pltpu.VMEM((nh, n, 1, tq), F32)]


def _mla_kernel(tq, nh, q_ref, k_ref, vt_ref, gate_ref, y_ref, s_ref, p_ref, acc_ref, m_ref):
    n = SEQ // tq

    def scores(g, j, c):
        cols = slice(g * MLA_HW, (g + 1) * MLA_HW)
        return _qk(k_ref[_rows(j, tq), cols], q_ref[_rows(c, tq), cols])

    _flash_flat_t(nh, n, tq, scores, None, lambda g, j: vt_ref[g, j], s_ref, p_ref, acc_ref, m_ref)
    _flash_finish_all(nh, n, tq, MLA_V, acc_ref, gate_ref, y_ref)


def _mla_attn(qa, ka, vat, gate, tq, nh):
    nq = SEQ // tq
    return pl.pallas_call(
        functools.partial(_mla_kernel, tq, nh),
        out_shape=jax.ShapeDtypeStruct((N_TOK, MLA_HEADS * MLA_V), BF16),
        grid=(BATCH, MLA_HEADS // nh),
        in_specs=[pl.BlockSpec((SEQ, nh * MLA_HW), lambda b, h: (b, h)),
                  pl.BlockSpec((SEQ, nh * MLA_HW), lambda b, h: (b, h)),
                  pl.BlockSpec((nh, nq, VT_ROWS, tq), lambda b, h: (h, b, 0, 0)),
                  pl.BlockSpec((SEQ, nh * MLA_V), lambda b, h: (b, h))],
        out_specs=pl.BlockSpec((SEQ, nh * MLA_V), lambda b, h: (b, h)),
        scratch_shapes=_flash_scratch(nh, nq, tq),
        compiler_params=_cparams(("parallel", "parallel")),
        name="mla_attn",
    )(qa, ka, vat, gate)


MOBA_TILE = 2 * MOBA_BLOCK


def _moba_flat_kernel(nh, q_ref, k_ref, vt_ref, gate_ref, y_ref, km_ref, nb_ref, s_ref, p_ref, acc_ref, m_ref):
    L = MOBA_BLOCK
    T = MOBA_TILE
    D = MOBA_DH
    nblk = SEQ // L
    ntile = SEQ // T
    cols = [slice(g * D, (g + 1) * D) for g in range(nh)]

    for g in range(nh):
        for j in range(nblk):
            blk = k_ref[j * L:(j + 1) * L, cols[g]].astype(F32)
            km_ref[g, j:j + 1, :] = jnp.sum(blk, axis=0, keepdims=True) * (1.0 / L)

    blk_i = lax.broadcasted_iota(jnp.int32, (nblk, T), 0)
    blk_f = blk_i.astype(F32)
    second = lax.broadcasted_iota(jnp.int32, (nblk, T), 1) >= L

    def select(c, carry):
        own = 2 * c + second.astype(jnp.int32)
        past = blk_i < own
        for g in range(nh):
            gt = jnp.where(past, _qk(km_ref[g].astype(BF16), q_ref[_rows(c, T), cols[g]]), -jnp.inf)
            sel = jnp.zeros((nblk, T), jnp.bool_)
            for _ in range(MOBA_TOPK):
                mx = jnp.max(gt, axis=0, keepdims=True)
                first = jnp.min(jnp.where(gt == mx, blk_f, float(nblk)), axis=0, keepdims=True)
                pick = blk_f == first
                sel = jnp.logical_or(sel, pick)
                gt = jnp.where(pick, -jnp.inf, gt)
            keep = jnp.logical_or(jnp.logical_and(sel, past), blk_i == own)
            nb_ref[g, c] = jnp.where(keep, 0.0, NEG)
        return carry

    lax.fori_loop(0, ntile, select, 0)

    def scores(g, j, c):
        return _qk(k_ref[_rows(j, T), cols[g]], q_ref[_rows(c, T), cols[g]])

    def bias(g, j, c):
        return [nb_ref[g, c, pl.ds(2 * j, 1), :], nb_ref[g, c, pl.ds(2 * j + 1, 1), :]]

    _flash_flat_t(nh, ntile, T, scores, bias, lambda g, j: vt_ref[g, j], s_ref, p_ref, acc_ref, m_ref)
    _flash_finish_all(nh, ntile, T, D, acc_ref, gate_ref, y_ref)


def _moba_flat_attn(qb, kb, vbt, gate, nh):
    T = MOBA_TILE
    D = MOBA_DH
    nblk = SEQ // MOBA_BLOCK
    ntile = SEQ // T
    return pl.pallas_call(
        functools.partial(_moba_flat_kernel, nh),
        out_shape=jax.ShapeDtypeStruct((N_TOK, MOBA_HEADS * D), BF16),
        grid=(BATCH, MOBA_HEADS // nh),
        in_specs=[pl.BlockSpec((SEQ, nh * D), lambda b, h: (b, h)),
                  pl.BlockSpec((SEQ, nh * D), lambda b, h: (b, h)),
                  pl.BlockSpec((nh, ntile, VT_ROWS, T), lambda b, h: (h, b, 0, 0)),
                  pl.BlockSpec((SEQ, nh * D), lambda b, h: (b, h))],
        out_specs=pl.BlockSpec((SEQ, nh * D), lambda b, h: (b, h)),
        scratch_shapes=[pltpu.VMEM((nh, nblk, D), F32), pltpu.VMEM((nh, ntile, nblk, T), F32)]
        + _flash_scratch(nh, ntile, T),
        compiler_params=_cparams(("parallel", "parallel")),
        name="moba_attn",
    )(qb, kb, vbt, gate)


def _mem_prep_kernel(mem_ref, g_ref, w_ref, kv_ref):
    mn = _rms(mem_ref[...], g_ref[...]).astype(BF16)
    kv_ref[...] = jnp.dot(mn, w_ref[...], preferred_element_type=F32).astype(BF16)


def _mem_prep(mem2d, g, w):
    g2 = g.reshape(1, -1)
    return pl.pallas_call(
        _mem_prep_kernel,
        out_shape=jax.ShapeDtypeStruct((BATCH * N_MEM, 2 * MEMQ_W), BF16),
        grid=(BATCH,),
        in_specs=[pl.BlockSpec((N_MEM, D_MODEL), lambda b: (b, 0)),
                  pl.BlockSpec(g2.shape, lambda b: (0, 0)),
                  pl.BlockSpec(w.shape, lambda b: (0, 0))],
        out_specs=pl.BlockSpec((N_MEM, 2 * MEMQ_W), lambda b: (b, 0)),
        compiler_params=_cparams(("parallel",)),
        name="mem_prep",
    )(mem2d, g2, w)


def _mem_attn_kernel(q_ref, kv_ref, gate_ref, y_ref):
    cols = [slice(h * MEM_DH, (h + 1) * MEM_DH) for h in range(MEM_HEADS)]
    ss = [_qk(q_ref[:, c], kv_ref[:, c]) for c in cols]
    ps = [jnp.exp2(s - jnp.max(s, axis=1, keepdims=True)) for s in ss]
    for h, (c, p) in enumerate(zip(cols, ps)):
        l = jnp.sum(p, axis=1, keepdims=True)
        a = MEMQ_W + h * MEM_DH
        o = jnp.dot(p.astype(BF16), kv_ref[:, a:a + MEM_DH], preferred_element_type=F32) / l
        y_ref[:, c] = (o * _silu(gate_ref[:, c].astype(F32))).astype(y_ref.dtype)


def _mem_attn(qm, kvm, gate, tq):
    nq = SEQ // tq
    return pl.pallas_call(
        _mem_attn_kernel,
        out_shape=jax.ShapeDtypeStruct((N_TOK, MEMQ_W), BF16),
        grid=(BATCH, nq),
        in_specs=[pl.BlockSpec((tq, MEMQ_W), lambda b, i: (b * nq + i, 0)),
                  pl.BlockSpec((N_MEM, 2 * MEMQ_W), lambda b, i: (b, 0)),
                  pl.BlockSpec((tq, MEMQ_W), lambda b, i: (b * nq + i, 0))],
        out_specs=pl.BlockSpec((tq, MEMQ_W), lambda b, i: (b * nq + i, 0)),
        compiler_params=_cparams(("parallel", "parallel")),
        name="mem_attn",
    )(qm, kvm, gate)


def _band_bias(n, lo_off, hi_off):
    i = lax.broadcasted_iota(jnp.int32, (Q_BLOCK, 2 * Q_BLOCK), 0)
    c = lax.broadcasted_iota(jnp.int32, (Q_BLOCK, 2 * Q_BLOCK), 1)
    vis = (c - i >= lo_off) & (c - i <= hi_off) & ((c >= Q_BLOCK) | (n > 0))
    return jnp.where(vis, 0.0, NEG).astype(F32)


SWA_BLOCKS_PER_STEP = 8


def _swa_kernel(q_ref, kvp_ref, kvo_ref, gate_ref, sink_ref, y_ref):
    n = pl.program_id(1)
    for u in range(SWA_BLOCKS_PER_STEP):
        rows = slice(u * Q_BLOCK, (u + 1) * Q_BLOCK)
        prev = kvp_ref if u == 0 else kvo_ref.at[(u - 1) * Q_BLOCK:u * Q_BLOCK, :]
        _swa_block(n if u == 0 else 1, q_ref.at[rows, :], prev, kvo_ref.at[rows, :], gate_ref.at[rows, :],
                   sink_ref, y_ref.at[rows, :])


def _swa_block(n, q_ref, kvp_ref, kvo_ref, gate_ref, sink_ref, y_ref):
    QB = Q_BLOCK
    G = SWA_HEADS // SWA_KV_HEADS
    P = G // 2
    c = lax.broadcasted_iota(jnp.int32, (2 * QB, QB), 0)
    i = lax.broadcasted_iota(jnp.int32, (2 * QB, QB), 1)
    vis = (c - i >= QB - (SWA_WINDOW - 1)) & (c - i <= QB) & ((c >= QB) | (n > 0))
    bias = jnp.where(vis, 0.0, NEG).astype(F32)
    bias4 = jnp.concatenate([bias] * P, axis=1)
    k2 = jnp.concatenate([kvp_ref[:, :LANES], kvo_ref[:, :LANES]], axis=0).astype(F32)
    v2 = jnp.concatenate([kvp_ref[:, LANES:], kvo_ref[:, LANES:]], axis=0).astype(F32)
    k2r = pltpu.roll(k2, SWA_DH, 1)
    v2r = pltpu.roll(v2, SWA_DH, 1)
    lane = lax.broadcasted_iota(jnp.int32, (2 * QB, LANES), 1)
    lo = lane < SWA_DH
    top = lax.broadcasted_iota(jnp.int32, (LANES, P * QB), 0) < SWA_DH
    vts, ss = [], []
    for kv in range(SWA_KV_HEADS):
        ka, kb_ = (k2, k2r) if kv == 0 else (k2r, k2)
        va, vb_ = (v2, v2r) if kv == 0 else (v2r, v2)
        k_lo = jnp.where(lo, ka, 0.0).astype(BF16)
        k_hi = jnp.where(lo, 0.0, kb_).astype(BF16)
        vts.append((jnp.where(lo, va, 0.0).T.astype(BF16), jnp.where(lo, 0.0, vb_).T.astype(BF16)))
        base = kv * P
        q4 = jnp.concatenate([q_ref[:, (base + p) * LANES:(base + p + 1) * LANES] for p in range(P)],
                             axis=0)
        ss.append((_qk(k_lo, q4) + bias4, _qk(k_hi, q4) + bias4))
    stats = []
    for pair in ss:
        st = []
        for s in pair:
            m = jnp.max(s, axis=0, keepdims=True)
            p = jnp.exp2(s - m)
            st.append((m, p, jnp.sum(p, axis=0, keepdims=True)))
        stats.append(st)
    for kv in range(SWA_KV_HEADS):
        (m_lo, p_lo, l_lo), (m_hi, p_hi, l_hi) = stats[kv]
        vt_lo, vt_hi = vts[kv]
        base = kv * P
        ot = (jnp.dot(vt_lo, p_lo.astype(BF16), preferred_element_type=F32)
              + jnp.dot(vt_hi, p_hi.astype(BF16), preferred_element_type=F32))
        cols = slice(base * LANES, (base + P) * LANES)
        f_lo = 1.0 / ((1.0 + jnp.exp2(sink_ref[0:1, cols] * LOG2E - (m_lo + jnp.log2(l_lo)))) * l_lo)
        f_hi = 1.0 / ((1.0 + jnp.exp2(sink_ref[1:2, cols] * LOG2E - (m_hi + jnp.log2(l_hi)))) * l_hi)
        ot = ot * jnp.where(top, f_lo, f_hi)
        for p in range(P):
            cc = slice((base + p) * LANES, (base + p + 1) * LANES)
            o = ot[:, p * QB:(p + 1) * QB].T
            y_ref[:, cc] = (o * _silu(gate_ref[:, cc].astype(F32))).astype(y_ref.dtype)


def _swa_attn(qc, kvc, gate, sink_rows):
    R = SWA_BLOCKS_PER_STEP
    ns = SEQ // (R * Q_BLOCK)
    w = SWA_HEADS * SWA_DH
    return pl.pallas_call(
        _swa_kernel,
        out_shape=jax.ShapeDtypeStruct((N_TOK, w), BF16),
        grid=(BATCH, ns),
        in_specs=[pl.BlockSpec((R * Q_BLOCK, w), lambda b, n: (b * ns + n, 0)),
                  pl.BlockSpec((Q_BLOCK, 2 * LANES), lambda b, n: (jnp.maximum((b * ns + n) * R - 1, 0), 0)),
                  pl.BlockSpec((R * Q_BLOCK, 2 * LANES), lambda b, n: (b * ns + n, 0)),
                  pl.BlockSpec((R * Q_BLOCK, w), lambda b, n: (b * ns + n, 0)),
                  pl.BlockSpec((2, w), lambda b, n: (0, 0))],
        out_specs=pl.BlockSpec((R * Q_BLOCK, w), lambda b, n: (b * ns + n, 0)),
        compiler_params=_cparams(("parallel", "parallel")),
        name="swa_attn",
    )(qc, kvc, kvc, gate, sink_rows)


DIL_BLOCKS_PER_STEP = 8
DIL_GROUP = 2
MAX_STRIDE = 4


def _dil_kernel(R, nres, q_ref, kp_ref, ko_ref, vp_ref, vo_ref, o_ref, lse_ref):
    QB = Q_BLOCK
    n = pl.program_id(2)
    lane = lax.broadcasted_iota(jnp.int32, (QB, LANES), 1)
    W = DIL_HEADS * DIL_DH
    for rr in range(nres):
        for u0 in range(0, R, DIL_GROUP):
            tasks = []
            for u in range(u0, u0 + DIL_GROUP):
                rows = slice(u * QB, (u + 1) * QB)
                bias = _band_bias(n if u == 0 else 1, 0, QB)
                for h in range(DIL_HEADS):
                    c = slice(rr * W + h * DIL_DH, rr * W + (h + 1) * DIL_DH)
                    k_prev = kp_ref[:, c] if u == 0 else ko_ref[(u - 1) * QB:u * QB, c]
                    v_prev = vp_ref[:, c] if u == 0 else vo_ref[(u - 1) * QB:u * QB, c]
                    k = jnp.concatenate([k_prev, ko_ref[rows, c]], axis=0)
                    v = jnp.concatenate([v_prev, vo_ref[rows, c]], axis=0)
                    tasks.append((u, h, c, _qk(q_ref[rows, c], k) + bias, v))
            soft = []
            for u, h, c, s, v in tasks:
                m = jnp.max(s, axis=1, keepdims=True)
                p = jnp.exp2(s - m)
                soft.append((m, p, jnp.sum(p, axis=1, keepdims=True)))
            lse_all = {u: jnp.zeros((QB, LANES), F32) for u in range(u0, u0 + DIL_GROUP)}
            for (u, h, c, _, v), (m, p, l) in zip(tasks, soft):
                o_ref[u * QB:(u + 1) * QB, c] = (
                    jnp.dot(p.astype(BF16), v, preferred_element_type=F32) / l).astype(o_ref.dtype)
                lse_all[u] = jnp.where(lane == h, m + jnp.log2(l), lse_all[u])
            for u, val in lse_all.items():
                lse_ref[u * QB:(u + 1) * QB, rr * LANES:(rr + 1) * LANES] = val


def _dil_attn(qv, kv, vv, dil):
    w = DIL_HEADS * DIL_DH
    L = SEQ // dil
    R = min(DIL_BLOCKS_PER_STEP, L // Q_BLOCK)
    nres = min(dil, DIL_BLOCKS_PER_STEP // R)
    ns = L // (R * Q_BLOCK)
    cur = lambda b, r, n: (b * ns + n, r)
    prev = lambda b, r, n: (jnp.maximum((b * ns + n) * R - 1, 0), r)
    blk = (R * Q_BLOCK, nres * w)
    pblk = (Q_BLOCK, nres * w)
    return pl.pallas_call(
        functools.partial(_dil_kernel, R, nres),
        out_shape=[jax.ShapeDtypeStruct((BATCH * L, dil * w), BF16),
                   jax.ShapeDtypeStruct((BATCH * L, dil * LANES), F32)],
        grid=(BATCH, dil // nres, ns),
        in_specs=[pl.BlockSpec(blk, cur), pl.BlockSpec(pblk, prev), pl.BlockSpec(blk, cur),
                  pl.BlockSpec(pblk, prev), pl.BlockSpec(blk, cur)],
        out_specs=[pl.BlockSpec(blk, cur), pl.BlockSpec((R * Q_BLOCK, nres * LANES), cur)],
        compiler_params=_cparams(("parallel", "parallel", "parallel")),
        name=f"dil_attn_d{dil}",
    )(qv, kv, kv, vv, vv)


def _dil_combine_kernel(tm, *refs):
    np_ = len(DIL_PATTERNS)
    o_refs, l_refs = refs[:np_], refs[np_:2 * np_]
    gate_ref, y_ref = refs[2 * np_], refs[2 * np_ + 1]
    stage = refs[2 * np_ + 2:]
    w = DIL_HEADS * DIL_DH
    os_, ls = [], []
    k = 0
    for (_, d), o_ref, l_ref in zip(DIL_PATTERNS, o_refs, l_refs):
        if d == 1:
            os_.append(lambda h, o_ref=o_ref: o_ref[:, h * DIL_DH:(h + 1) * DIL_DH].astype(F32))
            ls.append(l_ref[...])
            continue
        so, sl = stage[2 * k], stage[2 * k + 1]
        k += 1
        if d <= MAX_STRIDE:
            hops = [(so, sl, d, lambda r: r)]
        else:
            assert d == MAX_STRIDE * MAX_STRIDE
            co, cl = stage[-2], stage[-1]
            q4 = tm // MAX_STRIDE
            hops = [(co, cl, MAX_STRIDE, lambda r: (r % MAX_STRIDE) * q4 + r // MAX_STRIDE)]
        dst_o, dst_l, stride, start = hops[0]
        for r in range(d):
            rows = pl.ds(start(r), tm // d, stride=stride)
            for h in range(DIL_HEADS):
                a = r * w + h * DIL_DH
                dst_o[h, rows, :] = o_ref[:, a:a + DIL_DH].astype(F32)
            dst_l[rows, :] = l_ref[:, r * LANES:(r + 1) * LANES]
        if d > MAX_STRIDE:
            for r in range(MAX_STRIDE):
                rows = pl.ds(r, q4, stride=MAX_STRIDE)
                blk = slice(r * q4, (r + 1) * q4)
                for h in range(DIL_HEADS):
                    so[h, rows, :] = co[h, blk, :]
                sl[rows, :] = cl[blk, :]
        os_.append(lambda h, so=so: so[h])
        ls.append(sl[...])
    mx = functools.reduce(jnp.maximum, ls)
    es = [jnp.exp2(x - mx) for x in ls]
    den = functools.reduce(lambda a, b: a + b, es)
    ws = [e / den for e in es]
    lane = lax.broadcasted_iota(jnp.int32, ws[0].shape, 1)
    for h in range(DIL_HEADS):
        c = slice(h * DIL_DH, (h + 1) * DIL_DH)
        wh = [jnp.sum(jnp.where(lane == h, wgt, 0.0), axis=1, keepdims=True) for wgt in ws]
        o = functools.reduce(lambda a, b: a + b, [wh[p] * os_[p](h) for p in range(np_)])
        y_ref[:, c] = (o * _silu(gate_ref[:, c].astype(F32))).astype(y_ref.dtype)


def _dil_combine(outs, lses, gate, tm):
    w = DIL_HEADS * DIL_DH
    in_specs = [pl.BlockSpec((tm // d, d * w), lambda i: (i, 0)) for _, d in DIL_PATTERNS]
    in_specs += [pl.BlockSpec((tm // d, d * LANES), lambda i: (i, 0)) for _, d in DIL_PATTERNS]
    in_specs.append(pl.BlockSpec((tm, w), lambda i: (i, 0)))
    scratch = []
    for _, d in DIL_PATTERNS:
        if d > 1:
            scratch += [pltpu.VMEM((DIL_HEADS, tm, DIL_DH), F32), pltpu.VMEM((tm, LANES), F32)]
    if any(d > MAX_STRIDE for _, d in DIL_PATTERNS):
        scratch += [pltpu.VMEM((DIL_HEADS, tm, DIL_DH), F32), pltpu.VMEM((tm, LANES), F32)]
    return pl.pallas_call(
        functools.partial(_dil_combine_kernel, tm),
        out_shape=jax.ShapeDtypeStruct((N_TOK, w), BF16),
        grid=(N_TOK // tm,),
        in_specs=in_specs,
        out_specs=pl.BlockSpec((tm, w), lambda i: (i, 0)),
        scratch_shapes=scratch,
        compiler_params=_cparams(("parallel",)),
        name="dil_combine",
    )(*outs, *lses, gate)


def _outproj_kernel(nparts, final, *refs):
    ys, ws = refs[:nparts], refs[nparts:2 * nparts]
    x_ref = refs[2 * nparts]
    o_ref = refs[-1]
    acc = x_ref[...]
    for y, w in zip(ys, ws):
        acc = acc + jnp.dot(y[...], w[...], preferred_element_type=F32)
    if final:
        acc = _rms(acc, refs[2 * nparts + 1][...])
    o_ref[...] = acc


def _outproj(ys, ws, x, final_g, tm):
    n_tok, d = x.shape
    in_specs = [pl.BlockSpec((tm, y.shape[1]), lambda i: (i, 0)) for y in ys]
    in_specs += [pl.BlockSpec(w.shape, lambda i: (0, 0)) for w in ws]
    in_specs.append(pl.BlockSpec((tm, d), lambda i: (i, 0)))
    args = [*ys, *ws, x]
    if final_g is not None:
        in_specs.append(pl.BlockSpec((1, d), lambda i: (0, 0)))
        args.append(final_g.reshape(1, d))
    return pl.pallas_call(
        functools.partial(_outproj_kernel, len(ys), final_g is not None),
        out_shape=jax.ShapeDtypeStruct((n_tok, d), F32),
        grid=(n_tok // tm,),
        in_specs=in_specs,
        out_specs=pl.BlockSpec((tm, d), lambda i: (i, 0)),
        compiler_params=_cparams(("parallel",)),
        name="outproj",
    )(*args)


PROJ_TM = 512
PROJ_TM_WIDE = 1024
MEM_TQ = 1024
ATTN_TQ = 512
MLA_HEADS_PER_STEP = 2
MOBA_HEADS_PER_STEP = 2


def _even_layer(x, mem2d, norm_g, w_in, q_norm_g, w_uq, kv_norm_g, w_ukv, mem_norm_g, w_mem_kv, w_out):
    n_lat = MLA_Q_RANK + MLA_KV_RANK + MLA_ROPE
    wlat = jnp.pad(w_in[:, :n_lat], ((0, 0), (0, LANES - MLA_ROPE))).astype(BF16)
    n_bqkv = n_lat + 3 * MOBA_HEADS * MOBA_DH
    w_bqkv = w_in[:, n_lat:n_bqkv].astype(BF16)
    w_rest = w_in[:, n_bqkv:].astype(BF16)
    wuq = w_uq.reshape(MLA_Q_RANK, MLA_HEADS, MLA_NOPE + MLA_ROPE)
    wuq = jnp.pad(wuq, ((0, 0), (0, 0), (0, MLA_HW - MLA_NOPE - MLA_ROPE)))
    wuq = wuq.reshape(MLA_Q_RANK, MLA_HEADS * MLA_HW).astype(BF16)
    wukv = w_ukv.reshape(MLA_KV_RANK, MLA_HEADS, MLA_NOPE + MLA_V)
    wukv = jnp.concatenate([wukv[:, :, :MLA_NOPE].reshape(MLA_KV_RANK, -1),
                            wukv[:, :, MLA_NOPE:].reshape(MLA_KV_RANK, -1)], axis=1).astype(BF16)

    cos_a, sin_a = _rope_tables(MLA_ROPE, LANES)
    qa, ka, va = _mla_prep(x, norm_g, wlat, q_norm_g, wuq, kv_norm_g, wukv, cos_a, sin_a, ATTN_TQ)

    rot = MOBA_DH // ROT_FRAC
    cos_b, sin_b = _rope_tables(rot, MOBA_DH)
    sc = MOBA_DH ** -0.5 * LOG2E
    wb = MOBA_HEADS * MOBA_DH
    plain = [("plain",)]
    groups_qkv = [
        (wb, [("rope", 0, rot // 2, MOBA_DH, sc)] * MOBA_HEADS, "rows"),
        (wb, [("rope", 0, rot // 2, MOBA_DH, 1.0)] * MOBA_HEADS, "rows"),
        (wb, plain * MOBA_HEADS, ("vT", MOBA_TILE)),
    ]
    groups_rest = [
        (MEMQ_W, [("scale", MEM_DH ** -0.5 * LOG2E)] * MEM_HEADS, "rows"),
        (MLA_HEADS * MLA_V, plain * MLA_HEADS, "rows"),
        (wb, plain * MOBA_HEADS, "rows"),
        (MEMQ_W, plain * MEM_HEADS, "rows"),
    ]
    qb, kb, vb = _inproj(x, norm_g, w_bqkv, groups_qkv, [cos_b, sin_b], PROJ_TM)
    qm, gate_a, gate_b, gate_m = _inproj(x, norm_g, w_rest, groups_rest, [], PROJ_TM_WIDE)

    y_a = _mla_attn(qa, ka, va, gate_a, ATTN_TQ, MLA_HEADS_PER_STEP)
    y_b = _moba_flat_attn(qb, kb, vb, gate_b, MOBA_HEADS_PER_STEP)
    kvm = _mem_prep(mem2d, mem_norm_g, w_mem_kv.astype(BF16))
    y_m = _mem_attn(qm, kvm, gate_m, MEM_TQ)

    a, b = MLA_HEADS * MLA_V, MLA_HEADS * MLA_V + MOBA_HEADS * MOBA_DH
    return [y_a, y_b, y_m], [w_out[:a].astype(BF16), w_out[a:b].astype(BF16), w_out[b:].astype(BF16)]


def _odd_layer(x, mem2d, norm_g, w_in, sinks, mem_norm_g, w_mem_kv, w_out):
    rot_c = SWA_DH // ROT_FRAC
    rot_d = DIL_DH // ROT_FRAC
    cos_c, sin_c = _rope_tables(rot_c, SWA_DH)
    cos_d, sin_d = _rope_tables(rot_d, DIL_DH)
    wq = SWA_HEADS * SWA_DH
    wd = DIL_HEADS * DIL_DH
    wkv = 2 * SWA_KV_HEADS * SWA_DH
    rc = lambda s: ("rope", 0, rot_c // 2, SWA_DH, s)
    rd = lambda s: ("rope", 1, rot_d // 2, DIL_DH, s)
    plain = [("plain",)]
    groups_qkv = [
        (wq, [rc(SWA_DH ** -0.5 * LOG2E)] * (wq // LANES), "rows"),
        (wkv, [rc(1.0), ("plain",)], "rows"),
        (wd, [rd(DIL_DH ** -0.5 * LOG2E)] * DIL_HEADS, "dilated"),
        (wd, [rd(1.0)] * DIL_HEADS, "dilated"),
        (wd, plain * DIL_HEADS, "dilated"),
    ]
    groups_rest = [
        (MEMQ_W, [("scale", MEM_DH ** -0.5 * LOG2E)] * MEM_HEADS, "rows"),
        (wq, plain * (wq // LANES), "rows"),
        (wd, plain * DIL_HEADS, "rows"),
        (MEMQ_W, plain * MEM_HEADS, "rows"),
    ]
    n_qkv = wq + wkv + 3 * wd
    res = _inproj(x, norm_g, w_in[:, :n_qkv].astype(BF16), groups_qkv, [cos_c, sin_c, cos_d, sin_d], PROJ_TM)
    qc, kvc = res[0], res[1]
    nv = 1 + len(DIL_VIEWS)
    qd, kd, vd = (res[2 + i * nv:2 + (i + 1) * nv] for i in range(3))
    qm, gate_c, gate_d, gate_m = _inproj(x, norm_g, w_in[:, n_qkv:].astype(BF16), groups_rest, [], PROJ_TM_WIDE)

    sink_rows = jnp.repeat(sinks.astype(F32).reshape(SWA_HEADS // 2, 2).T, LANES, axis=1)
    y_c = _swa_attn(qc, kvc, gate_c, sink_rows)
    view_of = {1: 0, **{d: 1 + i for i, d in enumerate(DIL_VIEWS)}}
    outs, lses = zip(*[_dil_attn(qd[view_of[dil]], kd[view_of[dil]], vd[view_of[dil]], dil)
                       for _, dil in DIL_PATTERNS])
    y_d = _dil_combine(outs, lses, gate_d, PROJ_TM_WIDE)
    kvm = _mem_prep(mem2d, mem_norm_g, w_mem_kv.astype(BF16))
    y_m = _mem_attn(qm, kvm, gate_m, MEM_TQ)

    return [y_c, y_d, y_m], [w_out[:wq].astype(BF16), w_out[wq:wq + wd].astype(BF16),
                             w_out[wq + wd:].astype(BF16)]


def kernel(x, mem, ev_norm_g, ev_w_in, ev_q_norm_g, ev_w_uq, ev_kv_norm_g, ev_w_ukv, ev_mem_norm_g,
           ev_w_mem_kv, ev_w_out, od_norm_g, od_w_in, od_sinks, od_mem_norm_g, od_w_mem_kv, od_w_out,
           final_norm_g):
    x2 = x.reshape(N_TOK, D_MODEL)
    mem2d = mem.reshape(BATCH * N_MEM, D_MODEL)
    ys, ws = _even_layer(x2, mem2d, ev_norm_g[0], ev_w_in[0], ev_q_norm_g[0], ev_w_uq[0], ev_kv_norm_g[0],
                         ev_w_ukv[0], ev_mem_norm_g[0], ev_w_mem_kv[0], ev_w_out[0])
    x2 = _outproj(ys, ws, x2, None, PROJ_TM)
    ys, ws = _odd_layer(x2, mem2d, od_norm_g[0], od_w_in[0], od_sinks[0], od_mem_norm_g[0],
                        od_w_mem_kv[0], od_w_out[0])
    x2 = _outproj(ys, ws, x2, final_norm_g, PROJ_TM)
    return x2.reshape(BATCH, SEQ, D_MODEL)
```

```python
import functools

import numpy as np
import jax
import jax.numpy as jnp
from jax import lax
from jax.experimental import pallas as pl
from jax.experimental.pallas import tpu as pltpu

D_MODEL = 2048
BATCH = 4
SEQ = 4096
N_TOK = BATCH * SEQ
N_MEM = 256
ROPE_THETA = 500000.0
ROT_FRAC = 4
EPS = 1e-6

MLA_HEADS = 8
MLA_Q_RANK = 512
MLA_KV_RANK = 256
MLA_NOPE = 128
MLA_ROPE = 64
MLA_V = 128
MOBA_HEADS = 8
MOBA_DH = 128
MOBA_BLOCK = 256
MOBA_TOPK = 3
SWA_HEADS = 16
SWA_KV_HEADS = 2
SWA_DH = 64
SWA_WINDOW = 128
DIL_HEADS = 6
DIL_DH = 128
DIL_PATTERNS = ((128, 1), (512, 4), (2048, 16))
MEM_HEADS = 4
MEM_DH = 128
MEMQ_W = MEM_HEADS * MEM_DH
Q_BLOCK = 128

LANES = 128
VMEM_LIMIT = 52 * 1024 * 1024

NEG = -1e30
BF16 = jnp.bfloat16
F32 = jnp.float32


def _cparams(sem):
    return pltpu.CompilerParams(dimension_semantics=sem, vmem_limit_bytes=VMEM_LIMIT)


def _rms(x, g):
    ms = jnp.mean(x * x, axis=-1, keepdims=True)
    return (x * lax.rsqrt(ms + EPS)) * g


def _silu(g):
    return g / (1.0 + jnp.exp(-g))


def _rope_slab(x, cos_t, sin_t, half, period):
    lane = lax.broadcasted_iota(jnp.int32, x.shape, 1)
    up = pltpu.roll(x, LANES - half, 1)
    dn = pltpu.roll(x, half, 1)
    sw = jnp.where((lane & (period - 1)) < half, up, dn)
    return x * cos_t + sw * sin_t


def _qk(q, k):
    return lax.dot_general(q, k, (((1,), (1,)), ((), ())), preferred_element_type=F32)


def _rope_tables(rot_dim, period):
    half = rot_dim // 2
    inv = 1.0 / (ROPE_THETA ** (jnp.arange(0, rot_dim, 2, dtype=F32) / rot_dim))
    ang = jnp.arange(SEQ, dtype=F32)[:, None] * inv[None, :]
    c, s = jnp.cos(ang), jnp.sin(ang)
    d = np.arange(LANES) % period
    idx = d % half
    cos_t = jnp.where(d < rot_dim, c[:, idx], 1.0)
    sin_t = jnp.where(d < half, -s[:, idx], jnp.where(d < rot_dim, s[:, idx], 0.0))
    return cos_t.astype(F32), sin_t.astype(F32)


PROJ_CHUNK = 512
DIL_VIEWS = tuple(d for _, d in DIL_PATTERNS if d > 1)


def _inproj_kernel(groups, ntab, tm, x_ref, g_ref, *rest):
    tabs = rest[:2 * ntab]
    w_ref = rest[2 * ntab]
    ndil = sum(layout == "dilated" for _, _, layout in groups)
    nscr = ndil + (1 if ndil else 0)
    refs = list(rest[2 * ntab + 1:len(rest) - nscr])
    stage = list(rest[len(rest) - nscr:])
    chain_ref = stage.pop() if ndil else None
    xn = _rms(x_ref[...], g_ref[...]).astype(BF16)
    c0 = 0
    for ncols, kinds, layout in groups:
        o_ref = refs.pop(0)
        views = [refs.pop(0) for _ in DIL_VIEWS] if layout == "dilated" else []
        st_ref = stage.pop(0) if layout == "dilated" else None
        for s0 in range(0, ncols, PROJ_CHUNK):
            n = min(PROJ_CHUNK, ncols - s0)
            acc = jnp.dot(xn, w_ref[:, c0 + s0:c0 + s0 + n], preferred_element_type=F32)
            for c in range(n // LANES):
                col = s0 + c * LANES
                kind = kinds[col // LANES]
                piece = acc[:, c * LANES:(c + 1) * LANES]
                if kind[0] == "rope":
                    _, ti, half, period, sc = kind
                    piece = _rope_slab(piece, tabs[2 * ti][...], tabs[2 * ti + 1][...], half, period)
                    if sc != 1.0:
                        piece = piece * sc
                elif kind[0] == "scale":
                    piece = piece * kind[1]
                if layout == "rows" or layout == "dilated":
                    o_ref[:, col:col + LANES] = piece.astype(o_ref.dtype)
                    if layout == "dilated":
                        st_ref[col // LANES] = piece
                else:
                    tb = layout[1]
                    for u in range(tm // tb):
                        o_ref[col // LANES, u] = _vt_block(piece[u * tb:(u + 1) * tb, :])
        src, src_d = st_ref, 1
        for vi, (d, v_ref) in enumerate(zip(DIL_VIEWS, views)):
            keep = chain_ref if vi + 1 < len(views) else None
            rows_src, rows_dst = tm // src_d, tm // d
            for r in range(d):
                start = (r % src_d) * rows_src + r // src_d
                for c in range(ncols // LANES):
                    blk = src[c, pl.ds(start, rows_dst, stride=d // src_d), :]
                    a = r * ncols + c * LANES
                    v_ref[:, a:a + LANES] = blk.astype(v_ref.dtype)
                    if keep is not None:
                        keep[c, r * rows_dst:(r + 1) * rows_dst, :] = blk
            src, src_d = keep, d
        c0 += ncols


def _inproj(x, g, w, groups, tables, tm):
    n_tok, d = x.shape
    assert sum(nc for nc, _, _ in groups) == w.shape[1]
    seq_tiles = SEQ // tm
    in_specs = [pl.BlockSpec((tm, d), lambda i: (i, 0)),
                pl.BlockSpec((1, d), lambda i: (0, 0))]
    args = [x, g.reshape(1, d)]
    for t in tables:
        in_specs.append(pl.BlockSpec((tm, LANES), lambda i: (i % seq_tiles, 0)))
        args.append(t)
    in_specs.append(pl.BlockSpec(w.shape, lambda i: (0, 0), pipeline_mode=pl.Buffered(1)))
    args.append(w)
    out_specs, out_shapes = [], []
    for nc, _, layout in groups:
        if layout == "rows" or layout == "dilated":
            out_specs.append(pl.BlockSpec((tm, nc), lambda i: (i, 0)))
            out_shapes.append(jax.ShapeDtypeStruct((n_tok, nc), BF16))
            if layout == "dilated":
                for dil in DIL_VIEWS:
                    out_specs.append(pl.BlockSpec((tm // dil, dil * nc), lambda i: (i, 0)))
                    out_shapes.append(jax.ShapeDtypeStruct((n_tok // dil, dil * nc), BF16))
        else:
            tb = layout[1]
            out_specs.append(pl.BlockSpec((nc // LANES, tm // tb, VT_ROWS, tb), lambda i: (0, i, 0, 0)))
            out_shapes.append(jax.ShapeDtypeStruct((nc // LANES, n_tok // tb, VT_ROWS, tb), BF16))
    return pl.pallas_call(
        functools.partial(_inproj_kernel, groups, len(tables) // 2, tm),
        out_shape=out_shapes,
        grid=(n_tok // tm,),
        in_specs=in_specs,
        out_specs=out_specs,
        scratch_shapes=(lambda dil: dil + dil[:1])(
            [pltpu.VMEM((nc // LANES, tm, LANES), F32) for nc, _, layout in groups if layout == "dilated"]),
        compiler_params=_cparams(("parallel",)),
        name="inproj",
    )(*args)


MLA_HW = 256


def _mla_prep_kernel(scale, x_ref, g_ref, wlat_ref, qg_ref, wuq_ref, kvg_ref, wukv_ref,
                     cos_ref, sin_ref, qa_ref, ka_ref, va_ref):
    tm = x_ref.shape[0]
    halves = [slice(u * (tm // 2), (u + 1) * (tm // 2)) for u in range(2)]
    xns = [_rms(x_ref[r, :], g_ref[...]).astype(BF16) for r in halves]
    lats = [jnp.dot(xn, wlat_ref[...], preferred_element_type=F32) for xn in xns]
    cqs = [_rms(lat[:, :MLA_Q_RANK], qg_ref[...]).astype(BF16) for lat in lats]
    ckvs = [_rms(lat[:, MLA_Q_RANK:MLA_Q_RANK + MLA_KV_RANK], kvg_ref[...]).astype(BF16) for lat in lats]
    qs = [jnp.dot(cq, wuq_ref[...], preferred_element_type=F32) for cq in cqs]
    kvs = [jnp.dot(ckv, wukv_ref[...], preferred_element_type=F32) for ckv in ckvs]
    for r, lat, q, kv in zip(halves, lats, qs, kvs):
        cos_t, sin_t = cos_ref[r, :], sin_ref[r, :]
        kpe = lat[:, MLA_Q_RANK + MLA_KV_RANK:]
        kpe = _rope_slab(kpe, cos_t, sin_t, MLA_ROPE // 2, LANES).astype(BF16)
        for h in range(MLA_HEADS):
            a = h * MLA_HW
            qa_ref[r, a:a + LANES] = (q[:, a:a + LANES] * scale).astype(BF16)
            qr = _rope_slab(q[:, a + LANES:a + 2 * LANES], cos_t, sin_t, MLA_ROPE // 2, LANES)
            qa_ref[r, a + LANES:a + 2 * LANES] = (qr * scale).astype(BF16)
            ka_ref[r, a:a + LANES] = kv[:, h * LANES:(h + 1) * LANES].astype(BF16)
            ka_ref[r, a + LANES:a + 2 * LANES] = kpe
            v0 = MLA_HEADS * MLA_NOPE + h * MLA_V
            va_ref[h, 0, :, r] = _vt_block(kv[:, v0:v0 + MLA_V])


def _mla_prep(x, g, wlat, qg, wuq, kvg, wukv, cos_t, sin_t, tm):
    n_tok, d = x.shape
    seq_tiles = SEQ // tm
    full = lambda a: pl.BlockSpec(a.shape, lambda i: (0, 0))
    qg2, kvg2, g2 = qg.reshape(1, -1), kvg.reshape(1, -1), g.reshape(1, d)
    scale = (MLA_NOPE + MLA_ROPE) ** -0.5 * LOG2E
    return pl.pallas_call(
        functools.partial(_mla_prep_kernel, scale),
        out_shape=[jax.ShapeDtypeStruct((n_tok, MLA_HEADS * MLA_HW), BF16),
                   jax.ShapeDtypeStruct((n_tok, MLA_HEADS * MLA_HW), BF16),
                   jax.ShapeDtypeStruct((MLA_HEADS, n_tok // tm, VT_ROWS, tm), BF16)],
        grid=(n_tok // tm,),
        in_specs=[pl.BlockSpec((tm, d), lambda i: (i, 0)), full(g2), full(wlat), full(qg2), full(wuq),
                  full(kvg2), full(wukv),
                  pl.BlockSpec((tm, LANES), lambda i: (i % seq_tiles, 0)),
                  pl.BlockSpec((tm, LANES), lambda i: (i % seq_tiles, 0))],
        out_specs=[pl.BlockSpec((tm, MLA_HEADS * MLA_HW), lambda i: (i, 0)),
                   pl.BlockSpec((tm, MLA_HEADS * MLA_HW), lambda i: (i, 0)),
                   pl.BlockSpec((MLA_HEADS, 1, VT_ROWS, tm), lambda i: (0, i, 0, 0))],
        compiler_params=_cparams(("parallel",)),
        name="mla_prep",
    )(x, g2, wlat, qg2, wuq, kvg2, wukv, cos_t, sin_t)


LOG2E = 1.4426950408889634
VT_TAIL = 16
VT_ROWS = LANES + VT_TAIL
FLASH_CHUNK = 128
FLASH_UNROLL = 2


def _vt_block(v):
    tb = v.shape[0]
    row = lax.broadcasted_iota(jnp.int32, (VT_TAIL, tb), 0)
    tail = jnp.where(row == 0, 1.0, 0.0).astype(F32)
    return jnp.concatenate([v.T, tail], axis=0).astype(BF16)


def _flash_finish_all(nh, n, tq, dv, acc_ref, y_ref):
    def body(c, _):
        rows = pl.ds(pl.multiple_of(c * tq, tq), tq)
        for g in range(nh):
            cols = slice(g * dv, (g + 1) * dv)
            a = acc_ref.at[g, c]
            y_ref[rows, cols] = (a[:LANES, :] / a[LANES:LANES + 1, :]).T.astype(y_ref.dtype)
        return 0

    lax.fori_loop(0, n, body, 0)


def _flash_flat_t(nh, n, tq, score_fn, row_bias, vt_fn, s_ref, p_ref, acc_ref, m_ref):
    n_off = n * (n - 1) // 2
    assert n % 2 == 0 and n_off % 2 == 0
    acc_ref[...] = jnp.zeros_like(acc_ref)
    p_ref[...] = jnp.zeros_like(p_ref)
    m_ref[...] = jnp.full(m_ref.shape, NEG, F32)
    key = lax.broadcasted_iota(jnp.int32, (tq, tq), 0)
    qry = lax.broadcasted_iota(jnp.int32, (tq, tq), 1)

    def diag_scores(c):
        return lambda g: jnp.where(key <= qry, score_fn(g, c, c), NEG)

    def off_scores(j, c):
        return lambda g: score_fn(g, j, jnp.minimum(c, n - 1))

    def step(rd, wr, nxt, cur, prev, alphas):
        (jc, cc), (jp, cp) = cur, prev
        for g in range(nh):
            s_ref[wr, g] = nxt(g)
        pend = [jnp.dot(vt_fn(g, jp), p_ref[wr, g], preferred_element_type=F32) for g in range(nh)]
        out = []
        for g in range(nh):
            s_view = s_ref.at[rd, g]
            m_prev = m_ref[g, cc]
            nbs = [None] if row_bias is None else row_bias(g, jc, cc)
            hk = tq // len(nbs)
            m_new = m_prev
            for u, nb in enumerate(nbs):
                mb = functools.reduce(jnp.maximum, [
                    jnp.max(s_view[r0:r0 + FLASH_CHUNK, :], axis=0, keepdims=True)
                    for r0 in range(u * hk, (u + 1) * hk, FLASH_CHUNK)])
                m_new = jnp.maximum(m_new, mb if nb is None else mb + nb)
            m_ref[g, cc] = m_new
            acc_ref[g, cp] = alphas[g] * acc_ref[g, cp] + pend[g]
            for u, nb in enumerate(nbs):
                shift = m_new if nb is None else m_new - nb
                for r0 in range(u * hk, (u + 1) * hk, FLASH_CHUNK):
                    rows = slice(r0, r0 + FLASH_CHUNK)
                    p_ref[rd, g, rows, :] = jnp.exp2((s_view[rows, :] - shift).astype(BF16))
            out.append(jnp.exp2(m_prev - m_new))
        return tuple(out)

    def advance(j, c):
        wrap = j + 1 >= c
        return jnp.where(wrap, 0, j + 1), jnp.where(wrap, c + 1, c)

    for g in range(nh):
        s_ref[0, g] = diag_scores(0)(g)
    alphas = (jnp.ones((1, tq), F32),) * nh

    def diag_pair(u, alphas):
        i = 2 * u
        ip = jnp.maximum(i - 1, 0)
        alphas = step(0, 1, diag_scores(i + 1), (i, i), (ip, ip), alphas)
        return step(1, 0, diag_scores(i + 2), (i + 1, i + 1), (i, i), alphas)

    alphas = lax.fori_loop(0, (n - 2) // 2, diag_pair, alphas)
    alphas = step(0, 1, diag_scores(n - 1), (n - 2, n - 2), (n - 3, n - 3), alphas)
    alphas = step(1, 0, off_scores(0, 1), (n - 1, n - 1), (n - 2, n - 2), alphas)

    def off_steps(u, carry):
        cur, prev, alphas = carry[0:2], carry[2:4], carry[4:]
        for k in range(FLASH_UNROLL):
            nxt = advance(*cur)
            alphas = step(k % 2, 1 - k % 2, off_scores(*nxt), cur, prev, alphas)
            cur, prev = nxt, cur
        return tuple(cur) + tuple(prev) + tuple(alphas)

    assert n_off % FLASH_UNROLL == 0 and FLASH_UNROLL % 2 == 0
    i32 = jnp.int32
    carry = lax.fori_loop(0, n_off // FLASH_UNROLL, off_steps,
                          (i32(0), i32(1), i32(n - 1), i32(n - 1)) + tuple(alphas))
    jl, cl = carry[2], carry[3]
    for g in range(nh):
        pend = jnp.dot(vt_fn(g, jl), p_ref[1, g], preferred_element_type=F32)
        acc_ref[g, cl] = carry[4 + g] * acc_ref[g, cl] + pend


def _rows(i, t):
    return pl.ds(i * t, t) if isinstance(i, int) else pl.ds(pl.multiple_of(i * t, t), t)


def _flash_scratch(nh, n, tq):
    return [pltpu.VMEM((2, nh, tq, tq), F32), pltpu.VMEM((2, nh, tq, tq), BF16),
            pltpu.VMEM((nh, n, VT_ROWS, tq), F32), pltpu.VMEM((nh, n, 1, tq), F32)]


def _mla_kernel(tq, nh, q_ref, k_ref, vt_ref, y_ref, s_ref, p_ref, acc_ref, m_ref):
    n = SEQ // tq

    def scores(g, j, c):
        cols = slice(g * MLA_HW, (g + 1) * MLA_HW)
        return _qk(k_ref[_rows(j, tq), cols], q_ref[_rows(c, tq), cols])

    _flash_flat_t(nh, n, tq, scores, None, lambda g, j: vt_ref[g, j], s_ref, p_ref, acc_ref, m_ref)
    _flash_finish_all(nh, n, tq, MLA_V, acc_ref, y_ref)


def _mla_attn(qa, ka, vat, tq, nh):
    nq = SEQ // tq
    return pl.pallas_call(
        functools.partial(_mla_kernel, tq, nh),
        out_shape=jax.ShapeDtypeStruct((N_TOK, MLA_HEADS * MLA_V), BF16),
        grid=(BATCH, MLA_HEADS // nh),
        in_specs=[pl.BlockSpec((SEQ, nh * MLA_HW), lambda b, h: (b, h)),
                  pl.BlockSpec((SEQ, nh * MLA_HW), lambda b, h: (b, h)),
                  pl.BlockSpec((nh, nq, VT_ROWS, tq), lambda b, h: (h, b, 0, 0))],
        out_specs=pl.BlockSpec((SEQ, nh * MLA_V), lambda b, h: (b, h)),
        scratch_shapes=_flash_scratch(nh, nq, tq),
        compiler_params=_cparams(("parallel", "parallel")),
        name="mla_attn",
    )(qa, ka, vat)


MOBA_TILE = 2 * MOBA_BLOCK


def _moba_flat_kernel(nh, q_ref, k_ref, vt_ref, y_ref, km_ref, nb_ref, s_ref, p_ref, acc_ref, m_ref):
    L = MOBA_BLOCK
    T = MOBA_TILE
    D = MOBA_DH
    nblk = SEQ // L
    ntile = SEQ // T
    cols = [slice(g * D, (g + 1) * D) for g in range(nh)]

    for g in range(nh):
        for j in range(nblk):
            blk = k_ref[j * L:(j + 1) * L, cols[g]].astype(F32)
            km_ref[g, j:j + 1, :] = jnp.sum(blk, axis=0, keepdims=True) * (1.0 / L)

    blk_i = lax.broadcasted_iota(jnp.int32, (nblk, T), 0)
    blk_f = blk_i.astype(F32)
    second = lax.broadcasted_iota(jnp.int32, (nblk, T), 1) >= L

    def select(c, carry):
        own = 2 * c + second.astype(jnp.int32)
        past = blk_i < own
        for g in range(nh):
            gt = jnp.where(past, _qk(km_ref[g].astype(BF16), q_ref[_rows(c, T), cols[g]]), -jnp.inf)
            sel = jnp.zeros((nblk, T), jnp.bool_)
            for _ in range(MOBA_TOPK):
                mx = jnp.max(gt, axis=0, keepdims=True)
                first = jnp.min(jnp.where(gt == mx, blk_f, float(nblk)), axis=0, keepdims=True)
                pick = blk_f == first
                sel = jnp.logical_or(sel, pick)
                gt = jnp.where(pick, -jnp.inf, gt)
            keep = jnp.logical_or(jnp.logical_and(sel, past), blk_i == own)
            nb_ref[g, c] = jnp.where(keep, 0.0, NEG)
        return carry

    lax.fori_loop(0, ntile, select, 0)

    def scores(g, j, c):
        return _qk(k_ref[_rows(j, T), cols[g]], q_ref[_rows(c, T), cols[g]])

    def bias(g, j, c):
        return [nb_ref[g, c, pl.ds(2 * j, 1), :], nb_ref[g, c, pl.ds(2 * j + 1, 1), :]]

    _flash_flat_t(nh, ntile, T, scores, bias, lambda g, j: vt_ref[g, j], s_ref, p_ref, acc_ref, m_ref)
    _flash_finish_all(nh, ntile, T, D, acc_ref, y_ref)


def _moba_flat_attn(qb, kb, vbt, nh):
    T = MOBA_TILE
    D = MOBA_DH
    nblk = SEQ // MOBA_BLOCK
    ntile = SEQ // T
    return pl.pallas_call(
        functools.partial(_moba_flat_kernel, nh),
        out_shape=jax.ShapeDtypeStruct((N_TOK, MOBA_HEADS * D), BF16),
        grid=(BATCH, MOBA_HEADS // nh),
        in_specs=[pl.BlockSpec((SEQ, nh * D), lambda b, h: (b, h)),
                  pl.BlockSpec((SEQ, nh * D), lambda b, h: (b, h)),
                  pl.BlockSpec((nh, ntile, VT_ROWS, T), lambda b, h: (h, b, 0, 0))],
        out_specs=pl.BlockSpec((SEQ, nh * D), lambda b, h: (b, h)),
        scratch_shapes=[pltpu.VMEM((nh, nblk, D), F32), pltpu.VMEM((nh, ntile, nblk, T), F32)]
        + _flash_scratch(nh, ntile, T),
        compiler_params=_cparams(("parallel", "parallel")),
        name="moba_attn",
    )(qb, kb, vbt)


def _mem_prep_kernel(mem_ref, g_ref, w_ref, kv_ref):
    mn = _rms(mem_ref[...], g_ref[...]).astype(BF16)
    kv_ref[...] = jnp.dot(mn, w_ref[...], preferred_element_type=F32).astype(BF16)


def _mem_prep(mem2d, g, w):
    g2 = g.reshape(1, -1)
    return pl.pallas_call(
        _mem_prep_kernel,
        out_shape=jax.ShapeDtypeStruct((BATCH * N_MEM, 2 * MEMQ_W), BF16),
        grid=(BATCH,),
        in_specs=[pl.BlockSpec((N_MEM, D_MODEL), lambda b: (b, 0)),
                  pl.BlockSpec(g2.shape, lambda b: (0, 0)),
                  pl.BlockSpec(w.shape, lambda b: (0, 0))],
        out_specs=pl.BlockSpec((N_MEM, 2 * MEMQ_W), lambda b: (b, 0)),
        compiler_params=_cparams(("parallel",)),
        name="mem_prep",
    )(mem2d, g2, w)


def _mem_attn_kernel(q_ref, kv_ref, gate_ref, y_ref):
    cols = [slice(h * MEM_DH, (h + 1) * MEM_DH) for h in range(MEM_HEADS)]
    ss = [_qk(q_ref[:, c], kv_ref[:, c]) for c in cols]
    ps = [jnp.exp2(s - jnp.max(s, axis=1, keepdims=True)) for s in ss]
    for h, (c, p) in enumerate(zip(cols, ps)):
        l = jnp.sum(p, axis=1, keepdims=True)
        a = MEMQ_W + h * MEM_DH
        o = jnp.dot(p.astype(BF16), kv_ref[:, a:a + MEM_DH], preferred_element_type=F32) / l
        y_ref[:, c] = (o * _silu(gate_ref[:, c].astype(F32))).astype(y_ref.dtype)


def _mem_attn(qm, kvm, gate, tq):
    nq = SEQ // tq
    return pl.pallas_call(
        _mem_attn_kernel,
        out_shape=jax.ShapeDtypeStruct((N_TOK, MEMQ_W), BF16),
        grid=(BATCH, nq),
        in_specs=[pl.BlockSpec((tq, MEMQ_W), lambda b, i: (b * nq + i, 0)),
                  pl.BlockSpec((N_MEM, 2 * MEMQ_W), lambda b, i: (b, 0)),
                  pl.BlockSpec((tq, MEMQ_W), lambda b, i: (b * nq + i, 0))],
        out_specs=pl.BlockSpec((tq, MEMQ_W), lambda b, i: (b * nq + i, 0)),
        compiler_params=_cparams(("parallel", "parallel")),
        name="mem_attn",
    )(qm, kvm, gate)


def _band_bias(n, lo_off, hi_off):
    i = lax.broadcasted_iota(jnp.int32, (Q_BLOCK, 2 * Q_BLOCK), 0)
    c = lax.broadcasted_iota(jnp.int32, (Q_BLOCK, 2 * Q_BLOCK), 1)
    vis = (c - i >= lo_off) & (c - i <= hi_off) & ((c >= Q_BLOCK) | (n > 0))
    return jnp.where(vis, 0.0, NEG).astype(F32)


SWA_BLOCKS_PER_STEP = 8


def _swa_kernel(q_ref, kvp_ref, kvo_ref, gate_ref, sink_ref, y_ref):
    n = pl.program_id(1)
    for u in range(SWA_BLOCKS_PER_STEP):
        rows = slice(u * Q_BLOCK, (u + 1) * Q_BLOCK)
        prev = kvp_ref if u == 0 else kvo_ref.at[(u - 1) * Q_BLOCK:u * Q_BLOCK, :]
        _swa_block(n if u == 0 else 1, q_ref.at[rows, :], prev, kvo_ref.at[rows, :], gate_ref.at[rows, :],
                   sink_ref, y_ref.at[rows, :])


def _swa_block(n, q_ref, kvp_ref, kvo_ref, gate_ref, sink_ref, y_ref):
    QB = Q_BLOCK
    G = SWA_HEADS // SWA_KV_HEADS
    P = G // 2
    c = lax.broadcasted_iota(jnp.int32, (2 * QB, QB), 0)
    i = lax.broadcasted_iota(jnp.int32, (2 * QB, QB), 1)
    vis = (c - i >= QB - (SWA_WINDOW - 1)) & (c - i <= QB) & ((c >= QB) | (n > 0))
    bias = jnp.where(vis, 0.0, NEG).astype(F32)
    bias4 = jnp.concatenate([bias] * P, axis=1)
    k2 = jnp.concatenate([kvp_ref[:, :LANES], kvo_ref[:, :LANES]], axis=0).astype(F32)
    v2 = jnp.concatenate([kvp_ref[:, LANES:], kvo_ref[:, LANES:]], axis=0).astype(F32)
    k2r = pltpu.roll(k2, SWA_DH, 1)
    v2r = pltpu.roll(v2, SWA_DH, 1)
    lane = lax.broadcasted_iota(jnp.int32, (2 * QB, LANES), 1)
    lo = lane < SWA_DH
    top = lax.broadcasted_iota(jnp.int32, (LANES, P * QB), 0) < SWA_DH
    vts, ss = [], []
    for kv in range(SWA_KV_HEADS):
        ka, kb_ = (k2, k2r) if kv == 0 else (k2r, k2)
        va, vb_ = (v2, v2r) if kv == 0 else (v2r, v2)
        k_lo = jnp.where(lo, ka, 0.0).astype(BF16)
        k_hi = jnp.where(lo, 0.0, kb_).astype(BF16)
        vts.append((jnp.where(lo, va, 0.0).T.astype(BF16), jnp.where(lo, 0.0, vb_).T.astype(BF16)))
        base = kv * P
        q4 = jnp.concatenate([q_ref[:, (base + p) * LANES:(base + p + 1) * LANES] for p in range(P)],
                             axis=0)
        ss.append((_qk(k_lo, q4) + bias4, _qk(k_hi, q4) + bias4))
    stats = []
    for pair in ss:
        st = []
        for s in pair:
            m = jnp.max(s, axis=0, keepdims=True)
            p = jnp.exp2(s - m)
            st.append((m, p, jnp.sum(p, axis=0, keepdims=True)))
        stats.append(st)
    for kv in range(SWA_KV_HEADS):
        (m_lo, p_lo, l_lo), (m_hi, p_hi, l_hi) = stats[kv]
        vt_lo, vt_hi = vts[kv]
        base = kv * P
        ot = (jnp.dot(vt_lo, p_lo.astype(BF16), preferred_element_type=F32)
              + jnp.dot(vt_hi, p_hi.astype(BF16), preferred_element_type=F32))
        cols = slice(base * LANES, (base + P) * LANES)
        f_lo = 1.0 / ((1.0 + jnp.exp2(sink_ref[0:1, cols] * LOG2E - (m_lo + jnp.log2(l_lo)))) * l_lo)
        f_hi = 1.0 / ((1.0 + jnp.exp2(sink_ref[1:2, cols] * LOG2E - (m_hi + jnp.log2(l_hi)))) * l_hi)
        ot = ot * jnp.where(top, f_lo, f_hi)
        for p in range(P):
            cc = slice((base + p) * LANES, (base + p + 1) * LANES)
            o = ot[:, p * QB:(p + 1) * QB].T
            y_ref[:, cc] = (o * _silu(gate_ref[:, cc].astype(F32))).astype(y_ref.dtype)


def _swa_attn(qc, kvc, gate, sink_rows):
    R = SWA_BLOCKS_PER_STEP
    ns = SEQ // (R * Q_BLOCK)
    w = SWA_HEADS * SWA_DH
    return pl.pallas_call(
        _swa_kernel,
        out_shape=jax.ShapeDtypeStruct((N_TOK, w), BF16),
        grid=(BATCH, ns),
        in_specs=[pl.BlockSpec((R * Q_BLOCK, w), lambda b, n: (b * ns + n, 0)),
                  pl.BlockSpec((Q_BLOCK, 2 * LANES), lambda b, n: (jnp.maximum((b * ns + n) * R - 1, 0), 0)),
                  pl.BlockSpec((R * Q_BLOCK, 2 * LANES), lambda b, n: (b * ns + n, 0)),
                  pl.BlockSpec((R * Q_BLOCK, w), lambda b, n: (b * ns + n, 0)),
                  pl.BlockSpec((2, w), lambda b, n: (0, 0))],
        out_specs=pl.BlockSpec((R * Q_BLOCK, w), lambda b, n: (b * ns + n, 0)),
        compiler_params=_cparams(("parallel", "parallel")),
        name="swa_attn",
    )(qc, kvc, kvc, gate, sink_rows)


DIL_BLOCKS_PER_STEP = 8
DIL_GROUP = 2
MAX_STRIDE = 4


def _dil_kernel(R, nres, q_ref, kp_ref, ko_ref, vp_ref, vo_ref, o_ref, lse_ref):
    QB = Q_BLOCK
    n = pl.program_id(2)
    lane = lax.broadcasted_iota(jnp.int32, (QB, LANES), 1)
    W = DIL_HEADS * DIL_DH
    for rr in range(nres):
        for u0 in range(0, R, DIL_GROUP):
            tasks = []
            for u in range(u0, u0 + DIL_GROUP):
                rows = slice(u * QB, (u + 1) * QB)
                bias = _band_bias(n if u == 0 else 1, 0, QB)
                for h in range(DIL_HEADS):
                    c = slice(rr * W + h * DIL_DH, rr * W + (h + 1) * DIL_DH)
                    k_prev = kp_ref[:, c] if u == 0 else ko_ref[(u - 1) * QB:u * QB, c]
                    v_prev = vp_ref[:, c] if u == 0 else vo_ref[(u - 1) * QB:u * QB, c]
                    k = jnp.concatenate([k_prev, ko_ref[rows, c]], axis=0)
                    v = jnp.concatenate([v_prev, vo_ref[rows, c]], axis=0)
                    tasks.append((u, h, c, _qk(q_ref[rows, c], k) + bias, v))
            soft = []
            for u, h, c, s, v in tasks:
                m = jnp.max(s, axis=1, keepdims=True)
                p = jnp.exp2(s - m)
                soft.append((m, p, jnp.sum(p, axis=1, keepdims=True)))
            lse_all = {u: jnp.zeros((QB, LANES), F32) for u in range(u0, u0 + DIL_GROUP)}
            for (u, h, c, _, v), (m, p, l) in zip(tasks, soft):
                o_ref[u * QB:(u + 1) * QB, c] = (
                    jnp.dot(p.astype(BF16), v, preferred_element_type=F32) / l).astype(o_ref.dtype)
                lse_all[u] = jnp.where(lane == h, m + jnp.log2(l), lse_all[u])
            for u, val in lse_all.items():
                lse_ref[u * QB:(u + 1) * QB, rr * LANES:(rr + 1) * LANES] = val


def _dil_attn(qv, kv, vv, dil):
    w = DIL_HEADS * DIL_DH
    L = SEQ // dil
    R = min(DIL_BLOCKS_PER_STEP, L // Q_BLOCK)
    nres = min(dil, DIL_BLOCKS_PER_STEP // R)
    ns = L // (R * Q_BLOCK)
    cur = lambda b, r, n: (b * ns + n, r)
    prev = lambda b, r, n: (jnp.maximum((b * ns + n) * R - 1, 0), r)
    blk = (R * Q_BLOCK, nres * w)
    pblk = (Q_BLOCK, nres * w)
    return pl.pallas_call(
        functools.partial(_dil_kernel, R, nres),
        out_shape=[jax.ShapeDtypeStruct((BATCH * L, dil * w), BF16),
                   jax.ShapeDtypeStruct((BATCH * L, dil * LANES), F32)],
        grid=(BATCH, dil // nres, ns),
        in_specs=[pl.BlockSpec(blk, cur), pl.BlockSpec(pblk, prev), pl.BlockSpec(blk, cur),
                  pl.BlockSpec(pblk, prev), pl.BlockSpec(blk, cur)],
        out_specs=[pl.BlockSpec(blk, cur), pl.BlockSpec((R * Q_BLOCK, nres * LANES), cur)],
        compiler_params=_cparams(("parallel", "parallel", "parallel")),
        name=f"dil_attn_d{dil}",
    )(qv, kv, kv, vv, vv)


def _dil_combine_kernel(tm, *refs):
    np_ = len(DIL_PATTERNS)
    o_refs, l_refs = refs[:np_], refs[np_:2 * np_]
    gate_ref, y_ref = refs[2 * np_], refs[2 * np_ + 1]
    stage = refs[2 * np_ + 2:]
    w = DIL_HEADS * DIL_DH
    os_, ls = [], []
    k = 0
    for (_, d), o_ref, l_ref in zip(DIL_PATTERNS, o_refs, l_refs):
        if d == 1:
            os_.append(lambda h, o_ref=o_ref: o_ref[:, h * DIL_DH:(h + 1) * DIL_DH].astype(F32))
            ls.append(l_ref[...])
            continue
        so, sl = stage[2 * k], stage[2 * k + 1]
        k += 1
        if d <= MAX_STRIDE:
            hops = [(so, sl, d, lambda r: r)]
        else:
            assert d == MAX_STRIDE * MAX_STRIDE
            co, cl = stage[-2], stage[-1]
            q4 = tm // MAX_STRIDE
            hops = [(co, cl, MAX_STRIDE, lambda r: (r % MAX_STRIDE) * q4 + r // MAX_STRIDE)]
        dst_o, dst_l, stride, start = hops[0]
        for r in range(d):
            rows = pl.ds(start(r), tm // d, stride=stride)
            for h in range(DIL_HEADS):
                a = r * w + h * DIL_DH
                dst_o[h, rows, :] = o_ref[:, a:a + DIL_DH].astype(F32)
            dst_l[rows, :] = l_ref[:, r * LANES:(r + 1) * LANES]
        if d > MAX_STRIDE:
            for r in range(MAX_STRIDE):
                rows = pl.ds(r, q4, stride=MAX_STRIDE)
                blk = slice(r * q4, (r + 1) * q4)
                for h in range(DIL_HEADS):
                    so[h, rows, :] = co[h, blk, :]
                sl[rows, :] = cl[blk, :]
        os_.append(lambda h, so=so: so[h])
        ls.append(sl[...])
    mx = functools.reduce(jnp.maximum, ls)
    es = [jnp.exp2(x - mx) for x in ls]
    den = functools.reduce(lambda a, b: a + b, es)
    ws = [e / den for e in es]
    lane = lax.broadcasted_iota(jnp.int32, ws[0].shape, 1)
    for h in range(DIL_HEADS):
        c = slice(h * DIL_DH, (h + 1) * DIL_DH)
        wh = [jnp.sum(jnp.where(lane == h, wgt, 0.0), axis=1, keepdims=True) for wgt in ws]
        o = functools.reduce(lambda a, b: a + b, [wh[p] * os_[p](h) for p in range(np_)])
        y_ref[:, c] = (o * _silu(gate_ref[:, c].astype(F32))).astype(y_ref.dtype)


def _dil_combine(outs, lses, gate, tm):
    w = DIL_HEADS * DIL_DH
    in_specs = [pl.BlockSpec((tm // d, d * w), lambda i: (i, 0)) for _, d in DIL_PATTERNS]
    in_specs += [pl.BlockSpec((tm // d, d * LANES), lambda i: (i, 0)) for _, d in DIL_PATTERNS]
    in_specs.append(pl.BlockSpec((tm, w), lambda i: (i, 0)))
    scratch = []
    for _, d in DIL_PATTERNS:
        if d > 1:
            scratch += [pltpu.VMEM((DIL_HEADS, tm, DIL_DH), F32), pltpu.VMEM((tm, LANES), F32)]
    if any(d > MAX_STRIDE for _, d in DIL_PATTERNS):
        scratch += [pltpu.VMEM((DIL_HEADS, tm, DIL_DH), F32), pltpu.VMEM((tm, LANES), F32)]
    return pl.pallas_call(
        functools.partial(_dil_combine_kernel, tm),
        out_shape=jax.ShapeDtypeStruct((N_TOK, w), BF16),
        grid=(N_TOK // tm,),
        in_specs=in_specs,
        out_specs=pl.BlockSpec((tm, w), lambda i: (i, 0)),
        scratch_shapes=scratch,
        compiler_params=_cparams(("parallel",)),
        name="dil_combine",
    )(*outs, *lses, gate)


def _outproj_kernel(gated, final, *refs):
    nparts = len(gated)
    refs = list(refs)
    ys = []
    for has_gate in gated:
        y = refs.pop(0)[...]
        if has_gate:
            y = (y.astype(F32) * _silu(refs.pop(0)[...].astype(F32))).astype(BF16)
        ys.append(y)
    ws = refs[:nparts]
    x_ref = refs[nparts]
    o_ref = refs[-1]
    acc = x_ref[...]
    for y, w in zip(ys, ws):
        acc = acc + jnp.dot(y, w[...], preferred_element_type=F32)
    if final:
        acc = _rms(acc, refs[nparts + 1][...])
    o_ref[...] = acc


def _outproj(ys, ws, x, final_g, tm):
    n_tok, d = x.shape
    gated = tuple(isinstance(y, tuple) for y in ys)
    flat = [a for y in ys for a in (y if isinstance(y, tuple) else (y,))]
    in_specs = [pl.BlockSpec((tm, a.shape[1]), lambda i: (i, 0)) for a in flat]
    in_specs += [pl.BlockSpec(w.shape, lambda i: (0, 0)) for w in ws]
    in_specs.append(pl.BlockSpec((tm, d), lambda i: (i, 0)))
    args = [*flat, *ws, x]
    if final_g is not None:
        in_specs.append(pl.BlockSpec((1, d), lambda i: (0, 0)))
        args.append(final_g.reshape(1, d))
    return pl.pallas_call(
        functools.partial(_outproj_kernel, gated, final_g is not None),
        out_shape=jax.ShapeDtypeStruct((n_tok, d), F32),
        grid=(n_tok // tm,),
        in_specs=in_specs,
        out_specs=pl.BlockSpec((tm, d), lambda i: (i, 0)),
        compiler_params=_cparams(("parallel",)),
        name="outproj",
    )(*args)


PROJ_TM = 512
PROJ_TM_WIDE = 1024
MEM_TQ = 1024
ATTN_TQ = 512
MLA_HEADS_PER_STEP = 2
MOBA_HEADS_PER_STEP = 2


def _even_layer(x, mem2d, norm_g, w_in, q_norm_g, w_uq, kv_norm_g, w_ukv, mem_norm_g, w_mem_kv, w_out):
    n_lat = MLA_Q_RANK + MLA_KV_RANK + MLA_ROPE
    wlat = jnp.pad(w_in[:, :n_lat], ((0, 0), (0, LANES - MLA_ROPE))).astype(BF16)
    n_bqkv = n_lat + 3 * MOBA_HEADS * MOBA_DH
    w_bqkv = w_in[:, n_lat:n_bqkv].astype(BF16)
    w_rest = w_in[:, n_bqkv:].astype(BF16)
    wuq = w_uq.reshape(MLA_Q_RANK, MLA_HEADS, MLA_NOPE + MLA_ROPE)
    wuq = jnp.pad(wuq, ((0, 0), (0, 0), (0, MLA_HW - MLA_NOPE - MLA_ROPE)))
    wuq = wuq.reshape(MLA_Q_RANK, MLA_HEADS * MLA_HW).astype(BF16)
    wukv = w_ukv.reshape(MLA_KV_RANK, MLA_HEADS, MLA_NOPE + MLA_V)
    wukv = jnp.concatenate([wukv[:, :, :MLA_NOPE].reshape(MLA_KV_RANK, -1),
                            wukv[:, :, MLA_NOPE:].reshape(MLA_KV_RANK, -1)], axis=1).astype(BF16)

    cos_a, sin_a = _rope_tables(MLA_ROPE, LANES)
    qa, ka, va = _mla_prep(x, norm_g, wlat, q_norm_g, wuq, kv_norm_g, wukv, cos_a, sin_a, ATTN_TQ)

    rot = MOBA_DH // ROT_FRAC
    cos_b, sin_b = _rope_tables(rot, MOBA_DH)
    sc = MOBA_DH ** -0.5 * LOG2E
    wb = MOBA_HEADS * MOBA_DH
    plain = [("plain",)]
    groups_qkv = [
        (wb, [("rope", 0, rot // 2, MOBA_DH, sc)] * MOBA_HEADS, "rows"),
        (wb, [("rope", 0, rot // 2, MOBA_DH, 1.0)] * MOBA_HEADS, "rows"),
        (wb, plain * MOBA_HEADS, ("vT", MOBA_TILE)),
    ]
    groups_rest = [
        (MEMQ_W, [("scale", MEM_DH ** -0.5 * LOG2E)] * MEM_HEADS, "rows"),
        (MLA_HEADS * MLA_V, plain * MLA_HEADS, "rows"),
        (wb, plain * MOBA_HEADS, "rows"),
        (MEMQ_W, plain * MEM_HEADS, "rows"),
    ]
    qb, kb, vb = _inproj(x, norm_g, w_bqkv, groups_qkv, [cos_b, sin_b], PROJ_TM)
    qm, gate_a, gate_b, gate_m = _inproj(x, norm_g, w_rest, groups_rest, [], PROJ_TM_WIDE)

    y_a = (_mla_attn(qa, ka, va, ATTN_TQ, MLA_HEADS_PER_STEP), gate_a)
    y_b = (_moba_flat_attn(qb, kb, vb, MOBA_HEADS_PER_STEP), gate_b)
    kvm = _mem_prep(mem2d, mem_norm_g, w_mem_kv.astype(BF16))
    y_m = _mem_attn(qm, kvm, gate_m, MEM_TQ)

    a, b = MLA_HEADS * MLA_V, MLA_HEADS * MLA_V + MOBA_HEADS * MOBA_DH
    return [y_a, y_b, y_m], [w_out[:a].astype(BF16), w_out[a:b].astype(BF16), w_out[b:].astype(BF16)]


def _odd_layer(x, mem2d, norm_g, w_in, sinks, mem_norm_g, w_mem_kv, w_out):
    rot_c = SWA_DH // ROT_FRAC
    rot_d = DIL_DH // ROT_FRAC
    cos_c, sin_c = _rope_tables(rot_c, SWA_DH)
    cos_d, sin_d = _rope_tables(rot_d, DIL_DH)
    wq = SWA_HEADS * SWA_DH
    wd = DIL_HEADS * DIL_DH
    wkv = 2 * SWA_KV_HEADS * SWA_DH
    rc = lambda s: ("rope", 0, rot_c // 2, SWA_DH, s)
    rd = lambda s: ("rope", 1, rot_d // 2, DIL_DH, s)
    plain = [("plain",)]
    groups_qkv = [
        (wq, [rc(SWA_DH ** -0.5 * LOG2E)] * (wq // LANES), "rows"),
        (wkv, [rc(1.0), ("plain",)], "rows"),
        (wd, [rd(DIL_DH ** -0.5 * LOG2E)] * DIL_HEADS, "dilated"),
        (wd, [rd(1.0)] * DIL_HEADS, "dilated"),
        (wd, plain * DIL_HEADS, "dilated"),
    ]
    groups_rest = [
        (MEMQ_W, [("scale", MEM_DH ** -0.5 * LOG2E)] * MEM_HEADS, "rows"),
        (wq, plain * (wq // LANES), "rows"),
        (wd, plain * DIL_HEADS, "rows"),
        (MEMQ_W, plain * MEM_HEADS, "rows"),
    ]
    n_qkv = wq + wkv + 3 * wd
    res = _inproj(x, norm_g, w_in[:, :n_qkv].astype(BF16), groups_qkv, [cos_c, sin_c, cos_d, sin_d], PROJ_TM)
    qc, kvc = res[0], res[1]
    nv = 1 + len(DIL_VIEWS)
    qd, kd, vd = (res[2 + i * nv:2 + (i + 1) * nv] for i in range(3))
    qm, gate_c, gate_d, gate_m = _inproj(x, norm_g, w_in[:, n_qkv:].astype(BF16), groups_rest, [], PROJ_TM_WIDE)

    sink_rows = jnp.repeat(sinks.astype(F32).reshape(SWA_HEADS // 2, 2).T, LANES, axis=1)
    y_c = _swa_attn(qc, kvc, gate_c, sink_rows)
    view_of = {1: 0, **{d: 1 + i for i, d in enumerate(DIL_VIEWS)}}
    outs, lses = zip(*[_dil_attn(qd[view_of[dil]], kd[view_of[dil]], vd[view_of[dil]], dil)
                       for _, dil in DIL_PATTERNS])
    y_d = _dil_combine(outs, lses, gate_d, PROJ_TM_WIDE)
    kvm = _mem_prep(mem2d, mem_norm_g, w_mem_kv.astype(BF16))
    y_m = _mem_attn(qm, kvm, gate_m, MEM_TQ)

    return [y_c, y_d, y_m], [w_out[:wq].astype(BF16), w_out[wq:wq + wd].astype(BF16),
                             w_out[wq + wd:].astype(BF16)]


def kernel(x, mem, ev_norm_g, ev_w_in, ev_q_norm_g, ev_w_uq, ev_kv_norm_g, ev_w_ukv, ev_mem_norm_g,
           ev_w_mem_kv, ev_w_out, od_norm_g, od_w_in, od_sinks, od_mem_norm_g, od_w_mem_kv, od_w_out,
           final_norm_g):
    x2 = x.reshape(N_TOK, D_MODEL)
    mem2d = mem.reshape(BATCH * N_MEM, D_MODEL)
    ys, ws = _even_layer(x2, mem2d, ev_norm_g[0], ev_w_in[0], ev_q_norm_g[0], ev_w_uq[0], ev_kv_norm_g[0],
                         ev_w_ukv[0], ev_mem_norm_g[0], ev_w_mem_kv[0], ev_w_out[0])
    x2 = _outproj(ys, ws, x2, None, PROJ_TM)
    ys, ws = _odd_layer(x2, mem2d, od_norm_g[0], od_w_in[0], od_sinks[0], od_mem_norm_g[0],
                        od_w_mem_kv[0], od_w_out[0])
    x2 = _outproj(ys, ws, x2, final_norm_g, PROJ_TM)
    return x2.reshape(BATCH, SEQ, D_MODEL)
```
